```python
import math
import jax, jax.numpy as jnp
from jax import lax
import numpy as np


D_MODEL = 2048
BATCH = 1
SEQ = 16384
DEPTH = 2

D_FF = 256 * ((8 * D_MODEL // 3 + 255) // 256)
W_A = D_MODEL // 4
W_B = D_MODEL // 4
GLA_HEADS = 4
GLA_DK = D_MODEL // 16
GLA_DV = D_MODEL // 8
W_CK = GLA_HEADS * GLA_DK
W_CV = GLA_HEADS * GLA_DV
GLA_RANK = 16
GLA_TAU = 16.0
GLA_CHUNK = 64
HY_ORDER = 2
HY_BANDS = 16
HY_EMB = 2 * HY_BANDS + 1
HY_HIDDEN = 64
HY_FAST_DECAY = 0.3
HY_SLOW_DECAY = 1.5
HY_TARGET = 1e-2
N_BRANCH = 3
EPS = 1e-5
SECTIONS = (W_A, W_A, W_A, W_B, W_B, W_B, W_CK, W_CK, W_CV, W_CV, GLA_RANK, GLA_RANK, D_MODEL, D_MODEL, D_MODEL)
N_IN = sum(SECTIONS)

kernel_name = 'hybrid_gated_conv_hyena_gla_encoder'


def rms_norm(x, g):
    xf = x.astype(jnp.float32)
    y = xf * lax.rsqrt(jnp.mean(xf * xf, axis=-1, keepdims=True) + EPS)
    return (y * g.astype(jnp.float32)).astype(x.dtype)


def swiglu(x, w_gu, w_down):
    g, u = jnp.split(x @ w_gu, 2, axis=-1)
    return (jax.nn.silu(g) * u) @ w_down


def conv3_centered(x, w):
    xp = jnp.pad(x, ((0, 0), (1, 1), (0, 0)))
    return xp[:, :-2] * w[0] + xp[:, 1:-1] * w[1] + xp[:, 2:] * w[2]


def split_sections(p):
    idx, acc = [], 0
    for s in SECTIONS[:-1]:
        acc += s
        idx.append(acc)
    return jnp.split(p, idx, axis=-1)


def hyena_filters(L, w1, b1, w2, b2, w3, b3, w_out, freq):
    f32 = jnp.float32
    pos = jnp.arange(L, dtype=f32)
    t = pos / max(L - 1, 1)
    bands = jnp.linspace(1e-4, HY_BANDS - 1, HY_BANDS, dtype=f32)
    ang = (2.0 * math.pi / L) * pos[:, None] * bands[None, :]
    z = jnp.concatenate([t[:, None], jnp.cos(ang), -jnp.sin(ang)], axis=-1)
    fr = freq.astype(f32)
    h = jnp.sin(fr * (z @ w1.astype(f32) + b1.astype(f32)))
    h = jnp.sin(fr * (h @ w2.astype(f32) + b2.astype(f32)))
    h = jnp.sin(fr * (h @ w3.astype(f32) + b3.astype(f32)))
    h = (h @ w_out.astype(f32)).reshape(L, HY_ORDER, 2, W_B)
    deltas = jnp.linspace(math.log(HY_TARGET) / HY_SLOW_DECAY, math.log(HY_TARGET) / HY_FAST_DECAY, W_B, dtype=f32)
    decay = jnp.exp(-t[:, None] * jnp.abs(deltas)[None, :])
    h = h * decay[:, None, None, :]
    return h / (jnp.sum(jnp.abs(h), axis=0, keepdims=True) + EPS)


def bidir_long_conv(z, h_fwd, h_bwd):
    L = z.shape[1]
    k = jnp.concatenate([h_fwd, jnp.zeros_like(h_fwd[:1]), h_bwd[:0:-1]], axis=0)
    kf = jnp.fft.rfft(k, n=2 * L, axis=0)
    zf = jnp.fft.rfft(z.astype(jnp.float32), n=2 * L, axis=1)
    y = jnp.fft.irfft(zf * kf[None], n=2 * L, axis=1)[:, :L]
    return y.astype(z.dtype)


def gla_chunked(q, k, v, g):
    f32 = jnp.float32
    bsz, nh, L, dk = q.shape
    dv = v.shape[-1]
    n = L // GLA_CHUNK
    rs = lambda a: a.reshape(bsz, nh, n, GLA_CHUNK, a.shape[-1])
    q, k, v, g = rs(q.astype(f32)), rs(k.astype(f32)), rs(v.astype(f32)), rs(g.astype(f32))
    b = jnp.cumsum(g, axis=3)
    qe = q * jnp.exp(b)
    ke = k * jnp.exp(-b)
    mask = jnp.tril(jnp.ones((GLA_CHUNK, GLA_CHUNK), dtype=bool))
    a = jnp.where(mask, jnp.einsum('bhnid,bhnjd->bhnij', qe, ke), 0.0)
    o = jnp.einsum('bhnij,bhnjv->bhniv', a, v)
    b_last = b[:, :, :, -1:, :]
    kd = k * jnp.exp(b_last - b)
    upd = jnp.einsum('bhncd,bhncv->nbhdv', kd, v)
    dec = jnp.moveaxis(jnp.exp(b_last[:, :, :, 0, :]), 2, 0)

    def step(s, inp):
        d, uu = inp
        return d[..., None] * s + uu, s

    s0 = jnp.zeros((bsz, nh, dk, dv), f32)
    _, s_prev = lax.scan(step, s0, (dec, upd))
    o = o + jnp.einsum('bhncd,nbhdv->bhncv', qe, s_prev)
    return o.reshape(bsz, nh, L, dv)


def gla_bidir(q, k, v, g_f, g_b):
    flip = lambda a: jnp.flip(a, axis=2)
    return gla_chunked(q, k, v, g_f) + flip(gla_chunked(flip(q), flip(k), flip(v), flip(g_b)))


def to_heads(a, d):
    bsz, L, _ = a.shape
    return a.reshape(bsz, L, GLA_HEADS, d).transpose(0, 2, 1, 3)


def setup_inputs(seed: int = 0) -> dict:
    key = jax.random.key(seed)
    ks = jax.random.split(key, 32)
    f32 = jnp.float32
    nrm = lambda k, shape, scale: scale * jax.random.normal(k, shape, f32)
    gain = lambda k, shape: 1.0 + 0.01 * jax.random.normal(k, shape, f32)
    D, F = D_MODEL, D_FF
    return {
        'x': jax.random.normal(ks[0], (BATCH, SEQ, D), f32),
        'ffn1_norm': gain(ks[1], (DEPTH, D)),
        'ffn1_w_gu': nrm(ks[2], (DEPTH, D, 2 * F), D ** -0.5),
        'ffn1_w_down': nrm(ks[3], (DEPTH, F, D), F ** -0.5),
        'mix_norm': gain(ks[4], (DEPTH, D)),
        'w_in': nrm(ks[5], (DEPTH, D, N_IN), D ** -0.5),
        'b_in': nrm(ks[6], (DEPTH, N_IN), 0.01),
        'conv_a': nrm(ks[7], (DEPTH, 3, W_A), 3 ** -0.5),
        'conv_b': nrm(ks[8], (DEPTH, HY_ORDER + 1, 3, W_B), 3 ** -0.5),
        'hf_w1': nrm(ks[9], (DEPTH, HY_EMB, HY_HIDDEN), HY_EMB ** -0.5),
        'hf_b1': nrm(ks[10], (DEPTH, HY_HIDDEN), 0.02),
        'hf_w2': nrm(ks[11], (DEPTH, HY_HIDDEN, HY_HIDDEN), HY_HIDDEN ** -0.5),
        'hf_b2': nrm(ks[12], (DEPTH, HY_HIDDEN), 0.02),
        'hf_w3': nrm(ks[13], (DEPTH, HY_HIDDEN, HY_HIDDEN), HY_HIDDEN ** -0.5),
        'hf_b3': nrm(ks[14], (DEPTH, HY_HIDDEN), 0.02),
        'hf_w_out': nrm(ks[15], (DEPTH, HY_HIDDEN, HY_ORDER * 2 * W_B), HY_HIDDEN ** -0.5),
        'hf_freq': gain(ks[16], (DEPTH, HY_HIDDEN)),
        'hy_skip': nrm(ks[17], (DEPTH, HY_ORDER, W_B), 0.1),
        'gk_w2': nrm(ks[18], (DEPTH, 2, GLA_RANK, W_CK), GLA_RANK ** -0.5),
        'gk_b': nrm(ks[19], (DEPTH, 2, W_CK), 0.01),
        'gla_norm': gain(ks[20], (DEPTH, GLA_DV)),
        'w_br_a': nrm(ks[21], (DEPTH, W_A, D), W_A ** -0.5),
        'w_br_b': nrm(ks[22], (DEPTH, W_B, D), W_B ** -0.5),
        'w_br_c': nrm(ks[23], (DEPTH, W_CV, D), W_CV ** -0.5),
        'w_o': nrm(ks[24], (DEPTH, D, D), D ** -0.5),
        'ffn2_norm': gain(ks[25], (DEPTH, D)),
        'ffn2_w_gu': nrm(ks[26], (DEPTH, D, 2 * F), D ** -0.5),
        'ffn2_w_down': nrm(ks[27], (DEPTH, F, D), F ** -0.5),
        'final_norm': gain(ks[28], (D,)),
    }


def reference(x, ffn1_norm, ffn1_w_gu, ffn1_w_down, mix_norm, w_in, b_in, conv_a, conv_b,
              hf_w1, hf_b1, hf_w2, hf_b2, hf_w3, hf_b3, hf_w_out, hf_freq, hy_skip,
              gk_w2, gk_b, gla_norm, w_br_a, w_br_b, w_br_c, w_o,
              ffn2_norm, ffn2_w_gu, ffn2_w_down, final_norm):
    bsz, L, _ = x.shape
    for l in range(DEPTH):
        h = x + 0.5 * swiglu(rms_norm(x, ffn1_norm[l]), ffn1_w_gu[l], ffn1_w_down[l])
        u = rms_norm(h, mix_norm[l])
        (xa, ba, ca, hv, hx1, hx2, q, k, v, r, lr_f, lr_b, ga, gb, gc) = split_sections(u @ w_in[l] + b_in[l])

        y_a = (ba * conv3_centered(ca * xa, conv_a[l])) @ w_br_a[l]

        hv = conv3_centered(hv, conv_b[l, 0])
        hx1 = conv3_centered(hx1, conv_b[l, 1])
        hx2 = conv3_centered(hx2, conv_b[l, 2])
        filt = hyena_filters(L, hf_w1[l], hf_b1[l], hf_w2[l], hf_b2[l], hf_w3[l], hf_b3[l], hf_w_out[l], hf_freq[l])
        z = hv
        for i, gate in enumerate((hx1, hx2)):
            z = gate * (bidir_long_conv(z, filt[:, i, 0], filt[:, i, 1]) + hy_skip[l, i] * z)
        y_b = z @ w_br_b[l]

        g_f = jax.nn.log_sigmoid((lr_f @ gk_w2[l, 0] + gk_b[l, 0]).astype(jnp.float32)) / GLA_TAU
        g_b = jax.nn.log_sigmoid((lr_b @ gk_w2[l, 1] + gk_b[l, 1]).astype(jnp.float32)) / GLA_TAU
        o = gla_bidir(to_heads(q, GLA_DK) * (GLA_DK ** -0.5), to_heads(k, GLA_DK), to_heads(v, GLA_DV),
                      to_heads(g_f, GLA_DK), to_heads(g_b, GLA_DK))
        o = rms_norm(o.transpose(0, 2, 1, 3).astype(u.dtype), gla_norm[l]).reshape(bsz, L, W_CV)
        y_c = (o * jax.nn.silu(r)) @ w_br_c[l]

        merged = jax.nn.sigmoid(ga) * y_a + jax.nn.sigmoid(gb) * y_b + jax.nn.sigmoid(gc) * y_c
        h = h + merged @ w_o[l]

        x = h + 0.5 * swiglu(rms_norm(h, ffn2_norm[l]), ffn2_w_gu[l], ffn2_w_down[l])
    return rms_norm(x, final_norm)
```

```python
import functools
import math

import numpy as np
import jax
import jax.numpy as jnp
from jax import lax
from jax.experimental import pallas as pl
from jax.experimental.pallas import tpu as pltpu

F32 = jnp.float32
BF16 = jnp.bfloat16
HIGHEST = lax.Precision.HIGHEST

EPS = 1e-5
GLA_HEADS = 4
GLA_RANK = 16
GLA_TAU = 16.0
GLA_CHUNK = 64
HY_ORDER = 2
HY_BANDS = 16
HY_FAST_DECAY = 0.3
HY_SLOW_DECAY = 1.5
HY_TARGET = 1e-2

LANES = 128
SUBLANES = 8
DFT_N1 = 256
VMEM_LIMIT = 56 * 1024 * 1024


def _params(sem):
    return pltpu.CompilerParams(dimension_semantics=sem, vmem_limit_bytes=VMEM_LIMIT)


def _rms(x, g):
    return x * lax.rsqrt(jnp.mean(x * x, axis=-1, keepdims=True) + EPS) * g


def _dot(a, b):
    return jnp.dot(a, b, preferred_element_type=F32)


def _dot_hi(a, b):
    return jnp.dot(a, b, preferred_element_type=F32, precision=HIGHEST)


def _ffn_body(x_ref, g_ref, wg_ref, wu_ref, wd_ref, fn_ref, o_ref, xn_ref, acc_ref, *, final):
    j = pl.program_id(1)

    @pl.when(j == 0)
    def _():
        xn_ref[...] = _rms(x_ref[...], g_ref[...]).astype(BF16)
        acc_ref[...] = jnp.zeros_like(acc_ref)

    xn = xn_ref[...]
    g = _dot(xn, wg_ref[...])
    u = _dot(xn, wu_ref[...])
    h = (g * jax.nn.sigmoid(g) * u).astype(BF16)
    acc_ref[...] += _dot(h, wd_ref[...])

    @pl.when(j == pl.num_programs(1) - 1)
    def _():
        y = x_ref[...] + 0.5 * acc_ref[...]
        if final:
            y = _rms(y, fn_ref[...])
        o_ref[...] = y


def _ffn(x, norm_g, w_gu, w_down, final_g, *, final, tm=512, tf=512):
    L, D = x.shape
    F = w_down.shape[0]
    nf = F // tf
    return pl.pallas_call(
        functools.partial(_ffn_body, final=final),
        out_shape=jax.ShapeDtypeStruct((L, D), F32),
        grid=(L // tm, nf),
        in_specs=[
            pl.BlockSpec((tm, D), lambda i, j: (i, 0)),
            pl.BlockSpec((1, D), lambda i, j: (0, 0)),
            pl.BlockSpec((D, tf), lambda i, j: (0, j)),
            pl.BlockSpec((D, tf), lambda i, j: (0, j + nf)),
            pl.BlockSpec((tf, D), lambda i, j: (j, 0)),
            pl.BlockSpec((1, D), lambda i, j: (0, 0)),
        ],
        out_specs=pl.BlockSpec((tm, D), lambda i, j: (i, 0)),
        scratch_shapes=[pltpu.VMEM((tm, D), BF16), pltpu.VMEM((tm, D), F32)],
        compiler_params=_params(("parallel", "arbitrary")),
        name="ffn",
    )(x, norm_g.reshape(1, D), w_gu, w_gu, w_down, final_g.reshape(1, D))


def _proj_body(h_ref, g_ref, w_ref, b_ref, wlr_ref, blr_ref, p_ref, lr_ref, xn_ref):
    @pl.when(pl.program_id(1) == 0)
    def _():
        xn = _rms(h_ref[...], g_ref[...]).astype(BF16)
        xn_ref[...] = xn
        lr_ref[...] = _dot(xn, wlr_ref[...]) + blr_ref[...]

    p_ref[...] = _dot(xn_ref[...], w_ref[...]) + b_ref[...]


def _proj(h, norm_g, w, b, w_lr, b_lr, *, tm=512, tn=512):
    L, D = h.shape
    n_out = w.shape[1]
    return pl.pallas_call(
        _proj_body,
        out_shape=(jax.ShapeDtypeStruct((L, n_out), F32), jax.ShapeDtypeStruct((L, LANES), F32)),
        grid=(L // tm, n_out // tn),
        in_specs=[
            pl.BlockSpec((tm, D), lambda i, j: (i, 0)),
            pl.BlockSpec((1, D), lambda i, j: (0, 0)),
            pl.BlockSpec((D, tn), lambda i, j: (0, j)),
            pl.BlockSpec((1, tn), lambda i, j: (0, j)),
            pl.BlockSpec((D, LANES), lambda i, j: (0, 0)),
            pl.BlockSpec((1, LANES), lambda i, j: (0, 0)),
        ],
        out_specs=(pl.BlockSpec((tm, tn), lambda i, j: (i, j)),
                   pl.BlockSpec((tm, LANES), lambda i, j: (i, 0))),
        scratch_shapes=[pltpu.VMEM((tm, D), BF16)],
        compiler_params=_params(("parallel", "arbitrary")),
        name="proj",
    )(h, norm_g.reshape(1, D), w, b, w_lr, b_lr)


def _conv3(x, x_before, x_after, w):
    tm = x.shape[0]
    row = lax.broadcasted_iota(jnp.int32, x.shape, 0)
    down = jnp.where(row == 0, x_before, pltpu.roll(x, 1, axis=0))
    up = jnp.where(row == tm - 1, x_after, pltpu.roll(x, tm - 1, axis=0))
    return down * w[0:1] + x * w[1:2] + up * w[2:3]


def _prep_body(m_ref, pv_ref, nx_ref, ca_w_ref, cb_w_ref, a_ref, z_ref, g1_ref, g2_ref, *, wa):
    i = pl.program_id(0)
    has_prev = (i > 0).astype(F32)
    has_next = (i < pl.num_programs(0) - 1).astype(F32)
    last = SUBLANES - 1

    def sec(k):
        sl = slice(k * wa, (k + 1) * wa)
        return (m_ref[:, sl], pv_ref[last:last + 1, sl] * has_prev, nx_ref[0:1, sl] * has_next)

    xa, xa_p, xa_n = sec(0)
    ba, _, _ = sec(1)
    ca, ca_p, ca_n = sec(2)
    a_ref[...] = (ba * _conv3(ca * xa, ca_p * xa_p, ca_n * xa_n, ca_w_ref[...])).astype(BF16)
    for k, o_ref in enumerate((z_ref, g1_ref, g2_ref)):
        x, x_p, x_n = sec(3 + k)
        o_ref[...] = _conv3(x, x_p, x_n, cb_w_ref[k])


def _prep(p, conv_a, conv_b, *, wa, tm=256):
    L = p.shape[0]
    width = 6 * wa
    rb = tm // SUBLANES
    nrb = L // SUBLANES
    return pl.pallas_call(
        functools.partial(_prep_body, wa=wa),
        out_shape=(jax.ShapeDtypeStruct((L, wa), BF16),) + (jax.ShapeDtypeStruct((L, wa), F32),) * 3,
        grid=(L // tm,),
        in_specs=[
            pl.BlockSpec((tm, width), lambda i: (i, 0)),
            pl.BlockSpec((SUBLANES, width), lambda i: (jnp.maximum(i * rb - 1, 0), 0)),
            pl.BlockSpec((SUBLANES, width), lambda i: (jnp.minimum((i + 1) * rb, nrb - 1), 0)),
            pl.BlockSpec((3, wa), lambda i: (0, 0)),
            pl.BlockSpec((3, 3, wa), lambda i: (0, 0, 0)),
        ],
        out_specs=(pl.BlockSpec((tm, wa), lambda i: (i, 0)),) * 4,
        compiler_params=_params(("parallel",)),
        name="prep",
    )(p, p, p, conv_a, conv_b)


def _filter_body(z_ref, w1_ref, b1_ref, w2_ref, b2_ref, w3_ref, b3_ref, fr_ref, wo_ref, dl_ref,
                 k_ref, s_ref, *, half_tiles, tr):
    i = pl.program_id(0)
    z = z_ref[...]
    fr = fr_ref[...]
    h = jnp.sin(fr * (_dot_hi(z, w1_ref[...]) + b1_ref[...]))
    h = jnp.sin(fr * (_dot_hi(h, w2_ref[...]) + b2_ref[...]))
    h = jnp.sin(fr * (_dot_hi(h, w3_ref[...]) + b3_ref[...]))
    t = z[:, 0:1]
    decay = jnp.exp(-t * jnp.abs(dl_ref[...]))
    hd = _dot_hi(h, wo_ref[0]) * decay

    @pl.when(i % half_tiles == 0)
    def _():
        s_ref[...] = jnp.zeros_like(s_ref)

    s_ref[0] += jnp.sum(jnp.abs(hd), axis=0, keepdims=True)
    row = lax.broadcasted_iota(jnp.int32, hd.shape, 0)
    zero_row = jnp.where(i == half_tiles, 0, -1)
    k_ref[...] = jnp.where(row == zero_row, 0.0, hd)


def _filters(z_ext, w1, b1, w2, b2, w3, b3, freq, w_out2, deltas2, *, tr=512):
    n_rows, zw = z_ext.shape
    hid = w2.shape[0]
    width = w_out2.shape[2]
    half_tiles = n_rows // (2 * tr)
    full = lambda shape: pl.BlockSpec(shape, lambda i: (0,) * len(shape))
    return pl.pallas_call(
        functools.partial(_filter_body, half_tiles=half_tiles, tr=tr),
        out_shape=(jax.ShapeDtypeStruct((n_rows, width), F32), jax.ShapeDtypeStruct((2, 1, width), F32)),
        grid=(n_rows // tr,),
        in_specs=[
            pl.BlockSpec((tr, zw), lambda i: (i, 0)),
            full((zw, hid)), full((1, hid)), full((hid, hid)), full((1, hid)),
            full((hid, hid)), full((1, hid)), full((1, hid)),
            pl.BlockSpec((1, hid, width), lambda i: (i // half_tiles, 0, 0)),
            full((1, width)),
        ],
        out_specs=(pl.BlockSpec((tr, width), lambda i: (i, 0)),
                   pl.BlockSpec((1, 1, width), lambda i: (i // half_tiles, 0, 0))),
        compiler_params=_params(("arbitrary",)),
        name="hyena_filters",
    )(z_ext, w1, b1, w2, b2, w3, b3, freq, w_out2, deltas2)


def _dft_consts(n1, n2):
    n = n1 * n2
    k = np.arange(n1)
    ang1 = 2.0 * np.pi * ((k[:, None] * k[None, :]) % n1) / n1
    c1, s1 = np.cos(ang1), np.sin(ang1)
    m = np.arange(n2)
    ang2 = 2.0 * np.pi * ((m[:, None] * m[None, :]) % n2) / n2
    c2, s2 = np.cos(ang2), np.sin(ang2)
    angt = 2.0 * np.pi * (k[:, None] * m[None, :]) / n
    as_bf = lambda a: jnp.asarray(a, F32).astype(BF16)
    return dict(
        fwd1=as_bf(np.concatenate([c1, -s1], axis=0)),
        inv1c=as_bf(c1[: n1 // 2] / n), inv1s=as_bf(-s1[: n1 // 2] / n),
        fwd2=as_bf(np.block([[c2, s2], [-s2, c2]])),
        inv2=as_bf(np.block([[c2, -s2], [s2, c2]])),
        tw_c=jnp.asarray(np.cos(angt), F32), tw_s=jnp.asarray(np.sin(angt), F32),
    )


def _outer_fwd_body(x_ref, f_ref, *rest, scaled):
    if scaled:
        s_ref, ar_ref, ai_ref = rest
        half = x_ref.shape[0] // 2
        inv = 1.0 / (s_ref[...] + EPS)
        x = jnp.concatenate([x_ref[:half, :] * inv[0:1], x_ref[half:, :] * inv[1:2]], axis=0)
    else:
        ar_ref, ai_ref = rest
        x = x_ref[...]
    n1 = ar_ref.shape[0]
    a = _dot(f_ref[...], x.astype(BF16))
    ar_ref[...] = a[:n1]
    ai_ref[...] = a[n1:]


def _outer_fwd(x2d, fwd1, sums2d=None, *, tc=2048):
    rows, cols = x2d.shape
    n1 = fwd1.shape[1]
    tc = min(tc, cols)
    scaled = sums2d is not None
    args = [x2d, fwd1[:, :rows]]
    in_specs = [pl.BlockSpec((rows, tc), lambda j: (0, j)), pl.BlockSpec((2 * n1, rows), lambda j: (0, 0))]
    if scaled:
        args.append(sums2d)
        in_specs.append(pl.BlockSpec((2, tc), lambda j: (0, j)))
    return pl.pallas_call(
        functools.partial(_outer_fwd_body, scaled=scaled),
        out_shape=(jax.ShapeDtypeStruct((n1, cols), F32),) * 2,
        grid=(cols // tc,),
        in_specs=in_specs,
        out_specs=(pl.BlockSpec((n1, tc), lambda j: (0, j)),) * 2,
        compiler_params=_params(("parallel",)),
        name="dft_outer_fwd",
    )(*args)


def _lane_tile(t, width):
    return jnp.concatenate([t] * (width // LANES), axis=1) if width > LANES else t


def _twiddled(ar, ai, tc, ts):
    return ar * tc + ai * ts, ai * tc - ar * ts


def _inner_fwd(ar, ai, tc, ts, f2):
    n2 = ar.shape[0]
    br, bi = _twiddled(ar, ai, tc, ts)
    x = _dot(f2, jnp.concatenate([br, bi], axis=0).astype(BF16))
    return x[:n2], x[n2:]


def _mid_spec_body(ar_ref, ai_ref, tc_ref, ts_ref, f2_ref, xr_ref, xi_ref):
    width = ar_ref.shape[2]
    for s in range(ar_ref.shape[0]):
        tc, ts = _lane_tile(tc_ref[s], width), _lane_tile(ts_ref[s], width)
        xr_ref[s], xi_ref[s] = _inner_fwd(ar_ref[s], ai_ref[s], tc, ts, f2_ref[...])


def _mid_conv_body(ar_ref, ai_ref, kr_ref, ki_ref, tc_ref, ts_ref, f2_ref, g2_ref, dr_ref, di_ref):
    width = ar_ref.shape[2]
    n2 = ar_ref.shape[1]
    for s in range(ar_ref.shape[0]):
        tc, ts = _lane_tile(tc_ref[s], width), _lane_tile(ts_ref[s], width)
        xr, xi = _inner_fwd(ar_ref[s], ai_ref[s], tc, ts, f2_ref[...])
        kr, ki = kr_ref[s], ki_ref[s]
        yr = xr * kr - xi * ki
        yi = xr * ki + xi * kr
        c = _dot(g2_ref[...], jnp.concatenate([yr, yi], axis=0).astype(BF16))
        cr, ci = c[:n2], c[n2:]
        dr_ref[s] = cr * tc - ci * ts
        di_ref[s] = ci * tc + cr * ts


def _mid(ar, ai, tw_c, tw_s, fwd2, inv2=None, kr=None, ki=None, order=0, *, kb=4):
    n1, n2, width = ar.shape
    conv = kr is not None
    slab = pl.BlockSpec((kb, n2, width), lambda i: (i, 0, 0))
    tw = pl.BlockSpec((kb, n2, LANES), lambda i: (i, 0, 0))
    mat = pl.BlockSpec((2 * n2, 2 * n2), lambda i: (0, 0))
    if conv:
        kslab = pl.BlockSpec((kb, n2, width), lambda i: (i, 0, order))
        args = (ar, ai, kr, ki, tw_c, tw_s, fwd2, inv2)
        in_specs = [slab, slab, kslab, kslab, tw, tw, mat, mat]
        body, name = _mid_conv_body, "hyena_mid_conv"
    else:
        args = (ar, ai, tw_c, tw_s, fwd2)
        in_specs = [slab, slab, tw, tw, mat]
        body, name = _mid_spec_body, "hyena_mid_spec"
    return pl.pallas_call(
        body,
        out_shape=(jax.ShapeDtypeStruct((n1, n2, width), F32),) * 2,
        grid=(n1 // kb,),
        in_specs=in_specs,
        out_specs=(slab, slab),
        compiler_params=_params(("parallel",)),
        name=name,
    )(*args)


def _outer_inv_body(dr_ref, di_ref, c_ref, s_ref, gate_ref, z_ref, skip_ref, o_ref):
    y = _dot(c_ref[...], dr_ref[...].astype(BF16)) + _dot(s_ref[...], di_ref[...].astype(BF16))
    z = z_ref[...]
    o_ref[...] = gate_ref[...] * (y + skip_ref[...] * z)


def _outer_inv(dr, di, inv1c, inv1s, gate2d, z2d, skip2d, *, tc=2048):
    n1, cols = dr.shape
    half = n1 // 2
    tc = min(tc, cols)
    col = lambda r: pl.BlockSpec((r, tc), lambda j: (0, j))
    mat = pl.BlockSpec((half, n1), lambda j: (0, 0))
    return pl.pallas_call(
        _outer_inv_body,
        out_shape=jax.ShapeDtypeStruct((half, cols), F32),
        grid=(cols // tc,),
        in_specs=[col(n1), col(n1), mat, mat, col(half), col(half), col(1)],
        out_specs=col(half),
        compiler_params=_params(("parallel",)),
        name="dft_outer_inv",
    )(dr, di, inv1c, inv1s, gate2d, z2d, skip2d)


def _log_sigmoid(x):
    return jnp.minimum(x, 0.0) - jnp.log1p(jnp.exp(-jnp.abs(x)))


def _gla_direction(q_ref, k_ref, v_ref, lr_ref, w, b, o_ref, st_ref, *, reverse, scale):
    tb, dk = q_ref.shape
    chunk = GLA_CHUNK
    row = lax.broadcasted_iota(jnp.int32, (chunk, chunk), 0)
    col = lax.broadcasted_iota(jnp.int32, (chunk, chunk), 1)
    mask = (col >= row) if reverse else (col <= row)
    cum = mask.astype(F32)
    chunks = range(tb // chunk)
    for c in (reversed(chunks) if reverse else chunks):
        sl = slice(c * chunk, (c + 1) * chunk)
        g = _log_sigmoid(_dot_hi(lr_ref[sl, :], w) + b) / GLA_TAU
        bc = _dot_hi(cum, g)
        bl = bc[0:1] if reverse else bc[chunk - 1:chunk]
        k = k_ref[sl, :]
        v = v_ref[sl, :].astype(BF16)
        qe = (q_ref[sl, :] * scale * jnp.exp(bc)).astype(BF16)
        ke = (k * jnp.exp(-bc)).astype(BF16)
        kd = (k * jnp.exp(bl - bc)).astype(BF16)
        a = lax.dot_general(qe, ke, (((1,), (1,)), ((), ())), preferred_element_type=F32)
        a = jnp.where(mask, a, 0.0).astype(BF16)
        st = st_ref[...]
        inter = lax.dot_general(qe, st.astype(BF16), (((1,), (1,)), ((), ())), preferred_element_type=F32)
        o_ref[sl, :] = _dot(a, v) + inter
        upd = lax.dot_general(v, kd, (((0,), (0,)), ((), ())), preferred_element_type=F32)
        st_ref[...] = st * jnp.exp(bl) + upd


def _gla_body(qf, kf, vf, lrf, qb, kb, vb, lrb, w_ref, b_ref, of_ref, ob_ref, sf_ref, sb_ref, *, scale):
    @pl.when(pl.program_id(1) == 0)
    def _():
        sf_ref[...] = jnp.zeros_like(sf_ref)
        sb_ref[...] = jnp.zeros_like(sb_ref)

    _gla_direction(qf, kf, vf, lrf, w_ref[0, 0], b_ref[0, 0], of_ref, sf_ref, reverse=False, scale=scale)
    _gla_direction(qb, kb, vb, lrb, w_ref[0, 1], b_ref[0, 1], ob_ref, sb_ref, reverse=True, scale=scale)


def _gla(p, lr, w_pad, b_pad, *, q_off, k_off, v_off, dk, dv, tb=256):
    L = p.shape[0]
    nb = L // tb
    heads = w_pad.shape[0]
    fwd = lambda width, off: pl.BlockSpec((tb, width), lambda h, i: (i, off // width + h))
    bwd = lambda width, off: pl.BlockSpec((tb, width), lambda h, i: (nb - 1 - i, off // width + h))
    return pl.pallas_call(
        functools.partial(_gla_body, scale=dk ** -0.5),
        out_shape=(jax.ShapeDtypeStruct((L, heads * dv), F32),) * 2,
        grid=(heads, nb),
        in_specs=[
            fwd(dk, q_off), fwd(dk, k_off), fwd(dv, v_off), pl.BlockSpec((tb, LANES), lambda h, i: (i, 0)),
            bwd(dk, q_off), bwd(dk, k_off), bwd(dv, v_off), pl.BlockSpec((tb, LANES), lambda h, i: (nb - 1 - i, 0)),
            pl.BlockSpec((1, 2, LANES, dk), lambda h, i: (h, 0, 0, 0)),
            pl.BlockSpec((1, 2, 1, dk), lambda h, i: (h, 0, 0, 0)),
        ],
        out_specs=(pl.BlockSpec((tb, dv), lambda h, i: (i, h)),
                   pl.BlockSpec((tb, dv), lambda h, i: (nb - 1 - i, h))),
        scratch_shapes=[pltpu.VMEM((dv, dk), F32), pltpu.VMEM((dv, dk), F32)],
        compiler_params=_params(("parallel", "arbitrary")),
        name="gla",
    )(p, p, p, lr, p, p, p, lr, w_pad, b_pad)


def _merge_body(h_ref, g_ref, a_ref, z_ref, of_ref, ob_ref, r_ref, gn_ref,
                wga_ref, wgb_ref, wgc_ref, bga_ref, bgb_ref, bgc_ref,
                wa_ref, wb_ref, wc_ref, wo_ref, o_ref, u_ref, zb_ref, c_ref, acc_ref, *, dv):
    j = pl.program_id(1)

    @pl.when(j == 0)
    def _():
        u_ref[...] = _rms(h_ref[...], g_ref[...]).astype(BF16)
        zb_ref[...] = z_ref[...].astype(BF16)
        o = of_ref[...] + ob_ref[...]
        r = r_ref[...]
        gate = r * jax.nn.sigmoid(r)
        for hd in range(o.shape[1] // dv):
            sl = slice(hd * dv, (hd + 1) * dv)
            c_ref[:, sl] = (_rms(o[:, sl], gn_ref[...]) * gate[:, sl]).astype(BF16)
        acc_ref[...] = jnp.zeros_like(acc_ref)

    u = u_ref[...]
    m = jax.nn.sigmoid(_dot(u, wga_ref[...]) + bga_ref[...]) * _dot(a_ref[...], wa_ref[...])
    m += jax.nn.sigmoid(_dot(u, wgb_ref[...]) + bgb_ref[...]) * _dot(zb_ref[...], wb_ref[...])
    m += jax.nn.sigmoid(_dot(u, wgc_ref[...]) + bgc_ref[...]) * _dot(c_ref[...], wc_ref[...])
    acc_ref[...] += _dot(m.astype(BF16), wo_ref[...])

    @pl.when(j == pl.num_programs(1) - 1)
    def _():
        o_ref[...] = h_ref[...] + acc_ref[...]


def _merge(h, norm_g, a_pre, z, o_f, o_b, p, gla_norm, w_gate, b_gate, w_a, w_b, w_c, w_o,
           *, r_off, tm=512, tn=256):
    L, D = h.shape
    wa, wb, wc = w_a.shape[0], w_b.shape[0], w_c.shape[0]
    dv = gla_norm.shape[0]
    nj = D // tn
    row = lambda width, blk=0: pl.BlockSpec((tm, width), lambda i, j: (i, blk))
    gate_w = lambda k: pl.BlockSpec((D, tn), lambda i, j: (0, j + k * nj))
    gate_b = lambda k: pl.BlockSpec((1, tn), lambda i, j: (0, j + k * nj))
    br_w = lambda width: pl.BlockSpec((width, tn), lambda i, j: (0, j))
    return pl.pallas_call(
        functools.partial(_merge_body, dv=dv),
        out_shape=jax.ShapeDtypeStruct((L, D), F32),
        grid=(L // tm, nj),
        in_specs=[
            row(D), pl.BlockSpec((1, D), lambda i, j: (0, 0)),
            row(wa), row(wb), row(wc), row(wc), row(wc, r_off // wc),
            pl.BlockSpec((1, dv), lambda i, j: (0, 0)),
            gate_w(0), gate_w(1), gate_w(2), gate_b(0), gate_b(1), gate_b(2),
            br_w(wa), br_w(wb), br_w(wc),
            pl.BlockSpec((tn, D), lambda i, j: (j, 0)),
        ],
        out_specs=row(D),
        scratch_shapes=[pltpu.VMEM((tm, D), BF16), pltpu.VMEM((tm, wb), BF16),
                        pltpu.VMEM((tm, wc), BF16), pltpu.VMEM((tm, D), F32)],
        compiler_params=_params(("parallel", "arbitrary")),
        name="merge",
    )(h, norm_g.reshape(1, D), a_pre, z, o_f, o_b, p, gla_norm.reshape(1, dv),
      w_gate, w_gate, w_gate, b_gate, b_gate, b_gate, w_a, w_b, w_c, w_o)


def _position_features(L):
    pos = jnp.arange(L, dtype=F32)
    t = pos / max(L - 1, 1)
    bands = jnp.linspace(1e-4, HY_BANDS - 1, HY_BANDS, dtype=F32)
    ang = (2.0 * math.pi / L) * pos[:, None] * bands[None, :]
    z = jnp.concatenate([t[:, None], jnp.cos(ang), -jnp.sin(ang)], axis=-1)
    z_ext = jnp.concatenate([z, z[:1], z[:0:-1]], axis=0)
    return jnp.pad(z_ext, ((0, 0), (0, LANES - z.shape[1])))


def _hyena(z0, g1, g2, skip, filt, consts, tw_c, tw_s):
    L, C = z0.shape
    n1 = DFT_N1
    n2 = 2 * L // n1
    k_time, sums = filt
    kw = k_time.shape[1]
    sums2d = jnp.tile(sums.reshape(2, kw), (1, n2))
    fr, fi = _outer_fwd(k_time.reshape(n1, n2 * kw), consts["fwd1"], sums2d)
    kr, ki = _mid(fr.reshape(n1, n2, kw), fi.reshape(n1, n2, kw), tw_c, tw_s, consts["fwd2"])
    z = z0.reshape(n1 // 2, n2 * C)
    for order, gate in enumerate((g1, g2)):
        ar, ai = _outer_fwd(z, consts["fwd1"])
        dr, di = _mid(ar.reshape(n1, n2, C), ai.reshape(n1, n2, C), tw_c, tw_s, consts["fwd2"],
                      consts["inv2"], kr, ki, order)
        skip2d = jnp.tile(skip[order].reshape(1, C), (1, n2))
        z = _outer_inv(dr.reshape(n1, n2 * C), di.reshape(n1, n2 * C), consts["inv1c"], consts["inv1s"],
                       gate.reshape(n1 // 2, n2 * C), z, skip2d)
    return z.reshape(L, C)


def kernel(x, ffn1_norm, ffn1_w_gu, ffn1_w_down, mix_norm, w_in, b_in, conv_a, conv_b, hf_w1, hf_b1, hf_w2, hf_b2, hf_w3, hf_b3, hf_w_out, hf_freq, hy_skip, gk_w2, gk_b, gla_norm, w_br_a, w_br_b, w_br_c, w_o, ffn2_norm, ffn2_w_gu, ffn2_w_down, final_norm):
    bsz, L, D = x.shape
    depth = ffn1_norm.shape[0]
    wa = conv_a.shape[2]
    wb = conv_b.shape[3]
    wck = gk_w2.shape[3]
    dv = gla_norm.shape[1]
    wcv = GLA_HEADS * dv
    dk = wck // GLA_HEADS
    hid = hf_w2.shape[1]
    n_main = 3 * wa + 3 * wb + 2 * wck + 2 * wcv
    q_off = 3 * wa + 3 * wb
    k_off, v_off, r_off = q_off + wck, q_off + 2 * wck, q_off + 2 * wck + wcv
    lr_off = n_main
    gate_off = n_main + 2 * GLA_RANK
    n1 = DFT_N1
    n2 = 2 * L // n1

    consts = _dft_consts(n1, n2)
    tw_c = jnp.broadcast_to(consts["tw_c"][:, :, None], (n1, n2, LANES))
    tw_s = jnp.broadcast_to(consts["tw_s"][:, :, None], (n1, n2, LANES))
    z_ext = _position_features(L)
    deltas = jnp.linspace(math.log(HY_TARGET) / HY_SLOW_DECAY, math.log(HY_TARGET) / HY_FAST_DECAY, wb, dtype=F32)
    deltas2 = jnp.tile(deltas.reshape(1, wb), (1, HY_ORDER))

    outs = []
    for bi in range(bsz):
        h = x[bi]
        for l in range(depth):
            h = _ffn(h, ffn1_norm[l], ffn1_w_gu[l].astype(BF16), ffn1_w_down[l].astype(BF16),
                     final_norm, final=False)

            w_lr = jnp.pad(w_in[l][:, lr_off:gate_off], ((0, 0), (0, LANES - 2 * GLA_RANK))).astype(BF16)
            b_lr = jnp.pad(b_in[l][lr_off:gate_off], (0, LANES - 2 * GLA_RANK)).reshape(1, LANES)
            p, lr = _proj(h, mix_norm[l], w_in[l][:, :n_main].astype(BF16), b_in[l][:n_main].reshape(1, n_main),
                          w_lr, b_lr)

            a_pre, z0, g1, g2 = _prep(p, conv_a[l], conv_b[l], wa=wa)

            w_out2 = hf_w_out[l].reshape(hid, HY_ORDER, 2, wb).transpose(2, 0, 1, 3).reshape(2, hid, HY_ORDER * wb)
            w1 = jnp.pad(hf_w1[l], ((0, LANES - hf_w1.shape[1]), (0, 0)))
            filt = _filters(z_ext, w1, hf_b1[l].reshape(1, hid), hf_w2[l], hf_b2[l].reshape(1, hid),
                            hf_w3[l], hf_b3[l].reshape(1, hid), hf_freq[l].reshape(1, hid), w_out2, deltas2)
            zb = _hyena(z0, g1, g2, hy_skip[l], filt, consts, tw_c, tw_s)

            gw = gk_w2[l].reshape(2, GLA_RANK, GLA_HEADS, dk).transpose(2, 0, 1, 3)
            w_pad = jnp.zeros((GLA_HEADS, 2, LANES, dk), F32)
            w_pad = w_pad.at[:, 0, :GLA_RANK].set(gw[:, 0]).at[:, 1, GLA_RANK:2 * GLA_RANK].set(gw[:, 1])
            b_pad = gk_b[l].reshape(2, GLA_HEADS, 1, dk).transpose(1, 0, 2, 3)
            o_f, o_b = _gla(p, lr, w_pad, b_pad, q_off=q_off, k_off=k_off, v_off=v_off, dk=dk, dv=dv)

            h = _merge(h, mix_norm[l], a_pre, zb, o_f, o_b, p, gla_norm[l],
                       w_in[l][:, gate_off:].astype(BF16), b_in[l][gate_off:].reshape(1, 3 * D),
                       w_br_a[l].astype(BF16), w_br_b[l].astype(BF16), w_br_c[l].astype(BF16),
                       w_o[l].astype(BF16), r_off=r_off)

            h = _ffn(h, ffn2_norm[l], ffn2_w_gu[l].astype(BF16), ffn2_w_down[l].astype(BF16),
                     final_norm, final=(l == depth - 1))
        outs.append(h)
    return jnp.stack(outs, axis=0)
```

```python
import functools
import math

import numpy as np
import jax
import jax.numpy as jnp
from jax import lax
from jax.experimental import pallas as pl
from jax.experimental.pallas import tpu as pltpu

F32 = jnp.float32
BF16 = jnp.bfloat16
HIGHEST = lax.Precision.HIGHEST

EPS = 1e-5
GLA_HEADS = 4
GLA_RANK = 16
GLA_TAU = 16.0
GLA_CHUNK = 64
HY_ORDER = 2
HY_BANDS = 16
HY_FAST_DECAY = 0.3
HY_SLOW_DECAY = 1.5
HY_TARGET = 1e-2

LANES = 128
SUBLANES = 8
DFT_N1 = 256
VMEM_LIMIT = 56 * 1024 * 1024


def _params(sem):
    return pltpu.CompilerParams(dimension_semantics=sem, vmem_limit_bytes=VMEM_LIMIT)


def _rms(x, g):
    return x * lax.rsqrt(jnp.mean(x * x, axis=-1, keepdims=True) + EPS) * g


def _dot(a, b):
    return jnp.dot(a, b, preferred_element_type=F32)


def _dot_hi(a, b):
    return jnp.dot(a, b, preferred_element_type=F32, precision=HIGHEST)


def _cast_body(x_ref, o_ref):
    o_ref[...] = x_ref[...].astype(o_ref.dtype)


def _to_bf16(w, layer, cols=None, *, block_bytes=4 << 20):
    _, rows, width = w.shape
    cols = width if cols is None else cols
    tr = rows
    while tr * cols * 4 > block_bytes and tr % 32 == 0:
        tr //= 2
    return pl.pallas_call(
        _cast_body,
        out_shape=jax.ShapeDtypeStruct((rows, cols), BF16),
        grid=(rows // tr,),
        in_specs=[pl.BlockSpec((None, tr, cols), lambda i: (layer, i, 0))],
        out_specs=pl.BlockSpec((tr, cols), lambda i: (i, 0)),
        compiler_params=_params(("parallel",)),
        name="to_bf16",
    )(w)


def _ffn_body(x_ref, g_ref, wg_ref, wu_ref, wd_ref, fn_ref, o_ref, xn_ref, acc_ref, *, final):
    j = pl.program_id(1)

    @pl.when(j == 0)
    def _():
        xn_ref[...] = _rms(x_ref[...], g_ref[...]).astype(BF16)
        acc_ref[...] = jnp.zeros_like(acc_ref)

    xn = xn_ref[...]
    g = _dot(xn, wg_ref[...])
    u = _dot(xn, wu_ref[...])
    h = (g * jax.nn.sigmoid(g) * u).astype(BF16)
    acc_ref[...] += _dot(h, wd_ref[...])

    @pl.when(j == pl.num_programs(1) - 1)
    def _():
        y = x_ref[...] + 0.5 * acc_ref[...]
        if final:
            y = _rms(y, fn_ref[...])
        o_ref[...] = y


def _ffn(x, norm_g, w_gu, w_down, final_g, *, final, tm=512, tf=512):
    L, D = x.shape
    F = w_down.shape[0]
    nf = F // tf
    return pl.pallas_call(
        functools.partial(_ffn_body, final=final),
        out_shape=jax.ShapeDtypeStruct((L, D), F32),
        grid=(L // tm, nf),
        in_specs=[
            pl.BlockSpec((tm, D), lambda i, j: (i, 0)),
            pl.BlockSpec((1, D), lambda i, j: (0, 0)),
            pl.BlockSpec((D, tf), lambda i, j: (0, j)),
            pl.BlockSpec((D, tf), lambda i, j: (0, j + nf)),
            pl.BlockSpec((tf, D), lambda i, j: (j, 0)),
            pl.BlockSpec((1, D), lambda i, j: (0, 0)),
        ],
        out_specs=pl.BlockSpec((tm, D), lambda i, j: (i, 0)),
        scratch_shapes=[pltpu.VMEM((tm, D), BF16), pltpu.VMEM((tm, D), F32)],
        compiler_params=_params(("parallel", "arbitrary")),
        name="ffn",
    )(x, norm_g.reshape(1, D), w_gu, w_gu, w_down, final_g.reshape(1, D))


def _proj_body(h_ref, g_ref, w_ref, b_ref, wlr_ref, blr_ref, p_ref, lr_ref, xn_ref):
    @pl.when(pl.program_id(1) == 0)
    def _():
        xn = _rms(h_ref[...], g_ref[...]).astype(BF16)
        xn_ref[...] = xn
        lr_ref[...] = _dot(xn, wlr_ref[...]) + blr_ref[...]

    p_ref[...] = _dot(xn_ref[...], w_ref[...]) + b_ref[...]


def _proj(h, norm_g, w, b, w_lr, b_lr, *, tm=512, tn=512):
    L, D = h.shape
    n_out = w.shape[1]
    return pl.pallas_call(
        _proj_body,
        out_shape=(jax.ShapeDtypeStruct((L, n_out), F32), jax.ShapeDtypeStruct((L, LANES), F32)),
        grid=(L // tm, n_out // tn),
        in_specs=[
            pl.BlockSpec((tm, D), lambda i, j: (i, 0)),
            pl.BlockSpec((1, D), lambda i, j: (0, 0)),
            pl.BlockSpec((D, tn), lambda i, j: (0, j)),
            pl.BlockSpec((1, tn), lambda i, j: (0, j)),
            pl.BlockSpec((D, LANES), lambda i, j: (0, 0)),
            pl.BlockSpec((1, LANES), lambda i, j: (0, 0)),
        ],
        out_specs=(pl.BlockSpec((tm, tn), lambda i, j: (i, j)),
                   pl.BlockSpec((tm, LANES), lambda i, j: (i, 0))),
        scratch_shapes=[pltpu.VMEM((tm, D), BF16)],
        compiler_params=_params(("parallel", "arbitrary")),
        name="proj",
    )(h, norm_g.reshape(1, D), w, b, w_lr, b_lr)


def _conv3(x, x_before, x_after, w):
    tm = x.shape[0]
    row = lax.broadcasted_iota(jnp.int32, x.shape, 0)
    down = jnp.where(row == 0, x_before, pltpu.roll(x, 1, axis=0))
    up = jnp.where(row == tm - 1, x_after, pltpu.roll(x, tm - 1, axis=0))
    return down * w[0:1] + x * w[1:2] + up * w[2:3]


def _prep_body(m_ref, pv_ref, nx_ref, ca_w_ref, cb_w_ref, a_ref, z_ref, g1_ref, g2_ref, *, wa):
    i = pl.program_id(0)
    has_prev = (i > 0).astype(F32)
    has_next = (i < pl.num_programs(0) - 1).astype(F32)
    last = SUBLANES - 1

    def sec(k):
        sl = slice(k * wa, (k + 1) * wa)
        return (m_ref[:, sl], pv_ref[last:last + 1, sl] * has_prev, nx_ref[0:1, sl] * has_next)

    xa, xa_p, xa_n = sec(0)
    ba, _, _ = sec(1)
    ca, ca_p, ca_n = sec(2)
    a_ref[...] = (ba * _conv3(ca * xa, ca_p * xa_p, ca_n * xa_n, ca_w_ref[...])).astype(BF16)
    for k, o_ref in enumerate((z_ref, g1_ref, g2_ref)):
        x, x_p, x_n = sec(3 + k)
        o_ref[...] = _conv3(x, x_p, x_n, cb_w_ref[k])


def _prep(p, conv_a, conv_b, *, wa, tm=256):
    L = p.shape[0]
    width = 6 * wa
    rb = tm // SUBLANES
    nrb = L // SUBLANES
    return pl.pallas_call(
        functools.partial(_prep_body, wa=wa),
        out_shape=(jax.ShapeDtypeStruct((L, wa), BF16),) + (jax.ShapeDtypeStruct((L, wa), F32),) * 3,
        grid=(L // tm,),
        in_specs=[
            pl.BlockSpec((tm, width), lambda i: (i, 0)),
            pl.BlockSpec((SUBLANES, width), lambda i: (jnp.maximum(i * rb - 1, 0), 0)),
            pl.BlockSpec((SUBLANES, width), lambda i: (jnp.minimum((i + 1) * rb, nrb - 1), 0)),
            pl.BlockSpec((3, wa), lambda i: (0, 0)),
            pl.BlockSpec((3, 3, wa), lambda i: (0, 0, 0)),
        ],
        out_specs=(pl.BlockSpec((tm, wa), lambda i: (i, 0)),) * 4,
        compiler_params=_params(("parallel",)),
        name="prep",
    )(p, p, p, conv_a, conv_b)


def _filter_body(z_ref, w1_ref, b1_ref, w2_ref, b2_ref, w3_ref, b3_ref, fr_ref, wo_ref, dl_ref,
                 k_ref, s_ref, *, half_tiles, tr):
    i = pl.program_id(0)
    z = z_ref[...]
    fr = fr_ref[...]
    h = jnp.sin(fr * (_dot_hi(z, w1_ref[...]) + b1_ref[...]))
    h = jnp.sin(fr * (_dot_hi(h, w2_ref[...]) + b2_ref[...]))
    h = jnp.sin(fr * (_dot_hi(h, w3_ref[...]) + b3_ref[...]))
    t = z[:, 0:1]
    decay = jnp.exp(-t * jnp.abs(dl_ref[...]))
    hd = _dot_hi(h, wo_ref[0]) * decay

    @pl.when(i % half_tiles == 0)
    def _():
        s_ref[...] = jnp.zeros_like(s_ref)

    s_ref[0] += jnp.sum(jnp.abs(hd), axis=0, keepdims=True)
    row = lax.broadcasted_iota(jnp.int32, hd.shape, 0)
    zero_row = jnp.where(i == half_tiles, 0, -1)
    k_ref[...] = jnp.where(row == zero_row, 0.0, hd)


def _filters(z_ext, w1, b1, w2, b2, w3, b3, freq, w_out2, deltas2, *, tr=512):
    n_rows, zw = z_ext.shape
    hid = w2.shape[0]
    width = w_out2.shape[2]
    half_tiles = n_rows // (2 * tr)
    full = lambda shape: pl.BlockSpec(shape, lambda i: (0,) * len(shape))
    return pl.pallas_call(
        functools.partial(_filter_body, half_tiles=half_tiles, tr=tr),
        out_shape=(jax.ShapeDtypeStruct((n_rows, width), F32), jax.ShapeDtypeStruct((2, 1, width), F32)),
        grid=(n_rows // tr,),
        in_specs=[
            pl.BlockSpec((tr, zw), lambda i: (i, 0)),
            full((zw, hid)), full((1, hid)), full((hid, hid)), full((1, hid)),
            full((hid, hid)), full((1, hid)), full((1, hid)),
            pl.BlockSpec((1, hid, width), lambda i: (i // half_tiles, 0, 0)),
            full((1, width)),
        ],
        out_specs=(pl.BlockSpec((tr, width), lambda i: (i, 0)),
                   pl.BlockSpec((1, 1, width), lambda i: (i // half_tiles, 0, 0))),
        compiler_params=_params(("arbitrary",)),
        name="hyena_filters",
    )(z_ext, w1, b1, w2, b2, w3, b3, freq, w_out2, deltas2)


def _dft_consts(n1, n2):
    n = n1 * n2
    k = np.arange(n1)
    ang1 = 2.0 * np.pi * ((k[:, None] * k[None, :]) % n1) / n1
    c1, s1 = np.cos(ang1), np.sin(ang1)
    m = np.arange(n2)
    ang2 = 2.0 * np.pi * ((m[:, None] * m[None, :]) % n2) / n2
    c2, s2 = np.cos(ang2), np.sin(ang2)
    angt = 2.0 * np.pi * (k[:, None] * m[None, :]) / n
    as_bf = lambda a: jnp.asarray(a, F32).astype(BF16)
    return dict(
        fwd1=as_bf(np.concatenate([c1, -s1], axis=0)),
        inv1c=as_bf(c1[: n1 // 2] / n), inv1s=as_bf(-s1[: n1 // 2] / n),
        fwd2=as_bf(np.block([[c2, s2], [-s2, c2]])),
        inv2=as_bf(np.block([[c2, -s2], [s2, c2]])),
        tw_c=jnp.asarray(np.cos(angt), F32), tw_s=jnp.asarray(np.sin(angt), F32),
    )


def _outer_fwd_body(x_ref, f_ref, *rest, scaled):
    if scaled:
        s_ref, ar_ref, ai_ref = rest
        inv = 1.0 / (s_ref[...] + EPS)
    else:
        ar_ref, ai_ref = rest
    rows, t2, _ = x_ref.shape
    n1 = ar_ref.shape[0]
    f = f_ref[...]
    for s in range(t2):
        x = x_ref[:, s, :]
        if scaled:
            half = rows // 2
            x = jnp.concatenate([x[:half] * inv[0], x[half:] * inv[1]], axis=0)
        a = _dot(f, x.astype(BF16))
        ar_ref[:, s, :] = a[:n1]
        ai_ref[:, s, :] = a[n1:]


def _outer_fwd(x3, fwd1, sums=None, *, t2=SUBLANES, tw=512):
    rows, n2, width = x3.shape
    n1 = fwd1.shape[1]
    tw = min(tw, width)
    scaled = sums is not None
    blk = lambda r: pl.BlockSpec((r, t2, tw), lambda j, c: (0, j, c))
    args = [x3, fwd1[:, :rows]]
    in_specs = [blk(rows), pl.BlockSpec((2 * n1, rows), lambda j, c: (0, 0))]
    if scaled:
        args.append(sums)
        in_specs.append(pl.BlockSpec((2, 1, tw), lambda j, c: (0, 0, c)))
    return pl.pallas_call(
        functools.partial(_outer_fwd_body, scaled=scaled),
        out_shape=(jax.ShapeDtypeStruct((n1, n2, width), F32),) * 2,
        grid=(n2 // t2, width // tw),
        in_specs=in_specs,
        out_specs=(blk(n1),) * 2,
        compiler_params=_params(("parallel", "parallel")),
        name="dft_outer_fwd",
    )(*args)


def _lane_tile(t, width):
    return jnp.concatenate([t] * (width // LANES), axis=1) if width > LANES else t


def _twiddled(ar, ai, tc, ts):
    return ar * tc + ai * ts, ai * tc - ar * ts


def _inner_fwd(ar, ai, tc, ts, f2):
    n2 = ar.shape[0]
    br, bi = _twiddled(ar, ai, tc, ts)
    x = _dot(f2, jnp.concatenate([br, bi], axis=0).astype(BF16))
    return x[:n2], x[n2:]


def _mid_spec_body(ar_ref, ai_ref, tc_ref, ts_ref, f2_ref, xr_ref, xi_ref):
    width = ar_ref.shape[2]
    for s in range(ar_ref.shape[0]):
        tc, ts = _lane_tile(tc_ref[s], width), _lane_tile(ts_ref[s], width)
        xr_ref[s], xi_ref[s] = _inner_fwd(ar_ref[s], ai_ref[s], tc, ts, f2_ref[...])


def _mid_conv_body(ar_ref, ai_ref, kr_ref, ki_ref, tc_ref, ts_ref, f2_ref, g2_ref, dr_ref, di_ref):
    width = ar_ref.shape[2]
    n2 = ar_ref.shape[1]
    for s in range(ar_ref.shape[0]):
        tc, ts = _lane_tile(tc_ref[s], width), _lane_tile(ts_ref[s], width)
        xr, xi = _inner_fwd(ar_ref[s], ai_ref[s], tc, ts, f2_ref[...])
        kr, ki = kr_ref[s], ki_ref[s]
        yr = xr * kr - xi * ki
        yi = xr * ki + xi * kr
        c = _dot(g2_ref[...], jnp.concatenate([yr, yi], axis=0).astype(BF16))
        cr, ci = c[:n2], c[n2:]
        dr_ref[s] = cr * tc - ci * ts
        di_ref[s] = ci * tc + cr * ts


def _mid(ar, ai, tw_c, tw_s, fwd2, inv2=None, kr=None, ki=None, order=0, *, kb=4):
    n1, n2, width = ar.shape
    conv = kr is not None
    slab = pl.BlockSpec((kb, n2, width), lambda i: (i, 0, 0))
    tw = pl.BlockSpec((kb, n2, LANES), lambda i: (i, 0, 0))
    mat = pl.BlockSpec((2 * n2, 2 * n2), lambda i: (0, 0))
    if conv:
        kslab = pl.BlockSpec((kb, n2, width), lambda i: (i, 0, order))
        args = (ar, ai, kr, ki, tw_c, tw_s, fwd2, inv2)
        in_specs = [slab, slab, kslab, kslab, tw, tw, mat, mat]
        body, name = _mid_conv_body, "hyena_mid_conv"
    else:
        args = (ar, ai, tw_c, tw_s, fwd2)
        in_specs = [slab, slab, tw, tw, mat]
        body, name = _mid_spec_body, "hyena_mid_spec"
    return pl.pallas_call(
        body,
        out_shape=(jax.ShapeDtypeStruct((n1, n2, width), F32),) * 2,
        grid=(n1 // kb,),
        in_specs=in_specs,
        out_specs=(slab, slab),
        compiler_params=_params(("parallel",)),
        name=name,
    )(*args)


def _outer_inv_body(dr_ref, di_ref, c_ref, s_ref, gate_ref, z_ref, skip_ref, o_ref):
    c, sm, skip = c_ref[...], s_ref[...], skip_ref[...]
    for s in range(dr_ref.shape[1]):
        y = _dot(c, dr_ref[:, s, :].astype(BF16)) + _dot(sm, di_ref[:, s, :].astype(BF16))
        o_ref[:, s, :] = gate_ref[:, s, :] * (y + skip * z_ref[:, s, :])


def _outer_inv(dr, di, inv1c, inv1s, gate3, z3, skip, *, t2=SUBLANES):
    n1, n2, width = dr.shape
    half = n1 // 2
    blk = lambda r: pl.BlockSpec((r, t2, width), lambda j: (0, j, 0))
    mat = pl.BlockSpec((half, n1), lambda j: (0, 0))
    return pl.pallas_call(
        _outer_inv_body,
        out_shape=jax.ShapeDtypeStruct((half, n2, width), F32),
        grid=(n2 // t2,),
        in_specs=[blk(n1), blk(n1), mat, mat, blk(half), blk(half), pl.BlockSpec((1, width), lambda j: (0, 0))],
        out_specs=blk(half),
        compiler_params=_params(("parallel",)),
        name="dft_outer_inv",
    )(dr, di, inv1c, inv1s, gate3, z3, skip)


def _log_sigmoid(x):
    return jnp.minimum(x, 0.0) - jnp.log1p(jnp.exp(-jnp.abs(x)))


_NT = (((1,), (1,)), ((), ()))
_TN = (((0,), (0,)), ((), ()))


def _gla_direction(q_ref, k_ref, v_ref, lr_ref, w, b, cum, o_ref, st_ref, *, reverse, scale):
    tb, dk = q_ref.shape
    sub = cum.shape[0]
    chunk = GLA_CHUNK
    cum_bf = cum.astype(BF16)
    qes, intras, upd, dec = [], [], [], []
    for s in range(tb // sub):
        rs = slice(s * sub, (s + 1) * sub)
        g = _log_sigmoid(_dot(lr_ref[rs, :].astype(BF16), w) + b) / GLA_TAU
        g_hi = g.astype(BF16)
        g_lo = (g - g_hi.astype(F32)).astype(BF16)
        bc = _dot(cum_bf, g_hi) + _dot(cum_bf, g_lo)
        k = k_ref[rs, :]
        v = v_ref[rs, :].astype(BF16)
        qe = (q_ref[rs, :] * scale * jnp.exp(bc)).astype(BF16)
        ke = (k * jnp.exp(-bc)).astype(BF16)
        a = lax.dot_general(qe, ke, _NT, preferred_element_type=F32)
        intra = _dot(jnp.where(cum > 0.5, a, 0.0).astype(BF16), v)
        for c in range(sub // chunk):
            sl = slice(c * chunk, (c + 1) * chunk)
            bl = bc[c * chunk:c * chunk + 1] if reverse else bc[(c + 1) * chunk - 1:(c + 1) * chunk]
            kd = (k[sl] * jnp.exp(bl - bc[sl])).astype(BF16)
            upd.append(lax.dot_general(v[sl], kd, _TN, preferred_element_type=F32))
            dec.append(jnp.exp(bl))
            qes.append(qe[sl])
            intras.append(intra[sl])
    st = st_ref[...]
    chunks = range(tb // chunk)
    for c in (reversed(chunks) if reverse else chunks):
        o_ref[c * chunk:(c + 1) * chunk, :] = intras[c] + lax.dot_general(
            qes[c], st.astype(BF16), _NT, preferred_element_type=F32)
        st = st * dec[c] + upd[c]
    st_ref[...] = st


def _gla_body(qf, kf, vf, lrf, qb, kb, vb, lrb, w_ref, b_ref, cum_ref, of_ref, ob_ref, sf_ref, sb_ref, *, scale):
    @pl.when(pl.program_id(1) == 0)
    def _():
        sf_ref[...] = jnp.zeros_like(sf_ref)
        sb_ref[...] = jnp.zeros_like(sb_ref)

    _gla_direction(qf, kf, vf, lrf, w_ref[0, 0], b_ref[0, 0], cum_ref[0], of_ref, sf_ref,
                   reverse=False, scale=scale)
    _gla_direction(qb, kb, vb, lrb, w_ref[0, 1], b_ref[0, 1], cum_ref[1], ob_ref, sb_ref,
                   reverse=True, scale=scale)


def _chunk_cumsum_matrices(sub):
    i = np.arange(sub)
    same = (i[:, None] // GLA_CHUNK) == (i[None, :] // GLA_CHUNK)
    fwd = same & (i[None, :] <= i[:, None])
    bwd = same & (i[None, :] >= i[:, None])
    return jnp.asarray(np.stack([fwd, bwd]), F32)


def _gla(p, lr, w_pad, b_pad, *, q_off, k_off, v_off, dk, dv, tb=512, sub=256):
    L = p.shape[0]
    tb = min(tb, L)
    nb = L // tb
    heads = w_pad.shape[0]
    fwd = lambda width, off: pl.BlockSpec((tb, width), lambda h, i: (i, off // width + h))
    bwd = lambda width, off: pl.BlockSpec((tb, width), lambda h, i: (nb - 1 - i, off // width + h))
    return pl.pallas_call(
        functools.partial(_gla_body, scale=dk ** -0.5),
        out_shape=(jax.ShapeDtypeStruct((L, heads * dv), F32),) * 2,
        grid=(heads, nb),
        in_specs=[
            fwd(dk, q_off), fwd(dk, k_off), fwd(dv, v_off), pl.BlockSpec((tb, LANES), lambda h, i: (i, 0)),
            bwd(dk, q_off), bwd(dk, k_off), bwd(dv, v_off), pl.BlockSpec((tb, LANES), lambda h, i: (nb - 1 - i, 0)),
            pl.BlockSpec((1, 2, LANES, dk), lambda h, i: (h, 0, 0, 0)),
            pl.BlockSpec((1, 2, 1, dk), lambda h, i: (h, 0, 0, 0)),
            pl.BlockSpec((2, sub, sub), lambda h, i: (0, 0, 0)),
        ],
        out_specs=(pl.BlockSpec((tb, dv), lambda h, i: (i, h)),
                   pl.BlockSpec((tb, dv), lambda h, i: (nb - 1 - i, h))),
        scratch_shapes=[pltpu.VMEM((dv, dk), F32), pltpu.VMEM((dv, dk), F32)],
        compiler_params=_params(("parallel", "arbitrary")),
        name="gla",
    )(p, p, p, lr, p, p, p, lr, w_pad.astype(BF16), b_pad, _chunk_cumsum_matrices(sub))


def _merge_body(h_ref, g_ref, a_ref, z_ref, of_ref, ob_ref, r_ref, gn_ref,
                wga_ref, wgb_ref, wgc_ref, bga_ref, bgb_ref, bgc_ref,
                wa_ref, wb_ref, wc_ref, wo_ref, o_ref, u_ref, zb_ref, c_ref, acc_ref, *, dv):
    j = pl.program_id(1)

    @pl.when(j == 0)
    def _():
        u_ref[...] = _rms(h_ref[...], g_ref[...]).astype(BF16)
        zb_ref[...] = z_ref[...].astype(BF16)
        o = of_ref[...] + ob_ref[...]
        r = r_ref[...]
        gate = r * jax.nn.sigmoid(r)
        for hd in range(o.shape[1] // dv):
            sl = slice(hd * dv, (hd + 1) * dv)
            c_ref[:, sl] = (_rms(o[:, sl], gn_ref[...]) * gate[:, sl]).astype(BF16)
        acc_ref[...] = jnp.zeros_like(acc_ref)

    u = u_ref[...]
    m = jax.nn.sigmoid(_dot(u, wga_ref[...]) + bga_ref[...]) * _dot(a_ref[...], wa_ref[...])
    m += jax.nn.sigmoid(_dot(u, wgb_ref[...]) + bgb_ref[...]) * _dot(zb_ref[...], wb_ref[...])
    m += jax.nn.sigmoid(_dot(u, wgc_ref[...]) + bgc_ref[...]) * _dot(c_ref[...], wc_ref[...])
    acc_ref[...] += _dot(m.astype(BF16), wo_ref[...])

    @pl.when(j == pl.num_programs(1) - 1)
    def _():
        o_ref[...] = h_ref[...] + acc_ref[...]


def _merge(h, norm_g, a_pre, z, o_f, o_b, p, gla_norm, w_gate, b_gate, w_a, w_b, w_c, w_o,
           *, r_off, tm=512, tn=256):
    L, D = h.shape
    wa, wb, wc = w_a.shape[0], w_b.shape[0], w_c.shape[0]
    dv = gla_norm.shape[0]
    nj = D // tn
    row = lambda width, blk=0: pl.BlockSpec((tm, width), lambda i, j: (i, blk))
    gate_w = lambda k: pl.BlockSpec((D, tn), lambda i, j: (0, j + k * nj))
    gate_b = lambda k: pl.BlockSpec((1, tn), lambda i, j: (0, j + k * nj))
    br_w = lambda width: pl.BlockSpec((width, tn), lambda i, j: (0, j))
    return pl.pallas_call(
        functools.partial(_merge_body, dv=dv),
        out_shape=jax.ShapeDtypeStruct((L, D), F32),
        grid=(L // tm, nj),
        in_specs=[
            row(D), pl.BlockSpec((1, D), lambda i, j: (0, 0)),
            row(wa), row(wb), row(wc), row(wc), row(wc, r_off // wc),
            pl.BlockSpec((1, dv), lambda i, j: (0, 0)),
            gate_w(0), gate_w(1), gate_w(2), gate_b(0), gate_b(1), gate_b(2),
            br_w(wa), br_w(wb), br_w(wc),
            pl.BlockSpec((tn, D), lambda i, j: (j, 0)),
        ],
        out_specs=row(D),
        scratch_shapes=[pltpu.VMEM((tm, D), BF16), pltpu.VMEM((tm, wb), BF16),
                        pltpu.VMEM((tm, wc), BF16), pltpu.VMEM((tm, D), F32)],
        compiler_params=_params(("parallel", "arbitrary")),
        name="merge",
    )(h, norm_g.reshape(1, D), a_pre, z, o_f, o_b, p, gla_norm.reshape(1, dv),
      w_gate, w_gate, w_gate, b_gate, b_gate, b_gate, w_a, w_b, w_c, w_o)


def _position_features(L):
    pos = jnp.arange(L, dtype=F32)
    t = pos / max(L - 1, 1)
    bands = jnp.linspace(1e-4, HY_BANDS - 1, HY_BANDS, dtype=F32)
    ang = (2.0 * math.pi / L) * pos[:, None] * bands[None, :]
    z = jnp.concatenate([t[:, None], jnp.cos(ang), -jnp.sin(ang)], axis=-1)
    z_ext = jnp.concatenate([z, z[:1], z[:0:-1]], axis=0)
    return jnp.pad(z_ext, ((0, 0), (0, LANES - z.shape[1])))


def _hyena(z0, g1, g2, skip, filt, consts, tw_c, tw_s):
    L, C = z0.shape
    n1 = DFT_N1
    n2 = 2 * L // n1
    k_time, sums = filt
    kw = k_time.shape[1]
    fr, fi = _outer_fwd(k_time.reshape(n1, n2, kw), consts["fwd1"], sums)
    kr, ki = _mid(fr, fi, tw_c, tw_s, consts["fwd2"])
    z = z0.reshape(n1 // 2, n2, C)
    for order, gate in enumerate((g1, g2)):
        ar, ai = _outer_fwd(z, consts["fwd1"])
        dr, di = _mid(ar, ai, tw_c, tw_s, consts["fwd2"], consts["inv2"], kr, ki, order)
        z = _outer_inv(dr, di, consts["inv1c"], consts["inv1s"], gate.reshape(n1 // 2, n2, C), z,
                       skip[order].reshape(1, C))
    return z.reshape(L, C)


def kernel(x, ffn1_norm, ffn1_w_gu, ffn1_w_down, mix_norm, w_in, b_in, conv_a, conv_b, hf_w1, hf_b1, hf_w2, hf_b2, hf_w3, hf_b3, hf_w_out, hf_freq, hy_skip, gk_w2, gk_b, gla_norm, w_br_a, w_br_b, w_br_c, w_o, ffn2_norm, ffn2_w_gu, ffn2_w_down, final_norm):
    bsz, L, D = x.shape
    depth = ffn1_norm.shape[0]
    wa = conv_a.shape[2]
    wb = conv_b.shape[3]
    wck = gk_w2.shape[3]
    dv = gla_norm.shape[1]
    wcv = GLA_HEADS * dv
    dk = wck // GLA_HEADS
    hid = hf_w2.shape[1]
    n_main = 3 * wa + 3 * wb + 2 * wck + 2 * wcv
    q_off = 3 * wa + 3 * wb
    k_off, v_off, r_off = q_off + wck, q_off + 2 * wck, q_off + 2 * wck + wcv
    lr_off = n_main
    gate_off = n_main + 2 * GLA_RANK
    n1 = DFT_N1
    n2 = 2 * L // n1

    consts = _dft_consts(n1, n2)
    tw_c = jnp.broadcast_to(consts["tw_c"][:, :, None], (n1, n2, LANES))
    tw_s = jnp.broadcast_to(consts["tw_s"][:, :, None], (n1, n2, LANES))
    z_ext = _position_features(L)
    deltas = jnp.linspace(math.log(HY_TARGET) / HY_SLOW_DECAY, math.log(HY_TARGET) / HY_FAST_DECAY, wb, dtype=F32)
    deltas2 = jnp.tile(deltas.reshape(1, wb), (1, HY_ORDER))

    outs = []
    for bi in range(bsz):
        h = x[bi]
        for l in range(depth):
            h = _ffn(h, ffn1_norm[l], _to_bf16(ffn1_w_gu, l), _to_bf16(ffn1_w_down, l),
                     final_norm, final=False)

            w_lr = jnp.pad(w_in[l][:, lr_off:gate_off], ((0, 0), (0, LANES - 2 * GLA_RANK))).astype(BF16)
            b_lr = jnp.pad(b_in[l][lr_off:gate_off], (0, LANES - 2 * GLA_RANK)).reshape(1, LANES)
            p, lr = _proj(h, mix_norm[l], _to_bf16(w_in, l, n_main), b_in[l][:n_main].reshape(1, n_main),
                          w_lr, b_lr)

            a_pre, z0, g1, g2 = _prep(p, conv_a[l], conv_b[l], wa=wa)

            w_out2 = hf_w_out[l].reshape(hid, HY_ORDER, 2, wb).transpose(2, 0, 1, 3).reshape(2, hid, HY_ORDER * wb)
            w1 = jnp.pad(hf_w1[l], ((0, LANES - hf_w1.shape[1]), (0, 0)))
            filt = _filters(z_ext, w1, hf_b1[l].reshape(1, hid), hf_w2[l], hf_b2[l].reshape(1, hid),
                            hf_w3[l], hf_b3[l].reshape(1, hid), hf_freq[l].reshape(1, hid), w_out2, deltas2)
            zb = _hyena(z0, g1, g2, hy_skip[l], filt, consts, tw_c, tw_s)

            gw = gk_w2[l].reshape(2, GLA_RANK, GLA_HEADS, dk).transpose(2, 0, 1, 3)
            w_pad = jnp.zeros((GLA_HEADS, 2, LANES, dk), F32)
            w_pad = w_pad.at[:, 0, :GLA_RANK].set(gw[:, 0]).at[:, 1, GLA_RANK:2 * GLA_RANK].set(gw[:, 1])
            b_pad = gk_b[l].reshape(2, GLA_HEADS, 1, dk).transpose(1, 0, 2, 3)
            o_f, o_b = _gla(p, lr, w_pad, b_pad, q_off=q_off, k_off=k_off, v_off=v_off, dk=dk, dv=dv)

            h = _merge(h, mix_norm[l], a_pre, zb, o_f, o_b, p, gla_norm[l],
                       w_in[l][:, gate_off:].astype(BF16), b_in[l][gate_off:].reshape(1, 3 * D),
                       _to_bf16(w_br_a, l), _to_bf16(w_br_b, l), _to_bf16(w_br_c, l),
                       _to_bf16(w_o, l), r_off=r_off)

            h = _ffn(h, ffn2_norm[l], _to_bf16(ffn2_w_gu, l), _to_bf16(ffn2_w_down, l),
                     final_norm, final=(l == depth - 1))
        outs.append(h)
    return jnp.stack(outs, axis=0)
```

```python
import functools
import math

import numpy as np
import jax
import jax.numpy as jnp
from jax import lax
from jax.experimental import pallas as pl
from jax.experimental.pallas import tpu as pltpu

F32 = jnp.float32
BF16 = jnp.bfloat16
HIGHEST = lax.Precision.HIGHEST

EPS = 1e-5
GLA_HEADS = 4
GLA_RANK = 16
GLA_TAU = 16.0
GLA_CHUNK = 64
HY_ORDER = 2
HY_BANDS = 16
HY_FAST_DECAY = 0.3
HY_SLOW_DECAY = 1.5
HY_TARGET = 1e-2

LANES = 128
SUBLANES = 8
DFT_N1 = 256
VMEM_LIMIT = 56 * 1024 * 1024


def _params(sem):
    return pltpu.CompilerParams(dimension_semantics=sem, vmem_limit_bytes=VMEM_LIMIT)


def _rms(x, g):
    return x * lax.rsqrt(jnp.mean(x * x, axis=-1, keepdims=True) + EPS) * g


def _dot(a, b):
    return jnp.dot(a, b, preferred_element_type=F32)


def _dot_hi(a, b):
    return jnp.dot(a, b, preferred_element_type=F32, precision=HIGHEST)


def _cast_body(x_ref, o_ref):
    o_ref[...] = x_ref[...].astype(o_ref.dtype)


def _to_bf16(w, layer, cols=None, *, block_bytes=4 << 20):
    _, rows, width = w.shape
    cols = width if cols is None else cols
    tr = rows
    while tr * cols * 4 > block_bytes and tr % 32 == 0:
        tr //= 2
    return pl.pallas_call(
        _cast_body,
        out_shape=jax.ShapeDtypeStruct((rows, cols), BF16),
        grid=(rows // tr,),
        in_specs=[pl.BlockSpec((None, tr, cols), lambda i: (layer, i, 0))],
        out_specs=pl.BlockSpec((tr, cols), lambda i: (i, 0)),
        compiler_params=_params(("parallel",)),
        name="to_bf16",
    )(w)


def _ffn_body(x_ref, g_ref, wg_ref, wu_ref, wd_ref, fn_ref, o_ref, xn_ref, acc_ref, *, final):
    j = pl.program_id(1)

    @pl.when(j == 0)
    def _():
        xn_ref[...] = _rms(x_ref[...], g_ref[...]).astype(BF16)
        acc_ref[...] = jnp.zeros_like(acc_ref)

    xn = xn_ref[...]
    g = _dot(xn, wg_ref[...])
    u = _dot(xn, wu_ref[...])
    h = (g * jax.nn.sigmoid(g) * u).astype(BF16)
    acc_ref[...] += _dot(h, wd_ref[...])

    @pl.when(j == pl.num_programs(1) - 1)
    def _():
        y = x_ref[...] + 0.5 * acc_ref[...]
        if final:
            y = _rms(y, fn_ref[...])
        o_ref[...] = y


def _ffn(x, norm_g, w_gu, w_down, final_g, *, final, tm=512, tf=512):
    L, D = x.shape
    F = w_down.shape[0]
    nf = F // tf
    return pl.pallas_call(
        functools.partial(_ffn_body, final=final),
        out_shape=jax.ShapeDtypeStruct((L, D), F32),
        grid=(L // tm, nf),
        in_specs=[
            pl.BlockSpec((tm, D), lambda i, j: (i, 0)),
            pl.BlockSpec((1, D), lambda i, j: (0, 0)),
            pl.BlockSpec((D, tf), lambda i, j: (0, j)),
            pl.BlockSpec((D, tf), lambda i, j: (0, j + nf)),
            pl.BlockSpec((tf, D), lambda i, j: (j, 0)),
            pl.BlockSpec((1, D), lambda i, j: (0, 0)),
        ],
        out_specs=pl.BlockSpec((tm, D), lambda i, j: (i, 0)),
        scratch_shapes=[pltpu.VMEM((tm, D), BF16), pltpu.VMEM((tm, D), F32)],
        compiler_params=_params(("parallel", "arbitrary")),
        name="ffn",
    )(x, norm_g.reshape(1, D), w_gu, w_gu, w_down, final_g.reshape(1, D))


def _proj_body(h_ref, g_ref, w_ref, b_ref, wlr_ref, blr_ref, p_ref, lr_ref, xn_ref):
    @pl.when(pl.program_id(1) == 0)
    def _():
        xn = _rms(h_ref[...], g_ref[...]).astype(BF16)
        xn_ref[...] = xn
        lr_ref[...] = _dot(xn, wlr_ref[...]) + blr_ref[...]

    p_ref[...] = _dot(xn_ref[...], w_ref[...]) + b_ref[...]


def _proj(h, norm_g, w, b, w_lr, b_lr, *, tm=512, tn=512):
    L, D = h.shape
    n_out = w.shape[1]
    return pl.pallas_call(
        _proj_body,
        out_shape=(jax.ShapeDtypeStruct((L, n_out), F32), jax.ShapeDtypeStruct((L, LANES), F32)),
        grid=(L // tm, n_out // tn),
        in_specs=[
            pl.BlockSpec((tm, D), lambda i, j: (i, 0)),
            pl.BlockSpec((1, D), lambda i, j: (0, 0)),
            pl.BlockSpec((D, tn), lambda i, j: (0, j)),
            pl.BlockSpec((1, tn), lambda i, j: (0, j)),
            pl.BlockSpec((D, LANES), lambda i, j: (0, 0)),
            pl.BlockSpec((1, LANES), lambda i, j: (0, 0)),
        ],
        out_specs=(pl.BlockSpec((tm, tn), lambda i, j: (i, j)),
                   pl.BlockSpec((tm, LANES), lambda i, j: (i, 0))),
        scratch_shapes=[pltpu.VMEM((tm, D), BF16)],
        compiler_params=_params(("parallel", "arbitrary")),
        name="proj",
    )(h, norm_g.reshape(1, D), w, b, w_lr, b_lr)


def _conv3(x, x_before, x_after, w):
    tm = x.shape[0]
    row = lax.broadcasted_iota(jnp.int32, x.shape, 0)
    down = jnp.where(row == 0, x_before, pltpu.roll(x, 1, axis=0))
    up = jnp.where(row == tm - 1, x_after, pltpu.roll(x, tm - 1, axis=0))
    return down * w[0:1] + x * w[1:2] + up * w[2:3]


def _prep_body(m_ref, pv_ref, nx_ref, ca_w_ref, cb_w_ref, a_ref, z_ref, g1_ref, g2_ref, *, wa):
    i = pl.program_id(0)
    has_prev = (i > 0).astype(F32)
    has_next = (i < pl.num_programs(0) - 1).astype(F32)
    last = SUBLANES - 1

    def sec(k):
        sl = slice(k * wa, (k + 1) * wa)
        return (m_ref[:, sl], pv_ref[last:last + 1, sl] * has_prev, nx_ref[0:1, sl] * has_next)

    xa, xa_p, xa_n = sec(0)
    ba, _, _ = sec(1)
    ca, ca_p, ca_n = sec(2)
    a_ref[...] = (ba * _conv3(ca * xa, ca_p * xa_p, ca_n * xa_n, ca_w_ref[...])).astype(BF16)
    for k, o_ref in enumerate((z_ref, g1_ref, g2_ref)):
        x, x_p, x_n = sec(3 + k)
        _store_groups(o_ref, _conv3(x, x_p, x_n, cb_w_ref[k]))


def _store_groups(o_ref, x):
    for g in range(o_ref.shape[0]):
        o_ref[g] = x[:, g * LANES:(g + 1) * LANES]


def _load_groups(ref):
    return jnp.concatenate([ref[g] for g in range(ref.shape[0])], axis=1)


def _prep(p, conv_a, conv_b, *, wa, tm=256):
    L = p.shape[0]
    width = 6 * wa
    rb = tm // SUBLANES
    nrb = L // SUBLANES
    grouped = pl.BlockSpec((wa // LANES, tm, LANES), lambda i: (0, i, 0))
    return pl.pallas_call(
        functools.partial(_prep_body, wa=wa),
        out_shape=(jax.ShapeDtypeStruct((L, wa), BF16),)
        + (jax.ShapeDtypeStruct((wa // LANES, L, LANES), F32),) * 3,
        grid=(L // tm,),
        in_specs=[
            pl.BlockSpec((tm, width), lambda i: (i, 0)),
            pl.BlockSpec((SUBLANES, width), lambda i: (jnp.maximum(i * rb - 1, 0), 0)),
            pl.BlockSpec((SUBLANES, width), lambda i: (jnp.minimum((i + 1) * rb, nrb - 1), 0)),
            pl.BlockSpec((3, wa), lambda i: (0, 0)),
            pl.BlockSpec((3, 3, wa), lambda i: (0, 0, 0)),
        ],
        out_specs=(pl.BlockSpec((tm, wa), lambda i: (i, 0)), grouped, grouped, grouped),
        compiler_params=_params(("parallel",)),
        name="prep",
    )(p, p, p, conv_a, conv_b)


def _filter_body(z_ref, zt_ref, w1_ref, b1_ref, w2_ref, b2_ref, w3_ref, b3_ref, fr_ref, wo_ref, dl_ref,
                 k_ref, s_ref, *, half_tiles, tr):
    i = pl.program_id(0)
    fr = fr_ref[...]
    h = jnp.sin(fr * (_dot_hi(w1_ref[...], zt_ref[...]) + b1_ref[...]))
    h = jnp.sin(fr * (_dot_hi(w2_ref[...], h) + b2_ref[...]))
    h = jnp.sin(fr * (_dot_hi(w3_ref[...], h) + b3_ref[...]))
    t = z_ref[:, 0:1]
    decay = jnp.exp(-t * jnp.abs(dl_ref[...]))
    hd = lax.dot_general(h, wo_ref[0], _TN, preferred_element_type=F32, precision=HIGHEST) * decay

    @pl.when(i % half_tiles == 0)
    def _():
        s_ref[...] = jnp.zeros_like(s_ref)

    s_ref[0] += jnp.sum(jnp.abs(hd), axis=0, keepdims=True)
    row = lax.broadcasted_iota(jnp.int32, hd.shape, 0)
    zero_row = jnp.where(i == half_tiles, 0, -1)
    _store_groups(k_ref, jnp.where(row == zero_row, 0.0, hd))


def _filters(z_ext, w1, b1, w2, b2, w3, b3, freq, w_out2, deltas2, *, tr=512):
    n_rows, zw = z_ext.shape
    hid = w2.shape[0]
    width = w_out2.shape[2]
    half_tiles = n_rows // (2 * tr)
    full = lambda shape: pl.BlockSpec(shape, lambda i: (0,) * len(shape))
    colv = lambda v: v.reshape(hid, 1)
    return pl.pallas_call(
        functools.partial(_filter_body, half_tiles=half_tiles, tr=tr),
        out_shape=(jax.ShapeDtypeStruct((width // LANES, n_rows, LANES), F32),
                   jax.ShapeDtypeStruct((2, 1, width), F32)),
        grid=(n_rows // tr,),
        in_specs=[
            pl.BlockSpec((tr, zw), lambda i: (i, 0)),
            pl.BlockSpec((zw, tr), lambda i: (0, i)),
            full((hid, zw)), full((hid, 1)), full((hid, hid)), full((hid, 1)),
            full((hid, hid)), full((hid, 1)), full((hid, 1)),
            pl.BlockSpec((1, hid, width), lambda i: (i // half_tiles, 0, 0)),
            full((1, width)),
        ],
        out_specs=(pl.BlockSpec((width // LANES, tr, LANES), lambda i: (0, i, 0)),
                   pl.BlockSpec((1, 1, width), lambda i: (i // half_tiles, 0, 0))),
        compiler_params=_params(("arbitrary",)),
        name="hyena_filters",
    )(z_ext, z_ext.T, w1.T, colv(b1), w2.T, colv(b2), w3.T, colv(b3), colv(freq), w_out2, deltas2)


def _dft_consts(n1, n2):
    n = n1 * n2
    k = np.arange(n1)
    ang1 = 2.0 * np.pi * ((k[:, None] * k[None, :]) % n1) / n1
    c1, s1 = np.cos(ang1), np.sin(ang1)
    m = np.arange(n2)
    ang2 = 2.0 * np.pi * ((m[:, None] * m[None, :]) % n2) / n2
    c2, s2 = np.cos(ang2), np.sin(ang2)
    angt = 2.0 * np.pi * (k[:, None] * m[None, :]) / n
    as_bf = lambda a: jnp.asarray(a, F32).astype(BF16)
    return dict(
        fwd1=as_bf(np.concatenate([c1, -s1], axis=0)),
        inv1c=as_bf(c1[: n1 // 2] / n), inv1s=as_bf(-s1[: n1 // 2] / n),
        fwd2=as_bf(np.block([[c2, s2], [-s2, c2]])),
        inv2=as_bf(np.block([[c2, -s2], [s2, c2]])),
        tw_c=jnp.asarray(np.cos(angt), F32), tw_s=jnp.asarray(np.sin(angt), F32),
    )


def _flatten_block(ref, buf):
    groups, n, t2, lanes = ref.shape
    for g in range(groups):
        buf[g] = ref[g].reshape(n * t2, lanes)


def _unflatten_block(buf, ref):
    groups, n, t2, lanes = ref.shape
    for g in range(groups):
        ref[g] = buf[g].reshape(n, t2, lanes)


def _load_strided(buf, s, t2):
    n = buf.shape[1] // t2
    return jnp.concatenate([buf[g, pl.ds(s, n, stride=t2), :] for g in range(buf.shape[0])], axis=1)


def _store_strided(buf, s, t2, x):
    n = buf.shape[1] // t2
    for g in range(buf.shape[0]):
        buf[g, pl.ds(s, n, stride=t2), :] = x[:, g * LANES:(g + 1) * LANES]


def _flat_scratch(groups, n, t2):
    return pltpu.VMEM((groups, n * t2, LANES), F32)


def _outer_fwd_body(x_ref, f_ref, *rest, scaled):
    if scaled:
        s_ref, ar_ref, ai_ref, xb, arb, aib = rest
        inv = 1.0 / (s_ref[...] + EPS)
    else:
        ar_ref, ai_ref, xb, arb, aib = rest
    rows, t2 = x_ref.shape[1:3]
    n1 = ar_ref.shape[1]
    f = f_ref[...]
    _flatten_block(x_ref, xb)
    for s in range(t2):
        x = _load_strided(xb, s, t2)
        if scaled:
            half = rows // 2
            x = jnp.concatenate([x[:half] * inv[0], x[half:] * inv[1]], axis=0)
        a = _dot(f, x.astype(BF16))
        _store_strided(arb, s, t2, a[:n1])
        _store_strided(aib, s, t2, a[n1:])
    _unflatten_block(arb, ar_ref)
    _unflatten_block(aib, ai_ref)


def _outer_fwd(x4, fwd1, sums=None, *, t2=SUBLANES, gt=4):
    groups, rows, n2, _ = x4.shape
    n1 = fwd1.shape[1]
    scaled = sums is not None
    blk = lambda r: pl.BlockSpec((gt, r, t2, LANES), lambda j, c: (c, 0, j, 0))
    args = [x4, fwd1[:, :rows]]
    in_specs = [blk(rows), pl.BlockSpec((2 * n1, rows), lambda j, c: (0, 0))]
    if scaled:
        args.append(sums)
        in_specs.append(pl.BlockSpec((2, 1, gt * LANES), lambda j, c: (0, 0, c)))
    return pl.pallas_call(
        functools.partial(_outer_fwd_body, scaled=scaled),
        out_shape=(jax.ShapeDtypeStruct((groups, n1, n2, LANES), F32),) * 2,
        grid=(n2 // t2, groups // gt),
        in_specs=in_specs,
        out_specs=(blk(n1),) * 2,
        scratch_shapes=[_flat_scratch(gt, rows, t2), _flat_scratch(gt, n1, t2), _flat_scratch(gt, n1, t2)],
        compiler_params=_params(("parallel", "parallel")),
        name="dft_outer_fwd",
    )(*args)


def _lane_tile(t, width):
    return jnp.concatenate([t] * (width // LANES), axis=1) if width > LANES else t


def _load_slab(ref, s):
    return jnp.concatenate([ref[g, s] for g in range(ref.shape[0])], axis=1)


def _store_slab(ref, s, x):
    for g in range(ref.shape[0]):
        ref[g, s] = x[:, g * LANES:(g + 1) * LANES]


def _twiddled(ar, ai, tc, ts):
    return ar * tc + ai * ts, ai * tc - ar * ts


def _inner_fwd(ar, ai, tc, ts, f2):
    n2 = ar.shape[0]
    br, bi = _twiddled(ar, ai, tc, ts)
    x = _dot(f2, jnp.concatenate([br, bi], axis=0).astype(BF16))
    return x[:n2], x[n2:]


def _mid_spec_body(ar_ref, ai_ref, tc_ref, ts_ref, f2_ref, xr_ref, xi_ref):
    width = ar_ref.shape[0] * LANES
    for s in range(ar_ref.shape[1]):
        tc, ts = _lane_tile(tc_ref[s], width), _lane_tile(ts_ref[s], width)
        xr, xi = _inner_fwd(_load_slab(ar_ref, s), _load_slab(ai_ref, s), tc, ts, f2_ref[...])
        _store_slab(xr_ref, s, xr)
        _store_slab(xi_ref, s, xi)


def _mid_conv_body(ar_ref, ai_ref, kr_ref, ki_ref, tc_ref, ts_ref, f2_ref, g2_ref, dr_ref, di_ref):
    width = ar_ref.shape[0] * LANES
    n2 = ar_ref.shape[2]
    for s in range(ar_ref.shape[1]):
        tc, ts = _lane_tile(tc_ref[s], width), _lane_tile(ts_ref[s], width)
        xr, xi = _inner_fwd(_load_slab(ar_ref, s), _load_slab(ai_ref, s), tc, ts, f2_ref[...])
        kr, ki = _load_slab(kr_ref, s), _load_slab(ki_ref, s)
        yr = xr * kr - xi * ki
        yi = xr * ki + xi * kr
        c = _dot(g2_ref[...], jnp.concatenate([yr, yi], axis=0).astype(BF16))
        cr, ci = c[:n2], c[n2:]
        _store_slab(dr_ref, s, cr * tc - ci * ts)
        _store_slab(di_ref, s, ci * tc + cr * ts)


def _mid(ar, ai, tw_c, tw_s, fwd2, inv2=None, kr=None, ki=None, order=0, *, kb=4, gt=4):
    groups, n1, n2, _ = ar.shape
    conv = kr is not None
    slab = pl.BlockSpec((gt, kb, n2, LANES), lambda i, q: (q, i, 0, 0))
    tw = pl.BlockSpec((kb, n2, LANES), lambda i, q: (i, 0, 0))
    mat = pl.BlockSpec((2 * n2, 2 * n2), lambda i, q: (0, 0))
    if conv:
        kslab = pl.BlockSpec((gt, kb, n2, LANES), lambda i, q: (order * (groups // gt) + q, i, 0, 0))
        args = (ar, ai, kr, ki, tw_c, tw_s, fwd2, inv2)
        in_specs = [slab, slab, kslab, kslab, tw, tw, mat, mat]
        body, name = _mid_conv_body, "hyena_mid_conv"
    else:
        args = (ar, ai, tw_c, tw_s, fwd2)
        in_specs = [slab, slab, tw, tw, mat]
        body, name = _mid_spec_body, "hyena_mid_spec"
    return pl.pallas_call(
        body,
        out_shape=(jax.ShapeDtypeStruct((groups, n1, n2, LANES), F32),) * 2,
        grid=(n1 // kb, groups // gt),
        in_specs=in_specs,
        out_specs=(slab, slab),
        compiler_params=_params(("parallel", "parallel")),
        name=name,
    )(*args)


def _outer_inv_body(dr_ref, di_ref, c_ref, s_ref, gate_ref, z_ref, skip_ref, o_ref, drb, dib, gb, zb, ob):
    c, sm, skip = c_ref[...], s_ref[...], skip_ref[...]
    t2 = dr_ref.shape[2]
    for ref, buf in ((dr_ref, drb), (di_ref, dib), (gate_ref, gb), (z_ref, zb)):
        _flatten_block(ref, buf)
    for s in range(t2):
        y = (_dot(c, _load_strided(drb, s, t2).astype(BF16))
             + _dot(sm, _load_strided(dib, s, t2).astype(BF16)))
        _store_strided(ob, s, t2, _load_strided(gb, s, t2) * (y + skip * _load_strided(zb, s, t2)))
    _unflatten_block(ob, o_ref)


def _outer_inv(dr, di, inv1c, inv1s, gate4, z4, skip, *, t2=SUBLANES):
    groups, n1, n2, _ = dr.shape
    half = n1 // 2
    blk = lambda r: pl.BlockSpec((groups, r, t2, LANES), lambda j: (0, 0, j, 0))
    mat = pl.BlockSpec((half, n1), lambda j: (0, 0))
    return pl.pallas_call(
        _outer_inv_body,
        out_shape=jax.ShapeDtypeStruct((groups, half, n2, LANES), F32),
        grid=(n2 // t2,),
        in_specs=[blk(n1), blk(n1), mat, mat, blk(half), blk(half),
                  pl.BlockSpec((1, groups * LANES), lambda j: (0, 0))],
        out_specs=blk(half),
        scratch_shapes=[_flat_scratch(groups, n1, t2), _flat_scratch(groups, n1, t2)]
        + [_flat_scratch(groups, half, t2)] * 3,
        compiler_params=_params(("parallel",)),
        name="dft_outer_inv",
    )(dr, di, inv1c, inv1s, gate4, z4, skip)


def _log_sigmoid(x):
    return jnp.minimum(x, 0.0) - jnp.log1p(jnp.exp(-jnp.abs(x)))


_NT = (((1,), (1,)), ((), ()))
_TN = (((0,), (0,)), ((), ()))


def _gla_direction(q_ref, k_ref, v_ref, lr_ref, w, b, cum, o_ref, st_ref, *, reverse, scale):
    tb, dk = q_ref.shape
    sub = cum.shape[0]
    chunk = GLA_CHUNK
    cum_bf = cum.astype(BF16)
    qes, intras, upd, dec = [], [], [], []
    for s in range(tb // sub):
        rs = slice(s * sub, (s + 1) * sub)
        g = _log_sigmoid(_dot(lr_ref[rs, :].astype(BF16), w) + b) / GLA_TAU
        g_hi = g.astype(BF16)
        g_lo = (g - g_hi.astype(F32)).astype(BF16)
        bc = _dot(cum_bf, g_hi) + _dot(cum_bf, g_lo)
        k = k_ref[rs, :]
        v = v_ref[rs, :].astype(BF16)
        qe = (q_ref[rs, :] * scale * jnp.exp(bc)).astype(BF16)
        ke = (k * jnp.exp(-bc)).astype(BF16)
        a = lax.dot_general(qe, ke, _NT, preferred_element_type=F32)
        intra = _dot(jnp.where(cum > 0.5, a, 0.0).astype(BF16), v)
        for c in range(sub // chunk):
            sl = slice(c * chunk, (c + 1) * chunk)
            bl = bc[c * chunk:c * chunk + 1] if reverse else bc[(c + 1) * chunk - 1:(c + 1) * chunk]
            kd = (k[sl] * jnp.exp(bl - bc[sl])).astype(BF16)
            upd.append(lax.dot_general(v[sl], kd, _TN, preferred_element_type=F32))
            dec.append(jnp.exp(bl))
            qes.append(qe[sl])
            intras.append(intra[sl])
    st = st_ref[...]
    chunks = range(tb // chunk)
    for c in (reversed(chunks) if reverse else chunks):
        o_ref[c * chunk:(c + 1) * chunk, :] = intras[c] + lax.dot_general(
            qes[c], st.astype(BF16), _NT, preferred_element_type=F32)
        st = st * dec[c] + upd[c]
    st_ref[...] = st


def _gla_body(qf, kf, vf, lrf, qb, kb, vb, lrb, w_ref, b_ref, cum_ref, of_ref, ob_ref, sf_ref, sb_ref, *, scale):
    @pl.when(pl.program_id(1) == 0)
    def _():
        sf_ref[...] = jnp.zeros_like(sf_ref)
        sb_ref[...] = jnp.zeros_like(sb_ref)

    _gla_direction(qf, kf, vf, lrf, w_ref[0, 0], b_ref[0, 0], cum_ref[0], of_ref, sf_ref,
                   reverse=False, scale=scale)
    _gla_direction(qb, kb, vb, lrb, w_ref[0, 1], b_ref[0, 1], cum_ref[1], ob_ref, sb_ref,
                   reverse=True, scale=scale)


def _chunk_cumsum_matrices(sub):
    i = np.arange(sub)
    same = (i[:, None] // GLA_CHUNK) == (i[None, :] // GLA_CHUNK)
    fwd = same & (i[None, :] <= i[:, None])
    bwd = same & (i[None, :] >= i[:, None])
    return jnp.asarray(np.stack([fwd, bwd]), F32)


def _gla(p, lr, w_pad, b_pad, *, q_off, k_off, v_off, dk, dv, tb=512, sub=256):
    L = p.shape[0]
    tb = min(tb, L)
    nb = L // tb
    heads = w_pad.shape[0]
    fwd = lambda width, off: pl.BlockSpec((tb, width), lambda h, i: (i, off // width + h))
    bwd = lambda width, off: pl.BlockSpec((tb, width), lambda h, i: (nb - 1 - i, off // width + h))
    return pl.pallas_call(
        functools.partial(_gla_body, scale=dk ** -0.5),
        out_shape=(jax.ShapeDtypeStruct((L, heads * dv), F32),) * 2,
        grid=(heads, nb),
        in_specs=[
            fwd(dk, q_off), fwd(dk, k_off), fwd(dv, v_off), pl.BlockSpec((tb, LANES), lambda h, i: (i, 0)),
            bwd(dk, q_off), bwd(dk, k_off), bwd(dv, v_off), pl.BlockSpec((tb, LANES), lambda h, i: (nb - 1 - i, 0)),
            pl.BlockSpec((1, 2, LANES, dk), lambda h, i: (h, 0, 0, 0)),
            pl.BlockSpec((1, 2, 1, dk), lambda h, i: (h, 0, 0, 0)),
            pl.BlockSpec((2, sub, sub), lambda h, i: (0, 0, 0)),
        ],
        out_specs=(pl.BlockSpec((tb, dv), lambda h, i: (i, h)),
                   pl.BlockSpec((tb, dv), lambda h, i: (nb - 1 - i, h))),
        scratch_shapes=[pltpu.VMEM((dv, dk), F32), pltpu.VMEM((dv, dk), F32)],
        compiler_params=_params(("parallel", "arbitrary")),
        name="gla",
    )(p, p, p, lr, p, p, p, lr, w_pad.astype(BF16), b_pad, _chunk_cumsum_matrices(sub))


def _merge_body(h_ref, g_ref, a_ref, z_ref, of_ref, ob_ref, r_ref, gn_ref,
                wga_ref, wgb_ref, wgc_ref, bga_ref, bgb_ref, bgc_ref,
                wa_ref, wb_ref, wc_ref, wo_ref, o_ref, u_ref, zb_ref, c_ref, acc_ref, *, dv):
    j = pl.program_id(1)

    @pl.when(j == 0)
    def _():
        u_ref[...] = _rms(h_ref[...], g_ref[...]).astype(BF16)
        zb_ref[...] = _load_groups(z_ref).astype(BF16)
        o = of_ref[...] + ob_ref[...]
        r = r_ref[...]
        gate = r * jax.nn.sigmoid(r)
        for hd in range(o.shape[1] // dv):
            sl = slice(hd * dv, (hd + 1) * dv)
            c_ref[:, sl] = (_rms(o[:, sl], gn_ref[...]) * gate[:, sl]).astype(BF16)
        acc_ref[...] = jnp.zeros_like(acc_ref)

    u = u_ref[...]
    m = jax.nn.sigmoid(_dot(u, wga_ref[...]) + bga_ref[...]) * _dot(a_ref[...], wa_ref[...])
    m += jax.nn.sigmoid(_dot(u, wgb_ref[...]) + bgb_ref[...]) * _dot(zb_ref[...], wb_ref[...])
    m += jax.nn.sigmoid(_dot(u, wgc_ref[...]) + bgc_ref[...]) * _dot(c_ref[...], wc_ref[...])
    acc_ref[...] += _dot(m.astype(BF16), wo_ref[...])

    @pl.when(j == pl.num_programs(1) - 1)
    def _():
        o_ref[...] = h_ref[...] + acc_ref[...]


def _merge(h, norm_g, a_pre, z, o_f, o_b, p, gla_norm, w_gate, b_gate, w_a, w_b, w_c, w_o,
           *, r_off, tm=512, tn=256):
    L, D = h.shape
    wa, wb, wc = w_a.shape[0], w_b.shape[0], w_c.shape[0]
    dv = gla_norm.shape[0]
    nj = D // tn
    row = lambda width, blk=0: pl.BlockSpec((tm, width), lambda i, j: (i, blk))
    gate_w = lambda k: pl.BlockSpec((D, tn), lambda i, j: (0, j + k * nj))
    gate_b = lambda k: pl.BlockSpec((1, tn), lambda i, j: (0, j + k * nj))
    br_w = lambda width: pl.BlockSpec((width, tn), lambda i, j: (0, j))
    return pl.pallas_call(
        functools.partial(_merge_body, dv=dv),
        out_shape=jax.ShapeDtypeStruct((L, D), F32),
        grid=(L // tm, nj),
        in_specs=[
            row(D), pl.BlockSpec((1, D), lambda i, j: (0, 0)),
            row(wa), pl.BlockSpec((wb // LANES, tm, LANES), lambda i, j: (0, i, 0)),
            row(wc), row(wc), row(wc, r_off // wc),
            pl.BlockSpec((1, dv), lambda i, j: (0, 0)),
            gate_w(0), gate_w(1), gate_w(2), gate_b(0), gate_b(1), gate_b(2),
            br_w(wa), br_w(wb), br_w(wc),
            pl.BlockSpec((tn, D), lambda i, j: (j, 0)),
        ],
        out_specs=row(D),
        scratch_shapes=[pltpu.VMEM((tm, D), BF16), pltpu.VMEM((tm, wb), BF16),
                        pltpu.VMEM((tm, wc), BF16), pltpu.VMEM((tm, D), F32)],
        compiler_params=_params(("parallel", "arbitrary")),
        name="merge",
    )(h, norm_g.reshape(1, D), a_pre, z, o_f, o_b, p, gla_norm.reshape(1, dv),
      w_gate, w_gate, w_gate, b_gate, b_gate, b_gate, w_a, w_b, w_c, w_o)


def _position_features(L):
    pos = jnp.arange(L, dtype=F32)
    t = pos / max(L - 1, 1)
    bands = jnp.linspace(1e-4, HY_BANDS - 1, HY_BANDS, dtype=F32)
    ang = (2.0 * math.pi / L) * pos[:, None] * bands[None, :]
    z = jnp.concatenate([t[:, None], jnp.cos(ang), -jnp.sin(ang)], axis=-1)
    z_ext = jnp.concatenate([z, z[:1], z[:0:-1]], axis=0)
    return jnp.pad(z_ext, ((0, 0), (0, LANES - z.shape[1])))


def _hyena(z0, g1, g2, skip, filt, consts, tw_c, tw_s):
    groups, L, _ = z0.shape
    n1 = DFT_N1
    n2 = 2 * L // n1
    k_time, sums = filt
    fr, fi = _outer_fwd(k_time.reshape(k_time.shape[0], n1, n2, LANES), consts["fwd1"], sums)
    kr, ki = _mid(fr, fi, tw_c, tw_s, consts["fwd2"])
    z = z0.reshape(groups, n1 // 2, n2, LANES)
    for order, gate in enumerate((g1, g2)):
        ar, ai = _outer_fwd(z, consts["fwd1"])
        dr, di = _mid(ar, ai, tw_c, tw_s, consts["fwd2"], consts["inv2"], kr, ki, order)
        z = _outer_inv(dr, di, consts["inv1c"], consts["inv1s"], gate.reshape(z.shape), z,
                       skip[order].reshape(1, groups * LANES))
    return z.reshape(groups, L, LANES)


def kernel(x, ffn1_norm, ffn1_w_gu, ffn1_w_down, mix_norm, w_in, b_in, conv_a, conv_b, hf_w1, hf_b1, hf_w2, hf_b2, hf_w3, hf_b3, hf_w_out, hf_freq, hy_skip, gk_w2, gk_b, gla_norm, w_br_a, w_br_b, w_br_c, w_o, ffn2_norm, ffn2_w_gu, ffn2_w_down, final_norm):
    bsz, L, D = x.shape
    depth = ffn1_norm.shape[0]
    wa = conv_a.shape[2]
    wb = conv_b.shape[3]
    wck = gk_w2.shape[3]
    dv = gla_norm.shape[1]
    wcv = GLA_HEADS * dv
    dk = wck // GLA_HEADS
    hid = hf_w2.shape[1]
    n_main = 3 * wa + 3 * wb + 2 * wck + 2 * wcv
    q_off = 3 * wa + 3 * wb
    k_off, v_off, r_off = q_off + wck, q_off + 2 * wck, q_off + 2 * wck + wcv
    lr_off = n_main
    gate_off = n_main + 2 * GLA_RANK
    n1 = DFT_N1
    n2 = 2 * L // n1

    consts = _dft_consts(n1, n2)
    tw_c = jnp.broadcast_to(consts["tw_c"][:, :, None], (n1, n2, LANES))
    tw_s = jnp.broadcast_to(consts["tw_s"][:, :, None], (n1, n2, LANES))
    z_ext = _position_features(L)
    deltas = jnp.linspace(math.log(HY_TARGET) / HY_SLOW_DECAY, math.log(HY_TARGET) / HY_FAST_DECAY, wb, dtype=F32)
    deltas2 = jnp.tile(deltas.reshape(1, wb), (1, HY_ORDER))

    outs = []
    for bi in range(bsz):
        h = x[bi]
        for l in range(depth):
            h = _ffn(h, ffn1_norm[l], _to_bf16(ffn1_w_gu, l), _to_bf16(ffn1_w_down, l),
                     final_norm, final=False)

            w_lr = jnp.pad(w_in[l][:, lr_off:gate_off], ((0, 0), (0, LANES - 2 * GLA_RANK))).astype(BF16)
            b_lr = jnp.pad(b_in[l][lr_off:gate_off], (0, LANES - 2 * GLA_RANK)).reshape(1, LANES)
            p, lr = _proj(h, mix_norm[l], _to_bf16(w_in, l, n_main), b_in[l][:n_main].reshape(1, n_main),
                          w_lr, b_lr)

            a_pre, z0, g1, g2 = _prep(p, conv_a[l], conv_b[l], wa=wa)

            w_out2 = hf_w_out[l].reshape(hid, HY_ORDER, 2, wb).transpose(2, 0, 1, 3).reshape(2, hid, HY_ORDER * wb)
            w1 = jnp.pad(hf_w1[l], ((0, LANES - hf_w1.shape[1]), (0, 0)))
            filt = _filters(z_ext, w1, hf_b1[l], hf_w2[l], hf_b2[l], hf_w3[l], hf_b3[l], hf_freq[l],
                            w_out2, deltas2)
            zb = _hyena(z0, g1, g2, hy_skip[l], filt, consts, tw_c, tw_s)

            gw = gk_w2[l].reshape(2, GLA_RANK, GLA_HEADS, dk).transpose(2, 0, 1, 3)
            w_pad = jnp.zeros((GLA_HEADS, 2, LANES, dk), F32)
            w_pad = w_pad.at[:, 0, :GLA_RANK].set(gw[:, 0]).at[:, 1, GLA_RANK:2 * GLA_RANK].set(gw[:, 1])
            b_pad = gk_b[l].reshape(2, GLA_HEADS, 1, dk).transpose(1, 0, 2, 3)
            o_f, o_b = _gla(p, lr, w_pad, b_pad, q_off=q_off, k_off=k_off, v_off=v_off, dk=dk, dv=dv)

            h = _merge(h, mix_norm[l], a_pre, zb, o_f, o_b, p, gla_norm[l],
                       w_in[l][:, gate_off:].astype(BF16), b_in[l][gate_off:].reshape(1, 3 * D),
                       _to_bf16(w_br_a, l), _to_bf16(w_br_b, l), _to_bf16(w_br_c, l),
                       _to_bf16(w_o, l), r_off=r_off)

            h = _ffn(h, ffn2_norm[l], _to_bf16(ffn2_w_gu, l), _to_bf16(ffn2_w_down, l),
                     final_norm, final=(l == depth - 1))
        outs.append(h)
    return jnp.stack(outs, axis=0)
```

```python
import functools
import math

import numpy as np
import jax
import jax.numpy as jnp
from jax import lax
from jax.experimental import pallas as pl
from jax.experimental.pallas import tpu as pltpu

F32 = jnp.float32
BF16 = jnp.bfloat16
HIGHEST = lax.Precision.HIGHEST

EPS = 1e-5
GLA_HEADS = 4
GLA_RANK = 16
GLA_TAU = 16.0
GLA_CHUNK = 64
HY_ORDER = 2
HY_BANDS = 16
HY_FAST_DECAY = 0.3
HY_SLOW_DECAY = 1.5
HY_TARGET = 1e-2

LANES = 128
SUBLANES = 8
DFT_N1 = 256
VMEM_LIMIT = 56 * 1024 * 1024


def _params(sem):
    return pltpu.CompilerParams(dimension_semantics=sem, vmem_limit_bytes=VMEM_LIMIT)


def _rms(x, g):
    return x * lax.rsqrt(jnp.mean(x * x, axis=-1, keepdims=True) + EPS) * g


def _dot(a, b):
    return jnp.dot(a, b, preferred_element_type=F32)


def _dot_hi(a, b):
    return jnp.dot(a, b, preferred_element_type=F32, precision=HIGHEST)


def _cast_body(x_ref, o_ref):
    o_ref[...] = x_ref[...].astype(o_ref.dtype)


def _to_bf16(w, layer, cols=None, *, block_bytes=4 << 20):
    _, rows, width = w.shape
    cols = width if cols is None else cols
    tr = rows
    while tr * cols * 4 > block_bytes and tr % 32 == 0:
        tr //= 2
    return pl.pallas_call(
        _cast_body,
        out_shape=jax.ShapeDtypeStruct((rows, cols), BF16),
        grid=(rows // tr,),
        in_specs=[pl.BlockSpec((None, tr, cols), lambda i: (layer, i, 0))],
        out_specs=pl.BlockSpec((tr, cols), lambda i: (i, 0)),
        compiler_params=_params(("parallel",)),
        name="to_bf16",
    )(w)


def _ffn_body(x_ref, g_ref, wg_ref, wu_ref, wd_ref, fn_ref, o_ref, xn_ref, *, final):
    j = pl.program_id(1)

    @pl.when(j == 0)
    def _():
        xn_ref[...] = _rms(x_ref[...], g_ref[...]).astype(BF16)
        o_ref[...] = jnp.zeros_like(o_ref)

    xn = xn_ref[...]
    g = _dot(xn, wg_ref[...])
    u = _dot(xn, wu_ref[...])
    h = (g * jax.nn.sigmoid(g) * u).astype(BF16)
    o_ref[...] += _dot(h, wd_ref[...])

    @pl.when(j == pl.num_programs(1) - 1)
    def _():
        y = x_ref[...] + 0.5 * o_ref[...]
        if final:
            y = _rms(y, fn_ref[...])
        o_ref[...] = y


def _ffn(x, norm_g, w_gu, w_down, final_g, *, final, tm=1024, tf=256):
    L, D = x.shape
    F = w_down.shape[0]
    tm = min(tm, L)
    nf = F // tf
    return pl.pallas_call(
        functools.partial(_ffn_body, final=final),
        out_shape=jax.ShapeDtypeStruct((L, D), F32),
        grid=(L // tm, nf),
        in_specs=[
            pl.BlockSpec((tm, D), lambda i, j: (i, 0)),
            pl.BlockSpec((1, D), lambda i, j: (0, 0)),
            pl.BlockSpec((D, tf), lambda i, j: (0, j)),
            pl.BlockSpec((D, tf), lambda i, j: (0, j + nf)),
            pl.BlockSpec((tf, D), lambda i, j: (j, 0)),
            pl.BlockSpec((1, D), lambda i, j: (0, 0)),
        ],
        out_specs=pl.BlockSpec((tm, D), lambda i, j: (i, 0)),
        scratch_shapes=[pltpu.VMEM((tm, D), BF16)],
        compiler_params=_params(("parallel", "arbitrary")),
        name="ffn",
    )(x, norm_g.reshape(1, D), w_gu, w_gu, w_down, final_g.reshape(1, D))


def _proj_body(h_ref, g_ref, w_ref, b_ref, wlr_ref, blr_ref, p_ref, lr_ref, *, tn):
    xn = _rms(h_ref[...], g_ref[...]).astype(BF16)
    lr_ref[...] = _dot(xn, wlr_ref[...]) + blr_ref[...]
    for c in range(p_ref.shape[1] // tn):
        sl = slice(c * tn, (c + 1) * tn)
        p_ref[:, sl] = _dot(xn, w_ref[:, sl]) + b_ref[:, sl]


def _proj(h, norm_g, w, b, w_lr, b_lr, *, tm=256, tn=512):
    L, D = h.shape
    n_out = w.shape[1]
    resident = lambda shape: pl.BlockSpec(shape, lambda i: (0, 0), pipeline_mode=pl.Buffered(1))
    return pl.pallas_call(
        functools.partial(_proj_body, tn=tn),
        out_shape=(jax.ShapeDtypeStruct((L, n_out), F32), jax.ShapeDtypeStruct((L, LANES), F32)),
        grid=(L // tm,),
        in_specs=[
            pl.BlockSpec((tm, D), lambda i: (i, 0)),
            resident((1, D)), resident((D, n_out)), resident((1, n_out)),
            resident((D, LANES)), resident((1, LANES)),
        ],
        out_specs=(pl.BlockSpec((tm, n_out), lambda i: (i, 0)),
                   pl.BlockSpec((tm, LANES), lambda i: (i, 0))),
        compiler_params=_params(("parallel",)),
        name="proj",
    )(h, norm_g.reshape(1, D), w, b, w_lr, b_lr)


def _conv3(x, x_before, x_after, w):
    tm = x.shape[0]
    row = lax.broadcasted_iota(jnp.int32, x.shape, 0)
    down = jnp.where(row == 0, x_before, pltpu.roll(x, 1, axis=0))
    up = jnp.where(row == tm - 1, x_after, pltpu.roll(x, tm - 1, axis=0))
    return down * w[0:1] + x * w[1:2] + up * w[2:3]


def _prep_body(m_ref, pv_ref, nx_ref, ca_w_ref, cb_w_ref, a_ref, z_ref, g1_ref, g2_ref, *, wa):
    i = pl.program_id(0)
    has_prev = (i > 0).astype(F32)
    has_next = (i < pl.num_programs(0) - 1).astype(F32)
    last = SUBLANES - 1

    def sec(k):
        sl = slice(k * wa, (k + 1) * wa)
        return (m_ref[:, sl], pv_ref[last:last + 1, sl] * has_prev, nx_ref[0:1, sl] * has_next)

    xa, xa_p, xa_n = sec(0)
    ba, _, _ = sec(1)
    ca, ca_p, ca_n = sec(2)
    a_ref[...] = (ba * _conv3(ca * xa, ca_p * xa_p, ca_n * xa_n, ca_w_ref[...])).astype(BF16)
    for k, o_ref in enumerate((z_ref, g1_ref, g2_ref)):
        x, x_p, x_n = sec(3 + k)
        _store_groups(o_ref, _conv3(x, x_p, x_n, cb_w_ref[k]))


def _store_groups(o_ref, x):
    for g in range(o_ref.shape[0]):
        o_ref[g] = x[:, g * LANES:(g + 1) * LANES]


def _load_groups(ref):
    return jnp.concatenate([ref[g] for g in range(ref.shape[0])], axis=1)


def _prep(p, conv_a, conv_b, *, wa, tm=256):
    L = p.shape[0]
    width = 6 * wa
    rb = tm // SUBLANES
    nrb = L // SUBLANES
    grouped = pl.BlockSpec((wa // LANES, tm, LANES), lambda i: (0, i, 0))
    return pl.pallas_call(
        functools.partial(_prep_body, wa=wa),
        out_shape=(jax.ShapeDtypeStruct((L, wa), BF16),)
        + (jax.ShapeDtypeStruct((wa // LANES, L, LANES), F32),) * 3,
        grid=(L // tm,),
        in_specs=[
            pl.BlockSpec((tm, width), lambda i: (i, 0)),
            pl.BlockSpec((SUBLANES, width), lambda i: (jnp.maximum(i * rb - 1, 0), 0)),
            pl.BlockSpec((SUBLANES, width), lambda i: (jnp.minimum((i + 1) * rb, nrb - 1), 0)),
            pl.BlockSpec((3, wa), lambda i: (0, 0)),
            pl.BlockSpec((3, 3, wa), lambda i: (0, 0, 0)),
        ],
        out_specs=(pl.BlockSpec((tm, wa), lambda i: (i, 0)), grouped, grouped, grouped),
        compiler_params=_params(("parallel",)),
        name="prep",
    )(p, p, p, conv_a, conv_b)


def _filter_body(z_ref, zt_ref, w1_ref, b1_ref, w2_ref, b2_ref, w3_ref, b3_ref, fr_ref, wo_ref, dl_ref,
                 k_ref, s_ref, *, half_tiles, tr):
    i = pl.program_id(0)
    fr = fr_ref[...]
    h = jnp.sin(fr * (_dot_hi(w1_ref[...], zt_ref[...]) + b1_ref[...]))
    h = jnp.sin(fr * (_dot_hi(w2_ref[...], h) + b2_ref[...]))
    h = jnp.sin(fr * (_dot_hi(w3_ref[...], h) + b3_ref[...]))
    t = z_ref[:, 0:1]
    decay = jnp.exp(-t * jnp.abs(dl_ref[...]))
    hd = lax.dot_general(h, wo_ref[0], _TN, preferred_element_type=F32, precision=HIGHEST) * decay

    @pl.when(i % half_tiles == 0)
    def _():
        s_ref[...] = jnp.zeros_like(s_ref)

    s_ref[0] += jnp.sum(jnp.abs(hd), axis=0, keepdims=True)
    row = lax.broadcasted_iota(jnp.int32, hd.shape, 0)
    zero_row = jnp.where(i == half_tiles, 0, -1)
    _store_groups(k_ref, jnp.where(row == zero_row, 0.0, hd))


def _filters(z_ext, w1, b1, w2, b2, w3, b3, freq, w_out2, deltas2, *, tr=512):
    n_rows, zw = z_ext.shape
    hid = w2.shape[0]
    width = w_out2.shape[2]
    half_tiles = n_rows // (2 * tr)
    full = lambda shape: pl.BlockSpec(shape, lambda i: (0,) * len(shape))
    colv = lambda v: v.reshape(hid, 1)
    return pl.pallas_call(
        functools.partial(_filter_body, half_tiles=half_tiles, tr=tr),
        out_shape=(jax.ShapeDtypeStruct((width // LANES, n_rows, LANES), F32),
                   jax.ShapeDtypeStruct((2, 1, width), F32)),
        grid=(n_rows // tr,),
        in_specs=[
            pl.BlockSpec((tr, zw), lambda i: (i, 0)),
            pl.BlockSpec((zw, tr), lambda i: (0, i)),
            full((hid, zw)), full((hid, 1)), full((hid, hid)), full((hid, 1)),
            full((hid, hid)), full((hid, 1)), full((hid, 1)),
            pl.BlockSpec((1, hid, width), lambda i: (i // half_tiles, 0, 0)),
            full((1, width)),
        ],
        out_specs=(pl.BlockSpec((width // LANES, tr, LANES), lambda i: (0, i, 0)),
                   pl.BlockSpec((1, 1, width), lambda i: (i // half_tiles, 0, 0))),
        compiler_params=_params(("arbitrary",)),
        name="hyena_filters",
    )(z_ext, z_ext.T, w1.T, colv(b1), w2.T, colv(b2), w3.T, colv(b3), colv(freq), w_out2, deltas2)


def _dft_consts(n1, n2):
    n = n1 * n2
    k = np.arange(n1)
    ang1 = 2.0 * np.pi * ((k[:, None] * k[None, :]) % n1) / n1
    c1, s1 = np.cos(ang1), np.sin(ang1)
    m = np.arange(n2)
    ang2 = 2.0 * np.pi * ((m[:, None] * m[None, :]) % n2) / n2
    c2, s2 = np.cos(ang2), np.sin(ang2)
    angt = 2.0 * np.pi * (k[:, None] * m[None, :]) / n
    as_bf = lambda a: jnp.asarray(a, F32).astype(BF16)
    return dict(
        fwd1=as_bf(np.concatenate([c1, -s1], axis=0)),
        inv1c=as_bf(c1[: n1 // 2] / n), inv1s=as_bf(-s1[: n1 // 2] / n),
        fwd2=as_bf(np.block([[c2, s2], [-s2, c2]])),
        inv2=as_bf(np.block([[c2, -s2], [s2, c2]])),
        tw_c=jnp.asarray(np.cos(angt), F32), tw_s=jnp.asarray(np.sin(angt), F32),
    )


def _flatten_block(ref, buf):
    groups, n, t2, lanes = ref.shape
    for g in range(groups):
        buf[g] = ref[g].reshape(n * t2, lanes)


def _unflatten_block(buf, ref):
    groups, n, t2, lanes = ref.shape
    for g in range(groups):
        ref[g] = buf[g].reshape(n, t2, lanes)


def _load_strided(buf, s, t2):
    n = buf.shape[1] // t2
    return jnp.concatenate([buf[g, pl.ds(s, n, stride=t2), :] for g in range(buf.shape[0])], axis=1)


def _store_strided(buf, s, t2, x):
    n = buf.shape[1] // t2
    for g in range(buf.shape[0]):
        buf[g, pl.ds(s, n, stride=t2), :] = x[:, g * LANES:(g + 1) * LANES]


def _flat_scratch(groups, n, t2):
    return pltpu.VMEM((groups, n * t2, LANES), F32)


def _outer_fwd_body(x_ref, f_ref, *rest, scaled):
    if scaled:
        s_ref, ar_ref, ai_ref, xb, arb, aib = rest
        inv = 1.0 / (s_ref[...] + EPS)
    else:
        ar_ref, ai_ref, xb, arb, aib = rest
    rows, t2 = x_ref.shape[1:3]
    n1 = ar_ref.shape[1]
    f = f_ref[...]
    _flatten_block(x_ref, xb)
    for s in range(t2):
        x = _load_strided(xb, s, t2)
        if scaled:
            half = rows // 2
            x = jnp.concatenate([x[:half] * inv[0], x[half:] * inv[1]], axis=0)
        a = _dot(f, x.astype(BF16))
        _store_strided(arb, s, t2, a[:n1])
        _store_strided(aib, s, t2, a[n1:])
    _unflatten_block(arb, ar_ref)
    _unflatten_block(aib, ai_ref)


def _outer_fwd(x4, fwd1, sums=None, *, t2=SUBLANES, gt=4):
    groups, rows, n2, _ = x4.shape
    n1 = fwd1.shape[1]
    scaled = sums is not None
    blk = lambda r: pl.BlockSpec((gt, r, t2, LANES), lambda j, c: (c, 0, j, 0))
    args = [x4, fwd1[:, :rows]]
    in_specs = [blk(rows), pl.BlockSpec((2 * n1, rows), lambda j, c: (0, 0))]
    if scaled:
        args.append(sums)
        in_specs.append(pl.BlockSpec((2, 1, gt * LANES), lambda j, c: (0, 0, c)))
    return pl.pallas_call(
        functools.partial(_outer_fwd_body, scaled=scaled),
        out_shape=(jax.ShapeDtypeStruct((groups, n1, n2, LANES), F32),) * 2,
        grid=(n2 // t2, groups // gt),
        in_specs=in_specs,
        out_specs=(blk(n1),) * 2,
        scratch_shapes=[_flat_scratch(gt, rows, t2), _flat_scratch(gt, n1, t2), _flat_scratch(gt, n1, t2)],
        compiler_params=_params(("parallel", "parallel")),
        name="dft_outer_fwd",
    )(*args)


def _lane_tile(t, width):
    return jnp.concatenate([t] * (width // LANES), axis=1) if width > LANES else t


def _load_slab(ref, s):
    return jnp.concatenate([ref[g, s] for g in range(ref.shape[0])], axis=1)


def _store_slab(ref, s, x):
    for g in range(ref.shape[0]):
        ref[g, s] = x[:, g * LANES:(g + 1) * LANES]


def _twiddled(ar, ai, tc, ts):
    return ar * tc + ai * ts, ai * tc - ar * ts


def _inner_fwd(ar, ai, tc, ts, f2):
    n2 = ar.shape[0]
    br, bi = _twiddled(ar, ai, tc, ts)
    x = _dot(f2, jnp.concatenate([br, bi], axis=0).astype(BF16))
    return x[:n2], x[n2:]


def _mid_spec_body(ar_ref, ai_ref, tc_ref, ts_ref, f2_ref, xr_ref, xi_ref):
    width = ar_ref.shape[0] * LANES
    for s in range(ar_ref.shape[1]):
        tc, ts = _lane_tile(tc_ref[s], width), _lane_tile(ts_ref[s], width)
        xr, xi = _inner_fwd(_load_slab(ar_ref, s), _load_slab(ai_ref, s), tc, ts, f2_ref[...])
        _store_slab(xr_ref, s, xr)
        _store_slab(xi_ref, s, xi)


def _mid_conv_body(ar_ref, ai_ref, kr_ref, ki_ref, tc_ref, ts_ref, f2_ref, g2_ref, dr_ref, di_ref):
    width = ar_ref.shape[0] * LANES
    n2 = ar_ref.shape[2]
    for s in range(ar_ref.shape[1]):
        tc, ts = _lane_tile(tc_ref[s], width), _lane_tile(ts_ref[s], width)
        xr, xi = _inner_fwd(_load_slab(ar_ref, s), _load_slab(ai_ref, s), tc, ts, f2_ref[...])
        kr, ki = _load_slab(kr_ref, s), _load_slab(ki_ref, s)
        yr = xr * kr - xi * ki
        yi = xr * ki + xi * kr
        c = _dot(g2_ref[...], jnp.concatenate([yr, yi], axis=0).astype(BF16))
        cr, ci = c[:n2], c[n2:]
        _store_slab(dr_ref, s, cr * tc - ci * ts)
        _store_slab(di_ref, s, ci * tc + cr * ts)


def _mid(ar, ai, tw_c, tw_s, fwd2, inv2=None, kr=None, ki=None, order=0, *, kb=4, gt=4):
    groups, n1, n2, _ = ar.shape
    conv = kr is not None
    slab = pl.BlockSpec((gt, kb, n2, LANES), lambda i, q: (q, i, 0, 0))
    tw = pl.BlockSpec((kb, n2, LANES), lambda i, q: (i, 0, 0))
    mat = pl.BlockSpec((2 * n2, 2 * n2), lambda i, q: (0, 0))
    if conv:
        kslab = pl.BlockSpec((gt, kb, n2, LANES), lambda i, q: (order * (groups // gt) + q, i, 0, 0))
        args = (ar, ai, kr, ki, tw_c, tw_s, fwd2, inv2)
        in_specs = [slab, slab, kslab, kslab, tw, tw, mat, mat]
        body, name = _mid_conv_body, "hyena_mid_conv"
    else:
        args = (ar, ai, tw_c, tw_s, fwd2)
        in_specs = [slab, slab, tw, tw, mat]
        body, name = _mid_spec_body, "hyena_mid_spec"
    return pl.pallas_call(
        body,
        out_shape=(jax.ShapeDtypeStruct((groups, n1, n2, LANES), F32),) * 2,
        grid=(n1 // kb, groups // gt),
        in_specs=in_specs,
        out_specs=(slab, slab),
        compiler_params=_params(("parallel", "parallel")),
        name=name,
    )(*args)


def _outer_inv_body(dr_ref, di_ref, c_ref, s_ref, gate_ref, z_ref, skip_ref, o_ref, drb, dib, gb, zb, ob):
    c, sm, skip = c_ref[...], s_ref[...], skip_ref[...]
    t2 = dr_ref.shape[2]
    for ref, buf in ((dr_ref, drb), (di_ref, dib), (gate_ref, gb), (z_ref, zb)):
        _flatten_block(ref, buf)
    for s in range(t2):
        y = (_dot(c, _load_strided(drb, s, t2).astype(BF16))
             + _dot(sm, _load_strided(dib, s, t2).astype(BF16)))
        _store_strided(ob, s, t2, _load_strided(gb, s, t2) * (y + skip * _load_strided(zb, s, t2)))
    _unflatten_block(ob, o_ref)


def _outer_inv(dr, di, inv1c, inv1s, gate4, z4, skip, *, t2=SUBLANES):
    groups, n1, n2, _ = dr.shape
    half = n1 // 2
    blk = lambda r: pl.BlockSpec((groups, r, t2, LANES), lambda j: (0, 0, j, 0))
    mat = pl.BlockSpec((half, n1), lambda j: (0, 0))
    return pl.pallas_call(
        _outer_inv_body,
        out_shape=jax.ShapeDtypeStruct((groups, half, n2, LANES), F32),
        grid=(n2 // t2,),
        in_specs=[blk(n1), blk(n1), mat, mat, blk(half), blk(half),
                  pl.BlockSpec((1, groups * LANES), lambda j: (0, 0))],
        out_specs=blk(half),
        scratch_shapes=[_flat_scratch(groups, n1, t2), _flat_scratch(groups, n1, t2)]
        + [_flat_scratch(groups, half, t2)] * 3,
        compiler_params=_params(("parallel",)),
        name="dft_outer_inv",
    )(dr, di, inv1c, inv1s, gate4, z4, skip)


def _log_sigmoid(x):
    return jnp.minimum(x, 0.0) - jnp.log1p(jnp.exp(-jnp.abs(x)))


_NT = (((1,), (1,)), ((), ()))
_TN = (((0,), (0,)), ((), ()))


def _gla_direction(q_ref, k_ref, v_ref, lr_ref, w, b, cum, o_ref, st_ref, *, reverse, scale):
    tb, dk = q_ref.shape
    sub = cum.shape[0]
    chunk = GLA_CHUNK
    cum_bf = cum.astype(BF16)
    qes, intras, upd, dec = [], [], [], []
    for s in range(tb // sub):
        rs = slice(s * sub, (s + 1) * sub)
        g = _log_sigmoid(_dot(lr_ref[rs, :].astype(BF16), w) + b) / GLA_TAU
        g_hi = g.astype(BF16)
        g_lo = (g - g_hi.astype(F32)).astype(BF16)
        bc = _dot(cum_bf, g_hi) + _dot(cum_bf, g_lo)
        k = k_ref[rs, :]
        v = v_ref[rs, :].astype(BF16)
        qe = (q_ref[rs, :] * scale * jnp.exp(bc)).astype(BF16)
        ke = (k * jnp.exp(-bc)).astype(BF16)
        a = lax.dot_general(qe, ke, _NT, preferred_element_type=F32)
        intra = _dot(jnp.where(cum > 0.5, a, 0.0).astype(BF16), v)
        for c in range(sub // chunk):
            sl = slice(c * chunk, (c + 1) * chunk)
            bl = bc[c * chunk:c * chunk + 1] if reverse else bc[(c + 1) * chunk - 1:(c + 1) * chunk]
            kd = (k[sl] * jnp.exp(bl - bc[sl])).astype(BF16)
            upd.append(lax.dot_general(v[sl], kd, _TN, preferred_element_type=F32))
            dec.append(jnp.exp(bl))
            qes.append(qe[sl])
            intras.append(intra[sl])
    st = st_ref[...]
    chunks = range(tb // chunk)
    for c in (reversed(chunks) if reverse else chunks):
        o_ref[c * chunk:(c + 1) * chunk, :] = intras[c] + lax.dot_general(
            qes[c], st.astype(BF16), _NT, preferred_element_type=F32)
        st = st * dec[c] + upd[c]
    st_ref[...] = st


def _gla_body(qf, kf, vf, lrf, qb, kb, vb, lrb, w_ref, b_ref, cum_ref, of_ref, ob_ref, sf_ref, sb_ref, *, scale):
    @pl.when(pl.program_id(1) == 0)
    def _():
        sf_ref[...] = jnp.zeros_like(sf_ref)
        sb_ref[...] = jnp.zeros_like(sb_ref)

    _gla_direction(qf, kf, vf, lrf, w_ref[0, 0], b_ref[0, 0], cum_ref[0], of_ref, sf_ref,
                   reverse=False, scale=scale)
    _gla_direction(qb, kb, vb, lrb, w_ref[0, 1], b_ref[0, 1], cum_ref[1], ob_ref, sb_ref,
                   reverse=True, scale=scale)


def _chunk_cumsum_matrices(sub):
    i = np.arange(sub)
    same = (i[:, None] // GLA_CHUNK) == (i[None, :] // GLA_CHUNK)
    fwd = same & (i[None, :] <= i[:, None])
    bwd = same & (i[None, :] >= i[:, None])
    return jnp.asarray(np.stack([fwd, bwd]), F32)


def _gla(p, lr, w_pad, b_pad, *, q_off, k_off, v_off, dk, dv, tb=512, sub=256):
    L = p.shape[0]
    tb = min(tb, L)
    nb = L // tb
    heads = w_pad.shape[0]
    fwd = lambda width, off: pl.BlockSpec((tb, width), lambda h, i: (i, off // width + h))
    bwd = lambda width, off: pl.BlockSpec((tb, width), lambda h, i: (nb - 1 - i, off // width + h))
    return pl.pallas_call(
        functools.partial(_gla_body, scale=dk ** -0.5),
        out_shape=(jax.ShapeDtypeStruct((L, heads * dv), F32),) * 2,
        grid=(heads, nb),
        in_specs=[
            fwd(dk, q_off), fwd(dk, k_off), fwd(dv, v_off), pl.BlockSpec((tb, LANES), lambda h, i: (i, 0)),
            bwd(dk, q_off), bwd(dk, k_off), bwd(dv, v_off), pl.BlockSpec((tb, LANES), lambda h, i: (nb - 1 - i, 0)),
            pl.BlockSpec((1, 2, LANES, dk), lambda h, i: (h, 0, 0, 0)),
            pl.BlockSpec((1, 2, 1, dk), lambda h, i: (h, 0, 0, 0)),
            pl.BlockSpec((2, sub, sub), lambda h, i: (0, 0, 0)),
        ],
        out_specs=(pl.BlockSpec((tb, dv), lambda h, i: (i, h)),
                   pl.BlockSpec((tb, dv), lambda h, i: (nb - 1 - i, h))),
        scratch_shapes=[pltpu.VMEM((dv, dk), F32), pltpu.VMEM((dv, dk), F32)],
        compiler_params=_params(("parallel", "arbitrary")),
        name="gla",
    )(p, p, p, lr, p, p, p, lr, w_pad.astype(BF16), b_pad, _chunk_cumsum_matrices(sub))


def _gla_post_body(of_ref, ob_ref, r_ref, gn_ref, c_ref, *, dv):
    o = of_ref[...] + ob_ref[...]
    r = r_ref[...]
    gate = r * jax.nn.sigmoid(r)
    for hd in range(o.shape[1] // dv):
        sl = slice(hd * dv, (hd + 1) * dv)
        c_ref[:, sl] = (_rms(o[:, sl], gn_ref[...]) * gate[:, sl]).astype(BF16)


def _gla_post(o_f, o_b, p, gla_norm, *, r_off, tm=512):
    L, wc = o_f.shape
    dv = gla_norm.shape[0]
    row = lambda blk=0: pl.BlockSpec((tm, wc), lambda i: (i, blk))
    return pl.pallas_call(
        functools.partial(_gla_post_body, dv=dv),
        out_shape=jax.ShapeDtypeStruct((L, wc), BF16),
        grid=(L // tm,),
        in_specs=[row(), row(), row(r_off // wc), pl.BlockSpec((1, dv), lambda i: (0, 0))],
        out_specs=row(),
        compiler_params=_params(("parallel",)),
        name="gla_post",
    )(o_f, o_b, p, gla_norm.reshape(1, dv))


def _merge_body(h_ref, g_ref, a_ref, z_ref, c_ref,
                wga_ref, wgb_ref, wgc_ref, bga_ref, bgb_ref, bgc_ref,
                wa_ref, wb_ref, wc_ref, wo_ref, o_ref, u_ref, zb_ref):
    j = pl.program_id(1)

    @pl.when(j == 0)
    def _():
        h = h_ref[...]
        u_ref[...] = _rms(h, g_ref[...]).astype(BF16)
        zb_ref[...] = _load_groups(z_ref).astype(BF16)
        o_ref[...] = h

    u = u_ref[...]
    m = jax.nn.sigmoid(_dot(u, wga_ref[...]) + bga_ref[...]) * _dot(a_ref[...], wa_ref[...])
    m += jax.nn.sigmoid(_dot(u, wgb_ref[...]) + bgb_ref[...]) * _dot(zb_ref[...], wb_ref[...])
    m += jax.nn.sigmoid(_dot(u, wgc_ref[...]) + bgc_ref[...]) * _dot(c_ref[...], wc_ref[...])
    o_ref[...] += _dot(m.astype(BF16), wo_ref[...])


def _merge(h, norm_g, a_pre, z, c_pre, w_gate, b_gate, w_a, w_b, w_c, w_o, *, tm=1024, tn=256):
    L, D = h.shape
    wa, wb, wc = w_a.shape[0], w_b.shape[0], w_c.shape[0]
    tm = min(tm, L)
    nj = D // tn
    row = lambda width: pl.BlockSpec((tm, width), lambda i, j: (i, 0))
    gate_w = lambda k: pl.BlockSpec((D, tn), lambda i, j: (0, j + k * nj))
    gate_b = lambda k: pl.BlockSpec((1, tn), lambda i, j: (0, j + k * nj))
    br_w = lambda width: pl.BlockSpec((width, tn), lambda i, j: (0, j))
    return pl.pallas_call(
        _merge_body,
        out_shape=jax.ShapeDtypeStruct((L, D), F32),
        grid=(L // tm, nj),
        in_specs=[
            pl.BlockSpec((tm, D), lambda i, j: (i, 0), pipeline_mode=pl.Buffered(1)),
            pl.BlockSpec((1, D), lambda i, j: (0, 0)),
            row(wa), pl.BlockSpec((wb // LANES, tm, LANES), lambda i, j: (0, i, 0)), row(wc),
            gate_w(0), gate_w(1), gate_w(2), gate_b(0), gate_b(1), gate_b(2),
            br_w(wa), br_w(wb), br_w(wc),
            pl.BlockSpec((tn, D), lambda i, j: (j, 0)),
        ],
        out_specs=row(D),
        scratch_shapes=[pltpu.VMEM((tm, D), BF16), pltpu.VMEM((tm, wb), BF16)],
        compiler_params=_params(("parallel", "arbitrary")),
        name="merge",
    )(h, norm_g.reshape(1, D), a_pre, z, c_pre,
      w_gate, w_gate, w_gate, b_gate, b_gate, b_gate, w_a, w_b, w_c, w_o)


def _position_features(L):
    pos = jnp.arange(L, dtype=F32)
    t = pos / max(L - 1, 1)
    bands = jnp.linspace(1e-4, HY_BANDS - 1, HY_BANDS, dtype=F32)
    ang = (2.0 * math.pi / L) * pos[:, None] * bands[None, :]
    z = jnp.concatenate([t[:, None], jnp.cos(ang), -jnp.sin(ang)], axis=-1)
    z_ext = jnp.concatenate([z, z[:1], z[:0:-1]], axis=0)
    return jnp.pad(z_ext, ((0, 0), (0, LANES - z.shape[1])))


def _hyena(z0, g1, g2, skip, filt, consts, tw_c, tw_s):
    groups, L, _ = z0.shape
    n1 = DFT_N1
    n2 = 2 * L // n1
    k_time, sums = filt
    fr, fi = _outer_fwd(k_time.reshape(k_time.shape[0], n1, n2, LANES), consts["fwd1"], sums)
    kr, ki = _mid(fr, fi, tw_c, tw_s, consts["fwd2"])
    z = z0.reshape(groups, n1 // 2, n2, LANES)
    for order, gate in enumerate((g1, g2)):
        ar, ai = _outer_fwd(z, consts["fwd1"])
        dr, di = _mid(ar, ai, tw_c, tw_s, consts["fwd2"], consts["inv2"], kr, ki, order)
        z = _outer_inv(dr, di, consts["inv1c"], consts["inv1s"], gate.reshape(z.shape), z,
                       skip[order].reshape(1, groups * LANES))
    return z.reshape(groups, L, LANES)


def kernel(x, ffn1_norm, ffn1_w_gu, ffn1_w_down, mix_norm, w_in, b_in, conv_a, conv_b, hf_w1, hf_b1, hf_w2, hf_b2, hf_w3, hf_b3, hf_w_out, hf_freq, hy_skip, gk_w2, gk_b, gla_norm, w_br_a, w_br_b, w_br_c, w_o, ffn2_norm, ffn2_w_gu, ffn2_w_down, final_norm):
    bsz, L, D = x.shape
    depth = ffn1_norm.shape[0]
    wa = conv_a.shape[2]
    wb = conv_b.shape[3]
    wck = gk_w2.shape[3]
    dv = gla_norm.shape[1]
    wcv = GLA_HEADS * dv
    dk = wck // GLA_HEADS
    hid = hf_w2.shape[1]
    n_main = 3 * wa + 3 * wb + 2 * wck + 2 * wcv
    q_off = 3 * wa + 3 * wb
    k_off, v_off, r_off = q_off + wck, q_off + 2 * wck, q_off + 2 * wck + wcv
    lr_off = n_main
    gate_off = n_main + 2 * GLA_RANK
    n1 = DFT_N1
    n2 = 2 * L // n1

    consts = _dft_consts(n1, n2)
    tw_c = jnp.broadcast_to(consts["tw_c"][:, :, None], (n1, n2, LANES))
    tw_s = jnp.broadcast_to(consts["tw_s"][:, :, None], (n1, n2, LANES))
    z_ext = _position_features(L)
    deltas = jnp.linspace(math.log(HY_TARGET) / HY_SLOW_DECAY, math.log(HY_TARGET) / HY_FAST_DECAY, wb, dtype=F32)
    deltas2 = jnp.tile(deltas.reshape(1, wb), (1, HY_ORDER))

    outs = []
    for bi in range(bsz):
        h = x[bi]
        for l in range(depth):
            h = _ffn(h, ffn1_norm[l], _to_bf16(ffn1_w_gu, l), _to_bf16(ffn1_w_down, l),
                     final_norm, final=False)

            w_lr = jnp.pad(w_in[l][:, lr_off:gate_off], ((0, 0), (0, LANES - 2 * GLA_RANK))).astype(BF16)
            b_lr = jnp.pad(b_in[l][lr_off:gate_off], (0, LANES - 2 * GLA_RANK)).reshape(1, LANES)
            p, lr = _proj(h, mix_norm[l], _to_bf16(w_in, l, n_main), b_in[l][:n_main].reshape(1, n_main),
                          w_lr, b_lr)

            a_pre, z0, g1, g2 = _prep(p, conv_a[l], conv_b[l], wa=wa)

            w_out2 = hf_w_out[l].reshape(hid, HY_ORDER, 2, wb).transpose(2, 0, 1, 3).reshape(2, hid, HY_ORDER * wb)
            w1 = jnp.pad(hf_w1[l], ((0, LANES - hf_w1.shape[1]), (0, 0)))
            filt = _filters(z_ext, w1, hf_b1[l], hf_w2[l], hf_b2[l], hf_w3[l], hf_b3[l], hf_freq[l],
                            w_out2, deltas2)
            zb = _hyena(z0, g1, g2, hy_skip[l], filt, consts, tw_c, tw_s)

            gw = gk_w2[l].reshape(2, GLA_RANK, GLA_HEADS, dk).transpose(2, 0, 1, 3)
            w_pad = jnp.zeros((GLA_HEADS, 2, LANES, dk), F32)
            w_pad = w_pad.at[:, 0, :GLA_RANK].set(gw[:, 0]).at[:, 1, GLA_RANK:2 * GLA_RANK].set(gw[:, 1])
            b_pad = gk_b[l].reshape(2, GLA_HEADS, 1, dk).transpose(1, 0, 2, 3)
            o_f, o_b = _gla(p, lr, w_pad, b_pad, q_off=q_off, k_off=k_off, v_off=v_off, dk=dk, dv=dv)

            c_pre = _gla_post(o_f, o_b, p, gla_norm[l], r_off=r_off)

            h = _merge(h, mix_norm[l], a_pre, zb, c_pre,
                       w_in[l][:, gate_off:].astype(BF16), b_in[l][gate_off:].reshape(1, 3 * D),
                       _to_bf16(w_br_a, l), _to_bf16(w_br_b, l), _to_bf16(w_br_c, l),
                       _to_bf16(w_o, l))

            h = _ffn(h, ffn2_norm[l], _to_bf16(ffn2_w_gu, l), _to_bf16(ffn2_w_down, l),
                     final_norm, final=(l == depth - 1))
        outs.append(h)
    return jnp.stack(outs, axis=0)
```

```python
import functools
import math

import numpy as np
import jax
import jax.numpy as jnp
from jax import lax
from jax.experimental import pallas as pl
from jax.experimental.pallas import tpu as pltpu

F32 = jnp.float32
BF16 = jnp.bfloat16
HIGHEST = lax.Precision.HIGHEST

EPS = 1e-5
GLA_HEADS = 4
GLA_RANK = 16
GLA_TAU = 16.0
GLA_CHUNK = 64
HY_ORDER = 2
HY_BANDS = 16
HY_FAST_DECAY = 0.3
HY_SLOW_DECAY = 1.5
HY_TARGET = 1e-2

LANES = 128
SUBLANES = 8
DFT_N1 = 256
VMEM_LIMIT = 56 * 1024 * 1024


def _params(sem):
    return pltpu.CompilerParams(dimension_semantics=sem, vmem_limit_bytes=VMEM_LIMIT)


def _rms(x, g):
    return x * lax.rsqrt(jnp.mean(x * x, axis=-1, keepdims=True) + EPS) * g


def _dot(a, b):
    return jnp.dot(a, b, preferred_element_type=F32)


def _dot_hi(a, b):
    return jnp.dot(a, b, preferred_element_type=F32, precision=HIGHEST)


def _cast_body(x_ref, o_ref):
    o_ref[...] = x_ref[...].astype(o_ref.dtype)


def _to_bf16(w, layer, cols=None, *, block_bytes=4 << 20):
    _, rows, width = w.shape
    cols = width if cols is None else cols
    tr = rows
    while tr * cols * 4 > block_bytes and tr % 32 == 0:
        tr //= 2
    return pl.pallas_call(
        _cast_body,
        out_shape=jax.ShapeDtypeStruct((rows, cols), BF16),
        grid=(rows // tr,),
        in_specs=[pl.BlockSpec((None, tr, cols), lambda i: (layer, i, 0))],
        out_specs=pl.BlockSpec((tr, cols), lambda i: (i, 0)),
        compiler_params=_params(("parallel",)),
        name="to_bf16",
    )(w)


def _ffn_body(x_ref, g_ref, wg_ref, wu_ref, wd_ref, fn_ref, o_ref, xn_ref, *, final):
    j = pl.program_id(1)

    @pl.when(j == 0)
    def _():
        xn_ref[...] = _rms(x_ref[...], g_ref[...]).astype(BF16)
        o_ref[...] = jnp.zeros_like(o_ref)

    xn = xn_ref[...]
    g = _dot(xn, wg_ref[...])
    u = _dot(xn, wu_ref[...])
    h = (g * jax.nn.sigmoid(g) * u).astype(BF16)
    o_ref[...] += _dot(h, wd_ref[...])

    @pl.when(j == pl.num_programs(1) - 1)
    def _():
        y = x_ref[...] + 0.5 * o_ref[...]
        if final:
            y = _rms(y, fn_ref[...])
        o_ref[...] = y


def _ffn(x, norm_g, w_gu, w_down, final_g, *, final, tm=1024, tf=256):
    L, D = x.shape
    F = w_down.shape[0]
    tm = min(tm, L)
    nf = F // tf
    return pl.pallas_call(
        functools.partial(_ffn_body, final=final),
        out_shape=jax.ShapeDtypeStruct((L, D), F32),
        grid=(L // tm, nf),
        in_specs=[
            pl.BlockSpec((tm, D), lambda i, j: (i, 0)),
            pl.BlockSpec((1, D), lambda i, j: (0, 0)),
            pl.BlockSpec((D, tf), lambda i, j: (0, j)),
            pl.BlockSpec((D, tf), lambda i, j: (0, j + nf)),
            pl.BlockSpec((tf, D), lambda i, j: (j, 0)),
            pl.BlockSpec((1, D), lambda i, j: (0, 0)),
        ],
        out_specs=pl.BlockSpec((tm, D), lambda i, j: (i, 0)),
        scratch_shapes=[pltpu.VMEM((tm, D), BF16)],
        compiler_params=_params(("parallel", "arbitrary")),
        name="ffn",
    )(x, norm_g.reshape(1, D), w_gu, w_gu, w_down, final_g.reshape(1, D))


def _proj_body(h_ref, g_ref, w_ref, b_ref, wlr_ref, blr_ref, p_ref, lr_ref, *, tn):
    xn = _rms(h_ref[...], g_ref[...]).astype(BF16)
    lr_ref[...] = _dot(xn, wlr_ref[...]) + blr_ref[...]
    for c in range(p_ref.shape[1] // tn):
        sl = slice(c * tn, (c + 1) * tn)
        p_ref[:, sl] = _dot(xn, w_ref[:, sl]) + b_ref[:, sl]


def _proj(h, norm_g, w, b, w_lr, b_lr, *, tm=256, tn=512):
    L, D = h.shape
    n_out = w.shape[1]
    resident = lambda shape: pl.BlockSpec(shape, lambda i: (0, 0), pipeline_mode=pl.Buffered(1))
    return pl.pallas_call(
        functools.partial(_proj_body, tn=tn),
        out_shape=(jax.ShapeDtypeStruct((L, n_out), F32), jax.ShapeDtypeStruct((L, LANES), F32)),
        grid=(L // tm,),
        in_specs=[
            pl.BlockSpec((tm, D), lambda i: (i, 0)),
            resident((1, D)), resident((D, n_out)), resident((1, n_out)),
            resident((D, LANES)), resident((1, LANES)),
        ],
        out_specs=(pl.BlockSpec((tm, n_out), lambda i: (i, 0)),
                   pl.BlockSpec((tm, LANES), lambda i: (i, 0))),
        compiler_params=_params(("parallel",)),
        name="proj",
    )(h, norm_g.reshape(1, D), w, b, w_lr, b_lr)


def _conv3(x, x_before, x_after, w):
    tm = x.shape[0]
    row = lax.broadcasted_iota(jnp.int32, x.shape, 0)
    down = jnp.where(row == 0, x_before, pltpu.roll(x, 1, axis=0))
    up = jnp.where(row == tm - 1, x_after, pltpu.roll(x, tm - 1, axis=0))
    return down * w[0:1] + x * w[1:2] + up * w[2:3]


def _prep_body(m_ref, pv_ref, nx_ref, ca_w_ref, cb_w_ref, a_ref, z_ref, g1_ref, g2_ref, *, wa):
    i = pl.program_id(0)
    has_prev = (i > 0).astype(F32)
    has_next = (i < pl.num_programs(0) - 1).astype(F32)
    last = SUBLANES - 1

    def sec(k):
        sl = slice(k * wa, (k + 1) * wa)
        return (m_ref[:, sl], pv_ref[last:last + 1, sl] * has_prev, nx_ref[0:1, sl] * has_next)

    xa, xa_p, xa_n = sec(0)
    ba, _, _ = sec(1)
    ca, ca_p, ca_n = sec(2)
    a_ref[...] = (ba * _conv3(ca * xa, ca_p * xa_p, ca_n * xa_n, ca_w_ref[...])).astype(BF16)
    for k, o_ref in enumerate((z_ref, g1_ref, g2_ref)):
        x, x_p, x_n = sec(3 + k)
        _store_groups(o_ref, _conv3(x, x_p, x_n, cb_w_ref[k]))


def _store_groups(o_ref, x):
    for g in range(o_ref.shape[0]):
        o_ref[g] = x[:, g * LANES:(g + 1) * LANES]


def _load_groups(ref):
    return jnp.concatenate([ref[g] for g in range(ref.shape[0])], axis=1)


def _prep(p, conv_a, conv_b, *, wa, tm=256):
    L = p.shape[0]
    width = 6 * wa
    rb = tm // SUBLANES
    nrb = L // SUBLANES
    grouped = pl.BlockSpec((wa // LANES, tm, LANES), lambda i: (0, i, 0))
    return pl.pallas_call(
        functools.partial(_prep_body, wa=wa),
        out_shape=(jax.ShapeDtypeStruct((L, wa), BF16),)
        + (jax.ShapeDtypeStruct((wa // LANES, L, LANES), F32),) * 3,
        grid=(L // tm,),
        in_specs=[
            pl.BlockSpec((tm, width), lambda i: (i, 0)),
            pl.BlockSpec((SUBLANES, width), lambda i: (jnp.maximum(i * rb - 1, 0), 0)),
            pl.BlockSpec((SUBLANES, width), lambda i: (jnp.minimum((i + 1) * rb, nrb - 1), 0)),
            pl.BlockSpec((3, wa), lambda i: (0, 0)),
            pl.BlockSpec((3, 3, wa), lambda i: (0, 0, 0)),
        ],
        out_specs=(pl.BlockSpec((tm, wa), lambda i: (i, 0)), grouped, grouped, grouped),
        compiler_params=_params(("parallel",)),
        name="prep",
    )(p, p, p, conv_a, conv_b)


def _filter_body(z_ref, zt_ref, w1_ref, b1_ref, w2_ref, b2_ref, w3_ref, b3_ref, fr_ref, wo_ref, dl_ref,
                 k_ref, s_ref, *, half_tiles, tr):
    i = pl.program_id(0)
    fr = fr_ref[...]
    h = jnp.sin(fr * (_dot_hi(w1_ref[...], zt_ref[...]) + b1_ref[...]))
    h = jnp.sin(fr * (_dot_hi(w2_ref[...], h) + b2_ref[...]))
    h = jnp.sin(fr * (_dot_hi(w3_ref[...], h) + b3_ref[...]))
    t = z_ref[:, 0:1]
    decay = jnp.exp(-t * jnp.abs(dl_ref[...]))
    hd = lax.dot_general(h, wo_ref[0], _TN, preferred_element_type=F32, precision=HIGHEST) * decay

    @pl.when(i % half_tiles == 0)
    def _():
        s_ref[...] = jnp.zeros_like(s_ref)

    s_ref[0] += jnp.sum(jnp.abs(hd), axis=0, keepdims=True)
    row = lax.broadcasted_iota(jnp.int32, hd.shape, 0)
    zero_row = jnp.where(i == half_tiles, 0, -1)
    _store_groups(k_ref, jnp.where(row == zero_row, 0.0, hd))


def _filters(z_ext, w1, b1, w2, b2, w3, b3, freq, w_out2, deltas2, *, tr=512):
    n_rows, zw = z_ext.shape
    hid = w2.shape[0]
    width = w_out2.shape[2]
    half_tiles = n_rows // (2 * tr)
    full = lambda shape: pl.BlockSpec(shape, lambda i: (0,) * len(shape))
    colv = lambda v: v.reshape(hid, 1)
    return pl.pallas_call(
        functools.partial(_filter_body, half_tiles=half_tiles, tr=tr),
        out_shape=(jax.ShapeDtypeStruct((width // LANES, n_rows, LANES), F32),
                   jax.ShapeDtypeStruct((2, 1, width), F32)),
        grid=(n_rows // tr,),
        in_specs=[
            pl.BlockSpec((tr, zw), lambda i: (i, 0)),
            pl.BlockSpec((zw, tr), lambda i: (0, i)),
            full((hid, zw)), full((hid, 1)), full((hid, hid)), full((hid, 1)),
            full((hid, hid)), full((hid, 1)), full((hid, 1)),
            pl.BlockSpec((1, hid, width), lambda i: (i // half_tiles, 0, 0)),
            full((1, width)),
        ],
        out_specs=(pl.BlockSpec((width // LANES, tr, LANES), lambda i: (0, i, 0)),
                   pl.BlockSpec((1, 1, width), lambda i: (i // half_tiles, 0, 0))),
        compiler_params=_params(("arbitrary",)),
        name="hyena_filters",
    )(z_ext, z_ext.T, w1.T, colv(b1), w2.T, colv(b2), w3.T, colv(b3), colv(freq), w_out2, deltas2)


def _dft_consts(n1, n2):
    n = n1 * n2
    k = np.arange(n1)
    ang1 = 2.0 * np.pi * ((k[:, None] * k[None, :]) % n1) / n1
    c1, s1 = np.cos(ang1), np.sin(ang1)
    m = np.arange(n2)
    ang2 = 2.0 * np.pi * ((m[:, None] * m[None, :]) % n2) / n2
    c2, s2 = np.cos(ang2), np.sin(ang2)
    angt = 2.0 * np.pi * (k[:, None] * m[None, :]) / n
    as_bf = lambda a: jnp.asarray(a, F32).astype(BF16)
    return dict(
        fwd1=as_bf(np.concatenate([c1, -s1], axis=0)),
        inv1c=as_bf(c1[: n1 // 2] / n), inv1s=as_bf(-s1[: n1 // 2] / n),
        fwd2=as_bf(np.block([[c2, s2], [-s2, c2]])),
        inv2=as_bf(np.block([[c2, -s2], [s2, c2]])),
        tw_c=jnp.asarray(np.cos(angt), F32), tw_s=jnp.asarray(np.sin(angt), F32),
    )


def _flatten_block(ref, buf):
    groups, n, t2, lanes = ref.shape
    for g in range(groups):
        buf[g] = ref[g].reshape(n * t2, lanes)


def _unflatten_block(buf, ref):
    groups, n, t2, lanes = ref.shape
    for g in range(groups):
        ref[g] = buf[g].reshape(n, t2, lanes)


def _load_strided(buf, s, t2):
    n = buf.shape[1] // t2
    return jnp.concatenate([buf[g, pl.ds(s, n, stride=t2), :] for g in range(buf.shape[0])], axis=1)


def _store_strided(buf, s, t2, x):
    n = buf.shape[1] // t2
    for g in range(buf.shape[0]):
        buf[g, pl.ds(s, n, stride=t2), :] = x[:, g * LANES:(g + 1) * LANES]


def _flat_scratch(groups, n, t2, dtype=F32):
    return pltpu.VMEM((groups, n * t2, LANES), dtype)


U32 = jnp.uint32


def _pack_complex(re, im):
    hi = lax.bitcast_convert_type(re.astype(BF16).astype(F32), U32)
    lo = lax.bitcast_convert_type(im.astype(BF16).astype(F32), U32)
    return hi | (lo >> 16)


def _unpack_complex(w):
    re = lax.bitcast_convert_type(w & jnp.uint32(0xFFFF0000), F32)
    im = lax.bitcast_convert_type(w << 16, F32)
    return re, im


def _outer_fwd_body(x_ref, f_ref, *rest, scaled):
    if scaled:
        s_ref, a_ref, xb, ab = rest
        inv = 1.0 / (s_ref[...] + EPS)
    else:
        a_ref, xb, ab = rest
    rows, t2 = x_ref.shape[1:3]
    n1 = a_ref.shape[1]
    f = f_ref[...]
    _flatten_block(x_ref, xb)
    for s in range(t2):
        x = _load_strided(xb, s, t2)
        if scaled:
            half = rows // 2
            x = jnp.concatenate([x[:half] * inv[0], x[half:] * inv[1]], axis=0)
        a = _dot(f, x.astype(BF16))
        _store_strided(ab, s, t2, _pack_complex(a[:n1], a[n1:]))
    _unflatten_block(ab, a_ref)


def _outer_fwd(x4, fwd1, sums=None, *, t2=SUBLANES, gt=4):
    groups, rows, n2, _ = x4.shape
    n1 = fwd1.shape[1]
    scaled = sums is not None
    blk = lambda r: pl.BlockSpec((gt, r, t2, LANES), lambda j, c: (c, 0, j, 0))
    args = [x4, fwd1[:, :rows]]
    in_specs = [blk(rows), pl.BlockSpec((2 * n1, rows), lambda j, c: (0, 0))]
    if scaled:
        args.append(sums)
        in_specs.append(pl.BlockSpec((2, 1, gt * LANES), lambda j, c: (0, 0, c)))
    return pl.pallas_call(
        functools.partial(_outer_fwd_body, scaled=scaled),
        out_shape=jax.ShapeDtypeStruct((groups, n1, n2, LANES), U32),
        grid=(n2 // t2, groups // gt),
        in_specs=in_specs,
        out_specs=blk(n1),
        scratch_shapes=[_flat_scratch(gt, rows, t2), _flat_scratch(gt, n1, t2, U32)],
        compiler_params=_params(("parallel", "parallel")),
        name="dft_outer_fwd",
    )(*args)


def _lane_tile(t, width):
    return jnp.concatenate([t] * (width // LANES), axis=1) if width > LANES else t


def _load_slab(ref, s):
    return jnp.concatenate([ref[g, s] for g in range(ref.shape[0])], axis=1)


def _store_slab(ref, s, x):
    for g in range(ref.shape[0]):
        ref[g, s] = x[:, g * LANES:(g + 1) * LANES]


def _twiddled(ar, ai, tc, ts):
    return ar * tc + ai * ts, ai * tc - ar * ts


def _inner_fwd(ar, ai, tc, ts, f2):
    n2 = ar.shape[0]
    br, bi = _twiddled(ar, ai, tc, ts)
    x = _dot(f2, jnp.concatenate([br, bi], axis=0).astype(BF16))
    return x[:n2], x[n2:]


def _mid_spec_body(a_ref, tc_ref, ts_ref, f2_ref, xr_ref, xi_ref):
    width = a_ref.shape[0] * LANES
    for s in range(a_ref.shape[1]):
        tc, ts = _lane_tile(tc_ref[s], width), _lane_tile(ts_ref[s], width)
        ar, ai = _unpack_complex(_load_slab(a_ref, s))
        xr, xi = _inner_fwd(ar, ai, tc, ts, f2_ref[...])
        _store_slab(xr_ref, s, xr.astype(xr_ref.dtype))
        _store_slab(xi_ref, s, xi.astype(xi_ref.dtype))


def _mid_conv_body(a_ref, kr_ref, ki_ref, tc_ref, ts_ref, f2_ref, g2_ref, d_ref):
    width = a_ref.shape[0] * LANES
    n2 = a_ref.shape[2]
    for s in range(a_ref.shape[1]):
        tc, ts = _lane_tile(tc_ref[s], width), _lane_tile(ts_ref[s], width)
        ar, ai = _unpack_complex(_load_slab(a_ref, s))
        xr, xi = _inner_fwd(ar, ai, tc, ts, f2_ref[...])
        kr, ki = _load_slab(kr_ref, s).astype(F32), _load_slab(ki_ref, s).astype(F32)
        yr = xr * kr - xi * ki
        yi = xr * ki + xi * kr
        c = _dot(g2_ref[...], jnp.concatenate([yr, yi], axis=0).astype(BF16))
        cr, ci = c[:n2], c[n2:]
        _store_slab(d_ref, s, _pack_complex(cr * tc - ci * ts, ci * tc + cr * ts))


def _mid(a, tw_c, tw_s, fwd2, inv2=None, kr=None, ki=None, order=0, *, kb=4, gt=4):
    groups, n1, n2, _ = a.shape
    conv = kr is not None
    slab = pl.BlockSpec((gt, kb, n2, LANES), lambda i, q: (q, i, 0, 0))
    tw = pl.BlockSpec((kb, n2, LANES), lambda i, q: (i, 0, 0))
    mat = pl.BlockSpec((2 * n2, 2 * n2), lambda i, q: (0, 0))
    if conv:
        kslab = pl.BlockSpec((gt, kb, n2, LANES), lambda i, q: (order * (groups // gt) + q, i, 0, 0))
        args = (a, kr, ki, tw_c, tw_s, fwd2, inv2)
        in_specs = [slab, kslab, kslab, tw, tw, mat, mat]
        body, name = _mid_conv_body, "hyena_mid_conv"
        out_shape, out_specs = jax.ShapeDtypeStruct(a.shape, U32), slab
    else:
        args = (a, tw_c, tw_s, fwd2)
        in_specs = [slab, tw, tw, mat]
        body, name = _mid_spec_body, "hyena_mid_spec"
        out_shape, out_specs = (jax.ShapeDtypeStruct(a.shape, BF16),) * 2, (slab, slab)
    return pl.pallas_call(
        body,
        out_shape=out_shape,
        grid=(n1 // kb, groups // gt),
        in_specs=in_specs,
        out_specs=out_specs,
        compiler_params=_params(("parallel", "parallel")),
        name=name,
    )(*args)


def _outer_inv_body(d_ref, c_ref, s_ref, gate_ref, z_ref, skip_ref, o_ref, db, gb, zb, ob):
    c, sm, skip = c_ref[...], s_ref[...], skip_ref[...]
    t2 = d_ref.shape[2]
    for ref, buf in ((d_ref, db), (gate_ref, gb), (z_ref, zb)):
        _flatten_block(ref, buf)
    for s in range(t2):
        dr, di = _unpack_complex(_load_strided(db, s, t2))
        y = _dot(c, dr.astype(BF16)) + _dot(sm, di.astype(BF16))
        _store_strided(ob, s, t2, _load_strided(gb, s, t2) * (y + skip * _load_strided(zb, s, t2)))
    _unflatten_block(ob, o_ref)


def _outer_inv(d, inv1c, inv1s, gate4, z4, skip, *, t2=SUBLANES):
    groups, n1, n2, _ = d.shape
    half = n1 // 2
    blk = lambda r: pl.BlockSpec((groups, r, t2, LANES), lambda j: (0, 0, j, 0))
    mat = pl.BlockSpec((half, n1), lambda j: (0, 0))
    return pl.pallas_call(
        _outer_inv_body,
        out_shape=jax.ShapeDtypeStruct((groups, half, n2, LANES), F32),
        grid=(n2 // t2,),
        in_specs=[blk(n1), mat, mat, blk(half), blk(half),
                  pl.BlockSpec((1, groups * LANES), lambda j: (0, 0))],
        out_specs=blk(half),
        scratch_shapes=[_flat_scratch(groups, n1, t2, U32)] + [_flat_scratch(groups, half, t2)] * 3,
        compiler_params=_params(("parallel",)),
        name="dft_outer_inv",
    )(d, inv1c, inv1s, gate4, z4, skip)


def _log_sigmoid(x):
    return jnp.minimum(x, 0.0) - jnp.log1p(jnp.exp(-jnp.abs(x)))


_NT = (((1,), (1,)), ((), ()))
_TN = (((0,), (0,)), ((), ()))


def _gla_direction(q_ref, k_ref, v_ref, lr_ref, w, b, cum, o_ref, st_ref, *, reverse, scale):
    tb, dk = q_ref.shape
    sub = cum.shape[0]
    chunk = GLA_CHUNK
    cum_bf = cum.astype(BF16)
    qes, intras, upd, dec = [], [], [], []
    for s in range(tb // sub):
        rs = slice(s * sub, (s + 1) * sub)
        g = _log_sigmoid(_dot(lr_ref[rs, :].astype(BF16), w) + b) / GLA_TAU
        g_hi = g.astype(BF16)
        g_lo = (g - g_hi.astype(F32)).astype(BF16)
        bc = _dot(cum_bf, g_hi) + _dot(cum_bf, g_lo)
        k = k_ref[rs, :]
        v = v_ref[rs, :].astype(BF16)
        qe = (q_ref[rs, :] * scale * jnp.exp(bc)).astype(BF16)
        ke = (k * jnp.exp(-bc)).astype(BF16)
        a = lax.dot_general(qe, ke, _NT, preferred_element_type=F32)
        intra = _dot(jnp.where(cum > 0.5, a, 0.0).astype(BF16), v)
        for c in range(sub // chunk):
            sl = slice(c * chunk, (c + 1) * chunk)
            bl = bc[c * chunk:c * chunk + 1] if reverse else bc[(c + 1) * chunk - 1:(c + 1) * chunk]
            kd = (k[sl] * jnp.exp(bl - bc[sl])).astype(BF16)
            upd.append(lax.dot_general(v[sl], kd, _TN, preferred_element_type=F32))
            dec.append(jnp.exp(bl))
            qes.append(qe[sl])
            intras.append(intra[sl])
    st = st_ref[...]
    chunks = range(tb // chunk)
    for c in (reversed(chunks) if reverse else chunks):
        o_ref[c * chunk:(c + 1) * chunk, :] = intras[c] + lax.dot_general(
            qes[c], st.astype(BF16), _NT, preferred_element_type=F32)
        st = st * dec[c] + upd[c]
    st_ref[...] = st


def _gla_body(qf, kf, vf, lrf, qb, kb, vb, lrb, w_ref, b_ref, cum_ref, of_ref, ob_ref, sf_ref, sb_ref, *, scale):
    @pl.when(pl.program_id(1) == 0)
    def _():
        sf_ref[...] = jnp.zeros_like(sf_ref)
        sb_ref[...] = jnp.zeros_like(sb_ref)

    _gla_direction(qf, kf, vf, lrf, w_ref[0, 0], b_ref[0, 0], cum_ref[0], of_ref, sf_ref,
                   reverse=False, scale=scale)
    _gla_direction(qb, kb, vb, lrb, w_ref[0, 1], b_ref[0, 1], cum_ref[1], ob_ref, sb_ref,
                   reverse=True, scale=scale)


def _chunk_cumsum_matrices(sub):
    i = np.arange(sub)
    same = (i[:, None] // GLA_CHUNK) == (i[None, :] // GLA_CHUNK)
    fwd = same & (i[None, :] <= i[:, None])
    bwd = same & (i[None, :] >= i[:, None])
    return jnp.asarray(np.stack([fwd, bwd]), F32)


def _gla(p, lr, w_pad, b_pad, *, q_off, k_off, v_off, dk, dv, tb=512, sub=256):
    L = p.shape[0]
    tb = min(tb, L)
    nb = L // tb
    heads = w_pad.shape[0]
    fwd = lambda width, off: pl.BlockSpec((tb, width), lambda h, i: (i, off // width + h))
    bwd = lambda width, off: pl.BlockSpec((tb, width), lambda h, i: (nb - 1 - i, off // width + h))
    return pl.pallas_call(
        functools.partial(_gla_body, scale=dk ** -0.5),
        out_shape=(jax.ShapeDtypeStruct((L, heads * dv), F32),) * 2,
        grid=(heads, nb),
        in_specs=[
            fwd(dk, q_off), fwd(dk, k_off), fwd(dv, v_off), pl.BlockSpec((tb, LANES), lambda h, i: (i, 0)),
            bwd(dk, q_off), bwd(dk, k_off), bwd(dv, v_off), pl.BlockSpec((tb, LANES), lambda h, i: (nb - 1 - i, 0)),
            pl.BlockSpec((1, 2, LANES, dk), lambda h, i: (h, 0, 0, 0)),
            pl.BlockSpec((1, 2, 1, dk), lambda h, i: (h, 0, 0, 0)),
            pl.BlockSpec((2, sub, sub), lambda h, i: (0, 0, 0)),
        ],
        out_specs=(pl.BlockSpec((tb, dv), lambda h, i: (i, h)),
                   pl.BlockSpec((tb, dv), lambda h, i: (nb - 1 - i, h))),
        scratch_shapes=[pltpu.VMEM((dv, dk), F32), pltpu.VMEM((dv, dk), F32)],
        compiler_params=_params(("parallel", "arbitrary")),
        name="gla",
    )(p, p, p, lr, p, p, p, lr, w_pad.astype(BF16), b_pad, _chunk_cumsum_matrices(sub))


def _gla_post_body(of_ref, ob_ref, r_ref, gn_ref, c_ref, *, dv):
    o = of_ref[...] + ob_ref[...]
    r = r_ref[...]
    gate = r * jax.nn.sigmoid(r)
    for hd in range(o.shape[1] // dv):
        sl = slice(hd * dv, (hd + 1) * dv)
        c_ref[:, sl] = (_rms(o[:, sl], gn_ref[...]) * gate[:, sl]).astype(BF16)


def _gla_post(o_f, o_b, p, gla_norm, *, r_off, tm=512):
    L, wc = o_f.shape
    dv = gla_norm.shape[0]
    row = lambda blk=0: pl.BlockSpec((tm, wc), lambda i: (i, blk))
    return pl.pallas_call(
        functools.partial(_gla_post_body, dv=dv),
        out_shape=jax.ShapeDtypeStruct((L, wc), BF16),
        grid=(L // tm,),
        in_specs=[row(), row(), row(r_off // wc), pl.BlockSpec((1, dv), lambda i: (0, 0))],
        out_specs=row(),
        compiler_params=_params(("parallel",)),
        name="gla_post",
    )(o_f, o_b, p, gla_norm.reshape(1, dv))


def _merge_body(h_ref, g_ref, a_ref, z_ref, c_ref,
                wga_ref, wgb_ref, wgc_ref, bga_ref, bgb_ref, bgc_ref,
                wa_ref, wb_ref, wc_ref, wo_ref, o_ref, u_ref, zb_ref):
    j = pl.program_id(1)

    @pl.when(j == 0)
    def _():
        h = h_ref[...]
        u_ref[...] = _rms(h, g_ref[...]).astype(BF16)
        zb_ref[...] = _load_groups(z_ref).astype(BF16)
        o_ref[...] = h

    u = u_ref[...]
    m = jax.nn.sigmoid(_dot(u, wga_ref[...]) + bga_ref[...]) * _dot(a_ref[...], wa_ref[...])
    m += jax.nn.sigmoid(_dot(u, wgb_ref[...]) + bgb_ref[...]) * _dot(zb_ref[...], wb_ref[...])
    m += jax.nn.sigmoid(_dot(u, wgc_ref[...]) + bgc_ref[...]) * _dot(c_ref[...], wc_ref[...])
    o_ref[...] += _dot(m.astype(BF16), wo_ref[...])


def _merge(h, norm_g, a_pre, z, c_pre, w_gate, b_gate, w_a, w_b, w_c, w_o, *, tm=1024, tn=256):
    L, D = h.shape
    wa, wb, wc = w_a.shape[0], w_b.shape[0], w_c.shape[0]
    tm = min(tm, L)
    nj = D // tn
    row = lambda width: pl.BlockSpec((tm, width), lambda i, j: (i, 0))
    gate_w = lambda k: pl.BlockSpec((D, tn), lambda i, j: (0, j + k * nj))
    gate_b = lambda k: pl.BlockSpec((1, tn), lambda i, j: (0, j + k * nj))
    br_w = lambda width: pl.BlockSpec((width, tn), lambda i, j: (0, j))
    return pl.pallas_call(
        _merge_body,
        out_shape=jax.ShapeDtypeStruct((L, D), F32),
        grid=(L // tm, nj),
        in_specs=[
            pl.BlockSpec((tm, D), lambda i, j: (i, 0), pipeline_mode=pl.Buffered(1)),
            pl.BlockSpec((1, D), lambda i, j: (0, 0)),
            row(wa), pl.BlockSpec((wb // LANES, tm, LANES), lambda i, j: (0, i, 0)), row(wc),
            gate_w(0), gate_w(1), gate_w(2), gate_b(0), gate_b(1), gate_b(2),
            br_w(wa), br_w(wb), br_w(wc),
            pl.BlockSpec((tn, D), lambda i, j: (j, 0)),
        ],
        out_specs=row(D),
        scratch_shapes=[pltpu.VMEM((tm, D), BF16), pltpu.VMEM((tm, wb), BF16)],
        compiler_params=_params(("parallel", "arbitrary")),
        name="merge",
    )(h, norm_g.reshape(1, D), a_pre, z, c_pre,
      w_gate, w_gate, w_gate, b_gate, b_gate, b_gate, w_a, w_b, w_c, w_o)


def _position_features(L):
    pos = jnp.arange(L, dtype=F32)
    t = pos / max(L - 1, 1)
    bands = jnp.linspace(1e-4, HY_BANDS - 1, HY_BANDS, dtype=F32)
    ang = (2.0 * math.pi / L) * pos[:, None] * bands[None, :]
    z = jnp.concatenate([t[:, None], jnp.cos(ang), -jnp.sin(ang)], axis=-1)
    z_ext = jnp.concatenate([z, z[:1], z[:0:-1]], axis=0)
    return jnp.pad(z_ext, ((0, 0), (0, LANES - z.shape[1])))


def _hyena(z0, g1, g2, skip, filt, consts, tw_c, tw_s):
    groups, L, _ = z0.shape
    n1 = DFT_N1
    n2 = 2 * L // n1
    k_time, sums = filt
    kf = _outer_fwd(k_time.reshape(k_time.shape[0], n1, n2, LANES), consts["fwd1"], sums)
    kr, ki = _mid(kf, tw_c, tw_s, consts["fwd2"])
    z = z0.reshape(groups, n1 // 2, n2, LANES)
    for order, gate in enumerate((g1, g2)):
        a = _outer_fwd(z, consts["fwd1"])
        d = _mid(a, tw_c, tw_s, consts["fwd2"], consts["inv2"], kr, ki, order)
        z = _outer_inv(d, consts["inv1c"], consts["inv1s"], gate.reshape(z.shape), z,
                       skip[order].reshape(1, groups * LANES))
    return z.reshape(groups, L, LANES)


def kernel(x, ffn1_norm, ffn1_w_gu, ffn1_w_down, mix_norm, w_in, b_in, conv_a, conv_b, hf_w1, hf_b1, hf_w2, hf_b2, hf_w3, hf_b3, hf_w_out, hf_freq, hy_skip, gk_w2, gk_b, gla_norm, w_br_a, w_br_b, w_br_c, w_o, ffn2_norm, ffn2_w_gu, ffn2_w_down, final_norm):
    bsz, L, D = x.shape
    depth = ffn1_norm.shape[0]
    wa = conv_a.shape[2]
    wb = conv_b.shape[3]
    wck = gk_w2.shape[3]
    dv = gla_norm.shape[1]
    wcv = GLA_HEADS * dv
    dk = wck // GLA_HEADS
    hid = hf_w2.shape[1]
    n_main = 3 * wa + 3 * wb + 2 * wck + 2 * wcv
    q_off = 3 * wa + 3 * wb
    k_off, v_off, r_off = q_off + wck, q_off + 2 * wck, q_off + 2 * wck + wcv
    lr_off = n_main
    gate_off = n_main + 2 * GLA_RANK
    n1 = DFT_N1
    n2 = 2 * L // n1

    consts = _dft_consts(n1, n2)
    tw_c = jnp.broadcast_to(consts["tw_c"][:, :, None], (n1, n2, LANES))
    tw_s = jnp.broadcast_to(consts["tw_s"][:, :, None], (n1, n2, LANES))
    z_ext = _position_features(L)
    deltas = jnp.linspace(math.log(HY_TARGET) / HY_SLOW_DECAY, math.log(HY_TARGET) / HY_FAST_DECAY, wb, dtype=F32)
    deltas2 = jnp.tile(deltas.reshape(1, wb), (1, HY_ORDER))

    outs = []
    for bi in range(bsz):
        h = x[bi]
        for l in range(depth):
            h = _ffn(h, ffn1_norm[l], _to_bf16(ffn1_w_gu, l), _to_bf16(ffn1_w_down, l),
                     final_norm, final=False)

            w_lr = jnp.pad(w_in[l][:, lr_off:gate_off], ((0, 0), (0, LANES - 2 * GLA_RANK))).astype(BF16)
            b_lr = jnp.pad(b_in[l][lr_off:gate_off], (0, LANES - 2 * GLA_RANK)).reshape(1, LANES)
            p, lr = _proj(h, mix_norm[l], _to_bf16(w_in, l, n_main), b_in[l][:n_main].reshape(1, n_main),
                          w_lr, b_lr)

            a_pre, z0, g1, g2 = _prep(p, conv_a[l], conv_b[l], wa=wa)

            w_out2 = hf_w_out[l].reshape(hid, HY_ORDER, 2, wb).transpose(2, 0, 1, 3).reshape(2, hid, HY_ORDER * wb)
            w1 = jnp.pad(hf_w1[l], ((0, LANES - hf_w1.shape[1]), (0, 0)))
            filt = _filters(z_ext, w1, hf_b1[l], hf_w2[l], hf_b2[l], hf_w3[l], hf_b3[l], hf_freq[l],
                            w_out2, deltas2)
            zb = _hyena(z0, g1, g2, hy_skip[l], filt, consts, tw_c, tw_s)

            gw = gk_w2[l].reshape(2, GLA_RANK, GLA_HEADS, dk).transpose(2, 0, 1, 3)
            w_pad = jnp.zeros((GLA_HEADS, 2, LANES, dk), F32)
            w_pad = w_pad.at[:, 0, :GLA_RANK].set(gw[:, 0]).at[:, 1, GLA_RANK:2 * GLA_RANK].set(gw[:, 1])
            b_pad = gk_b[l].reshape(2, GLA_HEADS, 1, dk).transpose(1, 0, 2, 3)
            o_f, o_b = _gla(p, lr, w_pad, b_pad, q_off=q_off, k_off=k_off, v_off=v_off, dk=dk, dv=dv)

            c_pre = _gla_post(o_f, o_b, p, gla_norm[l], r_off=r_off)

            h = _merge(h, mix_norm[l], a_pre, zb, c_pre,
                       w_in[l][:, gate_off:].astype(BF16), b_in[l][gate_off:].reshape(1, 3 * D),
                       _to_bf16(w_br_a, l), _to_bf16(w_br_b, l), _to_bf16(w_br_c, l),
                       _to_bf16(w_o, l))

            h = _ffn(h, ffn2_norm[l], _to_bf16(ffn2_w_gu, l), _to_bf16(ffn2_w_down, l),
                     final_norm, final=(l == depth - 1))
        outs.append(h)
    return jnp.stack(outs, axis=0)
```

```python
import functools
import math

import numpy as np
import jax
import jax.numpy as jnp
from jax import lax
from jax.experimental import pallas as pl
from jax.experimental.pallas import tpu as pltpu

F32 = jnp.float32
BF16 = jnp.bfloat16
HIGHEST = lax.Precision.HIGHEST

EPS = 1e-5
GLA_HEADS = 4
GLA_RANK = 16
GLA_TAU = 16.0
GLA_CHUNK = 64
HY_ORDER = 2
HY_BANDS = 16
HY_FAST_DECAY = 0.3
HY_SLOW_DECAY = 1.5
HY_TARGET = 1e-2

LANES = 128
SUBLANES = 8
DFT_N1 = 256
VMEM_LIMIT = 56 * 1024 * 1024


def _params(sem):
    return pltpu.CompilerParams(dimension_semantics=sem, vmem_limit_bytes=VMEM_LIMIT)


def _rms(x, g):
    return x * lax.rsqrt(jnp.mean(x * x, axis=-1, keepdims=True) + EPS) * g


def _dot(a, b):
    return jnp.dot(a, b, preferred_element_type=F32)


def _dot_hi(a, b):
    return jnp.dot(a, b, preferred_element_type=F32, precision=HIGHEST)


def _cast_body(x_ref, o_ref):
    o_ref[...] = x_ref[...].astype(o_ref.dtype)


def _to_bf16(w, layer, cols=None, *, block_bytes=4 << 20):
    _, rows, width = w.shape
    cols = width if cols is None else cols
    tr = rows
    while tr * cols * 4 > block_bytes and tr % 32 == 0:
        tr //= 2
    return pl.pallas_call(
        _cast_body,
        out_shape=jax.ShapeDtypeStruct((rows, cols), BF16),
        grid=(rows // tr,),
        in_specs=[pl.BlockSpec((None, tr, cols), lambda i: (layer, i, 0))],
        out_specs=pl.BlockSpec((tr, cols), lambda i: (i, 0)),
        compiler_params=_params(("parallel",)),
        name="to_bf16",
    )(w)


def _ffn_body(x_ref, g_ref, wg_ref, wu_ref, wd_ref, fn_ref, o_ref, xn_ref, *, final, sub):
    j = pl.program_id(1)

    @pl.when(j == 0)
    def _():
        xn_ref[...] = _rms(x_ref[...], g_ref[...]).astype(BF16)
        o_ref[...] = jnp.zeros_like(o_ref)

    xn = xn_ref[...]
    hs = []
    for c in range(wg_ref.shape[1] // sub):
        sl = slice(c * sub, (c + 1) * sub)
        g = _dot(xn, wg_ref[:, sl])
        u = _dot(xn, wu_ref[:, sl])
        hs.append((g * jax.nn.sigmoid(g) * u).astype(BF16))
    o_ref[...] += _dot(jnp.concatenate(hs, axis=1), wd_ref[...])

    @pl.when(j == pl.num_programs(1) - 1)
    def _():
        y = x_ref[...] + 0.5 * o_ref[...]
        if final:
            y = _rms(y, fn_ref[...])
        o_ref[...] = y


def _ffn(x, norm_g, w_gu, w_down, final_g, *, final, tm=1024, tf=512, sub=256):
    L, D = x.shape
    F = w_down.shape[0]
    tm = min(tm, L)
    if final:
        tf = sub
    nf = F // tf
    return pl.pallas_call(
        functools.partial(_ffn_body, final=final, sub=min(sub, tf)),
        out_shape=jax.ShapeDtypeStruct((L, D), F32),
        grid=(L // tm, nf),
        in_specs=[
            pl.BlockSpec((tm, D), lambda i, j: (i, 0)),
            pl.BlockSpec((1, D), lambda i, j: (0, 0)),
            pl.BlockSpec((D, tf), lambda i, j: (0, j)),
            pl.BlockSpec((D, tf), lambda i, j: (0, j + nf)),
            pl.BlockSpec((tf, D), lambda i, j: (j, 0)),
            pl.BlockSpec((1, D), lambda i, j: (0, 0)),
        ],
        out_specs=pl.BlockSpec((tm, D), lambda i, j: (i, 0)),
        scratch_shapes=[pltpu.VMEM((tm, D), BF16)],
        compiler_params=_params(("parallel", "arbitrary")),
        name="ffn",
    )(x, norm_g.reshape(1, D), w_gu, w_gu, w_down, final_g.reshape(1, D))


def _proj_body(h_ref, g_ref, w_ref, b_ref, wlr_ref, blr_ref, p_ref, lr_ref, *, tn):
    xn = _rms(h_ref[...], g_ref[...]).astype(BF16)
    lr_ref[...] = _dot(xn, wlr_ref[...]) + blr_ref[...]
    for c in range(p_ref.shape[1] // tn):
        sl = slice(c * tn, (c + 1) * tn)
        p_ref[:, sl] = _dot(xn, w_ref[:, sl]) + b_ref[:, sl]


def _proj(h, norm_g, w, b, w_lr, b_lr, *, tm=256, tn=512):
    L, D = h.shape
    n_out = w.shape[1]
    resident = lambda shape: pl.BlockSpec(shape, lambda i: (0, 0), pipeline_mode=pl.Buffered(1))
    return pl.pallas_call(
        functools.partial(_proj_body, tn=tn),
        out_shape=(jax.ShapeDtypeStruct((L, n_out), F32), jax.ShapeDtypeStruct((L, LANES), F32)),
        grid=(L // tm,),
        in_specs=[
            pl.BlockSpec((tm, D), lambda i: (i, 0)),
            resident((1, D)), resident((D, n_out)), resident((1, n_out)),
            resident((D, LANES)), resident((1, LANES)),
        ],
        out_specs=(pl.BlockSpec((tm, n_out), lambda i: (i, 0)),
                   pl.BlockSpec((tm, LANES), lambda i: (i, 0))),
        compiler_params=_params(("parallel",)),
        name="proj",
    )(h, norm_g.reshape(1, D), w, b, w_lr, b_lr)


def _conv3(x, x_before, x_after, w):
    tm = x.shape[0]
    row = lax.broadcasted_iota(jnp.int32, x.shape, 0)
    down = jnp.where(row == 0, x_before, pltpu.roll(x, 1, axis=0))
    up = jnp.where(row == tm - 1, x_after, pltpu.roll(x, tm - 1, axis=0))
    return down * w[0:1] + x * w[1:2] + up * w[2:3]


def _prep_body(m_ref, pv_ref, nx_ref, ca_w_ref, cb_w_ref, a_ref, z_ref, g1_ref, g2_ref, *, wa):
    i = pl.program_id(0)
    has_prev = (i > 0).astype(F32)
    has_next = (i < pl.num_programs(0) - 1).astype(F32)
    last = SUBLANES - 1

    def sec(k):
        sl = slice(k * wa, (k + 1) * wa)
        return (m_ref[:, sl], pv_ref[last:last + 1, sl] * has_prev, nx_ref[0:1, sl] * has_next)

    xa, xa_p, xa_n = sec(0)
    ba, _, _ = sec(1)
    ca, ca_p, ca_n = sec(2)
    a_ref[...] = (ba * _conv3(ca * xa, ca_p * xa_p, ca_n * xa_n, ca_w_ref[...])).astype(BF16)
    for k, o_ref in enumerate((z_ref, g1_ref, g2_ref)):
        x, x_p, x_n = sec(3 + k)
        _store_groups(o_ref, _conv3(x, x_p, x_n, cb_w_ref[k]))


def _store_groups(o_ref, x):
    for g in range(o_ref.shape[0]):
        o_ref[g] = x[:, g * LANES:(g + 1) * LANES]


def _load_groups(ref):
    return jnp.concatenate([ref[g] for g in range(ref.shape[0])], axis=1)


def _prep(p, conv_a, conv_b, *, wa, tm=256):
    L = p.shape[0]
    width = 6 * wa
    rb = tm // SUBLANES
    nrb = L // SUBLANES
    grouped = pl.BlockSpec((wa // LANES, tm, LANES), lambda i: (0, i, 0))
    return pl.pallas_call(
        functools.partial(_prep_body, wa=wa),
        out_shape=(jax.ShapeDtypeStruct((L, wa), BF16),)
        + (jax.ShapeDtypeStruct((wa // LANES, L, LANES), F32),) * 3,
        grid=(L // tm,),
        in_specs=[
            pl.BlockSpec((tm, width), lambda i: (i, 0)),
            pl.BlockSpec((SUBLANES, width), lambda i: (jnp.maximum(i * rb - 1, 0), 0)),
            pl.BlockSpec((SUBLANES, width), lambda i: (jnp.minimum((i + 1) * rb, nrb - 1), 0)),
            pl.BlockSpec((3, wa), lambda i: (0, 0)),
            pl.BlockSpec((3, 3, wa), lambda i: (0, 0, 0)),
        ],
        out_specs=(pl.BlockSpec((tm, wa), lambda i: (i, 0)), grouped, grouped, grouped),
        compiler_params=_params(("parallel",)),
        name="prep",
    )(p, p, p, conv_a, conv_b)


def _filter_body(w1t_ref, w1cs_ref, b1_ref, w2_ref, b2_ref, w3_ref, b3_ref, fr_ref, bands_ref, wo_ref, dl_ref,
                 k_ref, s_ref, *, half_tiles, tr, seq):
    i = pl.program_id(0)

    def lag_of(n):
        return jnp.where(n < seq, n, jnp.where(n == seq, 0, 2 * seq - n)).astype(F32)

    lag_row = lag_of(i * tr + lax.broadcasted_iota(jnp.int32, (1, tr), 1))
    ang = ((2.0 * math.pi / seq) * lag_row) * bands_ref[...]
    feats = jnp.concatenate([jnp.cos(ang), -jnp.sin(ang)], axis=0)
    fr = fr_ref[...]
    pre = w1t_ref[...] * (lag_row / max(seq - 1, 1)) + _dot_hi(w1cs_ref[...], feats) + b1_ref[...]
    h = jnp.sin(fr * pre)
    h = jnp.sin(fr * (_dot_hi(w2_ref[...], h) + b2_ref[...]))
    h = jnp.sin(fr * (_dot_hi(w3_ref[...], h) + b3_ref[...]))
    t = lag_of(i * tr + lax.broadcasted_iota(jnp.int32, (tr, 1), 0)) / max(seq - 1, 1)
    decay = jnp.exp(-t * jnp.abs(dl_ref[...]))
    hd = lax.dot_general(h, wo_ref[0], _TN, preferred_element_type=F32, precision=HIGHEST) * decay

    @pl.when(i % half_tiles == 0)
    def _():
        s_ref[...] = jnp.zeros_like(s_ref)

    s_ref[0] += jnp.sum(jnp.abs(hd), axis=0, keepdims=True)
    row = lax.broadcasted_iota(jnp.int32, hd.shape, 0)
    zero_row = jnp.where(i == half_tiles, 0, -1)
    _store_groups(k_ref, jnp.where(row == zero_row, 0.0, hd))


def _filters(seq, w1, b1, w2, b2, w3, b3, freq, w_out2, deltas2, *, tr=512):
    n_rows = 2 * seq
    hid = w2.shape[0]
    width = w_out2.shape[2]
    half_tiles = n_rows // (2 * tr)
    bands = jnp.linspace(1e-4, HY_BANDS - 1, HY_BANDS, dtype=F32).reshape(HY_BANDS, 1)
    full = lambda shape: pl.BlockSpec(shape, lambda i: (0,) * len(shape))
    colv = lambda v: v.reshape(hid, 1)
    return pl.pallas_call(
        functools.partial(_filter_body, half_tiles=half_tiles, tr=tr, seq=seq),
        out_shape=(jax.ShapeDtypeStruct((width // LANES, n_rows, LANES), F32),
                   jax.ShapeDtypeStruct((2, 1, width), F32)),
        grid=(n_rows // tr,),
        in_specs=[
            full((hid, 1)), full((hid, 2 * HY_BANDS)), full((hid, 1)), full((hid, hid)), full((hid, 1)),
            full((hid, hid)), full((hid, 1)), full((hid, 1)), full((HY_BANDS, 1)),
            pl.BlockSpec((1, hid, width), lambda i: (i // half_tiles, 0, 0)),
            full((1, width)),
        ],
        out_specs=(pl.BlockSpec((width // LANES, tr, LANES), lambda i: (0, i, 0)),
                   pl.BlockSpec((1, 1, width), lambda i: (i // half_tiles, 0, 0))),
        compiler_params=_params(("arbitrary",)),
        name="hyena_filters",
    )(w1[0:1].T, w1[1:].T, colv(b1), w2.T, colv(b2), w3.T, colv(b3), colv(freq), bands, w_out2, deltas2)


def _dft_consts(n1, n2):
    n = n1 * n2
    k = np.arange(n1)
    ang1 = 2.0 * np.pi * ((k[:, None] * k[None, :]) % n1) / n1
    c1, s1 = np.cos(ang1), np.sin(ang1)
    m = np.arange(n2)
    ang2 = 2.0 * np.pi * ((m[:, None] * m[None, :]) % n2) / n2
    c2, s2 = np.cos(ang2), np.sin(ang2)
    angt = 2.0 * np.pi * (k[:, None] * m[None, :]) / n
    as_bf = lambda a: jnp.asarray(a, F32).astype(BF16)
    return dict(
        fwd1=as_bf(np.concatenate([c1, -s1], axis=0)),
        inv1c=as_bf(c1[: n1 // 2] / n), inv1s=as_bf(-s1[: n1 // 2] / n),
        fwd2=as_bf(np.block([[c2, s2], [-s2, c2]])),
        inv2=as_bf(np.block([[c2, -s2], [s2, c2]])),
        tw_c=jnp.asarray(np.cos(angt), F32), tw_s=jnp.asarray(np.sin(angt), F32),
    )


def _flatten_block(ref, buf):
    groups, n, t2, lanes = ref.shape
    for g in range(groups):
        buf[g] = ref[g].reshape(n * t2, lanes)


def _unflatten_block(buf, ref):
    groups, n, t2, lanes = ref.shape
    for g in range(groups):
        ref[g] = buf[g].reshape(n, t2, lanes)


def _load_strided(buf, s, t2):
    n = buf.shape[1] // t2
    return jnp.concatenate([buf[g, pl.ds(s, n, stride=t2), :] for g in range(buf.shape[0])], axis=1)


def _store_strided(buf, s, t2, x):
    n = buf.shape[1] // t2
    for g in range(buf.shape[0]):
        buf[g, pl.ds(s, n, stride=t2), :] = x[:, g * LANES:(g + 1) * LANES]


def _flat_scratch(groups, n, t2, dtype=F32):
    return pltpu.VMEM((groups, n * t2, LANES), dtype)


U32 = jnp.uint32


def _pack_complex(re, im):
    hi = lax.bitcast_convert_type(re.astype(BF16).astype(F32), U32)
    lo = lax.bitcast_convert_type(im.astype(BF16).astype(F32), U32)
    return hi | (lo >> 16)


def _unpack_complex(w):
    re = lax.bitcast_convert_type(w & jnp.uint32(0xFFFF0000), F32)
    im = lax.bitcast_convert_type(w << 16, F32)
    return re, im


def _outer_fwd_body(x_ref, f_ref, *rest, scaled):
    if scaled:
        s_ref, a_ref, xb, ab = rest
        inv = 1.0 / (s_ref[...] + EPS)
    else:
        a_ref, xb, ab = rest
    rows, t2 = x_ref.shape[1:3]
    n1 = a_ref.shape[1]
    f = f_ref[...]
    _flatten_block(x_ref, xb)
    for s in range(t2):
        x = _load_strided(xb, s, t2)
        if scaled:
            half = rows // 2
            x = jnp.concatenate([x[:half] * inv[0], x[half:] * inv[1]], axis=0)
        a = _dot(f, x.astype(BF16))
        _store_strided(ab, s, t2, _pack_complex(a[:n1], a[n1:]))
    _unflatten_block(ab, a_ref)


def _outer_fwd(x4, fwd1, sums=None, *, t2=SUBLANES, gt=4):
    groups, rows, n2, _ = x4.shape
    n1 = fwd1.shape[1]
    scaled = sums is not None
    blk = lambda r: pl.BlockSpec((gt, r, t2, LANES), lambda j, c: (c, 0, j, 0))
    args = [x4, fwd1[:, :rows]]
    in_specs = [blk(rows), pl.BlockSpec((2 * n1, rows), lambda j, c: (0, 0))]
    if scaled:
        args.append(sums)
        in_specs.append(pl.BlockSpec((2, 1, gt * LANES), lambda j, c: (0, 0, c)))
    return pl.pallas_call(
        functools.partial(_outer_fwd_body, scaled=scaled),
        out_shape=jax.ShapeDtypeStruct((groups, n1, n2, LANES), U32),
        grid=(n2 // t2, groups // gt),
        in_specs=in_specs,
        out_specs=blk(n1),
        scratch_shapes=[_flat_scratch(gt, rows, t2), _flat_scratch(gt, n1, t2, U32)],
        compiler_params=_params(("parallel", "parallel")),
        name="dft_outer_fwd",
    )(*args)


def _lane_tile(t, width):
    return jnp.concatenate([t] * (width // LANES), axis=1) if width > LANES else t


def _load_slab(ref, s):
    return jnp.concatenate([ref[g, s] for g in range(ref.shape[0])], axis=1)


def _store_slab(ref, s, x):
    for g in range(ref.shape[0]):
        ref[g, s] = x[:, g * LANES:(g + 1) * LANES]


def _twiddled(ar, ai, tc, ts):
    return ar * tc + ai * ts, ai * tc - ar * ts


def _inner_fwd(ar, ai, tc, ts, f2):
    n2 = ar.shape[0]
    br, bi = _twiddled(ar, ai, tc, ts)
    x = _dot(f2, jnp.concatenate([br, bi], axis=0).astype(BF16))
    return x[:n2], x[n2:]


def _mid_spec_body(a_ref, tc_ref, ts_ref, f2_ref, xr_ref, xi_ref):
    width = a_ref.shape[0] * LANES
    for s in range(a_ref.shape[1]):
        tc, ts = _lane_tile(tc_ref[s], width), _lane_tile(ts_ref[s], width)
        ar, ai = _unpack_complex(_load_slab(a_ref, s))
        xr, xi = _inner_fwd(ar, ai, tc, ts, f2_ref[...])
        _store_slab(xr_ref, s, xr.astype(xr_ref.dtype))
        _store_slab(xi_ref, s, xi.astype(xi_ref.dtype))


def _mid_conv_body(a_ref, kr_ref, ki_ref, tc_ref, ts_ref, f2_ref, g2_ref, d_ref):
    width = a_ref.shape[0] * LANES
    n2 = a_ref.shape[2]
    for s in range(a_ref.shape[1]):
        tc, ts = _lane_tile(tc_ref[s], width), _lane_tile(ts_ref[s], width)
        ar, ai = _unpack_complex(_load_slab(a_ref, s))
        xr, xi = _inner_fwd(ar, ai, tc, ts, f2_ref[...])
        kr, ki = _load_slab(kr_ref, s).astype(F32), _load_slab(ki_ref, s).astype(F32)
        yr = xr * kr - xi * ki
        yi = xr * ki + xi * kr
        c = _dot(g2_ref[...], jnp.concatenate([yr, yi], axis=0).astype(BF16))
        cr, ci = c[:n2], c[n2:]
        _store_slab(d_ref, s, _pack_complex(cr * tc - ci * ts, ci * tc + cr * ts))


def _mid(a, tw_c, tw_s, fwd2, inv2=None, kr=None, ki=None, order=0, *, kb=4, gt=4):
    groups, n1, n2, _ = a.shape
    conv = kr is not None
    slab = pl.BlockSpec((gt, kb, n2, LANES), lambda i, q: (q, i, 0, 0))
    tw = pl.BlockSpec((kb, n2, LANES), lambda i, q: (i, 0, 0))
    mat = pl.BlockSpec((2 * n2, 2 * n2), lambda i, q: (0, 0))
    if conv:
        kslab = pl.BlockSpec((gt, kb, n2, LANES), lambda i, q: (order * (groups // gt) + q, i, 0, 0))
        args = (a, kr, ki, tw_c, tw_s, fwd2, inv2)
        in_specs = [slab, kslab, kslab, tw, tw, mat, mat]
        body, name = _mid_conv_body, "hyena_mid_conv"
        out_shape, out_specs = jax.ShapeDtypeStruct(a.shape, U32), slab
    else:
        args = (a, tw_c, tw_s, fwd2)
        in_specs = [slab, tw, tw, mat]
        body, name = _mid_spec_body, "hyena_mid_spec"
        out_shape, out_specs = (jax.ShapeDtypeStruct(a.shape, BF16),) * 2, (slab, slab)
    return pl.pallas_call(
        body,
        out_shape=out_shape,
        grid=(n1 // kb, groups // gt),
        in_specs=in_specs,
        out_specs=out_specs,
        compiler_params=_params(("parallel", "parallel")),
        name=name,
    )(*args)


def _outer_inv_body(d_ref, c_ref, s_ref, gate_ref, z_ref, skip_ref, o_ref, db, gb, zb, ob):
    c, sm, skip = c_ref[...], s_ref[...], skip_ref[...]
    t2 = d_ref.shape[2]
    for ref, buf in ((d_ref, db), (gate_ref, gb), (z_ref, zb)):
        _flatten_block(ref, buf)
    for s in range(t2):
        dr, di = _unpack_complex(_load_strided(db, s, t2))
        y = _dot(c, dr.astype(BF16)) + _dot(sm, di.astype(BF16))
        _store_strided(ob, s, t2, _load_strided(gb, s, t2) * (y + skip * _load_strided(zb, s, t2)))
    _unflatten_block(ob, o_ref)


def _outer_inv(d, inv1c, inv1s, gate4, z4, skip, *, t2=SUBLANES):
    groups, n1, n2, _ = d.shape
    half = n1 // 2
    blk = lambda r: pl.BlockSpec((groups, r, t2, LANES), lambda j: (0, 0, j, 0))
    mat = pl.BlockSpec((half, n1), lambda j: (0, 0))
    return pl.pallas_call(
        _outer_inv_body,
        out_shape=jax.ShapeDtypeStruct((groups, half, n2, LANES), F32),
        grid=(n2 // t2,),
        in_specs=[blk(n1), mat, mat, blk(half), blk(half),
                  pl.BlockSpec((1, groups * LANES), lambda j: (0, 0))],
        out_specs=blk(half),
        scratch_shapes=[_flat_scratch(groups, n1, t2, U32)] + [_flat_scratch(groups, half, t2)] * 3,
        compiler_params=_params(("parallel",)),
        name="dft_outer_inv",
    )(d, inv1c, inv1s, gate4, z4, skip)


def _log_sigmoid(x):
    return jnp.minimum(x, 0.0) - jnp.log1p(jnp.exp(-jnp.abs(x)))


_NT = (((1,), (1,)), ((), ()))
_TN = (((0,), (0,)), ((), ()))


def _gla_direction(q_ref, k_ref, v_ref, lr_ref, w, b, cum, o_ref, st_ref, *, reverse, scale):
    tb, dk = q_ref.shape
    sub = cum.shape[0]
    chunk = GLA_CHUNK
    cum_bf = cum.astype(BF16)
    qes, intras, upd, dec = [], [], [], []
    for s in range(tb // sub):
        rs = slice(s * sub, (s + 1) * sub)
        g = _log_sigmoid(_dot(lr_ref[rs, :].astype(BF16), w) + b) / GLA_TAU
        g_hi = g.astype(BF16)
        g_lo = (g - g_hi.astype(F32)).astype(BF16)
        bc = _dot(cum_bf, g_hi) + _dot(cum_bf, g_lo)
        k = k_ref[rs, :]
        v = v_ref[rs, :].astype(BF16)
        qe = (q_ref[rs, :] * scale * jnp.exp(bc)).astype(BF16)
        ke = (k * jnp.exp(-bc)).astype(BF16)
        a = lax.dot_general(qe, ke, _NT, preferred_element_type=F32)
        intra = _dot(jnp.where(cum > 0.5, a, 0.0).astype(BF16), v)
        for c in range(sub // chunk):
            sl = slice(c * chunk, (c + 1) * chunk)
            bl = bc[c * chunk:c * chunk + 1] if reverse else bc[(c + 1) * chunk - 1:(c + 1) * chunk]
            kd = (k[sl] * jnp.exp(bl - bc[sl])).astype(BF16)
            upd.append(lax.dot_general(v[sl], kd, _TN, preferred_element_type=F32))
            dec.append(jnp.exp(bl))
            qes.append(qe[sl])
            intras.append(intra[sl])
    st = st_ref[...]
    chunks = range(tb // chunk)
    for c in (reversed(chunks) if reverse else chunks):
        o_ref[c * chunk:(c + 1) * chunk, :] = intras[c] + lax.dot_general(
            qes[c], st.astype(BF16), _NT, preferred_element_type=F32)
        st = st * dec[c] + upd[c]
    st_ref[...] = st


def _gla_body(qf, kf, vf, lrf, qb, kb, vb, lrb, w_ref, b_ref, cum_ref, of_ref, ob_ref, sf_ref, sb_ref, *, scale):
    @pl.when(pl.program_id(1) == 0)
    def _():
        sf_ref[...] = jnp.zeros_like(sf_ref)
        sb_ref[...] = jnp.zeros_like(sb_ref)

    _gla_direction(qf, kf, vf, lrf, w_ref[0, 0], b_ref[0, 0], cum_ref[0], of_ref, sf_ref,
                   reverse=False, scale=scale)
    _gla_direction(qb, kb, vb, lrb, w_ref[0, 1], b_ref[0, 1], cum_ref[1], ob_ref, sb_ref,
                   reverse=True, scale=scale)


def _chunk_cumsum_matrices(sub):
    i = np.arange(sub)
    same = (i[:, None] // GLA_CHUNK) == (i[None, :] // GLA_CHUNK)
    fwd = same & (i[None, :] <= i[:, None])
    bwd = same & (i[None, :] >= i[:, None])
    return jnp.asarray(np.stack([fwd, bwd]), F32)


def _gla(p, lr, w_pad, b_pad, *, q_off, k_off, v_off, dk, dv, tb=512, sub=256):
    L = p.shape[0]
    tb = min(tb, L)
    nb = L // tb
    heads = w_pad.shape[0]
    fwd = lambda width, off: pl.BlockSpec((tb, width), lambda h, i: (i, off // width + h))
    bwd = lambda width, off: pl.BlockSpec((tb, width), lambda h, i: (nb - 1 - i, off // width + h))
    return pl.pallas_call(
        functools.partial(_gla_body, scale=dk ** -0.5),
        out_shape=(jax.ShapeDtypeStruct((L, heads * dv), F32),) * 2,
        grid=(heads, nb),
        in_specs=[
            fwd(dk, q_off), fwd(dk, k_off), fwd(dv, v_off), pl.BlockSpec((tb, LANES), lambda h, i: (i, 0)),
            bwd(dk, q_off), bwd(dk, k_off), bwd(dv, v_off), pl.BlockSpec((tb, LANES), lambda h, i: (nb - 1 - i, 0)),
            pl.BlockSpec((1, 2, LANES, dk), lambda h, i: (h, 0, 0, 0)),
            pl.BlockSpec((1, 2, 1, dk), lambda h, i: (h, 0, 0, 0)),
            pl.BlockSpec((2, sub, sub), lambda h, i: (0, 0, 0)),
        ],
        out_specs=(pl.BlockSpec((tb, dv), lambda h, i: (i, h)),
                   pl.BlockSpec((tb, dv), lambda h, i: (nb - 1 - i, h))),
        scratch_shapes=[pltpu.VMEM((dv, dk), F32), pltpu.VMEM((dv, dk), F32)],
        compiler_params=_params(("parallel", "arbitrary")),
        name="gla",
    )(p, p, p, lr, p, p, p, lr, w_pad.astype(BF16), b_pad, _chunk_cumsum_matrices(sub))


def _gla_post_body(of_ref, ob_ref, r_ref, gn_ref, c_ref, *, dv):
    o = of_ref[...] + ob_ref[...]
    r = r_ref[...]
    gate = r * jax.nn.sigmoid(r)
    for hd in range(o.shape[1] // dv):
        sl = slice(hd * dv, (hd + 1) * dv)
        c_ref[:, sl] = (_rms(o[:, sl], gn_ref[...]) * gate[:, sl]).astype(BF16)


def _gla_post(o_f, o_b, p, gla_norm, *, r_off, tm=512):
    L, wc = o_f.shape
    dv = gla_norm.shape[0]
    row = lambda blk=0: pl.BlockSpec((tm, wc), lambda i: (i, blk))
    return pl.pallas_call(
        functools.partial(_gla_post_body, dv=dv),
        out_shape=jax.ShapeDtypeStruct((L, wc), BF16),
        grid=(L // tm,),
        in_specs=[row(), row(), row(r_off // wc), pl.BlockSpec((1, dv), lambda i: (0, 0))],
        out_specs=row(),
        compiler_params=_params(("parallel",)),
        name="gla_post",
    )(o_f, o_b, p, gla_norm.reshape(1, dv))


def _merge_body(h_ref, g_ref, a_ref, z_ref, c_ref,
                wga_ref, wgb_ref, wgc_ref, bga_ref, bgb_ref, bgc_ref,
                wa_ref, wb_ref, wc_ref, m_ref, u_ref, zb_ref):
    @pl.when(pl.program_id(1) == 0)
    def _():
        u_ref[...] = _rms(h_ref[...], g_ref[...]).astype(BF16)
        zb_ref[...] = _load_groups(z_ref).astype(BF16)

    u = u_ref[...]
    m = jax.nn.sigmoid(_dot(u, wga_ref[...]) + bga_ref[...]) * _dot(a_ref[...], wa_ref[...])
    m += jax.nn.sigmoid(_dot(u, wgb_ref[...]) + bgb_ref[...]) * _dot(zb_ref[...], wb_ref[...])
    m += jax.nn.sigmoid(_dot(u, wgc_ref[...]) + bgc_ref[...]) * _dot(c_ref[...], wc_ref[...])
    m_ref[...] = m.astype(BF16)


def _merge(h, norm_g, a_pre, z, c_pre, w_gate, b_gate, w_a, w_b, w_c, *, tm=1024, tn=512):
    L, D = h.shape
    wa, wb, wc = w_a.shape[0], w_b.shape[0], w_c.shape[0]
    tm = min(tm, L)
    nj = D // tn
    row = lambda width: pl.BlockSpec((tm, width), lambda i, j: (i, 0))
    gate_w = lambda k: pl.BlockSpec((D, tn), lambda i, j: (0, j + k * nj))
    gate_b = lambda k: pl.BlockSpec((1, tn), lambda i, j: (0, j + k * nj))
    br_w = lambda width: pl.BlockSpec((width, tn), lambda i, j: (0, j))
    return pl.pallas_call(
        _merge_body,
        out_shape=jax.ShapeDtypeStruct((L, D), BF16),
        grid=(L // tm, nj),
        in_specs=[
            pl.BlockSpec((tm, D), lambda i, j: (i, 0), pipeline_mode=pl.Buffered(1)),
            pl.BlockSpec((1, D), lambda i, j: (0, 0)),
            row(wa), pl.BlockSpec((wb // LANES, tm, LANES), lambda i, j: (0, i, 0)), row(wc),
            gate_w(0), gate_w(1), gate_w(2), gate_b(0), gate_b(1), gate_b(2),
            br_w(wa), br_w(wb), br_w(wc),
        ],
        out_specs=pl.BlockSpec((tm, tn), lambda i, j: (i, j)),
        scratch_shapes=[pltpu.VMEM((tm, D), BF16), pltpu.VMEM((tm, wb), BF16)],
        compiler_params=_params(("parallel", "arbitrary")),
        name="merge",
    )(h, norm_g.reshape(1, D), a_pre, z, c_pre,
      w_gate, w_gate, w_gate, b_gate, b_gate, b_gate, w_a, w_b, w_c)


def _out_proj_body(h_ref, m_ref, w_ref, o_ref, *, tn):
    m = m_ref[...]
    for c in range(o_ref.shape[1] // tn):
        sl = slice(c * tn, (c + 1) * tn)
        o_ref[:, sl] = h_ref[:, sl] + _dot(m, w_ref[:, sl])


def _out_proj(h, m, w_o, *, tm=512, tn=512):
    L, D = h.shape
    row = pl.BlockSpec((tm, D), lambda i: (i, 0))
    return pl.pallas_call(
        functools.partial(_out_proj_body, tn=tn),
        out_shape=jax.ShapeDtypeStruct((L, D), F32),
        grid=(L // tm,),
        in_specs=[row, row, pl.BlockSpec((D, D), lambda i: (0, 0), pipeline_mode=pl.Buffered(1))],
        out_specs=row,
        compiler_params=_params(("parallel",)),
        name="out_proj",
    )(h, m, w_o)


def _hyena(z0, g1, g2, skip, filt, consts, tw_c, tw_s):
    groups, L, _ = z0.shape
    n1 = DFT_N1
    n2 = 2 * L // n1
    k_time, sums = filt
    kf = _outer_fwd(k_time.reshape(k_time.shape[0], n1, n2, LANES), consts["fwd1"], sums)
    kr, ki = _mid(kf, tw_c, tw_s, consts["fwd2"])
    z = z0.reshape(groups, n1 // 2, n2, LANES)
    for order, gate in enumerate((g1, g2)):
        a = _outer_fwd(z, consts["fwd1"])
        d = _mid(a, tw_c, tw_s, consts["fwd2"], consts["inv2"], kr, ki, order)
        z = _outer_inv(d, consts["inv1c"], consts["inv1s"], gate.reshape(z.shape), z,
                       skip[order].reshape(1, groups * LANES))
    return z.reshape(groups, L, LANES)


def kernel(x, ffn1_norm, ffn1_w_gu, ffn1_w_down, mix_norm, w_in, b_in, conv_a, conv_b, hf_w1, hf_b1, hf_w2, hf_b2, hf_w3, hf_b3, hf_w_out, hf_freq, hy_skip, gk_w2, gk_b, gla_norm, w_br_a, w_br_b, w_br_c, w_o, ffn2_norm, ffn2_w_gu, ffn2_w_down, final_norm):
    bsz, L, D = x.shape
    depth = ffn1_norm.shape[0]
    wa = conv_a.shape[2]
    wb = conv_b.shape[3]
    wck = gk_w2.shape[3]
    dv = gla_norm.shape[1]
    wcv = GLA_HEADS * dv
    dk = wck // GLA_HEADS
    hid = hf_w2.shape[1]
    n_main = 3 * wa + 3 * wb + 2 * wck + 2 * wcv
    q_off = 3 * wa + 3 * wb
    k_off, v_off, r_off = q_off + wck, q_off + 2 * wck, q_off + 2 * wck + wcv
    lr_off = n_main
    gate_off = n_main + 2 * GLA_RANK
    n1 = DFT_N1
    n2 = 2 * L // n1

    consts = _dft_consts(n1, n2)
    tw_c = jnp.broadcast_to(consts["tw_c"][:, :, None], (n1, n2, LANES))
    tw_s = jnp.broadcast_to(consts["tw_s"][:, :, None], (n1, n2, LANES))
    deltas =jnp.linspace(math.log(HY_TARGET) / HY_SLOW_DECAY, math.log(HY_TARGET) / HY_FAST_DECAY, wb, dtype=F32)
    deltas2 = jnp.tile(deltas.reshape(1, wb), (1, HY_ORDER))

    outs = []
    for bi in range(bsz):
        h = x[bi]
        for l in range(depth):
            h = _ffn(h, ffn1_norm[l], _to_bf16(ffn1_w_gu, l), _to_bf16(ffn1_w_down, l),
                     final_norm, final=False)

            w_lr = jnp.pad(w_in[l][:, lr_off:gate_off], ((0, 0), (0, LANES - 2 * GLA_RANK))).astype(BF16)
            b_lr = jnp.pad(b_in[l][lr_off:gate_off], (0, LANES - 2 * GLA_RANK)).reshape(1, LANES)
            p, lr = _proj(h, mix_norm[l], w_in[l][:, :n_main].astype(BF16), b_in[l][:n_main].reshape(1, n_main),
                          w_lr, b_lr)

            a_pre, z0, g1, g2 = _prep(p, conv_a[l], conv_b[l], wa=wa)

            w_out2 = hf_w_out[l].reshape(hid, HY_ORDER, 2, wb).transpose(2, 0, 1, 3).reshape(2, hid, HY_ORDER * wb)
            filt = _filters(L, hf_w1[l], hf_b1[l], hf_w2[l], hf_b2[l], hf_w3[l], hf_b3[l], hf_freq[l],
                            w_out2, deltas2)
            zb = _hyena(z0, g1, g2, hy_skip[l], filt, consts, tw_c, tw_s)

            gw = gk_w2[l].reshape(2, GLA_RANK, GLA_HEADS, dk).transpose(2, 0, 1, 3)
            w_pad = jnp.zeros((GLA_HEADS, 2, LANES, dk), F32)
            w_pad = w_pad.at[:, 0, :GLA_RANK].set(gw[:, 0]).at[:, 1, GLA_RANK:2 * GLA_RANK].set(gw[:, 1])
            b_pad = gk_b[l].reshape(2, GLA_HEADS, 1, dk).transpose(1, 0, 2, 3)
            o_f, o_b = _gla(p, lr, w_pad, b_pad, q_off=q_off, k_off=k_off, v_off=v_off, dk=dk, dv=dv)

            c_pre = _gla_post(o_f, o_b, p, gla_norm[l], r_off=r_off)

            merged = _merge(h, mix_norm[l], a_pre, zb, c_pre,
                            w_in[l][:, gate_off:].astype(BF16), b_in[l][gate_off:].reshape(1, 3 * D),
                            _to_bf16(w_br_a, l), _to_bf16(w_br_b, l), _to_bf16(w_br_c, l))
            h = _out_proj(h, merged, _to_bf16(w_o, l))

            h = _ffn(h, ffn2_norm[l], _to_bf16(ffn2_w_gu, l), _to_bf16(ffn2_w_down, l),
                     final_norm, final=(l == depth - 1))
        outs.append(h)
    return jnp.stack(outs, axis=0)
```

```python
import functools
import math

import numpy as np
import jax
import jax.numpy as jnp
from jax import lax
from jax.experimental import pallas as pl
from jax.experimental.pallas import tpu as pltpu

F32 = jnp.float32
BF16 = jnp.bfloat16
HIGHEST = lax.Precision.HIGHEST

EPS = 1e-5
GLA_HEADS = 4
GLA_RANK = 16
GLA_TAU = 16.0
GLA_CHUNK = 64
HY_ORDER = 2
HY_BANDS = 16
HY_FAST_DECAY = 0.3
HY_SLOW_DECAY = 1.5
HY_TARGET = 1e-2

LANES = 128
SUBLANES = 8
DFT_N1 = 256
VMEM_LIMIT = 56 * 1024 * 1024


def _params(sem):
    return pltpu.CompilerParams(dimension_semantics=sem, vmem_limit_bytes=VMEM_LIMIT)


def _rms(x, g):
    return x * lax.rsqrt(jnp.mean(x * x, axis=-1, keepdims=True) + EPS) * g


def _dot(a, b):
    return jnp.dot(a, b, preferred_element_type=F32)


def _dot_hi(a, b):
    return jnp.dot(a, b, preferred_element_type=F32, precision=HIGHEST)


def _cast_body(x_ref, o_ref):
    o_ref[...] = x_ref[...].astype(o_ref.dtype)


def _to_bf16(w, layer, cols=None, *, block_bytes=4 << 20):
    _, rows, width = w.shape
    cols = width if cols is None else cols
    tr = rows
    while tr * cols * 4 > block_bytes and tr % 32 == 0:
        tr //= 2
    return pl.pallas_call(
        _cast_body,
        out_shape=jax.ShapeDtypeStruct((rows, cols), BF16),
        grid=(rows // tr,),
        in_specs=[pl.BlockSpec((None, tr, cols), lambda i: (layer, i, 0))],
        out_specs=pl.BlockSpec((tr, cols), lambda i: (i, 0)),
        compiler_params=_params(("parallel",)),
        name="to_bf16",
    )(w)


def _ffn_body(x_ref, g_ref, wg_ref, wu_ref, wd_ref, fn_ref, o_ref, xn_ref, *, final, sub):
    j = pl.program_id(1)

    @pl.when(j == 0)
    def _():
        xn_ref[...] = _rms(x_ref[...], g_ref[...]).astype(BF16)
        o_ref[...] = jnp.zeros_like(o_ref)

    xn = xn_ref[...]
    hs = []
    for c in range(wg_ref.shape[1] // sub):
        sl = slice(c * sub, (c + 1) * sub)
        g = _dot(xn, wg_ref[:, sl])
        u = _dot(xn, wu_ref[:, sl])
        hs.append((g * jax.nn.sigmoid(g) * u).astype(BF16))
    o_ref[...] += _dot(jnp.concatenate(hs, axis=1), wd_ref[...])

    @pl.when(j == pl.num_programs(1) - 1)
    def _():
        y = x_ref[...] + 0.5 * o_ref[...]
        if final:
            y = _rms(y, fn_ref[...])
        o_ref[...] = y


def _ffn(x, norm_g, w_gu, w_down, final_g, *, final, tm=1024, tf=512, sub=256):
    L, D = x.shape
    F = w_down.shape[0]
    tm = min(tm, L)
    if final:
        tf = sub
    nf = F // tf
    return pl.pallas_call(
        functools.partial(_ffn_body, final=final, sub=min(sub, tf)),
        out_shape=jax.ShapeDtypeStruct((L, D), F32),
        grid=(L // tm, nf),
        in_specs=[
            pl.BlockSpec((tm, D), lambda i, j: (i, 0)),
            pl.BlockSpec((1, D), lambda i, j: (0, 0)),
            pl.BlockSpec((D, tf), lambda i, j: (0, j)),
            pl.BlockSpec((D, tf), lambda i, j: (0, j + nf)),
            pl.BlockSpec((tf, D), lambda i, j: (j, 0)),
            pl.BlockSpec((1, D), lambda i, j: (0, 0)),
        ],
        out_specs=pl.BlockSpec((tm, D), lambda i, j: (i, 0)),
        scratch_shapes=[pltpu.VMEM((tm, D), BF16)],
        compiler_params=_params(("parallel", "arbitrary")),
        name="ffn",
    )(x, norm_g.reshape(1, D), w_gu, w_gu, w_down, final_g.reshape(1, D))


def _proj_body(h_ref, g_ref, w_ref, b_ref, wlr_ref, blr_ref, p_ref, lr_ref, *, tn):
    xn = _rms(h_ref[...], g_ref[...]).astype(BF16)
    lr_ref[...] = _dot(xn, wlr_ref[...]) + blr_ref[...]
    for c in range(p_ref.shape[1] // tn):
        sl = slice(c * tn, (c + 1) * tn)
        p_ref[:, sl] = _dot(xn, w_ref[:, sl]) + b_ref[:, sl]


def _proj(h, norm_g, w, b, w_lr, b_lr, *, tm=256, tn=512):
    L, D = h.shape
    n_out = w.shape[1]
    resident = lambda shape: pl.BlockSpec(shape, lambda i: (0, 0), pipeline_mode=pl.Buffered(1))
    return pl.pallas_call(
        functools.partial(_proj_body, tn=tn),
        out_shape=(jax.ShapeDtypeStruct((L, n_out), F32), jax.ShapeDtypeStruct((L, LANES), F32)),
        grid=(L // tm,),
        in_specs=[
            pl.BlockSpec((tm, D), lambda i: (i, 0)),
            resident((1, D)), resident((D, n_out)), resident((1, n_out)),
            resident((D, LANES)), resident((1, LANES)),
        ],
        out_specs=(pl.BlockSpec((tm, n_out), lambda i: (i, 0)),
                   pl.BlockSpec((tm, LANES), lambda i: (i, 0))),
        compiler_params=_params(("parallel",)),
        name="proj",
    )(h, norm_g.reshape(1, D), w, b, w_lr, b_lr)


def _conv3(x, x_before, x_after, w):
    tm = x.shape[0]
    row = lax.broadcasted_iota(jnp.int32, x.shape, 0)
    down = jnp.where(row == 0, x_before, pltpu.roll(x, 1, axis=0))
    up = jnp.where(row == tm - 1, x_after, pltpu.roll(x, tm - 1, axis=0))
    return down * w[0:1] + x * w[1:2] + up * w[2:3]


def _prep_body(m_ref, pv_ref, nx_ref, ca_w_ref, cb_w_ref, a_ref, z_ref, g1_ref, g2_ref, *, wa):
    i = pl.program_id(0)
    has_prev = (i > 0).astype(F32)
    has_next = (i < pl.num_programs(0) - 1).astype(F32)
    last = SUBLANES - 1

    def sec(k):
        sl = slice(k * wa, (k + 1) * wa)
        return (m_ref[:, sl], pv_ref[last:last + 1, sl] * has_prev, nx_ref[0:1, sl] * has_next)

    xa, xa_p, xa_n = sec(0)
    ba, _, _ = sec(1)
    ca, ca_p, ca_n = sec(2)
    a_ref[...] = (ba * _conv3(ca * xa, ca_p * xa_p, ca_n * xa_n, ca_w_ref[...])).astype(BF16)
    for k, o_ref in enumerate((z_ref, g1_ref, g2_ref)):
        x, x_p, x_n = sec(3 + k)
        _store_groups(o_ref, _conv3(x, x_p, x_n, cb_w_ref[k]))


def _store_groups(o_ref, x):
    for g in range(o_ref.shape[0]):
        o_ref[g] = x[:, g * LANES:(g + 1) * LANES]


def _load_groups(ref):
    return jnp.concatenate([ref[g] for g in range(ref.shape[0])], axis=1)


def _prep(p, conv_a, conv_b, *, wa, tm=256):
    L = p.shape[0]
    width = 6 * wa
    rb = tm // SUBLANES
    nrb = L // SUBLANES
    grouped = pl.BlockSpec((wa // LANES, tm, LANES), lambda i: (0, i, 0))
    return pl.pallas_call(
        functools.partial(_prep_body, wa=wa),
        out_shape=(jax.ShapeDtypeStruct((L, wa), BF16),)
        + (jax.ShapeDtypeStruct((wa // LANES, L, LANES), F32),) * 3,
        grid=(L // tm,),
        in_specs=[
            pl.BlockSpec((tm, width), lambda i: (i, 0)),
            pl.BlockSpec((SUBLANES, width), lambda i: (jnp.maximum(i * rb - 1, 0), 0)),
            pl.BlockSpec((SUBLANES, width), lambda i: (jnp.minimum((i + 1) * rb, nrb - 1), 0)),
            pl.BlockSpec((3, wa), lambda i: (0, 0)),
            pl.BlockSpec((3, 3, wa), lambda i: (0, 0, 0)),
        ],
        out_specs=(pl.BlockSpec((tm, wa), lambda i: (i, 0)), grouped, grouped, grouped),
        compiler_params=_params(("parallel",)),
        name="prep",
    )(p, p, p, conv_a, conv_b)


def _filter_body(w1t_ref, w1cs_ref, b1_ref, w2_ref, b2_ref, w3_ref, b3_ref, fr_ref, bands_ref, wo_ref, dl_ref,
                 k_ref, s_ref, *, half_tiles, tr, seq):
    i = pl.program_id(0)

    def lag_of(n):
        return jnp.where(n < seq, n, jnp.where(n == seq, 0, 2 * seq - n)).astype(F32)

    lag_row = lag_of(i * tr + lax.broadcasted_iota(jnp.int32, (1, tr), 1))
    ang = ((2.0 * math.pi / seq) * lag_row) * bands_ref[...]
    feats = jnp.concatenate([jnp.cos(ang), -jnp.sin(ang)], axis=0)
    fr = fr_ref[...]
    pre = w1t_ref[...] * (lag_row / max(seq - 1, 1)) + _dot_hi(w1cs_ref[...], feats) + b1_ref[...]
    h = jnp.sin(fr * pre)
    h = jnp.sin(fr * (_dot_hi(w2_ref[...], h) + b2_ref[...]))
    h = jnp.sin(fr * (_dot_hi(w3_ref[...], h) + b3_ref[...]))
    t = lag_of(i * tr + lax.broadcasted_iota(jnp.int32, (tr, 1), 0)) / max(seq - 1, 1)
    decay = jnp.exp(-t * jnp.abs(dl_ref[...]))
    hd = lax.dot_general(h, wo_ref[0], _TN, preferred_element_type=F32, precision=HIGHEST) * decay

    @pl.when(i % half_tiles == 0)
    def _():
        s_ref[...] = jnp.zeros_like(s_ref)

    s_ref[0] += jnp.sum(jnp.abs(hd), axis=0, keepdims=True)
    row = lax.broadcasted_iota(jnp.int32, hd.shape, 0)
    zero_row = jnp.where(i == half_tiles, 0, -1)
    _store_groups(k_ref, jnp.where(row == zero_row, 0.0, hd))


def _filters(seq, w1, b1, w2, b2, w3, b3, freq, w_out2, deltas2, *, tr=512):
    n_rows = 2 * seq
    hid = w2.shape[0]
    width = w_out2.shape[2]
    half_tiles = n_rows // (2 * tr)
    bands = jnp.linspace(1e-4, HY_BANDS - 1, HY_BANDS, dtype=F32).reshape(HY_BANDS, 1)
    full = lambda shape: pl.BlockSpec(shape, lambda i: (0,) * len(shape))
    colv = lambda v: v.reshape(hid, 1)
    return pl.pallas_call(
        functools.partial(_filter_body, half_tiles=half_tiles, tr=tr, seq=seq),
        out_shape=(jax.ShapeDtypeStruct((width // LANES, n_rows, LANES), F32),
                   jax.ShapeDtypeStruct((2, 1, width), F32)),
        grid=(n_rows // tr,),
        in_specs=[
            full((hid, 1)), full((hid, 2 * HY_BANDS)), full((hid, 1)), full((hid, hid)), full((hid, 1)),
            full((hid, hid)), full((hid, 1)), full((hid, 1)), full((HY_BANDS, 1)),
            pl.BlockSpec((1, hid, width), lambda i: (i // half_tiles, 0, 0)),
            full((1, width)),
        ],
        out_specs=(pl.BlockSpec((width // LANES, tr, LANES), lambda i: (0, i, 0)),
                   pl.BlockSpec((1, 1, width), lambda i: (i // half_tiles, 0, 0))),
        compiler_params=_params(("arbitrary",)),
        name="hyena_filters",
    )(w1[0:1].T, w1[1:].T, colv(b1), w2.T, colv(b2), w3.T, colv(b3), colv(freq), bands, w_out2, deltas2)


def _dft_consts(n1, n2):
    n = n1 * n2
    k = np.arange(n1)
    ang1 = 2.0 * np.pi * ((k[:, None] * k[None, :]) % n1) / n1
    c1, s1 = np.cos(ang1), np.sin(ang1)
    m = np.arange(n2)
    ang2 = 2.0 * np.pi * ((m[:, None] * m[None, :]) % n2) / n2
    c2, s2 = np.cos(ang2), np.sin(ang2)
    angt = 2.0 * np.pi * (k[:, None] * m[None, :]) / n
    as_bf = lambda a: jnp.asarray(a, F32).astype(BF16)
    f32 = lambda a: jnp.asarray(a, F32)
    c1j, s1j = f32(c1)[None], f32(s1)[None]
    tcj, tsj = f32(np.cos(angt)).T[:, :, None], f32(np.sin(angt)).T[:, :, None]
    cos_kn = c1j * tcj - s1j * tsj
    sin_kn = s1j * tcj + c1j * tsj
    half = n1 // 2
    return dict(
        fwd1=jnp.concatenate([cos_kn, -sin_kn], axis=1).astype(BF16),
        inv1=(jnp.concatenate([cos_kn, -sin_kn], axis=1).transpose(0, 2, 1)[:, :half] / n).astype(BF16),
        fwd2=as_bf(np.block([[c2, s2], [-s2, c2]])),
        inv2=as_bf(np.block([[c2, -s2], [s2, c2]])),
    )


def _flatten_block(ref, buf):
    groups, n, t2, lanes = ref.shape
    for g in range(groups):
        buf[g] = ref[g].reshape(n * t2, lanes)


def _unflatten_block(buf, ref):
    groups, n, t2, lanes = ref.shape
    for g in range(groups):
        ref[g] = buf[g].reshape(n, t2, lanes)


def _load_strided(buf, s, t2):
    n = buf.shape[1] // t2
    return jnp.concatenate([buf[g, pl.ds(s, n, stride=t2), :] for g in range(buf.shape[0])], axis=1)


def _store_strided(buf, s, t2, x):
    n = buf.shape[1] // t2
    for g in range(buf.shape[0]):
        buf[g, pl.ds(s, n, stride=t2), :] = x[:, g * LANES:(g + 1) * LANES]


def _flat_scratch(groups, n, t2, dtype=F32):
    return pltpu.VMEM((groups, n * t2, LANES), dtype)


U32 = jnp.uint32


def _pack_complex(re, im):
    hi = lax.bitcast_convert_type(re.astype(BF16).astype(F32), U32)
    lo = lax.bitcast_convert_type(im.astype(BF16).astype(F32), U32)
    return hi | (lo >> 16)


def _unpack_complex(w):
    re = lax.bitcast_convert_type(w & jnp.uint32(0xFFFF0000), F32)
    im = lax.bitcast_convert_type(w << 16, F32)
    return re, im


def _outer_fwd_body(x_ref, f_ref, *rest, scaled):
    if scaled:
        s_ref, a_ref, xb, ab = rest
        inv = 1.0 / (s_ref[...] + EPS)
    else:
        a_ref, xb, ab = rest
    rows, t2 = x_ref.shape[1:3]
    n1 = a_ref.shape[1]
    _flatten_block(x_ref, xb)
    for s in range(t2):
        x = _load_strided(xb, s, t2)
        if scaled:
            half = rows // 2
            x = jnp.concatenate([x[:half] * inv[0], x[half:] * inv[1]], axis=0)
        a = _dot(f_ref[s], x.astype(BF16))
        _store_strided(ab, s, t2, _pack_complex(a[:n1], a[n1:]))
    _unflatten_block(ab, a_ref)


def _outer_fwd(x4, fwd1, sums=None, *, t2=SUBLANES, gt=4):
    groups, rows, n2, _ = x4.shape
    n1 = fwd1.shape[2]
    scaled = sums is not None
    blk = lambda r: pl.BlockSpec((gt, r, t2, LANES), lambda j, c: (c, 0, j, 0))
    args = [x4, fwd1]
    in_specs = [blk(rows), pl.BlockSpec((t2, 2 * n1, rows), lambda j, c: (j, 0, 0))]
    if scaled:
        args.append(sums)
        in_specs.append(pl.BlockSpec((2, 1, gt * LANES), lambda j, c: (0, 0, c)))
    return pl.pallas_call(
        functools.partial(_outer_fwd_body, scaled=scaled),
        out_shape=jax.ShapeDtypeStruct((groups, n1, n2, LANES), U32),
        grid=(n2 // t2, groups // gt),
        in_specs=in_specs,
        out_specs=blk(n1),
        scratch_shapes=[_flat_scratch(gt, rows, t2), _flat_scratch(gt, n1, t2, U32)],
        compiler_params=_params(("parallel", "parallel")),
        name="dft_outer_fwd",
    )(*args)


def _load_slab(ref, s):
    return jnp.concatenate([ref[g, s] for g in range(ref.shape[0])], axis=1)


def _store_slab(ref, s, x):
    for g in range(ref.shape[0]):
        ref[g, s] = x[:, g * LANES:(g + 1) * LANES]


def _inner_fwd(a_packed, f2):
    n2 = a_packed.shape[0]
    ar, ai = _unpack_complex(a_packed)
    x = _dot(f2, jnp.concatenate([ar, ai], axis=0).astype(BF16))
    return x[:n2], x[n2:]


def _mid_spec_body(a_ref, f2_ref, xr_ref, xi_ref):
    for s in range(a_ref.shape[1]):
        xr, xi = _inner_fwd(_load_slab(a_ref, s), f2_ref[...])
        _store_slab(xr_ref, s, xr.astype(xr_ref.dtype))
        _store_slab(xi_ref, s, xi.astype(xi_ref.dtype))


def _mid_conv_body(a_ref, kr_ref, ki_ref, f2_ref, g2_ref, d_ref):
    n2 = a_ref.shape[2]
    for s in range(a_ref.shape[1]):
        xr, xi = _inner_fwd(_load_slab(a_ref, s), f2_ref[...])
        kr, ki = _load_slab(kr_ref, s).astype(F32), _load_slab(ki_ref, s).astype(F32)
        yr = xr * kr - xi * ki
        yi = xr * ki + xi * kr
        c = _dot(g2_ref[...], jnp.concatenate([yr, yi], axis=0).astype(BF16))
        _store_slab(d_ref, s, _pack_complex(c[:n2], c[n2:]))


def _mid(a, fwd2, inv2=None, kr=None, ki=None, order=0, *, kb=4, gt=4):
    groups, n1, n2, _ = a.shape
    conv = kr is not None
    slab = pl.BlockSpec((gt, kb, n2, LANES), lambda i, q: (q, i, 0, 0))
    mat = pl.BlockSpec((2 * n2, 2 * n2), lambda i, q: (0, 0))
    if conv:
        kslab = pl.BlockSpec((gt, kb, n2, LANES), lambda i, q: (order * (groups // gt) + q, i, 0, 0))
        args = (a, kr, ki, fwd2, inv2)
        in_specs = [slab, kslab, kslab, mat, mat]
        body, name = _mid_conv_body, "hyena_mid_conv"
        out_shape, out_specs = jax.ShapeDtypeStruct(a.shape, U32), slab
    else:
        args = (a, fwd2)
        in_specs = [slab, mat]
        body, name = _mid_spec_body, "hyena_mid_spec"
        out_shape, out_specs = (jax.ShapeDtypeStruct(a.shape, BF16),) * 2, (slab, slab)
    return pl.pallas_call(
        body,
        out_shape=out_shape,
        grid=(n1 // kb, groups // gt),
        in_specs=in_specs,
        out_specs=out_specs,
        compiler_params=_params(("parallel", "parallel")),
        name=name,
    )(*args)


def _outer_inv_body(d_ref, e_ref, gate_ref, z_ref, skip_ref, o_ref, db, gb, zb, ob):
    skip = skip_ref[...]
    t2 = d_ref.shape[2]
    for ref, buf in ((d_ref, db), (gate_ref, gb), (z_ref, zb)):
        _flatten_block(ref, buf)
    for s in range(t2):
        dr, di = _unpack_complex(_load_strided(db, s, t2))
        y = _dot(e_ref[s], jnp.concatenate([dr, di], axis=0).astype(BF16))
        _store_strided(ob, s, t2, _load_strided(gb, s, t2) * (y + skip * _load_strided(zb, s, t2)))
    _unflatten_block(ob, o_ref)


def _outer_inv(d, inv1, gate4, z4, skip, *, t2=SUBLANES):
    groups, n1, n2, _ = d.shape
    half = n1 // 2
    blk = lambda r: pl.BlockSpec((groups, r, t2, LANES), lambda j: (0, 0, j, 0))
    return pl.pallas_call(
        _outer_inv_body,
        out_shape=jax.ShapeDtypeStruct((groups, half, n2, LANES), F32),
        grid=(n2 // t2,),
        in_specs=[blk(n1), pl.BlockSpec((t2, half, 2 * n1), lambda j: (j, 0, 0)), blk(half), blk(half),
                  pl.BlockSpec((1, groups * LANES), lambda j: (0, 0))],
        out_specs=blk(half),
        scratch_shapes=[_flat_scratch(groups, n1, t2, U32)] + [_flat_scratch(groups, half, t2)] * 3,
        compiler_params=_params(("parallel",)),
        name="dft_outer_inv",
    )(d, inv1, gate4, z4, skip)


def _log_sigmoid(x):
    return jnp.minimum(x, 0.0) - jnp.log1p(jnp.exp(-jnp.abs(x)))


_NT = (((1,), (1,)), ((), ()))
_TN = (((0,), (0,)), ((), ()))


def _chunk_matrices(sub):
    i = np.arange(sub)
    same = (i[:, None] // GLA_CHUNK) == (i[None, :] // GLA_CHUNK)
    fwd = same & (i[None, :] <= i[:, None])
    bwd = same & (i[None, :] >= i[:, None])
    return jnp.asarray(np.stack([fwd, bwd, same]), F32)


def _gla_gates_body(q_ref, k_ref, v_ref, lr_ref, w_ref, b_ref, cm_ref,
                    qe_ref, ke_ref, kd_ref, dec_ref, vb_ref, *, scale):
    tg, width = q_ref.shape
    sub = cm_ref.shape[1]
    chunk = GLA_CHUNK
    vb_ref[...] = v_ref[...].astype(BF16)
    same = cm_ref[2].astype(BF16)
    for s in range(tg // sub):
        rs = slice(s * sub, (s + 1) * sub)
        g = _log_sigmoid(_dot(lr_ref[rs, :].astype(BF16), w_ref[...]) + b_ref[...]) / GLA_TAU
        g_hi = g.astype(BF16)
        g_lo = (g - g_hi.astype(F32)).astype(BF16)
        q = q_ref[rs, :] * scale
        k = k_ref[rs, :]
        for d in range(2):
            cs = slice(d * width, (d + 1) * width)
            cum = cm_ref[d].astype(BF16)
            bc = _dot(cum, g_hi[:, cs]) + _dot(cum, g_lo[:, cs])
            bl = _dot(same, g_hi[:, cs]) + _dot(same, g_lo[:, cs])
            qe_ref[d, rs, :] = (q * jnp.exp(bc)).astype(BF16)
            ke_ref[d, rs, :] = (k * jnp.exp(-bc)).astype(BF16)
            kd_ref[d, rs, :] = (k * jnp.exp(bl - bc)).astype(BF16)
            for c in range(sub // chunk):
                row = s * (sub // chunk) + c
                dec_ref[d, row:row + 1, :] = jnp.exp(bl[c * chunk:c * chunk + 1])


def _gla_gates(p, lr, w_both, b_both, *, q_off, k_off, v_off, dk, dv, heads, tg=512, sub=256):
    L = p.shape[0]
    tg = min(tg, L)
    wk, wv = heads * dk, heads * dv
    opnd = jax.ShapeDtypeStruct((2, L, wk), BF16)
    row = lambda width, off: pl.BlockSpec((tg, width), lambda i: (i, off // width))
    both = pl.BlockSpec((2, tg, wk), lambda i: (0, i, 0))
    const = lambda shape: pl.BlockSpec(shape, lambda i: (0,) * len(shape))
    return pl.pallas_call(
        functools.partial(_gla_gates_body, scale=dk ** -0.5),
        out_shape=(opnd, opnd, opnd, jax.ShapeDtypeStruct((2, L // GLA_CHUNK, wk), F32),
                   jax.ShapeDtypeStruct((L, wv), BF16)),
        grid=(L // tg,),
        in_specs=[row(wk, q_off), row(wk, k_off), row(wv, v_off), row(LANES, 0),
                  const((LANES, 2 * wk)), const((1, 2 * wk)), const((3, sub, sub))],
        out_specs=(both, both, both, pl.BlockSpec((2, tg // GLA_CHUNK, wk), lambda i: (0, i, 0)),
                   pl.BlockSpec((tg, wv), lambda i: (i, 0))),
        compiler_params=_params(("parallel",)),
        name="gla_gates",
    )(p, p, p, lr, w_both, b_both, _chunk_matrices(sub))


def _gla_direction(qe_ref, ke_ref, kd_ref, v_ref, dec_ref, mask, o_ref, st_ref, *, reverse):
    tb = qe_ref.shape[0]
    sub = mask.shape[0]
    chunk = GLA_CHUNK
    intras = []
    for s in range(tb // sub):
        rs = slice(s * sub, (s + 1) * sub)
        a = lax.dot_general(qe_ref[rs, :], ke_ref[rs, :], _NT, preferred_element_type=F32)
        intra = _dot(jnp.where(mask > 0.5, a, 0.0).astype(BF16), v_ref[rs, :])
        intras += [intra[c * chunk:(c + 1) * chunk] for c in range(sub // chunk)]
    st = st_ref[...]
    chunks = range(tb // chunk)
    for c in (reversed(chunks) if reverse else chunks):
        sl = slice(c * chunk, (c + 1) * chunk)
        o_ref[sl, :] = intras[c] + lax.dot_general(qe_ref[sl, :], st.astype(BF16), _NT,
                                                   preferred_element_type=F32)
        upd = lax.dot_general(v_ref[sl, :], kd_ref[sl, :], _TN, preferred_element_type=F32)
        st = st * dec_ref[c:c + 1, :] + upd
    st_ref[...] = st


def _gla_body(qef, kef, kdf, vf, decf, qeb, keb, kdb, vb, decb, cm_ref, of_ref, ob_ref, sf_ref, sb_ref):
    @pl.when(pl.program_id(1) == 0)
    def _():
        sf_ref[...] = jnp.zeros_like(sf_ref)
        sb_ref[...] = jnp.zeros_like(sb_ref)

    _gla_direction(qef, kef, kdf, vf, decf, cm_ref[0], of_ref, sf_ref, reverse=False)
    _gla_direction(qeb, keb, kdb, vb, decb, cm_ref[1], ob_ref, sb_ref, reverse=True)


def _gla(qe, ke, kd, dec, vb, *, dk, dv, tb=512, sub=256):
    L = vb.shape[0]
    tb = min(tb, L)
    nb = L // tb
    heads = vb.shape[1] // dv
    nc = tb // GLA_CHUNK
    blk = lambda i, rev: (nb - 1 - i) if rev else i
    opnd = lambda rev: pl.BlockSpec((None, tb, dk), lambda h, i: (int(rev), blk(i, rev), h))
    vspec = lambda rev: pl.BlockSpec((tb, dv), lambda h, i: (blk(i, rev), h))
    dspec = lambda rev: pl.BlockSpec((None, nc, dk), lambda h, i: (int(rev), blk(i, rev), h))
    side = lambda rev: [opnd(rev), opnd(rev), opnd(rev), vspec(rev), dspec(rev)]
    return pl.pallas_call(
        _gla_body,
        out_shape=(jax.ShapeDtypeStruct((L, heads * dv), F32),) * 2,
        grid=(heads, nb),
        in_specs=side(False) + side(True) + [pl.BlockSpec((3, sub, sub), lambda h, i: (0, 0, 0))],
        out_specs=(vspec(False), vspec(True)),
        scratch_shapes=[pltpu.VMEM((dv, dk), F32), pltpu.VMEM((dv, dk), F32)],
        compiler_params=_params(("parallel", "arbitrary")),
        name="gla",
    )(qe, ke, kd, vb, dec, qe, ke, kd, vb, dec, _chunk_matrices(sub))


def _gla_post_body(of_ref, ob_ref, r_ref, gn_ref, c_ref, *, dv):
    o = of_ref[...] + ob_ref[...]
    r = r_ref[...]
    gate = r * jax.nn.sigmoid(r)
    for hd in range(o.shape[1] // dv):
        sl = slice(hd * dv, (hd + 1) * dv)
        c_ref[:, sl] = (_rms(o[:, sl], gn_ref[...]) * gate[:, sl]).astype(BF16)


def _gla_post(o_f, o_b, p, gla_norm, *, r_off, tm=512):
    L, wc = o_f.shape
    dv = gla_norm.shape[0]
    row = lambda blk=0: pl.BlockSpec((tm, wc), lambda i: (i, blk))
    return pl.pallas_call(
        functools.partial(_gla_post_body, dv=dv),
        out_shape=jax.ShapeDtypeStruct((L, wc), BF16),
        grid=(L // tm,),
        in_specs=[row(), row(), row(r_off // wc), pl.BlockSpec((1, dv), lambda i: (0, 0))],
        out_specs=row(),
        compiler_params=_params(("parallel",)),
        name="gla_post",
    )(o_f, o_b, p, gla_norm.reshape(1, dv))


def _merge_body(h_ref, g_ref, a_ref, z_ref, c_ref,
                wga_ref, wgb_ref, wgc_ref, bga_ref, bgb_ref, bgc_ref,
                wa_ref, wb_ref, wc_ref, m_ref, u_ref, zb_ref):
    @pl.when(pl.program_id(1) == 0)
    def _():
        u_ref[...] = _rms(h_ref[...], g_ref[...]).astype(BF16)
        zb_ref[...] = _load_groups(z_ref).astype(BF16)

    u = u_ref[...]
    m = jax.nn.sigmoid(_dot(u, wga_ref[...]) + bga_ref[...]) * _dot(a_ref[...], wa_ref[...])
    m += jax.nn.sigmoid(_dot(u, wgb_ref[...]) + bgb_ref[...]) * _dot(zb_ref[...], wb_ref[...])
    m += jax.nn.sigmoid(_dot(u, wgc_ref[...]) + bgc_ref[...]) * _dot(c_ref[...], wc_ref[...])
    m_ref[...] = m.astype(BF16)


def _merge(h, norm_g, a_pre, z, c_pre, w_gate, b_gate, w_a, w_b, w_c, *, tm=1024, tn=512):
    L, D = h.shape
    wa, wb, wc = w_a.shape[0], w_b.shape[0], w_c.shape[0]
    tm = min(tm, L)
    nj = D // tn
    row = lambda width: pl.BlockSpec((tm, width), lambda i, j: (i, 0))
    gate_w = lambda k: pl.BlockSpec((D, tn), lambda i, j: (0, j + k * nj))
    gate_b = lambda k: pl.BlockSpec((1, tn), lambda i, j: (0, j + k * nj))
    br_w = lambda width: pl.BlockSpec((width, tn), lambda i, j: (0, j))
    return pl.pallas_call(
        _merge_body,
        out_shape=jax.ShapeDtypeStruct((L, D), BF16),
        grid=(L // tm, nj),
        in_specs=[
            pl.BlockSpec((tm, D), lambda i, j: (i, 0), pipeline_mode=pl.Buffered(1)),
            pl.BlockSpec((1, D), lambda i, j: (0, 0)),
            row(wa), pl.BlockSpec((wb // LANES, tm, LANES), lambda i, j: (0, i, 0)), row(wc),
            gate_w(0), gate_w(1), gate_w(2), gate_b(0), gate_b(1), gate_b(2),
            br_w(wa), br_w(wb), br_w(wc),
        ],
        out_specs=pl.BlockSpec((tm, tn), lambda i, j: (i, j)),
        scratch_shapes=[pltpu.VMEM((tm, D), BF16), pltpu.VMEM((tm, wb), BF16)],
        compiler_params=_params(("parallel", "arbitrary")),
        name="merge",
    )(h, norm_g.reshape(1, D), a_pre, z, c_pre,
      w_gate, w_gate, w_gate, b_gate, b_gate, b_gate, w_a, w_b, w_c)


def _out_proj_body(h_ref, m_ref, w_ref, o_ref, *, tn):
    m = m_ref[...]
    for c in range(o_ref.shape[1] // tn):
        sl = slice(c * tn, (c + 1) * tn)
        o_ref[:, sl] = h_ref[:, sl] + _dot(m, w_ref[:, sl])


def _out_proj(h, m, w_o, *, tm=512, tn=512):
    L, D = h.shape
    row = pl.BlockSpec((tm, D), lambda i: (i, 0))
    return pl.pallas_call(
        functools.partial(_out_proj_body, tn=tn),
        out_shape=jax.ShapeDtypeStruct((L, D), F32),
        grid=(L // tm,),
        in_specs=[row, row, pl.BlockSpec((D, D), lambda i: (0, 0), pipeline_mode=pl.Buffered(1))],
        out_specs=row,
        compiler_params=_params(("parallel",)),
        name="out_proj",
    )(h, m, w_o)


def _hyena(z0, g1, g2, skip, filt, consts):
    groups, L, _ = z0.shape
    n1 = DFT_N1
    n2 = 2 * L // n1
    k_time, sums = filt
    kf = _outer_fwd(k_time.reshape(k_time.shape[0], n1, n2, LANES), consts["fwd1"], sums)
    kr, ki = _mid(kf, consts["fwd2"])
    z = z0.reshape(groups, n1 // 2, n2, LANES)
    for order, gate in enumerate((g1, g2)):
        a = _outer_fwd(z, consts["fwd1"])
        d = _mid(a, consts["fwd2"], consts["inv2"], kr, ki, order)
        z = _outer_inv(d, consts["inv1"], gate.reshape(z.shape), z, skip[order].reshape(1, groups * LANES))
    return z.reshape(groups, L, LANES)


def kernel(x, ffn1_norm, ffn1_w_gu, ffn1_w_down, mix_norm, w_in, b_in, conv_a, conv_b, hf_w1, hf_b1, hf_w2, hf_b2, hf_w3, hf_b3, hf_w_out, hf_freq, hy_skip, gk_w2, gk_b, gla_norm, w_br_a, w_br_b, w_br_c, w_o, ffn2_norm, ffn2_w_gu, ffn2_w_down, final_norm):
    bsz, L, D = x.shape
    depth = ffn1_norm.shape[0]
    wa = conv_a.shape[2]
    wb = conv_b.shape[3]
    wck = gk_w2.shape[3]
    dv = gla_norm.shape[1]
    wcv = GLA_HEADS * dv
    dk = wck // GLA_HEADS
    hid = hf_w2.shape[1]
    n_main = 3 * wa + 3 * wb + 2 * wck + 2 * wcv
    q_off = 3 * wa + 3 * wb
    k_off, v_off, r_off = q_off + wck, q_off + 2 * wck, q_off + 2 * wck + wcv
    lr_off = n_main
    gate_off = n_main + 2 * GLA_RANK
    n1 = DFT_N1
    n2 = 2 * L // n1

    consts = _dft_consts(n1, n2)
    deltas =jnp.linspace(math.log(HY_TARGET) / HY_SLOW_DECAY, math.log(HY_TARGET) / HY_FAST_DECAY, wb, dtype=F32)
    deltas2 = jnp.tile(deltas.reshape(1, wb), (1, HY_ORDER))

    outs = []
    for bi in range(bsz):
        h = x[bi]
        for l in range(depth):
            h = _ffn(h, ffn1_norm[l], _to_bf16(ffn1_w_gu, l), _to_bf16(ffn1_w_down, l),
                     final_norm, final=False)

            w_lr = jnp.pad(w_in[l][:, lr_off:gate_off], ((0, 0), (0, LANES - 2 * GLA_RANK))).astype(BF16)
            b_lr = jnp.pad(b_in[l][lr_off:gate_off], (0, LANES - 2 * GLA_RANK)).reshape(1, LANES)
            p, lr = _proj(h, mix_norm[l], w_in[l][:, :n_main].astype(BF16), b_in[l][:n_main].reshape(1, n_main),
                          w_lr, b_lr)

            a_pre, z0, g1, g2 = _prep(p, conv_a[l], conv_b[l], wa=wa)

            w_out2 = hf_w_out[l].reshape(hid, HY_ORDER, 2, wb).transpose(2, 0, 1, 3).reshape(2, hid, HY_ORDER * wb)
            filt = _filters(L, hf_w1[l], hf_b1[l], hf_w2[l], hf_b2[l], hf_w3[l], hf_b3[l], hf_freq[l],
                            w_out2, deltas2)
            zb = _hyena(z0, g1, g2, hy_skip[l], filt, consts)

            w_both = jnp.zeros((LANES, 2 * wck), F32)
            w_both = w_both.at[:GLA_RANK, :wck].set(gk_w2[l, 0]).at[GLA_RANK:2 * GLA_RANK, wck:].set(gk_w2[l, 1])
            qe, ke, kd, dec, vb = _gla_gates(p, lr, w_both.astype(BF16), gk_b[l].reshape(1, 2 * wck),
                                             q_off=q_off, k_off=k_off, v_off=v_off, dk=dk, dv=dv,
                                             heads=GLA_HEADS)
            o_f, o_b = _gla(qe, ke, kd, dec, vb, dk=dk, dv=dv)

            c_pre = _gla_post(o_f, o_b, p, gla_norm[l], r_off=r_off)

            merged = _merge(h, mix_norm[l], a_pre, zb, c_pre,
                            w_in[l][:, gate_off:].astype(BF16), b_in[l][gate_off:].reshape(1, 3 * D),
                            _to_bf16(w_br_a, l), _to_bf16(w_br_b, l), _to_bf16(w_br_c, l))
            h = _out_proj(h, merged, _to_bf16(w_o, l))

            h = _ffn(h, ffn2_norm[l], _to_bf16(ffn2_w_gu, l), _to_bf16(ffn2_w_down, l),
                     final_norm, final=(l == depth - 1))
        outs.append(h)
    return jnp.stack(outs, axis=0)
```

```python
import functools
import math

import numpy as np
import jax
import jax.numpy as jnp
from jax import lax
from jax.experimental import pallas as pl
from jax.experimental.pallas import tpu as pltpu

F32 = jnp.float32
BF16 = jnp.bfloat16
HIGHEST = lax.Precision.HIGHEST

EPS = 1e-5
GLA_HEADS = 4
GLA_RANK = 16
GLA_TAU = 16.0
GLA_CHUNK = 64
HY_ORDER = 2
HY_BANDS = 16
HY_FAST_DECAY = 0.3
HY_SLOW_DECAY = 1.5
HY_TARGET = 1e-2

LANES = 128
SUBLANES = 8
DFT_N1 = 256
ROW_CHUNK = 128
VMEM_LIMIT = 56 * 1024 * 1024


def _params(sem):
    return pltpu.CompilerParams(dimension_semantics=sem, vmem_limit_bytes=VMEM_LIMIT)


def _rms(x, g):
    return x * lax.rsqrt(jnp.mean(x * x, axis=-1, keepdims=True) + EPS) * g


def _dot(a, b):
    return jnp.dot(a, b, preferred_element_type=F32)


def _dot_hi(a, b):
    return jnp.dot(a, b, preferred_element_type=F32, precision=HIGHEST)


def _cast_body(x_ref, o_ref):
    o_ref[...] = x_ref[...].astype(o_ref.dtype)


def _to_bf16(w, layer, cols=None, *, block_bytes=4 << 20):
    _, rows, width = w.shape
    cols = width if cols is None else cols
    tr = rows
    while tr * cols * 4 > block_bytes and tr % 32 == 0:
        tr //= 2
    return pl.pallas_call(
        _cast_body,
        out_shape=jax.ShapeDtypeStruct((rows, cols), BF16),
        grid=(rows // tr,),
        in_specs=[pl.BlockSpec((None, tr, cols), lambda i: (layer, i, 0))],
        out_specs=pl.BlockSpec((tr, cols), lambda i: (i, 0)),
        compiler_params=_params(("parallel",)),
        name="to_bf16",
    )(w)


def _ffn_body(x_ref, g_ref, wg_ref, wu_ref, wd_ref, fn_ref, o_ref, xn_ref, *, final, sub):
    j = pl.program_id(1)

    row_chunks = [slice(r, r + ROW_CHUNK) for r in range(0, x_ref.shape[0], ROW_CHUNK)]

    @pl.when(j == 0)
    def _():
        for rs in row_chunks:
            xn_ref[rs, :] = _rms(x_ref[rs, :], g_ref[...]).astype(BF16)
        o_ref[...] = jnp.zeros_like(o_ref)

    xn = xn_ref[...]
    hs = []
    for c in range(wg_ref.shape[1] // sub):
        sl = slice(c * sub, (c + 1) * sub)
        g = _dot(xn, wg_ref[:, sl])
        u = _dot(xn, wu_ref[:, sl])
        hs.append((g * jax.nn.sigmoid(g) * u).astype(BF16))
    o_ref[...] += _dot(jnp.concatenate(hs, axis=1), wd_ref[...])

    @pl.when(j == pl.num_programs(1) - 1)
    def _():
        for rs in row_chunks:
            y = x_ref[rs, :] + 0.5 * o_ref[rs, :]
            if final:
                y = _rms(y, fn_ref[...])
            o_ref[rs, :] = y


def _ffn(x, norm_g, w_gu, w_down, final_g, *, final, tm=1024, tf=512, sub=256):
    L, D = x.shape
    F = w_down.shape[0]
    tm = min(tm, L)
    if final:
        tf = sub
    nf = F // tf
    return pl.pallas_call(
        functools.partial(_ffn_body, final=final, sub=min(sub, tf)),
        out_shape=jax.ShapeDtypeStruct((L, D), F32),
        grid=(L // tm, nf),
        in_specs=[
            pl.BlockSpec((tm, D), lambda i, j: (i, 0)),
            pl.BlockSpec((1, D), lambda i, j: (0, 0)),
            pl.BlockSpec((D, tf), lambda i, j: (0, j)),
            pl.BlockSpec((D, tf), lambda i, j: (0, j + nf)),
            pl.BlockSpec((tf, D), lambda i, j: (j, 0)),
            pl.BlockSpec((1, D), lambda i, j: (0, 0)),
        ],
        out_specs=pl.BlockSpec((tm, D), lambda i, j: (i, 0)),
        scratch_shapes=[pltpu.VMEM((tm, D), BF16)],
        compiler_params=_params(("parallel", "arbitrary")),
        name="ffn",
    )(x, norm_g.reshape(1, D), w_gu, w_gu, w_down, final_g.reshape(1, D))


def _proj_body(h_ref, g_ref, w_ref, b_ref, wlr_ref, blr_ref, p_ref, lr_ref, *, tn):
    xn = _rms(h_ref[...], g_ref[...]).astype(BF16)
    lr_ref[...] = _dot(xn, wlr_ref[...]) + blr_ref[...]
    for c in range(p_ref.shape[1] // tn):
        sl = slice(c * tn, (c + 1) * tn)
        p_ref[:, sl] = _dot(xn, w_ref[:, sl]) + b_ref[:, sl]


def _proj(h, norm_g, w, b, w_lr, b_lr, *, tm=256, tn=512):
    L, D = h.shape
    n_out = w.shape[1]
    resident = lambda shape: pl.BlockSpec(shape, lambda i: (0, 0), pipeline_mode=pl.Buffered(1))
    return pl.pallas_call(
        functools.partial(_proj_body, tn=tn),
        out_shape=(jax.ShapeDtypeStruct((L, n_out), F32), jax.ShapeDtypeStruct((L, LANES), F32)),
        grid=(L // tm,),
        in_specs=[
            pl.BlockSpec((tm, D), lambda i: (i, 0)),
            resident((1, D)), resident((D, n_out)), resident((1, n_out)),
            resident((D, LANES)), resident((1, LANES)),
        ],
        out_specs=(pl.BlockSpec((tm, n_out), lambda i: (i, 0)),
                   pl.BlockSpec((tm, LANES), lambda i: (i, 0))),
        compiler_params=_params(("parallel",)),
        name="proj",
    )(h, norm_g.reshape(1, D), w, b, w_lr, b_lr)


def _conv3(x, x_before, x_after, w):
    tm = x.shape[0]
    row = lax.broadcasted_iota(jnp.int32, x.shape, 0)
    down = jnp.where(row == 0, x_before, pltpu.roll(x, 1, axis=0))
    up = jnp.where(row == tm - 1, x_after, pltpu.roll(x, tm - 1, axis=0))
    return down * w[0:1] + x * w[1:2] + up * w[2:3]


def _prep_body(m_ref, pv_ref, nx_ref, ca_w_ref, cb_w_ref, a_ref, z_ref, g1_ref, g2_ref, *, wa):
    i = pl.program_id(0)
    has_prev = (i > 0).astype(F32)
    has_next = (i < pl.num_programs(0) - 1).astype(F32)
    last = SUBLANES - 1

    def sec(k):
        sl = slice(k * wa, (k + 1) * wa)
        return (m_ref[:, sl], pv_ref[last:last + 1, sl] * has_prev, nx_ref[0:1, sl] * has_next)

    xa, xa_p, xa_n = sec(0)
    ba, _, _ = sec(1)
    ca, ca_p, ca_n = sec(2)
    a_ref[...] = (ba * _conv3(ca * xa, ca_p * xa_p, ca_n * xa_n, ca_w_ref[...])).astype(BF16)
    for k, o_ref in enumerate((z_ref, g1_ref, g2_ref)):
        x, x_p, x_n = sec(3 + k)
        _store_groups(o_ref, _conv3(x, x_p, x_n, cb_w_ref[k]))


def _store_groups(o_ref, x):
    for g in range(o_ref.shape[0]):
        o_ref[g] = x[:, g * LANES:(g + 1) * LANES]


def _load_groups(ref):
    return jnp.concatenate([ref[g] for g in range(ref.shape[0])], axis=1)


def _prep(p, conv_a, conv_b, *, wa, tm=256):
    L = p.shape[0]
    width = 6 * wa
    rb = tm // SUBLANES
    nrb = L // SUBLANES
    grouped = pl.BlockSpec((wa // LANES, tm, LANES), lambda i: (0, i, 0))
    return pl.pallas_call(
        functools.partial(_prep_body, wa=wa),
        out_shape=(jax.ShapeDtypeStruct((L, wa), BF16),)
        + (jax.ShapeDtypeStruct((wa // LANES, L, LANES), F32),) * 3,
        grid=(L // tm,),
        in_specs=[
            pl.BlockSpec((tm, width), lambda i: (i, 0)),
            pl.BlockSpec((SUBLANES, width), lambda i: (jnp.maximum(i * rb - 1, 0), 0)),
            pl.BlockSpec((SUBLANES, width), lambda i: (jnp.minimum((i + 1) * rb, nrb - 1), 0)),
            pl.BlockSpec((3, wa), lambda i: (0, 0)),
            pl.BlockSpec((3, 3, wa), lambda i: (0, 0, 0)),
        ],
        out_specs=(pl.BlockSpec((tm, wa), lambda i: (i, 0)), grouped, grouped, grouped),
        compiler_params=_params(("parallel",)),
        name="prep",
    )(p, p, p, conv_a, conv_b)


def _filter_body(w1t_ref, w1cs_ref, b1_ref, w2_ref, b2_ref, w3_ref, b3_ref, fr_ref, bands_ref, wo_ref, dl_ref,
                 k_ref, s_ref, *, half_tiles, tr, seq):
    i = pl.program_id(0)

    def lag_of(n):
        return jnp.where(n < seq, n, jnp.where(n == seq, 0, 2 * seq - n)).astype(F32)

    lag_row = lag_of(i * tr + lax.broadcasted_iota(jnp.int32, (1, tr), 1))
    ang = ((2.0 * math.pi / seq) * lag_row) * bands_ref[...]
    feats = jnp.concatenate([jnp.cos(ang), -jnp.sin(ang)], axis=0)
    fr = fr_ref[...]
    pre = w1t_ref[...] * (lag_row / max(seq - 1, 1)) + _dot_hi(w1cs_ref[...], feats) + b1_ref[...]
    h = jnp.sin(fr * pre)
    h = jnp.sin(fr * (_dot_hi(w2_ref[...], h) + b2_ref[...]))
    h = jnp.sin(fr * (_dot_hi(w3_ref[...], h) + b3_ref[...]))
    t = lag_of(i * tr + lax.broadcasted_iota(jnp.int32, (tr, 1), 0)) / max(seq - 1, 1)
    decay = jnp.exp(-t * jnp.abs(dl_ref[...]))
    hd = lax.dot_general(h, wo_ref[0], _TN, preferred_element_type=F32, precision=HIGHEST) * decay

    @pl.when(i % half_tiles == 0)
    def _():
        s_ref[...] = jnp.zeros_like(s_ref)

    s_ref[0] += jnp.sum(jnp.abs(hd), axis=0, keepdims=True)
    row = lax.broadcasted_iota(jnp.int32, hd.shape, 0)
    zero_row = jnp.where(i == half_tiles, 0, -1)
    _store_groups(k_ref, jnp.where(row == zero_row, 0.0, hd))


def _filters(seq, w1, b1, w2, b2, w3, b3, freq, w_out2, deltas2, *, tr=512):
    n_rows = 2 * seq
    hid = w2.shape[0]
    width = w_out2.shape[2]
    half_tiles = n_rows // (2 * tr)
    bands = jnp.linspace(1e-4, HY_BANDS - 1, HY_BANDS, dtype=F32).reshape(HY_BANDS, 1)
    full = lambda shape: pl.BlockSpec(shape, lambda i: (0,) * len(shape))
    colv = lambda v: v.reshape(hid, 1)
    return pl.pallas_call(
        functools.partial(_filter_body, half_tiles=half_tiles, tr=tr, seq=seq),
        out_shape=(jax.ShapeDtypeStruct((width // LANES, n_rows, LANES), F32),
                   jax.ShapeDtypeStruct((2, 1, width), F32)),
        grid=(n_rows // tr,),
        in_specs=[
            full((hid, 1)), full((hid, 2 * HY_BANDS)), full((hid, 1)), full((hid, hid)), full((hid, 1)),
            full((hid, hid)), full((hid, 1)), full((hid, 1)), full((HY_BANDS, 1)),
            pl.BlockSpec((1, hid, width), lambda i: (i // half_tiles, 0, 0)),
            full((1, width)),
        ],
        out_specs=(pl.BlockSpec((width // LANES, tr, LANES), lambda i: (0, i, 0)),
                   pl.BlockSpec((1, 1, width), lambda i: (i // half_tiles, 0, 0))),
        compiler_params=_params(("arbitrary",)),
        name="hyena_filters",
    )(w1[0:1].T, w1[1:].T, colv(b1), w2.T, colv(b2), w3.T, colv(b3), colv(freq), bands, w_out2, deltas2)


def _dft_consts(n1, n2):
    n = n1 * n2
    k = np.arange(n1)
    m = np.arange(n2)
    ang2 = 2.0 * np.pi * ((m[:, None] * m[None, :]) % n2) / n2
    c2, s2 = np.cos(ang2), np.sin(ang2)
    as_bf = lambda a: jnp.asarray(np.asarray(a, np.float32).astype(jnp.bfloat16))
    time_index = k[None, None, :] * n2 + m[:, None, None]
    ang = 2.0 * np.pi * ((k[None, :, None] * time_index) % n) / n
    stage = np.concatenate([np.cos(ang), -np.sin(ang)], axis=1)
    half = n1 // 2
    return dict(
        fwd1=as_bf(stage),
        inv1=as_bf(stage.transpose(0, 2, 1)[:, :half] / n),
        fwd2=as_bf(np.block([[c2, s2], [-s2, c2]])),
        inv2=as_bf(np.block([[c2, -s2], [s2, c2]])),
    )


def _flatten_block(ref, buf):
    groups, n, t2, lanes = ref.shape
    for g in range(groups):
        buf[g] = ref[g].reshape(n * t2, lanes)


def _unflatten_block(buf, ref):
    groups, n, t2, lanes = ref.shape
    for g in range(groups):
        ref[g] = buf[g].reshape(n, t2, lanes)


def _load_strided(buf, s, t2):
    n = buf.shape[1] // t2
    return jnp.concatenate([buf[g, pl.ds(s, n, stride=t2), :] for g in range(buf.shape[0])], axis=1)


def _store_strided(buf, s, t2, x):
    n = buf.shape[1] // t2
    for g in range(buf.shape[0]):
        buf[g, pl.ds(s, n, stride=t2), :] = x[:, g * LANES:(g + 1) * LANES]


def _flat_scratch(groups, n, t2, dtype=F32):
    return pltpu.VMEM((groups, n * t2, LANES), dtype)


U32 = jnp.uint32


def _pack_complex(re, im):
    hi = lax.bitcast_convert_type(re.astype(BF16).astype(F32), U32)
    lo = lax.bitcast_convert_type(im.astype(BF16).astype(F32), U32)
    return hi | (lo >> 16)


def _unpack_complex(w):
    re = lax.bitcast_convert_type(w & jnp.uint32(0xFFFF0000), F32)
    im = lax.bitcast_convert_type(w << 16, F32)
    return re, im


def _outer_fwd_body(x_ref, f_ref, *rest, scaled):
    if scaled:
        s_ref, a_ref, xb, ab = rest
        inv = 1.0 / (s_ref[...] + EPS)
    else:
        a_ref, xb, ab = rest
    rows, t2 = x_ref.shape[1:3]
    n1 = a_ref.shape[1]
    _flatten_block(x_ref, xb)
    for s in range(t2):
        x = _load_strided(xb, s, t2)
        if scaled:
            half = rows // 2
            x = jnp.concatenate([x[:half] * inv[0], x[half:] * inv[1]], axis=0)
        a = _dot(f_ref[s], x.astype(BF16))
        _store_strided(ab, s, t2, _pack_complex(a[:n1], a[n1:]))
    _unflatten_block(ab, a_ref)


def _outer_fwd(x4, fwd1, sums=None, *, t2=SUBLANES, gt=4):
    groups, rows, n2, _ = x4.shape
    n1 = fwd1.shape[2]
    scaled = sums is not None
    blk = lambda r: pl.BlockSpec((gt, r, t2, LANES), lambda j, c: (c, 0, j, 0))
    args = [x4, fwd1]
    in_specs = [blk(rows), pl.BlockSpec((t2, 2 * n1, rows), lambda j, c: (j, 0, 0))]
    if scaled:
        args.append(sums)
        in_specs.append(pl.BlockSpec((2, 1, gt * LANES), lambda j, c: (0, 0, c)))
    return pl.pallas_call(
        functools.partial(_outer_fwd_body, scaled=scaled),
        out_shape=jax.ShapeDtypeStruct((groups, n1, n2, LANES), U32),
        grid=(n2 // t2, groups // gt),
        in_specs=in_specs,
        out_specs=blk(n1),
        scratch_shapes=[_flat_scratch(gt, rows, t2), _flat_scratch(gt, n1, t2, U32)],
        compiler_params=_params(("parallel", "parallel")),
        name="dft_outer_fwd",
    )(*args)


def _load_slab(ref, s):
    return jnp.concatenate([ref[g, s] for g in range(ref.shape[0])], axis=1)


def _store_slab(ref, s, x):
    for g in range(ref.shape[0]):
        ref[g, s] = x[:, g * LANES:(g + 1) * LANES]


def _inner_fwd(a_packed, f2):
    n2 = a_packed.shape[0]
    ar, ai = _unpack_complex(a_packed)
    x = _dot(f2, jnp.concatenate([ar, ai], axis=0).astype(BF16))
    return x[:n2], x[n2:]


def _mid_spec_body(a_ref, f2_ref, xr_ref, xi_ref):
    for s in range(a_ref.shape[1]):
        xr, xi = _inner_fwd(_load_slab(a_ref, s), f2_ref[...])
        _store_slab(xr_ref, s, xr.astype(xr_ref.dtype))
        _store_slab(xi_ref, s, xi.astype(xi_ref.dtype))


def _mid_conv_body(a_ref, kr_ref, ki_ref, f2_ref, g2_ref, d_ref):
    n2 = a_ref.shape[2]
    for s in range(a_ref.shape[1]):
        xr, xi = _inner_fwd(_load_slab(a_ref, s), f2_ref[...])
        kr, ki = _load_slab(kr_ref, s).astype(F32), _load_slab(ki_ref, s).astype(F32)
        yr = xr * kr - xi * ki
        yi = xr * ki + xi * kr
        c = _dot(g2_ref[...], jnp.concatenate([yr, yi], axis=0).astype(BF16))
        _store_slab(d_ref, s, _pack_complex(c[:n2], c[n2:]))


def _mid(a, fwd2, inv2=None, kr=None, ki=None, order=0, *, kb=4, gt=4):
    groups, n1, n2, _ = a.shape
    conv = kr is not None
    slab = pl.BlockSpec((gt, kb, n2, LANES), lambda i, q: (q, i, 0, 0))
    mat = pl.BlockSpec((2 * n2, 2 * n2), lambda i, q: (0, 0))
    if conv:
        kslab = pl.BlockSpec((gt, kb, n2, LANES), lambda i, q: (order * (groups // gt) + q, i, 0, 0))
        args = (a, kr, ki, fwd2, inv2)
        in_specs = [slab, kslab, kslab, mat, mat]
        body, name = _mid_conv_body, "hyena_mid_conv"
        out_shape, out_specs = jax.ShapeDtypeStruct(a.shape, U32), slab
    else:
        args = (a, fwd2)
        in_specs = [slab, mat]
        body, name = _mid_spec_body, "hyena_mid_spec"
        out_shape, out_specs = (jax.ShapeDtypeStruct(a.shape, BF16),) * 2, (slab, slab)
    return pl.pallas_call(
        body,
        out_shape=out_shape,
        grid=(n1 // kb, groups // gt),
        in_specs=in_specs,
        out_specs=out_specs,
        compiler_params=_params(("parallel", "parallel")),
        name=name,
    )(*args)


def _outer_inv_body(d_ref, e_ref, gate_ref, z_ref, skip_ref, o_ref, db, gb, zb, ob):
    skip = skip_ref[...]
    t2 = d_ref.shape[2]
    for ref, buf in ((d_ref, db), (gate_ref, gb), (z_ref, zb)):
        _flatten_block(ref, buf)
    for s in range(t2):
        dr, di = _unpack_complex(_load_strided(db, s, t2))
        y = _dot(e_ref[s], jnp.concatenate([dr, di], axis=0).astype(BF16))
        _store_strided(ob, s, t2, _load_strided(gb, s, t2) * (y + skip * _load_strided(zb, s, t2)))
    _unflatten_block(ob, o_ref)


def _outer_inv(d, inv1, gate4, z4, skip, *, t2=SUBLANES):
    groups, n1, n2, _ = d.shape
    half = n1 // 2
    blk = lambda r: pl.BlockSpec((groups, r, t2, LANES), lambda j: (0, 0, j, 0))
    return pl.pallas_call(
        _outer_inv_body,
        out_shape=jax.ShapeDtypeStruct((groups, half, n2, LANES), F32),
        grid=(n2 // t2,),
        in_specs=[blk(n1), pl.BlockSpec((t2, half, 2 * n1), lambda j: (j, 0, 0)), blk(half), blk(half),
                  pl.BlockSpec((1, groups * LANES), lambda j: (0, 0))],
        out_specs=blk(half),
        scratch_shapes=[_flat_scratch(groups, n1, t2, U32)] + [_flat_scratch(groups, half, t2)] * 3,
        compiler_params=_params(("parallel",)),
        name="dft_outer_inv",
    )(d, inv1, gate4, z4, skip)


def _log_sigmoid(x):
    return jnp.minimum(x, 0.0) - jnp.log1p(jnp.exp(-jnp.abs(x)))


_NT = (((1,), (1,)), ((), ()))
_TN = (((0,), (0,)), ((), ()))


def _chunk_matrices(sub):
    i = np.arange(sub)
    same = (i[:, None] // GLA_CHUNK) == (i[None, :] // GLA_CHUNK)
    fwd = same & (i[None, :] <= i[:, None])
    bwd = same & (i[None, :] >= i[:, None])
    return jnp.asarray(np.stack([fwd, bwd, same]), F32)


def _gla_gates_body(q_ref, k_ref, v_ref, lr_ref, w_ref, b_ref, cm_ref,
                    qe_ref, ke_ref, kd_ref, dec_ref, vb_ref, *, scale):
    tg, width = q_ref.shape
    sub = cm_ref.shape[1]
    chunk = GLA_CHUNK
    vb_ref[...] = v_ref[...].astype(BF16)
    same = cm_ref[2].astype(BF16)
    for s in range(tg // sub):
        rs = slice(s * sub, (s + 1) * sub)
        g = _log_sigmoid(_dot(lr_ref[rs, :].astype(BF16), w_ref[...]) + b_ref[...]) / GLA_TAU
        g_hi = g.astype(BF16)
        g_lo = (g - g_hi.astype(F32)).astype(BF16)
        q = q_ref[rs, :] * scale
        k = k_ref[rs, :]
        for d in range(2):
            cs = slice(d * width, (d + 1) * width)
            cum = cm_ref[d].astype(BF16)
            bc = _dot(cum, g_hi[:, cs]) + _dot(cum, g_lo[:, cs])
            bl = _dot(same, g_hi[:, cs]) + _dot(same, g_lo[:, cs])
            qe_ref[d, rs, :] = (q * jnp.exp(bc)).astype(BF16)
            ke_ref[d, rs, :] = (k * jnp.exp(-bc)).astype(BF16)
            kd_ref[d, rs, :] = (k * jnp.exp(bl - bc)).astype(BF16)
            for c in range(sub // chunk):
                row = s * (sub // chunk) + c
                dec_ref[d, row:row + 1, :] = jnp.exp(bl[c * chunk:c * chunk + 1])


def _gla_gates(p, lr, w_both, b_both, *, q_off, k_off, v_off, dk, dv, heads, tg=512, sub=256):
    L = p.shape[0]
    tg = min(tg, L)
    wk, wv = heads * dk, heads * dv
    opnd = jax.ShapeDtypeStruct((2, L, wk), BF16)
    row = lambda width, off: pl.BlockSpec((tg, width), lambda i: (i, off // width))
    both = pl.BlockSpec((2, tg, wk), lambda i: (0, i, 0))
    const = lambda shape: pl.BlockSpec(shape, lambda i: (0,) * len(shape))
    return pl.pallas_call(
        functools.partial(_gla_gates_body, scale=dk ** -0.5),
        out_shape=(opnd, opnd, opnd, jax.ShapeDtypeStruct((2, L // GLA_CHUNK, wk), F32),
                   jax.ShapeDtypeStruct((L, wv), BF16)),
        grid=(L // tg,),
        in_specs=[row(wk, q_off), row(wk, k_off), row(wv, v_off), row(LANES, 0),
                  const((LANES, 2 * wk)), const((1, 2 * wk)), const((3, sub, sub))],
        out_specs=(both, both, both, pl.BlockSpec((2, tg // GLA_CHUNK, wk), lambda i: (0, i, 0)),
                   pl.BlockSpec((tg, wv), lambda i: (i, 0))),
        compiler_params=_params(("parallel",)),
        name="gla_gates",
    )(p, p, p, lr, w_both, b_both, _chunk_matrices(sub))


def _gla_direction(qe_ref, ke_ref, kd_ref, v_ref, dec_ref, mask, o_ref, st_ref, *, reverse):
    tb = qe_ref.shape[0]
    sub = mask.shape[0]
    chunk = GLA_CHUNK
    intras = []
    for s in range(tb // sub):
        rs = slice(s * sub, (s + 1) * sub)
        a = lax.dot_general(qe_ref[rs, :], ke_ref[rs, :], _NT, preferred_element_type=F32)
        intra = _dot(jnp.where(mask > 0.5, a, 0.0).astype(BF16), v_ref[rs, :])
        intras += [intra[c * chunk:(c + 1) * chunk] for c in range(sub // chunk)]
    st = st_ref[...]
    chunks = range(tb // chunk)
    for c in (reversed(chunks) if reverse else chunks):
        sl = slice(c * chunk, (c + 1) * chunk)
        o_ref[sl, :] = intras[c] + lax.dot_general(qe_ref[sl, :], st.astype(BF16), _NT,
                                                   preferred_element_type=F32)
        upd = lax.dot_general(v_ref[sl, :], kd_ref[sl, :], _TN, preferred_element_type=F32)
        st = st * dec_ref[c:c + 1, :] + upd
    st_ref[...] = st


def _gla_body(qef, kef, kdf, vf, decf, qeb, keb, kdb, vb, decb, cm_ref, of_ref, ob_ref, sf_ref, sb_ref):
    @pl.when(pl.program_id(1) == 0)
    def _():
        sf_ref[...] = jnp.zeros_like(sf_ref)
        sb_ref[...] = jnp.zeros_like(sb_ref)

    _gla_direction(qef, kef, kdf, vf, decf, cm_ref[0], of_ref, sf_ref, reverse=False)
    _gla_direction(qeb, keb, kdb, vb, decb, cm_ref[1], ob_ref, sb_ref, reverse=True)


def _gla(qe, ke, kd, dec, vb, *, dk, dv, tb=1024, sub=256):
    L = vb.shape[0]
    tb = min(tb, L)
    nb = L // tb
    heads = vb.shape[1] // dv
    nc = tb // GLA_CHUNK
    blk = lambda i, rev: (nb - 1 - i) if rev else i
    opnd = lambda rev: pl.BlockSpec((None, tb, dk), lambda h, i: (int(rev), blk(i, rev), h))
    vspec = lambda rev: pl.BlockSpec((tb, dv), lambda h, i: (blk(i, rev), h))
    dspec = lambda rev: pl.BlockSpec((None, nc, dk), lambda h, i: (int(rev), blk(i, rev), h))
    side = lambda rev: [opnd(rev), opnd(rev), opnd(rev), vspec(rev), dspec(rev)]
    return pl.pallas_call(
        _gla_body,
        out_shape=(jax.ShapeDtypeStruct((L, heads * dv), F32),) * 2,
        grid=(heads, nb),
        in_specs=side(False) + side(True) + [pl.BlockSpec((3, sub, sub), lambda h, i: (0, 0, 0))],
        out_specs=(vspec(False), vspec(True)),
        scratch_shapes=[pltpu.VMEM((dv, dk), F32), pltpu.VMEM((dv, dk), F32)],
        compiler_params=_params(("parallel", "arbitrary")),
        name="gla",
    )(qe, ke, kd, vb, dec, qe, ke, kd, vb, dec, _chunk_matrices(sub))


def _gla_post_body(of_ref, ob_ref, r_ref, gn_ref, c_ref, *, dv):
    o = of_ref[...] + ob_ref[...]
    r = r_ref[...]
    gate = r * jax.nn.sigmoid(r)
    for hd in range(o.shape[1] // dv):
        sl = slice(hd * dv, (hd + 1) * dv)
        c_ref[:, sl] = (_rms(o[:, sl], gn_ref[...]) * gate[:, sl]).astype(BF16)


def _gla_post(o_f, o_b, p, gla_norm, *, r_off, tm=512):
    L, wc = o_f.shape
    dv = gla_norm.shape[0]
    row = lambda blk=0: pl.BlockSpec((tm, wc), lambda i: (i, blk))
    return pl.pallas_call(
        functools.partial(_gla_post_body, dv=dv),
        out_shape=jax.ShapeDtypeStruct((L, wc), BF16),
        grid=(L // tm,),
        in_specs=[row(), row(), row(r_off // wc), pl.BlockSpec((1, dv), lambda i: (0, 0))],
        out_specs=row(),
        compiler_params=_params(("parallel",)),
        name="gla_post",
    )(o_f, o_b, p, gla_norm.reshape(1, dv))


def _merge_body(h_ref, g_ref, a_ref, z_ref, c_ref,
                wga_ref, wgb_ref, wgc_ref, bga_ref, bgb_ref, bgc_ref,
                wa_ref, wb_ref, wc_ref, m_ref, u_ref, zb_ref):
    @pl.when(pl.program_id(1) == 0)
    def _():
        u_ref[...] = _rms(h_ref[...], g_ref[...]).astype(BF16)
        zb_ref[...] = _load_groups(z_ref).astype(BF16)

    u = u_ref[...]
    m = jax.nn.sigmoid(_dot(u, wga_ref[...]) + bga_ref[...]) * _dot(a_ref[...], wa_ref[...])
    m += jax.nn.sigmoid(_dot(u, wgb_ref[...]) + bgb_ref[...]) * _dot(zb_ref[...], wb_ref[...])
    m += jax.nn.sigmoid(_dot(u, wgc_ref[...]) + bgc_ref[...]) * _dot(c_ref[...], wc_ref[...])
    m_ref[...] = m.astype(BF16)


def _merge(h, norm_g, a_pre, z, c_pre, w_gate, b_gate, w_a, w_b, w_c, *, tm=1024, tn=512):
    L, D = h.shape
    wa, wb, wc = w_a.shape[0], w_b.shape[0], w_c.shape[0]
    tm = min(tm, L)
    nj = D // tn
    row = lambda width: pl.BlockSpec((tm, width), lambda i, j: (i, 0))
    gate_w = lambda k: pl.BlockSpec((D, tn), lambda i, j: (0, j + k * nj))
    gate_b = lambda k: pl.BlockSpec((1, tn), lambda i, j: (0, j + k * nj))
    br_w = lambda width: pl.BlockSpec((width, tn), lambda i, j: (0, j))
    return pl.pallas_call(
        _merge_body,
        out_shape=jax.ShapeDtypeStruct((L, D), BF16),
        grid=(L // tm, nj),
        in_specs=[
            pl.BlockSpec((tm, D), lambda i, j: (i, 0), pipeline_mode=pl.Buffered(1)),
            pl.BlockSpec((1, D), lambda i, j: (0, 0)),
            row(wa), pl.BlockSpec((wb // LANES, tm, LANES), lambda i, j: (0, i, 0)), row(wc),
            gate_w(0), gate_w(1), gate_w(2), gate_b(0), gate_b(1), gate_b(2),
            br_w(wa), br_w(wb), br_w(wc),
        ],
        out_specs=pl.BlockSpec((tm, tn), lambda i, j: (i, j)),
        scratch_shapes=[pltpu.VMEM((tm, D), BF16), pltpu.VMEM((tm, wb), BF16)],
        compiler_params=_params(("parallel", "arbitrary")),
        name="merge",
    )(h, norm_g.reshape(1, D), a_pre, z, c_pre,
      w_gate, w_gate, w_gate, b_gate, b_gate, b_gate, w_a, w_b, w_c)


def _out_proj_body(h_ref, m_ref, w_ref, o_ref, *, tn):
    m = m_ref[...]
    for c in range(o_ref.shape[1] // tn):
        sl = slice(c * tn, (c + 1) * tn)
        o_ref[:, sl] = h_ref[:, sl] + _dot(m, w_ref[:, sl])


def _out_proj(h, m, w_o, *, tm=512, tn=512):
    L, D = h.shape
    row = pl.BlockSpec((tm, D), lambda i: (i, 0))
    return pl.pallas_call(
        functools.partial(_out_proj_body, tn=tn),
        out_shape=jax.ShapeDtypeStruct((L, D), F32),
        grid=(L // tm,),
        in_specs=[row, row, pl.BlockSpec((D, D), lambda i: (0, 0), pipeline_mode=pl.Buffered(1))],
        out_specs=row,
        compiler_params=_params(("parallel",)),
        name="out_proj",
    )(h, m, w_o)


def _hyena(z0, g1, g2, skip, filt, consts):
    groups, L, _ = z0.shape
    n1 = DFT_N1
    n2 = 2 * L // n1
    k_time, sums = filt
    kf = _outer_fwd(k_time.reshape(k_time.shape[0], n1, n2, LANES), consts["fwd1"], sums)
    kr, ki = _mid(kf, consts["fwd2"])
    z = z0.reshape(groups, n1 // 2, n2, LANES)
    for order, gate in enumerate((g1, g2)):
        a = _outer_fwd(z, consts["fwd1"])
        d = _mid(a, consts["fwd2"], consts["inv2"], kr, ki, order)
        z = _outer_inv(d, consts["inv1"], gate.reshape(z.shape), z, skip[order].reshape(1, groups * LANES))
    return z.reshape(groups, L, LANES)


def kernel(x, ffn1_norm, ffn1_w_gu, ffn1_w_down, mix_norm, w_in, b_in, conv_a, conv_b, hf_w1, hf_b1, hf_w2, hf_b2, hf_w3, hf_b3, hf_w_out, hf_freq, hy_skip, gk_w2, gk_b, gla_norm, w_br_a, w_br_b, w_br_c, w_o, ffn2_norm, ffn2_w_gu, ffn2_w_down, final_norm):
    bsz, L, D = x.shape
    depth = ffn1_norm.shape[0]
    wa = conv_a.shape[2]
    wb = conv_b.shape[3]
    wck = gk_w2.shape[3]
    dv = gla_norm.shape[1]
    wcv = GLA_HEADS * dv
    dk = wck // GLA_HEADS
    hid = hf_w2.shape[1]
    n_main = 3 * wa + 3 * wb + 2 * wck + 2 * wcv
    q_off = 3 * wa + 3 * wb
    k_off, v_off, r_off = q_off + wck, q_off + 2 * wck, q_off + 2 * wck + wcv
    lr_off = n_main
    gate_off = n_main + 2 * GLA_RANK
    n1 = DFT_N1
    n2 = 2 * L // n1

    consts = _dft_consts(n1, n2)
    deltas =jnp.linspace(math.log(HY_TARGET) / HY_SLOW_DECAY, math.log(HY_TARGET) / HY_FAST_DECAY, wb, dtype=F32)
    deltas2 = jnp.tile(deltas.reshape(1, wb), (1, HY_ORDER))

    outs = []
    for bi in range(bsz):
        h = x[bi]
        for l in range(depth):
            h = _ffn(h, ffn1_norm[l], _to_bf16(ffn1_w_gu, l), _to_bf16(ffn1_w_down, l),
                     final_norm, final=False)

            w_lr = jnp.pad(w_in[l][:, lr_off:gate_off], ((0, 0), (0, LANES - 2 * GLA_RANK))).astype(BF16)
            b_lr = jnp.pad(b_in[l][lr_off:gate_off], (0, LANES - 2 * GLA_RANK)).reshape(1, LANES)
            p, lr = _proj(h, mix_norm[l], w_in[l][:, :n_main].astype(BF16), b_in[l][:n_main].reshape(1, n_main),
                          w_lr, b_lr)

            a_pre, z0, g1, g2 = _prep(p, conv_a[l], conv_b[l], wa=wa)

            w_out2 = hf_w_out[l].reshape(hid, HY_ORDER, 2, wb).transpose(2, 0, 1, 3).reshape(2, hid, HY_ORDER * wb)
            filt = _filters(L, hf_w1[l], hf_b1[l], hf_w2[l], hf_b2[l], hf_w3[l], hf_b3[l], hf_freq[l],
                            w_out2, deltas2)
            zb = _hyena(z0, g1, g2, hy_skip[l], filt, consts)

            w_both = jnp.zeros((LANES, 2 * wck), F32)
            w_both = w_both.at[:GLA_RANK, :wck].set(gk_w2[l, 0]).at[GLA_RANK:2 * GLA_RANK, wck:].set(gk_w2[l, 1])
            qe, ke, kd, dec, vb = _gla_gates(p, lr, w_both.astype(BF16), gk_b[l].reshape(1, 2 * wck),
                                             q_off=q_off, k_off=k_off, v_off=v_off, dk=dk, dv=dv,
                                             heads=GLA_HEADS)
            o_f, o_b = _gla(qe, ke, kd, dec, vb, dk=dk, dv=dv)

            c_pre = _gla_post(o_f, o_b, p, gla_norm[l], r_off=r_off)

            merged = _merge(h, mix_norm[l], a_pre, zb, c_pre,
                            w_in[l][:, gate_off:].astype(BF16), b_in[l][gate_off:].reshape(1, 3 * D),
                            _to_bf16(w_br_a, l), _to_bf16(w_br_b, l), _to_bf16(w_br_c, l))
            h = _out_proj(h, merged, _to_bf16(w_o, l))

            h = _ffn(h, ffn2_norm[l], _to_bf16(ffn2_w_gu, l), _to_bf16(ffn2_w_down, l),
                     final_norm, final=(l == depth - 1))
        outs.append(h)
    return jnp.stack(outs, axis=0)
```

```python
import functools
import math

import numpy as np
import jax
import jax.numpy as jnp
from jax import lax
from jax.experimental import pallas as pl
from jax.experimental.pallas import tpu as pltpu

F32 = jnp.float32
BF16 = jnp.bfloat16
HIGHEST = lax.Precision.HIGHEST

EPS = 1e-5
GLA_HEADS = 4
GLA_RANK = 16
GLA_TAU = 16.0
GLA_CHUNK = 64
HY_ORDER = 2
HY_BANDS = 16
HY_FAST_DECAY = 0.3
HY_SLOW_DECAY = 1.5
HY_TARGET = 1e-2

LANES = 128
SUBLANES = 8
DFT_N1 = 256
ROW_CHUNK = 128
VMEM_LIMIT = 56 * 1024 * 1024


def _params(sem):
    return pltpu.CompilerParams(dimension_semantics=sem, vmem_limit_bytes=VMEM_LIMIT)


def _rms(x, g):
    return x * lax.rsqrt(jnp.mean(x * x, axis=-1, keepdims=True) + EPS) * g


def _dot(a, b):
    return jnp.dot(a, b, preferred_element_type=F32)


def _dot_hi(a, b):
    return jnp.dot(a, b, preferred_element_type=F32, precision=HIGHEST)


def _cast_body(x_ref, o_ref):
    o_ref[...] = x_ref[...].astype(o_ref.dtype)


def _to_bf16(w, layer, cols=None, *, block_bytes=4 << 20):
    _, rows, width = w.shape
    cols = width if cols is None else cols
    tr = rows
    while tr * cols * 4 > block_bytes and tr % 32 == 0:
        tr //= 2
    return pl.pallas_call(
        _cast_body,
        out_shape=jax.ShapeDtypeStruct((rows, cols), BF16),
        grid=(rows // tr,),
        in_specs=[pl.BlockSpec((None, tr, cols), lambda i: (layer, i, 0))],
        out_specs=pl.BlockSpec((tr, cols), lambda i: (i, 0)),
        compiler_params=_params(("parallel",)),
        name="to_bf16",
    )(w)


def _ffn_body(x_ref, g_ref, wg_ref, wu_ref, wd_ref, fn_ref, o_ref, xn_ref, *, final, sub):
    j = pl.program_id(1)

    row_chunks = [slice(r, r + ROW_CHUNK) for r in range(0, x_ref.shape[0], ROW_CHUNK)]

    @pl.when(j == 0)
    def _():
        for rs in row_chunks:
            xn_ref[rs, :] = _rms(x_ref[rs, :], g_ref[...]).astype(BF16)
        o_ref[...] = jnp.zeros_like(o_ref)

    xn = xn_ref[...]
    hs = []
    for c in range(wg_ref.shape[1] // sub):
        sl = slice(c * sub, (c + 1) * sub)
        g = _dot(xn, wg_ref[:, sl])
        u = _dot(xn, wu_ref[:, sl])
        hs.append((g * jax.nn.sigmoid(g) * u).astype(BF16))
    o_ref[...] += _dot(jnp.concatenate(hs, axis=1), wd_ref[...])

    @pl.when(j == pl.num_programs(1) - 1)
    def _():
        for rs in row_chunks:
            y = x_ref[rs, :] + 0.5 * o_ref[rs, :]
            if final:
                y = _rms(y, fn_ref[...])
            o_ref[rs, :] = y


def _ffn(x, norm_g, w_gu, w_down, final_g, *, final, tm=1024, tf=512, sub=256):
    L, D = x.shape
    F = w_down.shape[0]
    tm = min(tm, L)
    if final:
        tf = sub
    nf = F // tf
    return pl.pallas_call(
        functools.partial(_ffn_body, final=final, sub=min(sub, tf)),
        out_shape=jax.ShapeDtypeStruct((L, D), F32),
        grid=(L // tm, nf),
        in_specs=[
            pl.BlockSpec((tm, D), lambda i, j: (i, 0)),
            pl.BlockSpec((1, D), lambda i, j: (0, 0)),
            pl.BlockSpec((D, tf), lambda i, j: (0, j)),
            pl.BlockSpec((D, tf), lambda i, j: (0, j + nf)),
            pl.BlockSpec((tf, D), lambda i, j: (j, 0)),
            pl.BlockSpec((1, D), lambda i, j: (0, 0)),
        ],
        out_specs=pl.BlockSpec((tm, D), lambda i, j: (i, 0)),
        scratch_shapes=[pltpu.VMEM((tm, D), BF16)],
        compiler_params=_params(("parallel", "arbitrary")),
        name="ffn",
    )(x, norm_g.reshape(1, D), w_gu, w_gu, w_down, final_g.reshape(1, D))


def _conv3_inner(x, w):
    n = x.shape[0]
    y = pltpu.roll(x, 1, axis=0) * w[0:1] + x * w[1:2] + pltpu.roll(x, n - 1, axis=0) * w[2:3]
    return y[SUBLANES:n - SUBLANES]


def _proj_body(h_ref, hp_ref, hn_ref, g_ref, w_ref, b_ref, wlr_ref, blr_ref, caw_ref, cbw_ref,
               p_ref, lr_ref, a_ref, z_ref, g1_ref, g2_ref, *, tn, wa):
    i = pl.program_id(0)
    tm = h_ref.shape[0]
    g = g_ref[...]
    xn = _rms(h_ref[...], g)
    xn_bf = xn.astype(BF16)
    lr_ref[...] = _dot(xn_bf, wlr_ref[...]) + blr_ref[...]
    conv_w = 6 * wa
    for c in range(p_ref.shape[1] // tn):
        src = slice(conv_w + c * tn, conv_w + (c + 1) * tn)
        p_ref[:, c * tn:(c + 1) * tn] = _dot(xn_bf, w_ref[:, src]) + b_ref[:, src]

    xh = jnp.concatenate([_rms(hp_ref[...], g), xn, _rms(hn_ref[...], g)], axis=0).astype(BF16)
    row = lax.broadcasted_iota(jnp.int32, (tm + 2 * SUBLANES, 1), 0)
    has_prev = (i > 0).astype(F32)
    has_next = (i < pl.num_programs(0) - 1).astype(F32)
    valid = jnp.where(row < SUBLANES, has_prev, jnp.where(row >= tm + SUBLANES, has_next, 1.0))

    def sec(k):
        sl = slice(k * wa, (k + 1) * wa)
        return (_dot(xh, w_ref[:, sl]) + b_ref[:, sl]) * valid

    xa, ba, ca = sec(0), sec(1), sec(2)
    a_ref[...] = (ba[SUBLANES:tm + SUBLANES] * _conv3_inner(ca * xa, caw_ref[...])).astype(BF16)
    for k, o_ref in enumerate((z_ref, g1_ref, g2_ref)):
        _store_groups(o_ref, _conv3_inner(sec(3 + k), cbw_ref[k]))


def _proj(h, norm_g, w, b, w_lr, b_lr, conv_a, conv_b, *, wa, tm=256, tn=512):
    L, D = h.shape
    conv_w = 6 * wa
    n_scan = w.shape[1] - conv_w
    rb = tm // SUBLANES
    nrb = L // SUBLANES
    resident = lambda shape: pl.BlockSpec(shape, lambda i: (0,) * len(shape), pipeline_mode=pl.Buffered(1))
    grouped = pl.BlockSpec((wa // LANES, tm, LANES), lambda i: (0, i, 0))
    return pl.pallas_call(
        functools.partial(_proj_body, tn=tn, wa=wa),
        out_shape=(jax.ShapeDtypeStruct((L, n_scan), F32), jax.ShapeDtypeStruct((L, LANES), F32),
                   jax.ShapeDtypeStruct((L, wa), BF16))
        + (jax.ShapeDtypeStruct((wa // LANES, L, LANES), F32),) * 3,
        grid=(L // tm,),
        in_specs=[
            pl.BlockSpec((tm, D), lambda i: (i, 0)),
            pl.BlockSpec((SUBLANES, D), lambda i: (jnp.maximum(i * rb - 1, 0), 0)),
            pl.BlockSpec((SUBLANES, D), lambda i: (jnp.minimum((i + 1) * rb, nrb - 1), 0)),
            resident((1, D)), resident(w.shape), resident((1, w.shape[1])),
            resident((D, LANES)), resident((1, LANES)), resident((3, wa)), resident((3, 3, wa)),
        ],
        out_specs=(pl.BlockSpec((tm, n_scan), lambda i: (i, 0)),
                   pl.BlockSpec((tm, LANES), lambda i: (i, 0)),
                   pl.BlockSpec((tm, wa), lambda i: (i, 0)), grouped, grouped, grouped),
        compiler_params=_params(("parallel",)),
        name="proj",
    )(h, h, h, norm_g.reshape(1, D), w, b, w_lr, b_lr, conv_a, conv_b)


def _store_groups(o_ref, x):
    for g in range(o_ref.shape[0]):
        o_ref[g] = x[:, g * LANES:(g + 1) * LANES]


def _load_groups(ref):
    return jnp.concatenate([ref[g] for g in range(ref.shape[0])], axis=1)


def _filter_body(w1t_ref, w1cs_ref, b1_ref, w2_ref, b2_ref, w3_ref, b3_ref, fr_ref, bands_ref, wo_ref, dl_ref,
                 k_ref, s_ref, *, half_tiles, tr, seq):
    i = pl.program_id(0)

    def lag_of(n):
        return jnp.where(n < seq, n, jnp.where(n == seq, 0, 2 * seq - n)).astype(F32)

    lag_row = lag_of(i * tr + lax.broadcasted_iota(jnp.int32, (1, tr), 1))
    ang = ((2.0 * math.pi / seq) * lag_row) * bands_ref[...]
    feats = jnp.concatenate([jnp.cos(ang), -jnp.sin(ang)], axis=0)
    fr = fr_ref[...]
    pre = w1t_ref[...] * (lag_row / max(seq - 1, 1)) + _dot_hi(w1cs_ref[...], feats) + b1_ref[...]
    h = jnp.sin(fr * pre)
    h = jnp.sin(fr * (_dot_hi(w2_ref[...], h) + b2_ref[...]))
    h = jnp.sin(fr * (_dot_hi(w3_ref[...], h) + b3_ref[...]))
    t = lag_of(i * tr + lax.broadcasted_iota(jnp.int32, (tr, 1), 0)) / max(seq - 1, 1)
    decay = jnp.exp(-t * jnp.abs(dl_ref[...]))
    hd = lax.dot_general(h, wo_ref[0], _TN, preferred_element_type=F32, precision=HIGHEST) * decay

    @pl.when(i % half_tiles == 0)
    def _():
        s_ref[...] = jnp.zeros_like(s_ref)

    s_ref[0] += jnp.sum(jnp.abs(hd), axis=0, keepdims=True)
    row = lax.broadcasted_iota(jnp.int32, hd.shape, 0)
    zero_row = jnp.where(i == half_tiles, 0, -1)
    _store_groups(k_ref, jnp.where(row == zero_row, 0.0, hd))


def _filters(seq, w1, b1, w2, b2, w3, b3, freq, w_out2, deltas2, *, tr=512):
    n_rows = 2 * seq
    hid = w2.shape[0]
    width = w_out2.shape[2]
    half_tiles = n_rows // (2 * tr)
    bands = jnp.linspace(1e-4, HY_BANDS - 1, HY_BANDS, dtype=F32).reshape(HY_BANDS, 1)
    full = lambda shape: pl.BlockSpec(shape, lambda i: (0,) * len(shape))
    colv = lambda v: v.reshape(hid, 1)
    return pl.pallas_call(
        functools.partial(_filter_body, half_tiles=half_tiles, tr=tr, seq=seq),
        out_shape=(jax.ShapeDtypeStruct((width // LANES, n_rows, LANES), F32),
                   jax.ShapeDtypeStruct((2, 1, width), F32)),
        grid=(n_rows // tr,),
        in_specs=[
            full((hid, 1)), full((hid, 2 * HY_BANDS)), full((hid, 1)), full((hid, hid)), full((hid, 1)),
            full((hid, hid)), full((hid, 1)), full((hid, 1)), full((HY_BANDS, 1)),
            pl.BlockSpec((1, hid, width), lambda i: (i // half_tiles, 0, 0)),
            full((1, width)),
        ],
        out_specs=(pl.BlockSpec((width // LANES, tr, LANES), lambda i: (0, i, 0)),
                   pl.BlockSpec((1, 1, width), lambda i: (i // half_tiles, 0, 0))),
        compiler_params=_params(("arbitrary",)),
        name="hyena_filters",
    )(w1[0:1].T, w1[1:].T, colv(b1), w2.T, colv(b2), w3.T, colv(b3), colv(freq), bands, w_out2, deltas2)


def _dft_consts(n1, n2):
    n = n1 * n2
    k = np.arange(n1)
    m = np.arange(n2)
    ang2 = 2.0 * np.pi * ((m[:, None] * m[None, :]) % n2) / n2
    c2, s2 = np.cos(ang2), np.sin(ang2)
    as_bf = lambda a: jnp.asarray(np.asarray(a, np.float32).astype(jnp.bfloat16))
    time_index = k[None, None, :] * n2 + m[:, None, None]
    ang = 2.0 * np.pi * ((k[None, :, None] * time_index) % n) / n
    stage = np.concatenate([np.cos(ang), -np.sin(ang)], axis=1)
    half = n1 // 2
    return dict(
        fwd1=as_bf(stage),
        inv1=as_bf(stage.transpose(0, 2, 1)[:, :half] / n),
        fwd2=as_bf(np.block([[c2, s2], [-s2, c2]])),
        inv2=as_bf(np.block([[c2, -s2], [s2, c2]])),
    )


def _flatten_block(ref, buf):
    groups, n, t2, lanes = ref.shape
    for g in range(groups):
        buf[g] = ref[g].reshape(n * t2, lanes)


def _unflatten_block(buf, ref):
    groups, n, t2, lanes = ref.shape
    for g in range(groups):
        ref[g] = buf[g].reshape(n, t2, lanes)


def _load_strided(buf, s, t2):
    n = buf.shape[1] // t2
    return jnp.concatenate([buf[g, pl.ds(s, n, stride=t2), :] for g in range(buf.shape[0])], axis=1)


def _store_strided(buf, s, t2, x):
    n = buf.shape[1] // t2
    for g in range(buf.shape[0]):
        buf[g, pl.ds(s, n, stride=t2), :] = x[:, g * LANES:(g + 1) * LANES]


def _flat_scratch(groups, n, t2, dtype=F32):
    return pltpu.VMEM((groups, n * t2, LANES), dtype)


U32 = jnp.uint32


def _pack_complex(re, im):
    hi = lax.bitcast_convert_type(re.astype(BF16).astype(F32), U32)
    lo = lax.bitcast_convert_type(im.astype(BF16).astype(F32), U32)
    return hi | (lo >> 16)


def _unpack_complex(w):
    re = lax.bitcast_convert_type(w & jnp.uint32(0xFFFF0000), F32)
    im = lax.bitcast_convert_type(w << 16, F32)
    return re, im


def _outer_fwd_body(x_ref, f_ref, *rest, scaled):
    if scaled:
        s_ref, a_ref, xb, ab = rest
        inv = 1.0 / (s_ref[...] + EPS)
    else:
        a_ref, xb, ab = rest
    rows, t2 = x_ref.shape[1:3]
    n1 = a_ref.shape[1]
    _flatten_block(x_ref, xb)
    for s in range(t2):
        x = _load_strided(xb, s, t2)
        if scaled:
            half = rows // 2
            x = jnp.concatenate([x[:half] * inv[0], x[half:] * inv[1]], axis=0)
        a = _dot(f_ref[s], x.astype(BF16))
        _store_strided(ab, s, t2, _pack_complex(a[:n1], a[n1:]))
    _unflatten_block(ab, a_ref)


def _outer_fwd(x4, fwd1, sums=None, *, t2=SUBLANES, gt=4):
    groups, rows, n2, _ = x4.shape
    n1 = fwd1.shape[2]
    scaled = sums is not None
    blk = lambda r: pl.BlockSpec((gt, r, t2, LANES), lambda j, c: (c, 0, j, 0))
    args = [x4, fwd1]
    in_specs = [blk(rows), pl.BlockSpec((t2, 2 * n1, rows), lambda j, c: (j, 0, 0))]
    if scaled:
        args.append(sums)
        in_specs.append(pl.BlockSpec((2, 1, gt * LANES), lambda j, c: (0, 0, c)))
    return pl.pallas_call(
        functools.partial(_outer_fwd_body, scaled=scaled),
        out_shape=jax.ShapeDtypeStruct((groups, n1, n2, LANES), U32),
        grid=(n2 // t2, groups // gt),
        in_specs=in_specs,
        out_specs=blk(n1),
        scratch_shapes=[_flat_scratch(gt, rows, t2), _flat_scratch(gt, n1, t2, U32)],
        compiler_params=_params(("parallel", "parallel")),
        name="dft_outer_fwd",
    )(*args)


def _load_slab(ref, s):
    return jnp.concatenate([ref[g, s] for g in range(ref.shape[0])], axis=1)


def _store_slab(ref, s, x):
    for g in range(ref.shape[0]):
        ref[g, s] = x[:, g * LANES:(g + 1) * LANES]


def _mid_conv_body(a_ref, k_ref, f2_ref, g2_ref, d_ref):
    n2 = a_ref.shape[2]
    width = a_ref.shape[0] * LANES
    for s in range(a_ref.shape[1]):
        re, im = _unpack_complex(jnp.concatenate([_load_slab(a_ref, s), _load_slab(k_ref, s)], axis=1))
        x = _dot(f2_ref[...], jnp.concatenate([re, im], axis=0).astype(BF16))
        xr, xi, kr, ki = x[:n2, :width], x[n2:, :width], x[:n2, width:], x[n2:, width:]
        yr = xr * kr - xi * ki
        yi = xr * ki + xi * kr
        c = _dot(g2_ref[...], jnp.concatenate([yr, yi], axis=0).astype(BF16))
        _store_slab(d_ref, s, _pack_complex(c[:n2], c[n2:]))


def _mid(a, kf, order, fwd2, inv2, *, kb=4):
    groups, n1, n2, _ = a.shape
    slab = pl.BlockSpec((groups, kb, n2, LANES), lambda i: (0, i, 0, 0))
    kslab = pl.BlockSpec((groups, kb, n2, LANES), lambda i: (order, i, 0, 0))
    mat = pl.BlockSpec((2 * n2, 2 * n2), lambda i: (0, 0))
    return pl.pallas_call(
        _mid_conv_body,
        out_shape=jax.ShapeDtypeStruct(a.shape, U32),
        grid=(n1 // kb,),
        in_specs=[slab, kslab, mat, mat],
        out_specs=slab,
        compiler_params=_params(("parallel",)),
        name="hyena_mid_conv",
    )(a, kf, fwd2, inv2)


def _outer_inv_body(d_ref, e_ref, gate_ref, z_ref, skip_ref, o_ref, db, gb, zb, ob):
    skip = skip_ref[...]
    t2 = d_ref.shape[2]
    for ref, buf in ((d_ref, db), (gate_ref, gb), (z_ref, zb)):
        _flatten_block(ref, buf)
    for s in range(t2):
        dr, di = _unpack_complex(_load_strided(db, s, t2))
        y = _dot(e_ref[s], jnp.concatenate([dr, di], axis=0).astype(BF16))
        _store_strided(ob, s, t2, _load_strided(gb, s, t2) * (y + skip * _load_strided(zb, s, t2)))
    _unflatten_block(ob, o_ref)


def _outer_inv(d, inv1, gate4, z4, skip, *, t2=SUBLANES):
    groups, n1, n2, _ = d.shape
    half = n1 // 2
    blk = lambda r: pl.BlockSpec((groups, r, t2, LANES), lambda j: (0, 0, j, 0))
    return pl.pallas_call(
        _outer_inv_body,
        out_shape=jax.ShapeDtypeStruct((groups, half, n2, LANES), F32),
        grid=(n2 // t2,),
        in_specs=[blk(n1), pl.BlockSpec((t2, half, 2 * n1), lambda j: (j, 0, 0)), blk(half), blk(half),
                  pl.BlockSpec((1, groups * LANES), lambda j: (0, 0))],
        out_specs=blk(half),
        scratch_shapes=[_flat_scratch(groups, n1, t2, U32)] + [_flat_scratch(groups, half, t2)] * 3,
        compiler_params=_params(("parallel",)),
        name="dft_outer_inv",
    )(d, inv1, gate4, z4, skip)


def _log_sigmoid(x):
    return jnp.minimum(x, 0.0) - jnp.log1p(jnp.exp(-jnp.abs(x)))


_NT = (((1,), (1,)), ((), ()))
_TN = (((0,), (0,)), ((), ()))


def _chunk_matrices(sub):
    i = np.arange(sub)
    same = (i[:, None] // GLA_CHUNK) == (i[None, :] // GLA_CHUNK)
    fwd = same & (i[None, :] <= i[:, None])
    bwd = same & (i[None, :] >= i[:, None])
    return jnp.asarray(np.stack([fwd, bwd, same]), F32)


def _gla_gates_body(q_ref, k_ref, v_ref, lr_ref, w_ref, b_ref, cm_ref,
                    qe_ref, ke_ref, kd_ref, dec_ref, vb_ref, *, scale):
    tg, width = q_ref.shape
    sub = cm_ref.shape[1]
    chunk = GLA_CHUNK
    vb_ref[...] = v_ref[...].astype(BF16)
    same = cm_ref[2].astype(BF16)
    for s in range(tg // sub):
        rs = slice(s * sub, (s + 1) * sub)
        g = _log_sigmoid(_dot(lr_ref[rs, :].astype(BF16), w_ref[...]) + b_ref[...]) / GLA_TAU
        g_hi = g.astype(BF16)
        g_lo = (g - g_hi.astype(F32)).astype(BF16)
        q = q_ref[rs, :] * scale
        k = k_ref[rs, :]
        for d in range(2):
            cs = slice(d * width, (d + 1) * width)
            cum = cm_ref[d].astype(BF16)
            bc = _dot(cum, g_hi[:, cs]) + _dot(cum, g_lo[:, cs])
            bl = _dot(same, g_hi[:, cs]) + _dot(same, g_lo[:, cs])
            qe_ref[d, rs, :] = (q * jnp.exp(bc)).astype(BF16)
            ke_ref[d, rs, :] = (k * jnp.exp(-bc)).astype(BF16)
            kd_ref[d, rs, :] = (k * jnp.exp(bl - bc)).astype(BF16)
            for c in range(sub // chunk):
                row = s * (sub // chunk) + c
                dec_ref[d, row:row + 1, :] = jnp.exp(bl[c * chunk:c * chunk + 1])


def _gla_gates(p, lr, w_both, b_both, *, q_off, k_off, v_off, dk, dv, heads, tg=512, sub=256):
    L = p.shape[0]
    tg = min(tg, L)
    wk, wv = heads * dk, heads * dv
    opnd = jax.ShapeDtypeStruct((2, L, wk), BF16)
    row = lambda width, off: pl.BlockSpec((tg, width), lambda i: (i, off // width))
    both = pl.BlockSpec((2, tg, wk), lambda i: (0, i, 0))
    const = lambda shape: pl.BlockSpec(shape, lambda i: (0,) * len(shape))
    return pl.pallas_call(
        functools.partial(_gla_gates_body, scale=dk ** -0.5),
        out_shape=(opnd, opnd, opnd, jax.ShapeDtypeStruct((2, L // GLA_CHUNK, wk), F32),
                   jax.ShapeDtypeStruct((L, wv), BF16)),
        grid=(L // tg,),
        in_specs=[row(wk, q_off), row(wk, k_off), row(wv, v_off), row(LANES, 0),
                  const((LANES, 2 * wk)), const((1, 2 * wk)), const((3, sub, sub))],
        out_specs=(both, both, both, pl.BlockSpec((2, tg // GLA_CHUNK, wk), lambda i: (0, i, 0)),
                   pl.BlockSpec((tg, wv), lambda i: (i, 0))),
        compiler_params=_params(("parallel",)),
        name="gla_gates",
    )(p, p, p, lr, w_both, b_both, _chunk_matrices(sub))


def _gla_direction(qe_ref, ke_ref, kd_ref, v_ref, dec_ref, mask, o_ref, st_ref, *, reverse):
    tb = qe_ref.shape[0]
    sub = mask.shape[0]
    chunk = GLA_CHUNK
    intras = []
    for s in range(tb // sub):
        rs = slice(s * sub, (s + 1) * sub)
        a = lax.dot_general(qe_ref[rs, :], ke_ref[rs, :], _NT, preferred_element_type=F32)
        intra = _dot(jnp.where(mask > 0.5, a, 0.0).astype(BF16), v_ref[rs, :])
        intras += [intra[c * chunk:(c + 1) * chunk] for c in range(sub // chunk)]
    st = st_ref[...]
    chunks = range(tb // chunk)
    for c in (reversed(chunks) if reverse else chunks):
        sl = slice(c * chunk, (c + 1) * chunk)
        o_ref[sl, :] = intras[c] + lax.dot_general(qe_ref[sl, :], st.astype(BF16), _NT,
                                                   preferred_element_type=F32)
        upd = lax.dot_general(v_ref[sl, :], kd_ref[sl, :], _TN, preferred_element_type=F32)
        st = st * dec_ref[c:c + 1, :] + upd
    st_ref[...] = st


def _gla_body(qef, kef, kdf, vf, decf, qeb, keb, kdb, vb, decb, cm_ref, of_ref, ob_ref, sf_ref, sb_ref):
    @pl.when(pl.program_id(1) == 0)
    def _():
        sf_ref[...] = jnp.zeros_like(sf_ref)
        sb_ref[...] = jnp.zeros_like(sb_ref)

    _gla_direction(qef, kef, kdf, vf, decf, cm_ref[0], of_ref, sf_ref, reverse=False)
    _gla_direction(qeb, keb, kdb, vb, decb, cm_ref[1], ob_ref, sb_ref, reverse=True)


def _gla(qe, ke, kd, dec, vb, *, dk, dv, tb=1024, sub=256):
    L = vb.shape[0]
    tb = min(tb, L)
    nb = L // tb
    heads = vb.shape[1] // dv
    nc = tb // GLA_CHUNK
    blk = lambda i, rev: (nb - 1 - i) if rev else i
    opnd = lambda rev: pl.BlockSpec((None, tb, dk), lambda h, i: (int(rev), blk(i, rev), h))
    vspec = lambda rev: pl.BlockSpec((tb, dv), lambda h, i: (blk(i, rev), h))
    dspec = lambda rev: pl.BlockSpec((None, nc, dk), lambda h, i: (int(rev), blk(i, rev), h))
    side = lambda rev: [opnd(rev), opnd(rev), opnd(rev), vspec(rev), dspec(rev)]
    return pl.pallas_call(
        _gla_body,
        out_shape=(jax.ShapeDtypeStruct((L, heads * dv), F32),) * 2,
        grid=(heads, nb),
        in_specs=side(False) + side(True) + [pl.BlockSpec((3, sub, sub), lambda h, i: (0, 0, 0))],
        out_specs=(vspec(False), vspec(True)),
        scratch_shapes=[pltpu.VMEM((dv, dk), F32), pltpu.VMEM((dv, dk), F32)],
        compiler_params=_params(("parallel", "arbitrary")),
        name="gla",
    )(qe, ke, kd, vb, dec, qe, ke, kd, vb, dec, _chunk_matrices(sub))


def _gla_post_body(of_ref, ob_ref, r_ref, gn_ref, c_ref, *, dv):
    o = of_ref[...] + ob_ref[...]
    r = r_ref[...]
    gate = r * jax.nn.sigmoid(r)
    for hd in range(o.shape[1] // dv):
        sl = slice(hd * dv, (hd + 1) * dv)
        c_ref[:, sl] = (_rms(o[:, sl], gn_ref[...]) * gate[:, sl]).astype(BF16)


def _gla_post(o_f, o_b, p, gla_norm, *, r_off, tm=512):
    L, wc = o_f.shape
    dv = gla_norm.shape[0]
    row = lambda blk=0: pl.BlockSpec((tm, wc), lambda i: (i, blk))
    return pl.pallas_call(
        functools.partial(_gla_post_body, dv=dv),
        out_shape=jax.ShapeDtypeStruct((L, wc), BF16),
        grid=(L // tm,),
        in_specs=[row(), row(), row(r_off // wc), pl.BlockSpec((1, dv), lambda i: (0, 0))],
        out_specs=row(),
        compiler_params=_params(("parallel",)),
        name="gla_post",
    )(o_f, o_b, p, gla_norm.reshape(1, dv))


def _merge_body(h_ref, g_ref, a_ref, z_ref, c_ref,
                wga_ref, wgb_ref, wgc_ref, bga_ref, bgb_ref, bgc_ref,
                wa_ref, wb_ref, wc_ref, m_ref, u_ref, zb_ref):
    @pl.when(pl.program_id(1) == 0)
    def _():
        u_ref[...] = _rms(h_ref[...], g_ref[...]).astype(BF16)
        zb_ref[...] = _load_groups(z_ref).astype(BF16)

    u = u_ref[...]
    m = jax.nn.sigmoid(_dot(u, wga_ref[...]) + bga_ref[...]) * _dot(a_ref[...], wa_ref[...])
    m += jax.nn.sigmoid(_dot(u, wgb_ref[...]) + bgb_ref[...]) * _dot(zb_ref[...], wb_ref[...])
    m += jax.nn.sigmoid(_dot(u, wgc_ref[...]) + bgc_ref[...]) * _dot(c_ref[...], wc_ref[...])
    m_ref[...] = m.astype(BF16)


def _merge(h, norm_g, a_pre, z, c_pre, w_gate, b_gate, w_a, w_b, w_c, *, tm=1024, tn=512):
    L, D = h.shape
    wa, wb, wc = w_a.shape[0], w_b.shape[0], w_c.shape[0]
    tm = min(tm, L)
    nj = D // tn
    row = lambda width: pl.BlockSpec((tm, width), lambda i, j: (i, 0))
    gate_w = lambda k: pl.BlockSpec((D, tn), lambda i, j: (0, j + k * nj))
    gate_b = lambda k: pl.BlockSpec((1, tn), lambda i, j: (0, j + k * nj))
    br_w = lambda width: pl.BlockSpec((width, tn), lambda i, j: (0, j))
    return pl.pallas_call(
        _merge_body,
        out_shape=jax.ShapeDtypeStruct((L, D), BF16),
        grid=(L // tm, nj),
        in_specs=[
            pl.BlockSpec((tm, D), lambda i, j: (i, 0), pipeline_mode=pl.Buffered(1)),
            pl.BlockSpec((1, D), lambda i, j: (0, 0)),
            row(wa), pl.BlockSpec((wb // LANES, tm, LANES), lambda i, j: (0, i, 0)), row(wc),
            gate_w(0), gate_w(1), gate_w(2), gate_b(0), gate_b(1), gate_b(2),
            br_w(wa), br_w(wb), br_w(wc),
        ],
        out_specs=pl.BlockSpec((tm, tn), lambda i, j: (i, j)),
        scratch_shapes=[pltpu.VMEM((tm, D), BF16), pltpu.VMEM((tm, wb), BF16)],
        compiler_params=_params(("parallel", "arbitrary")),
        name="merge",
    )(h, norm_g.reshape(1, D), a_pre, z, c_pre,
      w_gate, w_gate, w_gate, b_gate, b_gate, b_gate, w_a, w_b, w_c)


def _out_proj_body(h_ref, m_ref, w_ref, o_ref, *, tn):
    m = m_ref[...]
    for c in range(o_ref.shape[1] // tn):
        sl = slice(c * tn, (c + 1) * tn)
        o_ref[:, sl] = h_ref[:, sl] + _dot(m, w_ref[:, sl])


def _out_proj(h, m, w_o, *, tm=512, tn=512):
    L, D = h.shape
    row = pl.BlockSpec((tm, D), lambda i: (i, 0))
    return pl.pallas_call(
        functools.partial(_out_proj_body, tn=tn),
        out_shape=jax.ShapeDtypeStruct((L, D), F32),
        grid=(L // tm,),
        in_specs=[row, row, pl.BlockSpec((D, D), lambda i: (0, 0), pipeline_mode=pl.Buffered(1))],
        out_specs=row,
        compiler_params=_params(("parallel",)),
        name="out_proj",
    )(h, m, w_o)


def _hyena(z0, g1, g2, skip, filt, consts):
    groups, L, _ = z0.shape
    n1 = DFT_N1
    n2 = 2 * L // n1
    k_time, sums = filt
    kf = _outer_fwd(k_time.reshape(k_time.shape[0], n1, n2, LANES), consts["fwd1"], sums)
    z = z0.reshape(groups, n1 // 2, n2, LANES)
    for order, gate in enumerate((g1, g2)):
        a = _outer_fwd(z, consts["fwd1"])
        d = _mid(a, kf, order, consts["fwd2"], consts["inv2"])
        z = _outer_inv(d, consts["inv1"], gate.reshape(z.shape), z, skip[order].reshape(1, groups * LANES))
    return z.reshape(groups, L, LANES)


def kernel(x, ffn1_norm, ffn1_w_gu, ffn1_w_down, mix_norm, w_in, b_in, conv_a, conv_b, hf_w1, hf_b1, hf_w2, hf_b2, hf_w3, hf_b3, hf_w_out, hf_freq, hy_skip, gk_w2, gk_b, gla_norm, w_br_a, w_br_b, w_br_c, w_o, ffn2_norm, ffn2_w_gu, ffn2_w_down, final_norm):
    bsz, L, D = x.shape
    depth = ffn1_norm.shape[0]
    wa = conv_a.shape[2]
    wb = conv_b.shape[3]
    wck = gk_w2.shape[3]
    dv = gla_norm.shape[1]
    wcv = GLA_HEADS * dv
    dk = wck // GLA_HEADS
    hid = hf_w2.shape[1]
    assert wa == wb, "the projection kernel tiles both conv branches with one section width"
    n_main = 3 * wa + 3 * wb + 2 * wck + 2 * wcv
    q_off, k_off, v_off, r_off = 0, wck, 2 * wck, 2 * wck + wcv
    lr_off = n_main
    gate_off = n_main + 2 * GLA_RANK
    n1 = DFT_N1
    n2 = 2 * L // n1

    consts = _dft_consts(n1, n2)
    deltas =jnp.linspace(math.log(HY_TARGET) / HY_SLOW_DECAY, math.log(HY_TARGET) / HY_FAST_DECAY, wb, dtype=F32)
    deltas2 = jnp.tile(deltas.reshape(1, wb), (1, HY_ORDER))

    outs = []
    for bi in range(bsz):
        h = x[bi]
        for l in range(depth):
            h = _ffn(h, ffn1_norm[l], _to_bf16(ffn1_w_gu, l), _to_bf16(ffn1_w_down, l),
                     final_norm, final=False)

            w_lr = jnp.pad(w_in[l][:, lr_off:gate_off], ((0, 0), (0, LANES - 2 * GLA_RANK))).astype(BF16)
            b_lr = jnp.pad(b_in[l][lr_off:gate_off], (0, LANES - 2 * GLA_RANK)).reshape(1, LANES)
            p, lr, a_pre, z0, g1, g2 = _proj(
                h, mix_norm[l], w_in[l][:, :n_main].astype(BF16), b_in[l][:n_main].reshape(1, n_main),
                w_lr, b_lr, conv_a[l], conv_b[l], wa=wa)

            w_out2 = hf_w_out[l].reshape(hid, HY_ORDER, 2, wb).transpose(2, 0, 1, 3).reshape(2, hid, HY_ORDER * wb)
            filt = _filters(L, hf_w1[l], hf_b1[l], hf_w2[l], hf_b2[l], hf_w3[l], hf_b3[l], hf_freq[l],
                            w_out2, deltas2)
            zb = _hyena(z0, g1, g2, hy_skip[l], filt, consts)

            w_both = jnp.zeros((LANES, 2 * wck), F32)
            w_both = w_both.at[:GLA_RANK, :wck].set(gk_w2[l, 0]).at[GLA_RANK:2 * GLA_RANK, wck:].set(gk_w2[l, 1])
            qe, ke, kd, dec, vb = _gla_gates(p, lr, w_both.astype(BF16), gk_b[l].reshape(1, 2 * wck),
                                             q_off=q_off, k_off=k_off, v_off=v_off, dk=dk, dv=dv,
                                             heads=GLA_HEADS)
            o_f, o_b = _gla(qe, ke, kd, dec, vb, dk=dk, dv=dv)

            c_pre = _gla_post(o_f, o_b, p, gla_norm[l], r_off=r_off)

            merged = _merge(h, mix_norm[l], a_pre, zb, c_pre,
                            w_in[l][:, gate_off:].astype(BF16), b_in[l][gate_off:].reshape(1, 3 * D),
                            _to_bf16(w_br_a, l), _to_bf16(w_br_b, l), _to_bf16(w_br_c, l))
            h = _out_proj(h, merged, _to_bf16(w_o, l))

            h = _ffn(h, ffn2_norm[l], _to_bf16(ffn2_w_gu, l), _to_bf16(ffn2_w_down, l),
                     final_norm, final=(l == depth - 1))
        outs.append(h)
    return jnp.stack(outs, axis=0)
```

```python
import functools
import math

import numpy as np
import jax
import jax.numpy as jnp
from jax import lax
from jax.experimental import pallas as pl
from jax.experimental.pallas import tpu as pltpu

F32 = jnp.float32
BF16 = jnp.bfloat16
HIGHEST = lax.Precision.HIGHEST

EPS = 1e-5
GLA_HEADS = 4
GLA_RANK = 16
GLA_TAU = 16.0
GLA_CHUNK = 64
HY_ORDER = 2
HY_BANDS = 16
HY_FAST_DECAY = 0.3
HY_SLOW_DECAY = 1.5
HY_TARGET = 1e-2

LANES = 128
SUBLANES = 8
DFT_N1 = 256
ROW_CHUNK = 128
VMEM_LIMIT = 56 * 1024 * 1024


def _params(sem):
    return pltpu.CompilerParams(dimension_semantics=sem, vmem_limit_bytes=VMEM_LIMIT)


def _rms(x, g):
    return x * lax.rsqrt(jnp.mean(x * x, axis=-1, keepdims=True) + EPS) * g


def _dot(a, b):
    return jnp.dot(a, b, preferred_element_type=F32)


def _dot_hi(a, b):
    return jnp.dot(a, b, preferred_element_type=F32, precision=HIGHEST)


def _cast_body(x_ref, o_ref):
    o_ref[...] = x_ref[...].astype(o_ref.dtype)


def _to_bf16(w, layer, cols=None, *, block_bytes=4 << 20):
    _, rows, width = w.shape
    cols = width if cols is None else cols
    tr = rows
    while tr * cols * 4 > block_bytes and tr % 32 == 0:
        tr //= 2
    return pl.pallas_call(
        _cast_body,
        out_shape=jax.ShapeDtypeStruct((rows, cols), BF16),
        grid=(rows // tr,),
        in_specs=[pl.BlockSpec((None, tr, cols), lambda i: (layer, i, 0))],
        out_specs=pl.BlockSpec((tr, cols), lambda i: (i, 0)),
        compiler_params=_params(("parallel",)),
        name="to_bf16",
    )(w)


def _ffn_body(x_ref, g_ref, wg_ref, wu_ref, wd_ref, fn_ref, o_ref, xn_ref, *, final, sub):
    j = pl.program_id(1)

    row_chunks = [slice(r, r + ROW_CHUNK) for r in range(0, x_ref.shape[0], ROW_CHUNK)]

    @pl.when(j == 0)
    def _():
        for rs in row_chunks:
            xn_ref[rs, :] = _rms(x_ref[rs, :], g_ref[...]).astype(BF16)
        o_ref[...] = jnp.zeros_like(o_ref)

    xn = xn_ref[...]
    hs = []
    for c in range(wg_ref.shape[1] // sub):
        sl = slice(c * sub, (c + 1) * sub)
        g = _dot(xn, wg_ref[:, sl])
        u = _dot(xn, wu_ref[:, sl])
        hs.append((g * jax.nn.sigmoid(g) * u).astype(BF16))
    o_ref[...] += _dot(jnp.concatenate(hs, axis=1), wd_ref[...])

    @pl.when(j == pl.num_programs(1) - 1)
    def _():
        for rs in row_chunks:
            y = x_ref[rs, :] + 0.5 * o_ref[rs, :]
            if final:
                y = _rms(y, fn_ref[...])
            o_ref[rs, :] = y


def _ffn(x, norm_g, w_gu, w_down, final_g, *, final, tm=1024, tf=512, sub=256):
    L, D = x.shape
    F = w_down.shape[0]
    tm = min(tm, L)
    nf = F // tf
    x_mode = dict(pipeline_mode=pl.Buffered(1)) if final else {}
    return pl.pallas_call(
        functools.partial(_ffn_body, final=final, sub=min(sub, tf)),
        out_shape=jax.ShapeDtypeStruct((L, D), F32),
        grid=(L // tm, nf),
        in_specs=[
            pl.BlockSpec((tm, D), lambda i, j: (i, 0), **x_mode),
            pl.BlockSpec((1, D), lambda i, j: (0, 0)),
            pl.BlockSpec((D, tf), lambda i, j: (0, j)),
            pl.BlockSpec((D, tf), lambda i, j: (0, j + nf)),
            pl.BlockSpec((tf, D), lambda i, j: (j, 0)),
            pl.BlockSpec((1, D), lambda i, j: (0, 0)),
        ],
        out_specs=pl.BlockSpec((tm, D), lambda i, j: (i, 0)),
        scratch_shapes=[pltpu.VMEM((tm, D), BF16)],
        compiler_params=_params(("parallel", "arbitrary")),
        name="ffn",
    )(x, norm_g.reshape(1, D), w_gu, w_gu, w_down, final_g.reshape(1, D))


def _conv3_inner(x, w):
    n = x.shape[0]
    y = pltpu.roll(x, 1, axis=0) * w[0:1] + x * w[1:2] + pltpu.roll(x, n - 1, axis=0) * w[2:3]
    return y[SUBLANES:n - SUBLANES]


def _proj_body(h_ref, hp_ref, hn_ref, g_ref, w_ref, b_ref, wlr_ref, blr_ref, caw_ref, cbw_ref,
               p_ref, lr_ref, a_ref, z_ref, g1_ref, g2_ref, *, tn, wa):
    i = pl.program_id(0)
    tm = h_ref.shape[0]
    g = g_ref[...]
    xn = _rms(h_ref[...], g)
    xn_bf = xn.astype(BF16)
    lr_ref[...] = _dot(xn_bf, wlr_ref[...]) + blr_ref[...]
    conv_w = 6 * wa
    for c in range(p_ref.shape[1] // tn):
        src = slice(conv_w + c * tn, conv_w + (c + 1) * tn)
        p_ref[:, c * tn:(c + 1) * tn] = _dot(xn_bf, w_ref[:, src]) + b_ref[:, src]

    xh = jnp.concatenate([_rms(hp_ref[...], g), xn, _rms(hn_ref[...], g)], axis=0).astype(BF16)
    row = lax.broadcasted_iota(jnp.int32, (tm + 2 * SUBLANES, 1), 0)
    has_prev = (i > 0).astype(F32)
    has_next = (i < pl.num_programs(0) - 1).astype(F32)
    valid = jnp.where(row < SUBLANES, has_prev, jnp.where(row >= tm + SUBLANES, has_next, 1.0))

    def sec(k):
        sl = slice(k * wa, (k + 1) * wa)
        return (_dot(xh, w_ref[:, sl]) + b_ref[:, sl]) * valid

    xa, ba, ca = sec(0), sec(1), sec(2)
    a_ref[...] = (ba[SUBLANES:tm + SUBLANES] * _conv3_inner(ca * xa, caw_ref[...])).astype(BF16)
    for k, o_ref in enumerate((z_ref, g1_ref, g2_ref)):
        _store_groups(o_ref, _conv3_inner(sec(3 + k), cbw_ref[k]))


def _proj(h, norm_g, w, b, w_lr, b_lr, conv_a, conv_b, *, wa, tm=256, tn=512):
    L, D = h.shape
    conv_w = 6 * wa
    n_scan = w.shape[1] - conv_w
    rb = tm // SUBLANES
    nrb = L // SUBLANES
    resident = lambda shape: pl.BlockSpec(shape, lambda i: (0,) * len(shape), pipeline_mode=pl.Buffered(1))
    grouped = pl.BlockSpec((wa // LANES, tm, LANES), lambda i: (0, i, 0))
    return pl.pallas_call(
        functools.partial(_proj_body, tn=tn, wa=wa),
        out_shape=(jax.ShapeDtypeStruct((L, n_scan), F32), jax.ShapeDtypeStruct((L, LANES), F32),
                   jax.ShapeDtypeStruct((L, wa), BF16))
        + (jax.ShapeDtypeStruct((wa // LANES, L, LANES), F32),) * 3,
        grid=(L // tm,),
        in_specs=[
            pl.BlockSpec((tm, D), lambda i: (i, 0)),
            pl.BlockSpec((SUBLANES, D), lambda i: (jnp.maximum(i * rb - 1, 0), 0)),
            pl.BlockSpec((SUBLANES, D), lambda i: (jnp.minimum((i + 1) * rb, nrb - 1), 0)),
            resident((1, D)), resident(w.shape), resident((1, w.shape[1])),
            resident((D, LANES)), resident((1, LANES)), resident((3, wa)), resident((3, 3, wa)),
        ],
        out_specs=(pl.BlockSpec((tm, n_scan), lambda i: (i, 0)),
                   pl.BlockSpec((tm, LANES), lambda i: (i, 0)),
                   pl.BlockSpec((tm, wa), lambda i: (i, 0)), grouped, grouped, grouped),
        compiler_params=_params(("parallel",)),
        name="proj",
    )(h, h, h, norm_g.reshape(1, D), w, b, w_lr, b_lr, conv_a, conv_b)


def _store_groups(o_ref, x):
    for g in range(o_ref.shape[0]):
        o_ref[g] = x[:, g * LANES:(g + 1) * LANES]


def _load_groups(ref):
    return jnp.concatenate([ref[g] for g in range(ref.shape[0])], axis=1)


def _filter_body(w1t_ref, w1cs_ref, b1_ref, w2_ref, b2_ref, w3_ref, b3_ref, fr_ref, bands_ref, wo_ref, dl_ref,
                 k_ref, s_ref, *, half_tiles, tr, seq):
    i = pl.program_id(0)

    def lag_of(n):
        return jnp.where(n < seq, n, jnp.where(n == seq, 0, 2 * seq - n)).astype(F32)

    lag_row = lag_of(i * tr + lax.broadcasted_iota(jnp.int32, (1, tr), 1))
    ang = ((2.0 * math.pi / seq) * lag_row) * bands_ref[...]
    feats = jnp.concatenate([jnp.cos(ang), -jnp.sin(ang)], axis=0)
    fr = fr_ref[...]
    pre = w1t_ref[...] * (lag_row / max(seq - 1, 1)) + _dot_hi(w1cs_ref[...], feats) + b1_ref[...]
    h = jnp.sin(fr * pre)
    h = jnp.sin(fr * (_dot_hi(w2_ref[...], h) + b2_ref[...]))
    h = jnp.sin(fr * (_dot_hi(w3_ref[...], h) + b3_ref[...]))
    t = lag_of(i * tr + lax.broadcasted_iota(jnp.int32, (tr, 1), 0)) / max(seq - 1, 1)
    decay = jnp.exp(-t * jnp.abs(dl_ref[...]))
    hd = lax.dot_general(h, wo_ref[0], _TN, preferred_element_type=F32, precision=HIGHEST) * decay

    @pl.when(i % half_tiles == 0)
    def _():
        s_ref[...] = jnp.zeros_like(s_ref)

    s_ref[0] += jnp.sum(jnp.abs(hd), axis=0, keepdims=True)
    row = lax.broadcasted_iota(jnp.int32, hd.shape, 0)
    zero_row = jnp.where(i == half_tiles, 0, -1)
    _store_groups(k_ref, jnp.where(row == zero_row, 0.0, hd))


def _filters(seq, w1, b1, w2, b2, w3, b3, freq, w_out2, deltas2, *, tr=512):
    n_rows = 2 * seq
    hid = w2.shape[0]
    width = w_out2.shape[2]
    half_tiles = n_rows // (2 * tr)
    bands = jnp.linspace(1e-4, HY_BANDS - 1, HY_BANDS, dtype=F32).reshape(HY_BANDS, 1)
    full = lambda shape: pl.BlockSpec(shape, lambda i: (0,) * len(shape))
    colv = lambda v: v.reshape(hid, 1)
    return pl.pallas_call(
        functools.partial(_filter_body, half_tiles=half_tiles, tr=tr, seq=seq),
        out_shape=(jax.ShapeDtypeStruct((width // LANES, n_rows, LANES), F32),
                   jax.ShapeDtypeStruct((2, 1, width), F32)),
        grid=(n_rows // tr,),
        in_specs=[
            full((hid, 1)), full((hid, 2 * HY_BANDS)), full((hid, 1)), full((hid, hid)), full((hid, 1)),
            full((hid, hid)), full((hid, 1)), full((hid, 1)), full((HY_BANDS, 1)),
            pl.BlockSpec((1, hid, width), lambda i: (i // half_tiles, 0, 0)),
            full((1, width)),
        ],
        out_specs=(pl.BlockSpec((width // LANES, tr, LANES), lambda i: (0, i, 0)),
                   pl.BlockSpec((1, 1, width), lambda i: (i // half_tiles, 0, 0))),
        compiler_params=_params(("arbitrary",)),
        name="hyena_filters",
    )(w1[0:1].T, w1[1:].T, colv(b1), w2.T, colv(b2), w3.T, colv(b3), colv(freq), bands, w_out2, deltas2)


def _dft_consts(n1, n2):
    n = n1 * n2
    k = np.arange(n1)
    m = np.arange(n2)
    ang2 = 2.0 * np.pi * ((m[:, None] * m[None, :]) % n2) / n2
    c2, s2 = np.cos(ang2), np.sin(ang2)
    as_bf = lambda a: jnp.asarray(np.asarray(a, np.float32).astype(jnp.bfloat16))
    time_index = k[None, None, :] * n2 + m[:, None, None]
    ang = 2.0 * np.pi * ((k[None, :, None] * time_index) % n) / n
    stage = np.concatenate([np.cos(ang), -np.sin(ang)], axis=1)
    half = n1 // 2
    return dict(
        fwd1=as_bf(stage),
        inv1=as_bf(stage.transpose(0, 2, 1)[:, :half] / n),
        fwd2=as_bf(np.block([[c2, s2], [-s2, c2]])),
        inv2=as_bf(np.block([[c2, -s2], [s2, c2]])),
    )


def _flatten_block(ref, buf):
    groups, n, t2, lanes = ref.shape
    for g in range(groups):
        buf[g] = ref[g].reshape(n * t2, lanes)


def _unflatten_block(buf, ref):
    groups, n, t2, lanes = ref.shape
    for g in range(groups):
        ref[g] = buf[g].reshape(n, t2, lanes)


def _load_strided(buf, s, t2):
    n = buf.shape[1] // t2
    return jnp.concatenate([buf[g, pl.ds(s, n, stride=t2), :] for g in range(buf.shape[0])], axis=1)


def _store_strided(buf, s, t2, x):
    n = buf.shape[1] // t2
    for g in range(buf.shape[0]):
        buf[g, pl.ds(s, n, stride=t2), :] = x[:, g * LANES:(g + 1) * LANES]


def _flat_scratch(groups, n, t2, dtype=F32):
    return pltpu.VMEM((groups, n * t2, LANES), dtype)


U32 = jnp.uint32


def _pack_complex(re, im):
    hi = lax.bitcast_convert_type(re.astype(BF16).astype(F32), U32)
    lo = lax.bitcast_convert_type(im.astype(BF16).astype(F32), U32)
    return hi | (lo >> 16)


def _unpack_complex(w):
    re = lax.bitcast_convert_type(w & jnp.uint32(0xFFFF0000), F32)
    im = lax.bitcast_convert_type(w << 16, F32)
    return re, im


def _outer_fwd_body(x_ref, f_ref, *rest, scaled):
    if scaled:
        s_ref, a_ref, xb, ab = rest
        inv = 1.0 / (s_ref[...] + EPS)
    else:
        a_ref, xb, ab = rest
    rows, t2 = x_ref.shape[1:3]
    n1 = a_ref.shape[1]
    _flatten_block(x_ref, xb)
    for s in range(t2):
        x = _load_strided(xb, s, t2)
        if scaled:
            half = rows // 2
            x = jnp.concatenate([x[:half] * inv[0], x[half:] * inv[1]], axis=0)
        a = _dot(f_ref[s], x.astype(BF16))
        _store_strided(ab, s, t2, _pack_complex(a[:n1], a[n1:]))
    _unflatten_block(ab, a_ref)


def _outer_fwd(x4, fwd1, sums=None, *, t2=SUBLANES, gt=4):
    groups, rows, n2, _ = x4.shape
    n1 = fwd1.shape[2]
    scaled = sums is not None
    blk = lambda r: pl.BlockSpec((gt, r, t2, LANES), lambda j, c: (c, 0, j, 0))
    args = [x4, fwd1]
    in_specs = [blk(rows), pl.BlockSpec((t2, 2 * n1, rows), lambda j, c: (j, 0, 0))]
    if scaled:
        args.append(sums)
        in_specs.append(pl.BlockSpec((2, 1, gt * LANES), lambda j, c: (0, 0, c)))
    return pl.pallas_call(
        functools.partial(_outer_fwd_body, scaled=scaled),
        out_shape=jax.ShapeDtypeStruct((groups, n1, n2, LANES), U32),
        grid=(n2 // t2, groups // gt),
        in_specs=in_specs,
        out_specs=blk(n1),
        scratch_shapes=[_flat_scratch(gt, rows, t2), _flat_scratch(gt, n1, t2, U32)],
        compiler_params=_params(("parallel", "parallel")),
        name="dft_outer_fwd",
    )(*args)


def _load_slab(ref, s):
    return jnp.concatenate([ref[g, s] for g in range(ref.shape[0])], axis=1)


def _store_slab(ref, s, x):
    for g in range(ref.shape[0]):
        ref[g, s] = x[:, g * LANES:(g + 1) * LANES]


def _mid_conv_body(a_ref, k_ref, f2_ref, g2_ref, d_ref):
    n2 = a_ref.shape[2]
    width = a_ref.shape[0] * LANES
    for s in range(a_ref.shape[1]):
        re, im = _unpack_complex(jnp.concatenate([_load_slab(a_ref, s), _load_slab(k_ref, s)], axis=1))
        x = _dot(f2_ref[...], jnp.concatenate([re, im], axis=0).astype(BF16))
        xr, xi, kr, ki = x[:n2, :width], x[n2:, :width], x[:n2, width:], x[n2:, width:]
        yr = xr * kr - xi * ki
        yi = xr * ki + xi * kr
        c = _dot(g2_ref[...], jnp.concatenate([yr, yi], axis=0).astype(BF16))
        _store_slab(d_ref, s, _pack_complex(c[:n2], c[n2:]))


def _mid(a, kf, order, fwd2, inv2, *, kb=4):
    groups, n1, n2, _ = a.shape
    slab = pl.BlockSpec((groups, kb, n2, LANES), lambda i: (0, i, 0, 0))
    kslab = pl.BlockSpec((groups, kb, n2, LANES), lambda i: (order, i, 0, 0))
    mat = pl.BlockSpec((2 * n2, 2 * n2), lambda i: (0, 0))
    return pl.pallas_call(
        _mid_conv_body,
        out_shape=jax.ShapeDtypeStruct(a.shape, U32),
        grid=(n1 // kb,),
        in_specs=[slab, kslab, mat, mat],
        out_specs=slab,
        compiler_params=_params(("parallel",)),
        name="hyena_mid_conv",
    )(a, kf, fwd2, inv2)


def _outer_inv_body(d_ref, e_ref, gate_ref, z_ref, skip_ref, o_ref, db, gb, zb, ob):
    skip = skip_ref[...]
    t2 = d_ref.shape[2]
    for ref, buf in ((d_ref, db), (gate_ref, gb), (z_ref, zb)):
        _flatten_block(ref, buf)
    for s in range(t2):
        dr, di = _unpack_complex(_load_strided(db, s, t2))
        y = _dot(e_ref[s], jnp.concatenate([dr, di], axis=0).astype(BF16))
        _store_strided(ob, s, t2, _load_strided(gb, s, t2) * (y + skip * _load_strided(zb, s, t2)))
    _unflatten_block(ob, o_ref)


def _outer_inv(d, inv1, gate4, z4, skip, *, t2=SUBLANES):
    groups, n1, n2, _ = d.shape
    half = n1 // 2
    blk = lambda r: pl.BlockSpec((groups, r, t2, LANES), lambda j: (0, 0, j, 0))
    return pl.pallas_call(
        _outer_inv_body,
        out_shape=jax.ShapeDtypeStruct((groups, half, n2, LANES), F32),
        grid=(n2 // t2,),
        in_specs=[blk(n1), pl.BlockSpec((t2, half, 2 * n1), lambda j: (j, 0, 0)), blk(half), blk(half),
                  pl.BlockSpec((1, groups * LANES), lambda j: (0, 0))],
        out_specs=blk(half),
        scratch_shapes=[_flat_scratch(groups, n1, t2, U32)] + [_flat_scratch(groups, half, t2)] * 3,
        compiler_params=_params(("parallel",)),
        name="dft_outer_inv",
    )(d, inv1, gate4, z4, skip)


def _log_sigmoid(x):
    return jnp.minimum(x, 0.0) - jnp.log1p(jnp.exp(-jnp.abs(x)))


_NT = (((1,), (1,)), ((), ()))
_TN = (((0,), (0,)), ((), ()))


def _chunk_matrices(sub):
    i = np.arange(sub)
    same = (i[:, None] // GLA_CHUNK) == (i[None, :] // GLA_CHUNK)
    fwd = same & (i[None, :] <= i[:, None])
    bwd = same & (i[None, :] >= i[:, None])
    return jnp.asarray(np.stack([fwd, bwd, same]), F32)


def _gla_gates_body(q_ref, k_ref, v_ref, lr_ref, w_ref, b_ref, cm_ref,
                    qe_ref, ke_ref, kd_ref, dec_ref, vb_ref, *, scale):
    tg, width = q_ref.shape
    sub = cm_ref.shape[1]
    chunk = GLA_CHUNK
    vb_ref[...] = v_ref[...].astype(BF16)
    same = cm_ref[2].astype(BF16)
    for s in range(tg // sub):
        rs = slice(s * sub, (s + 1) * sub)
        g = _log_sigmoid(_dot(lr_ref[rs, :].astype(BF16), w_ref[...]) + b_ref[...]) / GLA_TAU
        g_hi = g.astype(BF16)
        g_lo = (g - g_hi.astype(F32)).astype(BF16)
        q = q_ref[rs, :] * scale
        k = k_ref[rs, :]
        for d in range(2):
            cs = slice(d * width, (d + 1) * width)
            cum = cm_ref[d].astype(BF16)
            bc = _dot(cum, g_hi[:, cs]) + _dot(cum, g_lo[:, cs])
            bl = _dot(same, g_hi[:, cs]) + _dot(same, g_lo[:, cs])
            qe_ref[d, rs, :] = (q * jnp.exp(bc)).astype(BF16)
            ke_ref[d, rs, :] = (k * jnp.exp(-bc)).astype(BF16)
            kd_ref[d, rs, :] = (k * jnp.exp(bl - bc)).astype(BF16)
            for c in range(sub // chunk):
                row = s * (sub // chunk) + c
                dec_ref[d, row:row + 1, :] = jnp.exp(bl[c * chunk:c * chunk + 1])


def _gla_gates(p, lr, w_both, b_both, *, q_off, k_off, v_off, dk, dv, heads, tg=512, sub=256):
    L = p.shape[0]
    tg = min(tg, L)
    wk, wv = heads * dk, heads * dv
    opnd = jax.ShapeDtypeStruct((2, L, wk), BF16)
    row = lambda width, off: pl.BlockSpec((tg, width), lambda i: (i, off // width))
    both = pl.BlockSpec((2, tg, wk), lambda i: (0, i, 0))
    const = lambda shape: pl.BlockSpec(shape, lambda i: (0,) * len(shape))
    return pl.pallas_call(
        functools.partial(_gla_gates_body, scale=dk ** -0.5),
        out_shape=(opnd, opnd, opnd, jax.ShapeDtypeStruct((2, L // GLA_CHUNK, wk), F32),
                   jax.ShapeDtypeStruct((L, wv), BF16)),
        grid=(L // tg,),
        in_specs=[row(wk, q_off), row(wk, k_off), row(wv, v_off), row(LANES, 0),
                  const((LANES, 2 * wk)), const((1, 2 * wk)), const((3, sub, sub))],
        out_specs=(both, both, both, pl.BlockSpec((2, tg // GLA_CHUNK, wk), lambda i: (0, i, 0)),
                   pl.BlockSpec((tg, wv), lambda i: (i, 0))),
        compiler_params=_params(("parallel",)),
        name="gla_gates",
    )(p, p, p, lr, w_both, b_both, _chunk_matrices(sub))


def _gla_direction(qe_ref, ke_ref, kd_ref, v_ref, dec_ref, mask, o_ref, st_ref, *, reverse):
    tb = qe_ref.shape[0]
    sub = mask.shape[0]
    chunk = GLA_CHUNK
    intras = []
    for s in range(tb // sub):
        rs = slice(s * sub, (s + 1) * sub)
        a = lax.dot_general(qe_ref[rs, :], ke_ref[rs, :], _NT, preferred_element_type=F32)
        intra = _dot(jnp.where(mask > 0.5, a, 0.0).astype(BF16), v_ref[rs, :])
        intras += [intra[c * chunk:(c + 1) * chunk] for c in range(sub // chunk)]
    st = st_ref[...]
    chunks = range(tb // chunk)
    for c in (reversed(chunks) if reverse else chunks):
        sl = slice(c * chunk, (c + 1) * chunk)
        o_ref[sl, :] = intras[c] + lax.dot_general(qe_ref[sl, :], st.astype(BF16), _NT,
                                                   preferred_element_type=F32)
        upd = lax.dot_general(v_ref[sl, :], kd_ref[sl, :], _TN, preferred_element_type=F32)
        st = st * dec_ref[c:c + 1, :] + upd
    st_ref[...] = st


def _gla_body(qef, kef, kdf, vf, decf, qeb, keb, kdb, vb, decb, cm_ref, of_ref, ob_ref, sf_ref, sb_ref):
    @pl.when(pl.program_id(1) == 0)
    def _():
        sf_ref[...] = jnp.zeros_like(sf_ref)
        sb_ref[...] = jnp.zeros_like(sb_ref)

    _gla_direction(qef, kef, kdf, vf, decf, cm_ref[0], of_ref, sf_ref, reverse=False)
    _gla_direction(qeb, keb, kdb, vb, decb, cm_ref[1], ob_ref, sb_ref, reverse=True)


def _gla(qe, ke, kd, dec, vb, *, dk, dv, tb=1024, sub=256):
    L = vb.shape[0]
    tb = min(tb, L)
    nb = L // tb
    heads = vb.shape[1] // dv
    nc = tb // GLA_CHUNK
    blk = lambda i, rev: (nb - 1 - i) if rev else i
    opnd = lambda rev: pl.BlockSpec((None, tb, dk), lambda h, i: (int(rev), blk(i, rev), h))
    vspec = lambda rev: pl.BlockSpec((tb, dv), lambda h, i: (blk(i, rev), h))
    dspec = lambda rev: pl.BlockSpec((None, nc, dk), lambda h, i: (int(rev), blk(i, rev), h))
    side = lambda rev: [opnd(rev), opnd(rev), opnd(rev), vspec(rev), dspec(rev)]
    return pl.pallas_call(
        _gla_body,
        out_shape=(jax.ShapeDtypeStruct((L, heads * dv), F32),) * 2,
        grid=(heads, nb),
        in_specs=side(False) + side(True) + [pl.BlockSpec((3, sub, sub), lambda h, i: (0, 0, 0))],
        out_specs=(vspec(False), vspec(True)),
        scratch_shapes=[pltpu.VMEM((dv, dk), F32), pltpu.VMEM((dv, dk), F32)],
        compiler_params=_params(("parallel", "arbitrary")),
        name="gla",
    )(qe, ke, kd, vb, dec, qe, ke, kd, vb, dec, _chunk_matrices(sub))


def _gla_post_body(of_ref, ob_ref, r_ref, gn_ref, c_ref, *, dv):
    o = of_ref[...] + ob_ref[...]
    r = r_ref[...]
    gate = r * jax.nn.sigmoid(r)
    for hd in range(o.shape[1] // dv):
        sl = slice(hd * dv, (hd + 1) * dv)
        c_ref[:, sl] = (_rms(o[:, sl], gn_ref[...]) * gate[:, sl]).astype(BF16)


def _gla_post(o_f, o_b, p, gla_norm, *, r_off, tm=512):
    L, wc = o_f.shape
    dv = gla_norm.shape[0]
    row = lambda blk=0: pl.BlockSpec((tm, wc), lambda i: (i, blk))
    return pl.pallas_call(
        functools.partial(_gla_post_body, dv=dv),
        out_shape=jax.ShapeDtypeStruct((L, wc), BF16),
        grid=(L // tm,),
        in_specs=[row(), row(), row(r_off // wc), pl.BlockSpec((1, dv), lambda i: (0, 0))],
        out_specs=row(),
        compiler_params=_params(("parallel",)),
        name="gla_post",
    )(o_f, o_b, p, gla_norm.reshape(1, dv))


def _merge_body(h_ref, g_ref, a_ref, z_ref, c_ref,
                wga_ref, wgb_ref, wgc_ref, bga_ref, bgb_ref, bgc_ref,
                wa_ref, wb_ref, wc_ref, m_ref, u_ref, zb_ref):
    @pl.when(pl.program_id(1) == 0)
    def _():
        u_ref[...] = _rms(h_ref[...], g_ref[...]).astype(BF16)
        zb_ref[...] = _load_groups(z_ref).astype(BF16)

    u = u_ref[...]
    m = jax.nn.sigmoid(_dot(u, wga_ref[...]) + bga_ref[...]) * _dot(a_ref[...], wa_ref[...])
    m += jax.nn.sigmoid(_dot(u, wgb_ref[...]) + bgb_ref[...]) * _dot(zb_ref[...], wb_ref[...])
    m += jax.nn.sigmoid(_dot(u, wgc_ref[...]) + bgc_ref[...]) * _dot(c_ref[...], wc_ref[...])
    m_ref[...] = m.astype(BF16)


def _merge(h, norm_g, a_pre, z, c_pre, w_gate, b_gate, w_a, w_b, w_c, *, tm=1024, tn=512):
    L, D = h.shape
    wa, wb, wc = w_a.shape[0], w_b.shape[0], w_c.shape[0]
    tm = min(tm, L)
    nj = D // tn
    row = lambda width: pl.BlockSpec((tm, width), lambda i, j: (i, 0))
    gate_w = lambda k: pl.BlockSpec((D, tn), lambda i, j: (0, j + k * nj))
    gate_b = lambda k: pl.BlockSpec((1, tn), lambda i, j: (0, j + k * nj))
    br_w = lambda width: pl.BlockSpec((width, tn), lambda i, j: (0, j))
    return pl.pallas_call(
        _merge_body,
        out_shape=jax.ShapeDtypeStruct((L, D), BF16),
        grid=(L // tm, nj),
        in_specs=[
            pl.BlockSpec((tm, D), lambda i, j: (i, 0), pipeline_mode=pl.Buffered(1)),
            pl.BlockSpec((1, D), lambda i, j: (0, 0)),
            row(wa), pl.BlockSpec((wb // LANES, tm, LANES), lambda i, j: (0, i, 0)), row(wc),
            gate_w(0), gate_w(1), gate_w(2), gate_b(0), gate_b(1), gate_b(2),
            br_w(wa), br_w(wb), br_w(wc),
        ],
        out_specs=pl.BlockSpec((tm, tn), lambda i, j: (i, j)),
        scratch_shapes=[pltpu.VMEM((tm, D), BF16), pltpu.VMEM((tm, wb), BF16)],
        compiler_params=_params(("parallel", "arbitrary")),
        name="merge",
    )(h, norm_g.reshape(1, D), a_pre, z, c_pre,
      w_gate, w_gate, w_gate, b_gate, b_gate, b_gate, w_a, w_b, w_c)


def _out_proj_body(h_ref, m_ref, w_ref, o_ref, *, tn):
    m = m_ref[...]
    for c in range(o_ref.shape[1] // tn):
        sl = slice(c * tn, (c + 1) * tn)
        o_ref[:, sl] = h_ref[:, sl] + _dot(m, w_ref[:, sl])


def _out_proj(h, m, w_o, *, tm=512, tn=512):
    L, D = h.shape
    row = pl.BlockSpec((tm, D), lambda i: (i, 0))
    return pl.pallas_call(
        functools.partial(_out_proj_body, tn=tn),
        out_shape=jax.ShapeDtypeStruct((L, D), F32),
        grid=(L // tm,),
        in_specs=[row, row, pl.BlockSpec((D, D), lambda i: (0, 0), pipeline_mode=pl.Buffered(1))],
        out_specs=row,
        compiler_params=_params(("parallel",)),
        name="out_proj",
    )(h, m, w_o)


def _hyena(z0, g1, g2, skip, filt, consts):
    groups, L, _ = z0.shape
    n1 = DFT_N1
    n2 = 2 * L // n1
    k_time, sums = filt
    kf = _outer_fwd(k_time.reshape(k_time.shape[0], n1, n2, LANES), consts["fwd1"], sums)
    z = z0.reshape(groups, n1 // 2, n2, LANES)
    for order, gate in enumerate((g1, g2)):
        a = _outer_fwd(z, consts["fwd1"])
        d = _mid(a, kf, order, consts["fwd2"], consts["inv2"])
        z = _outer_inv(d, consts["inv1"], gate.reshape(z.shape), z, skip[order].reshape(1, groups * LANES))
    return z.reshape(groups, L, LANES)


def kernel(x, ffn1_norm, ffn1_w_gu, ffn1_w_down, mix_norm, w_in, b_in, conv_a, conv_b, hf_w1, hf_b1, hf_w2, hf_b2, hf_w3, hf_b3, hf_w_out, hf_freq, hy_skip, gk_w2, gk_b, gla_norm, w_br_a, w_br_b, w_br_c, w_o, ffn2_norm, ffn2_w_gu, ffn2_w_down, final_norm):
    bsz, L, D = x.shape
    depth = ffn1_norm.shape[0]
    wa = conv_a.shape[2]
    wb = conv_b.shape[3]
    wck = gk_w2.shape[3]
    dv = gla_norm.shape[1]
    wcv = GLA_HEADS * dv
    dk = wck // GLA_HEADS
    hid = hf_w2.shape[1]
    assert wa == wb, "the projection kernel tiles both conv branches with one section width"
    n_main = 3 * wa + 3 * wb + 2 * wck + 2 * wcv
    q_off, k_off, v_off, r_off = 0, wck, 2 * wck, 2 * wck + wcv
    lr_off = n_main
    gate_off = n_main + 2 * GLA_RANK
    n1 = DFT_N1
    n2 = 2 * L // n1

    consts = _dft_consts(n1, n2)
    deltas =jnp.linspace(math.log(HY_TARGET) / HY_SLOW_DECAY, math.log(HY_TARGET) / HY_FAST_DECAY, wb, dtype=F32)
    deltas2 = jnp.tile(deltas.reshape(1, wb), (1, HY_ORDER))

    outs = []
    for bi in range(bsz):
        h = x[bi]
        for l in range(depth):
            h = _ffn(h, ffn1_norm[l], _to_bf16(ffn1_w_gu, l), _to_bf16(ffn1_w_down, l),
                     final_norm, final=False)

            w_lr = jnp.pad(w_in[l][:, lr_off:gate_off], ((0, 0), (0, LANES - 2 * GLA_RANK))).astype(BF16)
            b_lr = jnp.pad(b_in[l][lr_off:gate_off], (0, LANES - 2 * GLA_RANK)).reshape(1, LANES)
            p, lr, a_pre, z0, g1, g2 = _proj(
                h, mix_norm[l], w_in[l][:, :n_main].astype(BF16), b_in[l][:n_main].reshape(1, n_main),
                w_lr, b_lr, conv_a[l], conv_b[l], wa=wa)

            w_out2 = hf_w_out[l].reshape(hid, HY_ORDER, 2, wb).transpose(2, 0, 1, 3).reshape(2, hid, HY_ORDER * wb)
            filt = _filters(L, hf_w1[l], hf_b1[l], hf_w2[l], hf_b2[l], hf_w3[l], hf_b3[l], hf_freq[l],
                            w_out2, deltas2)
            zb = _hyena(z0, g1, g2, hy_skip[l], filt, consts)

            w_both = jnp.zeros((LANES, 2 * wck), F32)
            w_both = w_both.at[:GLA_RANK, :wck].set(gk_w2[l, 0]).at[GLA_RANK:2 * GLA_RANK, wck:].set(gk_w2[l, 1])
            qe, ke, kd, dec, vb = _gla_gates(p, lr, w_both.astype(BF16), gk_b[l].reshape(1, 2 * wck),
                                             q_off=q_off, k_off=k_off, v_off=v_off, dk=dk, dv=dv,
                                             heads=GLA_HEADS)
            o_f, o_b = _gla(qe, ke, kd, dec, vb, dk=dk, dv=dv)

            c_pre = _gla_post(o_f, o_b, p, gla_norm[l], r_off=r_off)

            merged = _merge(h, mix_norm[l], a_pre, zb, c_pre,
                            w_in[l][:, gate_off:].astype(BF16), b_in[l][gate_off:].reshape(1, 3 * D),
                            _to_bf16(w_br_a, l), _to_bf16(w_br_b, l), _to_bf16(w_br_c, l))
            h = _out_proj(h, merged, _to_bf16(w_o, l))

            h = _ffn(h, ffn2_norm[l], _to_bf16(ffn2_w_gu, l), _to_bf16(ffn2_w_down, l),
                     final_norm, final=(l == depth - 1))
        outs.append(h)
    return jnp.stack(outs, axis=0)
```

```python
import functools
import math

import numpy as np
import jax
import jax.numpy as jnp
from jax import lax
from jax.experimental import pallas as pl
from jax.experimental.pallas import tpu as pltpu

F32 = jnp.float32
BF16 = jnp.bfloat16
HIGHEST = lax.Precision.HIGHEST

EPS = 1e-5
GLA_HEADS = 4
GLA_RANK = 16
GLA_TAU = 16.0
GLA_CHUNK = 64
HY_ORDER = 2
HY_BANDS = 16
HY_FAST_DECAY = 0.3
HY_SLOW_DECAY = 1.5
HY_TARGET = 1e-2

LANES = 128
SUBLANES = 8
DFT_N1 = 256
ROW_CHUNK = 128
VMEM_LIMIT = 56 * 1024 * 1024


def _params(sem):
    return pltpu.CompilerParams(dimension_semantics=sem, vmem_limit_bytes=VMEM_LIMIT)


def _rms(x, g):
    return x * lax.rsqrt(jnp.mean(x * x, axis=-1, keepdims=True) + EPS) * g


def _dot(a, b):
    return jnp.dot(a, b, preferred_element_type=F32)


def _dot_hi(a, b):
    return jnp.dot(a, b, preferred_element_type=F32, precision=HIGHEST)


def _cast_body(x_ref, o_ref):
    o_ref[...] = x_ref[...].astype(o_ref.dtype)


def _to_bf16(w, layer, cols=None, *, block_bytes=4 << 20):
    _, rows, width = w.shape
    cols = width if cols is None else cols
    tr = rows
    while tr * cols * 4 > block_bytes and tr % 32 == 0:
        tr //= 2
    return pl.pallas_call(
        _cast_body,
        out_shape=jax.ShapeDtypeStruct((rows, cols), BF16),
        grid=(rows // tr,),
        in_specs=[pl.BlockSpec((None, tr, cols), lambda i: (layer, i, 0))],
        out_specs=pl.BlockSpec((tr, cols), lambda i: (i, 0)),
        compiler_params=_params(("parallel",)),
        name="to_bf16",
    )(w)


def _ffn_body(x_ref, g_ref, wg_ref, wu_ref, wd_ref, fn_ref, o_ref, xn_ref, *, final, sub):
    j = pl.program_id(1)

    row_chunks = [slice(r, r + ROW_CHUNK) for r in range(0, x_ref.shape[0], ROW_CHUNK)]

    @pl.when(j == 0)
    def _():
        for rs in row_chunks:
            xn_ref[rs, :] = _rms(x_ref[rs, :], g_ref[...]).astype(BF16)
        o_ref[...] = jnp.zeros_like(o_ref)

    xn = xn_ref[...]
    hs = []
    for c in range(wg_ref.shape[1] // sub):
        sl = slice(c * sub, (c + 1) * sub)
        g = _dot(xn, wg_ref[:, sl])
        u = _dot(xn, wu_ref[:, sl])
        hs.append((g * jax.nn.sigmoid(g) * u).astype(BF16))
    o_ref[...] += _dot(jnp.concatenate(hs, axis=1), wd_ref[...])

    @pl.when(j == pl.num_programs(1) - 1)
    def _():
        for rs in row_chunks:
            y = x_ref[rs, :] + 0.5 * o_ref[rs, :]
            if final:
                y = _rms(y, fn_ref[...])
            o_ref[rs, :] = y


def _ffn(x, norm_g, w_gu, w_down, final_g, *, final, tm=1024, tf=512, sub=256):
    L, D = x.shape
    F = w_down.shape[0]
    tm = min(tm, L)
    if final:
        tf = sub
    nf = F // tf
    return pl.pallas_call(
        functools.partial(_ffn_body, final=final, sub=min(sub, tf)),
        out_shape=jax.ShapeDtypeStruct((L, D), F32),
        grid=(L // tm, nf),
        in_specs=[
            pl.BlockSpec((tm, D), lambda i, j: (i, 0)),
            pl.BlockSpec((1, D), lambda i, j: (0, 0)),
            pl.BlockSpec((D, tf), lambda i, j: (0, j)),
            pl.BlockSpec((D, tf), lambda i, j: (0, j + nf)),
            pl.BlockSpec((tf, D), lambda i, j: (j, 0)),
            pl.BlockSpec((1, D), lambda i, j: (0, 0)),
        ],
        out_specs=pl.BlockSpec((tm, D), lambda i, j: (i, 0)),
        scratch_shapes=[pltpu.VMEM((tm, D), BF16)],
        compiler_params=_params(("parallel", "arbitrary")),
        name="ffn",
    )(x, norm_g.reshape(1, D), w_gu, w_gu, w_down, final_g.reshape(1, D))


def _conv3_inner(x, w):
    n = x.shape[0]
    y = pltpu.roll(x, 1, axis=0) * w[0:1] + x * w[1:2] + pltpu.roll(x, n - 1, axis=0) * w[2:3]
    return y[SUBLANES:n - SUBLANES]


def _proj_body(h_ref, hp_ref, hn_ref, g_ref, w_ref, b_ref, wlr_ref, blr_ref, caw_ref, cbw_ref,
               p_ref, lr_ref, a_ref, z_ref, g1_ref, g2_ref, *, tn, wa):
    i = pl.program_id(0)
    tm = h_ref.shape[0]
    g = g_ref[...]
    xn = _rms(h_ref[...], g)
    xn_bf = xn.astype(BF16)
    lr_ref[...] = _dot(xn_bf, wlr_ref[...]) + blr_ref[...]
    conv_w = 6 * wa
    for c in range(p_ref.shape[1] // tn):
        src = slice(conv_w + c * tn, conv_w + (c + 1) * tn)
        p_ref[:, c * tn:(c + 1) * tn] = _dot(xn_bf, w_ref[:, src]) + b_ref[:, src]

    xh = jnp.concatenate([_rms(hp_ref[...], g), xn, _rms(hn_ref[...], g)], axis=0).astype(BF16)
    row = lax.broadcasted_iota(jnp.int32, (tm + 2 * SUBLANES, 1), 0)
    has_prev = (i > 0).astype(F32)
    has_next = (i < pl.num_programs(0) - 1).astype(F32)
    valid = jnp.where(row < SUBLANES, has_prev, jnp.where(row >= tm + SUBLANES, has_next, 1.0))

    def sec(k):
        sl = slice(k * wa, (k + 1) * wa)
        return (_dot(xh, w_ref[:, sl]) + b_ref[:, sl]) * valid

    xa, ba, ca = sec(0), sec(1), sec(2)
    a_ref[...] = (ba[SUBLANES:tm + SUBLANES] * _conv3_inner(ca * xa, caw_ref[...])).astype(BF16)
    for k, o_ref in enumerate((z_ref, g1_ref, g2_ref)):
        _store_groups(o_ref, _conv3_inner(sec(3 + k), cbw_ref[k]))


def _proj(h, norm_g, w, b, w_lr, b_lr, conv_a, conv_b, *, n_main, wa, tm=256, tn=512):
    L, D = h.shape
    conv_w = 6 * wa
    n_scan = n_main - conv_w
    rb = tm // SUBLANES
    nrb = L // SUBLANES
    resident = lambda shape: pl.BlockSpec(shape, lambda i: (0,) * len(shape), pipeline_mode=pl.Buffered(1))
    grouped = pl.BlockSpec((wa // LANES, tm, LANES), lambda i: (0, i, 0))
    return pl.pallas_call(
        functools.partial(_proj_body, tn=tn, wa=wa),
        out_shape=(jax.ShapeDtypeStruct((L, n_scan), F32), jax.ShapeDtypeStruct((L, LANES), F32),
                   jax.ShapeDtypeStruct((L, wa), BF16))
        + (jax.ShapeDtypeStruct((wa // LANES, L, LANES), F32),) * 3,
        grid=(L // tm,),
        in_specs=[
            pl.BlockSpec((tm, D), lambda i: (i, 0)),
            pl.BlockSpec((SUBLANES, D), lambda i: (jnp.maximum(i * rb - 1, 0), 0)),
            pl.BlockSpec((SUBLANES, D), lambda i: (jnp.minimum((i + 1) * rb, nrb - 1), 0)),
            resident((1, D)), resident((D, n_main)), resident((1, n_main)),
            resident((D, LANES)), resident((1, LANES)), resident((3, wa)), resident((3, 3, wa)),
        ],
        out_specs=(pl.BlockSpec((tm, n_scan), lambda i: (i, 0)),
                   pl.BlockSpec((tm, LANES), lambda i: (i, 0)),
                   pl.BlockSpec((tm, wa), lambda i: (i, 0)), grouped, grouped, grouped),
        compiler_params=_params(("parallel",)),
        name="proj",
    )(h, h, h, norm_g.reshape(1, D), w, b, w_lr, b_lr, conv_a, conv_b)


def _store_groups(o_ref, x):
    for g in range(o_ref.shape[0]):
        o_ref[g] = x[:, g * LANES:(g + 1) * LANES]


def _load_groups(ref):
    return jnp.concatenate([ref[g] for g in range(ref.shape[0])], axis=1)


def _filter_body(w1t_ref, w1cs_ref, b1_ref, w2_ref, b2_ref, w3_ref, b3_ref, fr_ref, bands_ref, wo_ref, dl_ref,
                 k_ref, s_ref, *, half_tiles, tr, seq):
    i = pl.program_id(0)

    def lag_of(n):
        return jnp.where(n < seq, n, jnp.where(n == seq, 0, 2 * seq - n)).astype(F32)

    lag_row = lag_of(i * tr + lax.broadcasted_iota(jnp.int32, (1, tr), 1))
    ang = ((2.0 * math.pi / seq) * lag_row) * bands_ref[...]
    feats = jnp.concatenate([jnp.cos(ang), -jnp.sin(ang)], axis=0)
    fr = fr_ref[...]
    pre = w1t_ref[...] * (lag_row / max(seq - 1, 1)) + _dot_hi(w1cs_ref[...], feats) + b1_ref[...]
    h = jnp.sin(fr * pre)
    h = jnp.sin(fr * (_dot_hi(w2_ref[...], h) + b2_ref[...]))
    h = jnp.sin(fr * (_dot_hi(w3_ref[...], h) + b3_ref[...]))
    t = lag_of(i * tr + lax.broadcasted_iota(jnp.int32, (tr, 1), 0)) / max(seq - 1, 1)
    decay = jnp.exp(-t * jnp.abs(dl_ref[...]))
    hd = lax.dot_general(h, wo_ref[0], _TN, preferred_element_type=F32, precision=HIGHEST) * decay

    @pl.when(i % half_tiles == 0)
    def _():
        s_ref[...] = jnp.zeros_like(s_ref)

    s_ref[0] += jnp.sum(jnp.abs(hd), axis=0, keepdims=True)
    row = lax.broadcasted_iota(jnp.int32, hd.shape, 0)
    zero_row = jnp.where(i == half_tiles, 0, -1)
    _store_groups(k_ref, jnp.where(row == zero_row, 0.0, hd))


def _filters(seq, w1, b1, w2, b2, w3, b3, freq, w_out2, deltas2, *, tr=512):
    n_rows = 2 * seq
    hid = w2.shape[0]
    width = w_out2.shape[2]
    half_tiles = n_rows // (2 * tr)
    bands = jnp.linspace(1e-4, HY_BANDS - 1, HY_BANDS, dtype=F32).reshape(HY_BANDS, 1)
    full = lambda shape: pl.BlockSpec(shape, lambda i: (0,) * len(shape))
    colv = lambda v: v.reshape(hid, 1)
    return pl.pallas_call(
        functools.partial(_filter_body, half_tiles=half_tiles, tr=tr, seq=seq),
        out_shape=(jax.ShapeDtypeStruct((width // LANES, n_rows, LANES), F32),
                   jax.ShapeDtypeStruct((2, 1, width), F32)),
        grid=(n_rows // tr,),
        in_specs=[
            full((hid, 1)), full((hid, 2 * HY_BANDS)), full((hid, 1)), full((hid, hid)), full((hid, 1)),
            full((hid, hid)), full((hid, 1)), full((hid, 1)), full((HY_BANDS, 1)),
            pl.BlockSpec((1, hid, width), lambda i: (i // half_tiles, 0, 0)),
            full((1, width)),
        ],
        out_specs=(pl.BlockSpec((width // LANES, tr, LANES), lambda i: (0, i, 0)),
                   pl.BlockSpec((1, 1, width), lambda i: (i // half_tiles, 0, 0))),
        compiler_params=_params(("arbitrary",)),
        name="hyena_filters",
    )(w1[0:1].T, w1[1:].T, colv(b1), w2.T, colv(b2), w3.T, colv(b3), colv(freq), bands, w_out2, deltas2)


def _dft_consts(n1, n2):
    n = n1 * n2
    k = np.arange(n1)
    m = np.arange(n2)
    ang2 = 2.0 * np.pi * ((m[:, None] * m[None, :]) % n2) / n2
    c2, s2 = np.cos(ang2), np.sin(ang2)
    as_bf = lambda a: jnp.asarray(np.asarray(a, np.float32).astype(jnp.bfloat16))
    time_index = k[None, None, :] * n2 + m[:, None, None]
    ang = 2.0 * np.pi * ((k[None, :, None] * time_index) % n) / n
    stage = np.concatenate([np.cos(ang), -np.sin(ang)], axis=1)
    half = n1 // 2
    return dict(
        fwd1=as_bf(stage),
        inv1=as_bf(stage.transpose(0, 2, 1)[:, :half] / n),
        fwd2=as_bf(np.block([[c2, s2], [-s2, c2]])),
        inv2=as_bf(np.block([[c2, -s2], [s2, c2]])),
    )


def _flatten_block(ref, buf):
    groups, n, t2, lanes = ref.shape
    for g in range(groups):
        buf[g] = ref[g].reshape(n * t2, lanes)


def _unflatten_block(buf, ref):
    groups, n, t2, lanes = ref.shape
    for g in range(groups):
        ref[g] = buf[g].reshape(n, t2, lanes)


def _load_strided(buf, s, t2):
    n = buf.shape[1] // t2
    return jnp.concatenate([buf[g, pl.ds(s, n, stride=t2), :] for g in range(buf.shape[0])], axis=1)


def _store_strided(buf, s, t2, x):
    n = buf.shape[1] // t2
    for g in range(buf.shape[0]):
        buf[g, pl.ds(s, n, stride=t2), :] = x[:, g * LANES:(g + 1) * LANES]


def _flat_scratch(groups, n, t2, dtype=F32):
    return pltpu.VMEM((groups, n * t2, LANES), dtype)


U32 = jnp.uint32


def _pack_complex(re, im):
    hi = lax.bitcast_convert_type(re.astype(BF16).astype(F32), U32)
    lo = lax.bitcast_convert_type(im.astype(BF16).astype(F32), U32)
    return hi | (lo >> 16)


def _unpack_complex(w):
    re = lax.bitcast_convert_type(w & jnp.uint32(0xFFFF0000), F32)
    im = lax.bitcast_convert_type(w << 16, F32)
    return re, im


def _outer_fwd_body(x_ref, f_ref, *rest, scaled):
    if scaled:
        s_ref, a_ref, xb, ab = rest
        inv = 1.0 / (s_ref[...] + EPS)
    else:
        a_ref, xb, ab = rest
    rows, t2 = x_ref.shape[1:3]
    n1 = a_ref.shape[1]
    _flatten_block(x_ref, xb)
    for s in range(t2):
        x = _load_strided(xb, s, t2)
        if scaled:
            half = rows // 2
            x = jnp.concatenate([x[:half] * inv[0], x[half:] * inv[1]], axis=0)
        a = _dot(f_ref[s], x.astype(BF16))
        _store_strided(ab, s, t2, _pack_complex(a[:n1], a[n1:]))
    _unflatten_block(ab, a_ref)


def _outer_fwd(x4, fwd1, sums=None, *, t2=SUBLANES, gt=4):
    groups, rows, n2, _ = x4.shape
    n1 = fwd1.shape[2]
    scaled = sums is not None
    blk = lambda r: pl.BlockSpec((gt, r, t2, LANES), lambda j, c: (c, 0, j, 0))
    args = [x4, fwd1]
    in_specs = [blk(rows), pl.BlockSpec((t2, 2 * n1, rows), lambda j, c: (j, 0, 0))]
    if scaled:
        args.append(sums)
        in_specs.append(pl.BlockSpec((2, 1, gt * LANES), lambda j, c: (0, 0, c)))
    return pl.pallas_call(
        functools.partial(_outer_fwd_body, scaled=scaled),
        out_shape=jax.ShapeDtypeStruct((groups, n1, n2, LANES), U32),
        grid=(n2 // t2, groups // gt),
        in_specs=in_specs,
        out_specs=blk(n1),
        scratch_shapes=[_flat_scratch(gt, rows, t2), _flat_scratch(gt, n1, t2, U32)],
        compiler_params=_params(("parallel", "parallel")),
        name="dft_outer_fwd",
    )(*args)


def _load_slab(ref, s):
    return jnp.concatenate([ref[g, s] for g in range(ref.shape[0])], axis=1)


def _store_slab(ref, s, x):
    for g in range(ref.shape[0]):
        ref[g, s] = x[:, g * LANES:(g + 1) * LANES]


def _mid_conv_body(a_ref, k_ref, f2_ref, g2_ref, d_ref):
    n2 = a_ref.shape[2]
    width = a_ref.shape[0] * LANES
    for s in range(a_ref.shape[1]):
        re, im = _unpack_complex(jnp.concatenate([_load_slab(a_ref, s), _load_slab(k_ref, s)], axis=1))
        x = _dot(f2_ref[...], jnp.concatenate([re, im], axis=0).astype(BF16))
        xr, xi, kr, ki = x[:n2, :width], x[n2:, :width], x[:n2, width:], x[n2:, width:]
        yr = xr * kr - xi * ki
        yi = xr * ki + xi * kr
        c = _dot(g2_ref[...], jnp.concatenate([yr, yi], axis=0).astype(BF16))
        _store_slab(d_ref, s, _pack_complex(c[:n2], c[n2:]))


def _mid(a, kf, order, fwd2, inv2, *, kb=4):
    groups, n1, n2, _ = a.shape
    slab = pl.BlockSpec((groups, kb, n2, LANES), lambda i: (0, i, 0, 0))
    kslab = pl.BlockSpec((groups, kb, n2, LANES), lambda i: (order, i, 0, 0))
    mat = pl.BlockSpec((2 * n2, 2 * n2), lambda i: (0, 0))
    return pl.pallas_call(
        _mid_conv_body,
        out_shape=jax.ShapeDtypeStruct(a.shape, U32),
        grid=(n1 // kb,),
        in_specs=[slab, kslab, mat, mat],
        out_specs=slab,
        compiler_params=_params(("parallel",)),
        name="hyena_mid_conv",
    )(a, kf, fwd2, inv2)


def _outer_inv_body(d_ref, e_ref, gate_ref, z_ref, skip_ref, o_ref, db, gb, zb, ob):
    skip = skip_ref[...]
    t2 = d_ref.shape[2]
    for ref, buf in ((d_ref, db), (gate_ref, gb), (z_ref, zb)):
        _flatten_block(ref, buf)
    for s in range(t2):
        dr, di = _unpack_complex(_load_strided(db, s, t2))
        y = _dot(e_ref[s], jnp.concatenate([dr, di], axis=0).astype(BF16))
        _store_strided(ob, s, t2, _load_strided(gb, s, t2) * (y + skip * _load_strided(zb, s, t2)))
    _unflatten_block(ob, o_ref)


def _outer_inv(d, inv1, gate4, z4, skip, *, t2=SUBLANES):
    groups, n1, n2, _ = d.shape
    half = n1 // 2
    blk = lambda r: pl.BlockSpec((groups, r, t2, LANES), lambda j: (0, 0, j, 0))
    return pl.pallas_call(
        _outer_inv_body,
        out_shape=jax.ShapeDtypeStruct((groups, half, n2, LANES), F32),
        grid=(n2 // t2,),
        in_specs=[blk(n1), pl.BlockSpec((t2, half, 2 * n1), lambda j: (j, 0, 0)), blk(half), blk(half),
                  pl.BlockSpec((1, groups * LANES), lambda j: (0, 0))],
        out_specs=blk(half),
        scratch_shapes=[_flat_scratch(groups, n1, t2, U32)] + [_flat_scratch(groups, half, t2)] * 3,
        compiler_params=_params(("parallel",)),
        name="dft_outer_inv",
    )(d, inv1, gate4, z4, skip)


def _log_sigmoid(x):
    return jnp.minimum(x, 0.0) - jnp.log1p(jnp.exp(-jnp.abs(x)))


_NT = (((1,), (1,)), ((), ()))
_TN = (((0,), (0,)), ((), ()))


def _chunk_matrices(sub):
    i = np.arange(sub)
    same = (i[:, None] // GLA_CHUNK) == (i[None, :] // GLA_CHUNK)
    fwd = same & (i[None, :] <= i[:, None])
    bwd = same & (i[None, :] >= i[:, None])
    return jnp.asarray(np.stack([fwd, bwd, same]), F32)


def _gla_gates_body(q_ref, k_ref, v_ref, lr_ref, w_ref, b_ref, cm_ref,
                    qe_ref, ke_ref, kd_ref, dec_ref, vb_ref, *, scale):
    tg, width = q_ref.shape
    sub = cm_ref.shape[1]
    chunk = GLA_CHUNK
    vb_ref[...] = v_ref[...].astype(BF16)
    same = cm_ref[2].astype(BF16)
    for s in range(tg // sub):
        rs = slice(s * sub, (s + 1) * sub)
        g = _log_sigmoid(_dot(lr_ref[rs, :].astype(BF16), w_ref[...]) + b_ref[...]) / GLA_TAU
        g_hi = g.astype(BF16)
        g_lo = (g - g_hi.astype(F32)).astype(BF16)
        q = q_ref[rs, :] * scale
        k = k_ref[rs, :]
        for d in range(2):
            cs = slice(d * width, (d + 1) * width)
            cum = cm_ref[d].astype(BF16)
            bc = _dot(cum, g_hi[:, cs]) + _dot(cum, g_lo[:, cs])
            bl = _dot(same, g_hi[:, cs]) + _dot(same, g_lo[:, cs])
            qe_ref[d, rs, :] = (q * jnp.exp(bc)).astype(BF16)
            ke_ref[d, rs, :] = (k * jnp.exp(-bc)).astype(BF16)
            kd_ref[d, rs, :] = (k * jnp.exp(bl - bc)).astype(BF16)
            for c in range(sub // chunk):
                row = s * (sub // chunk) + c
                dec_ref[d, row:row + 1, :] = jnp.exp(bl[c * chunk:c * chunk + 1])


def _gla_gates(p, lr, w_both, b_both, *, q_off, k_off, v_off, dk, dv, heads, tg=512, sub=256):
    L = p.shape[0]
    tg = min(tg, L)
    wk, wv = heads * dk, heads * dv
    opnd = jax.ShapeDtypeStruct((2, L, wk), BF16)
    row = lambda width, off: pl.BlockSpec((tg, width), lambda i: (i, off // width))
    both = pl.BlockSpec((2, tg, wk), lambda i: (0, i, 0))
    const = lambda shape: pl.BlockSpec(shape, lambda i: (0,) * len(shape))
    return pl.pallas_call(
        functools.partial(_gla_gates_body, scale=dk ** -0.5),
        out_shape=(opnd, opnd, opnd, jax.ShapeDtypeStruct((2, L // GLA_CHUNK, wk), F32),
                   jax.ShapeDtypeStruct((L, wv), BF16)),
        grid=(L // tg,),
        in_specs=[row(wk, q_off), row(wk, k_off), row(wv, v_off), row(LANES, 0),
                  const((LANES, 2 * wk)), const((1, 2 * wk)), const((3, sub, sub))],
        out_specs=(both, both, both, pl.BlockSpec((2, tg // GLA_CHUNK, wk), lambda i: (0, i, 0)),
                   pl.BlockSpec((tg, wv), lambda i: (i, 0))),
        compiler_params=_params(("parallel",)),
        name="gla_gates",
    )(p, p, p, lr, w_both, b_both, _chunk_matrices(sub))


def _gla_direction(qe_ref, ke_ref, kd_ref, v_ref, dec_ref, mask, o_ref, st_ref, *, reverse):
    tb = qe_ref.shape[0]
    sub = mask.shape[0]
    chunk = GLA_CHUNK
    intras = []
    for s in range(tb // sub):
        rs = slice(s * sub, (s + 1) * sub)
        a = lax.dot_general(qe_ref[rs, :], ke_ref[rs, :], _NT, preferred_element_type=F32)
        intra = _dot(jnp.where(mask > 0.5, a, 0.0).astype(BF16), v_ref[rs, :])
        intras += [intra[c * chunk:(c + 1) * chunk] for c in range(sub // chunk)]
    st = st_ref[...]
    chunks = range(tb // chunk)
    for c in (reversed(chunks) if reverse else chunks):
        sl = slice(c * chunk, (c + 1) * chunk)
        o_ref[sl, :] = intras[c] + lax.dot_general(qe_ref[sl, :], st.astype(BF16), _NT,
                                                   preferred_element_type=F32)
        upd = lax.dot_general(v_ref[sl, :], kd_ref[sl, :], _TN, preferred_element_type=F32)
        st = st * dec_ref[c:c + 1, :] + upd
    st_ref[...] = st


def _gla_body(qef, kef, kdf, vf, decf, qeb, keb, kdb, vb, decb, cm_ref, of_ref, ob_ref, sf_ref, sb_ref):
    @pl.when(pl.program_id(1) == 0)
    def _():
        sf_ref[...] = jnp.zeros_like(sf_ref)
        sb_ref[...] = jnp.zeros_like(sb_ref)

    _gla_direction(qef, kef, kdf, vf, decf, cm_ref[0], of_ref, sf_ref, reverse=False)
    _gla_direction(qeb, keb, kdb, vb, decb, cm_ref[1], ob_ref, sb_ref, reverse=True)


def _gla(qe, ke, kd, dec, vb, *, dk, dv, tb=1024, sub=256):
    L = vb.shape[0]
    tb = min(tb, L)
    nb = L // tb
    heads = vb.shape[1] // dv
    nc = tb // GLA_CHUNK
    blk = lambda i, rev: (nb - 1 - i) if rev else i
    opnd = lambda rev: pl.BlockSpec((None, tb, dk), lambda h, i: (int(rev), blk(i, rev), h))
    vspec = lambda rev: pl.BlockSpec((tb, dv), lambda h, i: (blk(i, rev), h))
    dspec = lambda rev: pl.BlockSpec((None, nc, dk), lambda h, i: (int(rev), blk(i, rev), h))
    side = lambda rev: [opnd(rev), opnd(rev), opnd(rev), vspec(rev), dspec(rev)]
    return pl.pallas_call(
        _gla_body,
        out_shape=(jax.ShapeDtypeStruct((L, heads * dv), F32),) * 2,
        grid=(heads, nb),
        in_specs=side(False) + side(True) + [pl.BlockSpec((3, sub, sub), lambda h, i: (0, 0, 0))],
        out_specs=(vspec(False), vspec(True)),
        scratch_shapes=[pltpu.VMEM((dv, dk), F32), pltpu.VMEM((dv, dk), F32)],
        compiler_params=_params(("parallel", "arbitrary")),
        name="gla",
    )(qe, ke, kd, vb, dec, qe, ke, kd, vb, dec, _chunk_matrices(sub))


def _gla_post_body(of_ref, ob_ref, r_ref, gn_ref, c_ref, *, dv):
    o = of_ref[...] + ob_ref[...]
    r = r_ref[...]
    gate = r * jax.nn.sigmoid(r)
    for hd in range(o.shape[1] // dv):
        sl = slice(hd * dv, (hd + 1) * dv)
        c_ref[:, sl] = (_rms(o[:, sl], gn_ref[...]) * gate[:, sl]).astype(BF16)


def _gla_post(o_f, o_b, p, gla_norm, *, r_off, tm=512):
    L, wc = o_f.shape
    dv = gla_norm.shape[0]
    row = lambda blk=0: pl.BlockSpec((tm, wc), lambda i: (i, blk))
    return pl.pallas_call(
        functools.partial(_gla_post_body, dv=dv),
        out_shape=jax.ShapeDtypeStruct((L, wc), BF16),
        grid=(L // tm,),
        in_specs=[row(), row(), row(r_off // wc), pl.BlockSpec((1, dv), lambda i: (0, 0))],
        out_specs=row(),
        compiler_params=_params(("parallel",)),
        name="gla_post",
    )(o_f, o_b, p, gla_norm.reshape(1, dv))


def _merge_body(h_ref, g_ref, a_ref, z_ref, c_ref,
                wga_ref, wgb_ref, wgc_ref, bga_ref, bgb_ref, bgc_ref,
                wa_ref, wb_ref, wc_ref, m_ref, u_ref, zb_ref):
    @pl.when(pl.program_id(1) == 0)
    def _():
        u_ref[...] = _rms(h_ref[...], g_ref[...]).astype(BF16)
        zb_ref[...] = _load_groups(z_ref).astype(BF16)

    u = u_ref[...]
    m = jax.nn.sigmoid(_dot(u, wga_ref[...]) + bga_ref[...]) * _dot(a_ref[...], wa_ref[...])
    m += jax.nn.sigmoid(_dot(u, wgb_ref[...]) + bgb_ref[...]) * _dot(zb_ref[...], wb_ref[...])
    m += jax.nn.sigmoid(_dot(u, wgc_ref[...]) + bgc_ref[...]) * _dot(c_ref[...], wc_ref[...])
    m_ref[...] = m.astype(BF16)


def _merge(h, norm_g, a_pre, z, c_pre, w_gate, b_gate, w_a, w_b, w_c, *, tm=1024, tn=512):
    L, D = h.shape
    wa, wb, wc = w_a.shape[0], w_b.shape[0], w_c.shape[0]
    tm = min(tm, L)
    nj = D // tn
    row = lambda width: pl.BlockSpec((tm, width), lambda i, j: (i, 0))
    gate_w = lambda k: pl.BlockSpec((D, tn), lambda i, j: (0, j + k * nj))
    gate_b = lambda k: pl.BlockSpec((1, tn), lambda i, j: (0, j + k * nj))
    br_w = lambda width: pl.BlockSpec((width, tn), lambda i, j: (0, j))
    return pl.pallas_call(
        _merge_body,
        out_shape=jax.ShapeDtypeStruct((L, D), BF16),
        grid=(L // tm, nj),
        in_specs=[
            pl.BlockSpec((tm, D), lambda i, j: (i, 0), pipeline_mode=pl.Buffered(1)),
            pl.BlockSpec((1, D), lambda i, j: (0, 0)),
            row(wa), pl.BlockSpec((wb // LANES, tm, LANES), lambda i, j: (0, i, 0)), row(wc),
            gate_w(0), gate_w(1), gate_w(2), gate_b(0), gate_b(1), gate_b(2),
            br_w(wa), br_w(wb), br_w(wc),
        ],
        out_specs=pl.BlockSpec((tm, tn), lambda i, j: (i, j)),
        scratch_shapes=[pltpu.VMEM((tm, D), BF16), pltpu.VMEM((tm, wb), BF16)],
        compiler_params=_params(("parallel", "arbitrary")),
        name="merge",
    )(h, norm_g.reshape(1, D), a_pre, z, c_pre,
      w_gate, w_gate, w_gate, b_gate, b_gate, b_gate, w_a, w_b, w_c)


def _out_proj_body(h_ref, m_ref, w_ref, o_ref, *, tn):
    m = m_ref[...]
    for c in range(o_ref.shape[1] // tn):
        sl = slice(c * tn, (c + 1) * tn)
        o_ref[:, sl] = h_ref[:, sl] + _dot(m, w_ref[:, sl])


def _out_proj(h, m, w_o, *, tm=512, tn=512):
    L, D = h.shape
    row = pl.BlockSpec((tm, D), lambda i: (i, 0))
    return pl.pallas_call(
        functools.partial(_out_proj_body, tn=tn),
        out_shape=jax.ShapeDtypeStruct((L, D), F32),
        grid=(L // tm,),
        in_specs=[row, row, pl.BlockSpec((D, D), lambda i: (0, 0), pipeline_mode=pl.Buffered(1))],
        out_specs=row,
        compiler_params=_params(("parallel",)),
        name="out_proj",
    )(h, m, w_o)


def _hyena(z0, g1, g2, skip, filt, consts):
    groups, L, _ = z0.shape
    n1 = DFT_N1
    n2 = 2 * L // n1
    k_time, sums = filt
    kf = _outer_fwd(k_time.reshape(k_time.shape[0], n1, n2, LANES), consts["fwd1"], sums)
    z = z0.reshape(groups, n1 // 2, n2, LANES)
    for order, gate in enumerate((g1, g2)):
        a = _outer_fwd(z, consts["fwd1"])
        d = _mid(a, kf, order, consts["fwd2"], consts["inv2"])
        z = _outer_inv(d, consts["inv1"], gate.reshape(z.shape), z, skip[order].reshape(1, groups * LANES))
    return z.reshape(groups, L, LANES)


def kernel(x, ffn1_norm, ffn1_w_gu, ffn1_w_down, mix_norm, w_in, b_in, conv_a, conv_b, hf_w1, hf_b1, hf_w2, hf_b2, hf_w3, hf_b3, hf_w_out, hf_freq, hy_skip, gk_w2, gk_b, gla_norm, w_br_a, w_br_b, w_br_c, w_o, ffn2_norm, ffn2_w_gu, ffn2_w_down, final_norm):
    bsz, L, D = x.shape
    depth = ffn1_norm.shape[0]
    wa = conv_a.shape[2]
    wb = conv_b.shape[3]
    wck = gk_w2.shape[3]
    dv = gla_norm.shape[1]
    wcv = GLA_HEADS * dv
    dk = wck // GLA_HEADS
    hid = hf_w2.shape[1]
    assert wa == wb, "the projection kernel tiles both conv branches with one section width"
    n_main = 3 * wa + 3 * wb + 2 * wck + 2 * wcv
    q_off, k_off, v_off, r_off = 0, wck, 2 * wck, 2 * wck + wcv
    lr_off = n_main
    gate_off = n_main + 2 * GLA_RANK
    n1 = DFT_N1
    n2 = 2 * L // n1

    consts = _dft_consts(n1, n2)
    deltas =jnp.linspace(math.log(HY_TARGET) / HY_SLOW_DECAY, math.log(HY_TARGET) / HY_FAST_DECAY, wb, dtype=F32)
    deltas2 = jnp.tile(deltas.reshape(1, wb), (1, HY_ORDER))

    outs = []
    for bi in range(bsz):
        h = x[bi]
        for l in range(depth):
            h = _ffn(h, ffn1_norm[l], _to_bf16(ffn1_w_gu, l), _to_bf16(ffn1_w_down, l),
                     final_norm, final=False)

            w_in_bf = w_in[l].astype(BF16)
            w_lr = jnp.pad(w_in_bf[:, lr_off:gate_off], ((0, 0), (0, LANES - 2 * GLA_RANK)))
            b_lr = jnp.pad(b_in[l][lr_off:gate_off], (0, LANES - 2 * GLA_RANK)).reshape(1, LANES)
            p, lr, a_pre, z0, g1, g2 = _proj(
                h, mix_norm[l], w_in_bf, b_in[l].reshape(1, -1), w_lr, b_lr, conv_a[l], conv_b[l],
                n_main=n_main, wa=wa)

            w_out2 = hf_w_out[l].reshape(hid, HY_ORDER, 2, wb).transpose(2, 0, 1, 3).reshape(2, hid, HY_ORDER * wb)
            filt = _filters(L, hf_w1[l], hf_b1[l], hf_w2[l], hf_b2[l], hf_w3[l], hf_b3[l], hf_freq[l],
                            w_out2, deltas2)
            zb = _hyena(z0, g1, g2, hy_skip[l], filt, consts)

            w_both = jnp.zeros((LANES, 2 * wck), F32)
            w_both = w_both.at[:GLA_RANK, :wck].set(gk_w2[l, 0]).at[GLA_RANK:2 * GLA_RANK, wck:].set(gk_w2[l, 1])
            qe, ke, kd, dec, vb = _gla_gates(p, lr, w_both.astype(BF16), gk_b[l].reshape(1, 2 * wck),
                                             q_off=q_off, k_off=k_off, v_off=v_off, dk=dk, dv=dv,
                                             heads=GLA_HEADS)
            o_f, o_b = _gla(qe, ke, kd, dec, vb, dk=dk, dv=dv)

            c_pre = _gla_post(o_f, o_b, p, gla_norm[l], r_off=r_off)

            merged = _merge(h, mix_norm[l], a_pre, zb, c_pre,
                            w_in_bf[:, gate_off:], b_in[l][gate_off:].reshape(1, 3 * D),
                            _to_bf16(w_br_a, l), _to_bf16(w_br_b, l), _to_bf16(w_br_c, l))
            h = _out_proj(h, merged, _to_bf16(w_o, l))

            h = _ffn(h, ffn2_norm[l], _to_bf16(ffn2_w_gu, l), _to_bf16(ffn2_w_down, l),
                     final_norm, final=(l == depth - 1))
        outs.append(h)
    return jnp.stack(outs, axis=0)
```

```python
import functools
import math

import numpy as np
import jax
import jax.numpy as jnp
from jax import lax
from jax.experimental import pallas as pl
from jax.experimental.pallas import tpu as pltpu

F32 = jnp.float32
BF16 = jnp.bfloat16
HIGHEST = lax.Precision.HIGHEST

EPS = 1e-5
GLA_HEADS = 4
GLA_RANK = 16
GLA_TAU = 16.0
GLA_CHUNK = 64
HY_ORDER = 2
HY_BANDS = 16
HY_FAST_DECAY = 0.3
HY_SLOW_DECAY = 1.5
HY_TARGET = 1e-2

LANES = 128
SUBLANES = 8
DFT_N1 = 256
ROW_CHUNK = 128
VMEM_LIMIT = 56 * 1024 * 1024


def _params(sem):
    return pltpu.CompilerParams(dimension_semantics=sem, vmem_limit_bytes=VMEM_LIMIT)


def _rms(x, g):
    return x * lax.rsqrt(jnp.mean(x * x, axis=-1, keepdims=True) + EPS) * g


def _dot(a, b):
    return jnp.dot(a, b, preferred_element_type=F32)


def _dot_hi(a, b):
    return jnp.dot(a, b, preferred_element_type=F32, precision=HIGHEST)


def _dot3_tn(a, b):
    a_hi, b_hi = a.astype(BF16), b.astype(BF16)
    a_lo = (a - a_hi.astype(F32)).astype(BF16)
    b_lo = (b - b_hi.astype(F32)).astype(BF16)
    tn = lambda x, y: lax.dot_general(x, y, (((0,), (0,)), ((), ())), preferred_element_type=F32)
    return tn(a_hi, b_hi) + tn(a_lo, b_hi) + tn(a_hi, b_lo)


def _cast_body(x_ref, o_ref):
    o_ref[...] = x_ref[...].astype(o_ref.dtype)


def _to_bf16(w, layer, cols=None, *, block_bytes=4 << 20):
    _, rows, width = w.shape
    cols = width if cols is None else cols
    tr = rows
    while tr * cols * 4 > block_bytes and tr % 32 == 0:
        tr //= 2
    return pl.pallas_call(
        _cast_body,
        out_shape=jax.ShapeDtypeStruct((rows, cols), BF16),
        grid=(rows // tr,),
        in_specs=[pl.BlockSpec((None, tr, cols), lambda i: (layer, i, 0))],
        out_specs=pl.BlockSpec((tr, cols), lambda i: (i, 0)),
        compiler_params=_params(("parallel",)),
        name="to_bf16",
    )(w)


def _ffn_body(x_ref, g_ref, wg_ref, wu_ref, wd_ref, fn_ref, o_ref, xn_ref, *, final, sub):
    j = pl.program_id(1)

    row_chunks = [slice(r, r + ROW_CHUNK) for r in range(0, x_ref.shape[0], ROW_CHUNK)]

    @pl.when(j == 0)
    def _():
        for rs in row_chunks:
            xn_ref[rs, :] = _rms(x_ref[rs, :], g_ref[...]).astype(BF16)
        o_ref[...] = jnp.zeros_like(o_ref)

    xn = xn_ref[...]
    hs = []
    for c in range(wg_ref.shape[1] // sub):
        sl = slice(c * sub, (c + 1) * sub)
        g = _dot(xn, wg_ref[:, sl])
        u = _dot(xn, wu_ref[:, sl])
        hs.append((g * jax.nn.sigmoid(g) * u).astype(BF16))
    o_ref[...] += _dot(jnp.concatenate(hs, axis=1), wd_ref[...])

    @pl.when(j == pl.num_programs(1) - 1)
    def _():
        for rs in row_chunks:
            y = x_ref[rs, :] + 0.5 * o_ref[rs, :]
            if final:
                y = _rms(y, fn_ref[...])
            o_ref[rs, :] = y


def _ffn(x, norm_g, w_gu, w_down, final_g, *, final, tm=1024, tf=512, sub=256):
    L, D = x.shape
    F = w_down.shape[0]
    tm = min(tm, L)
    if final:
        tf = sub
    nf = F // tf
    return pl.pallas_call(
        functools.partial(_ffn_body, final=final, sub=min(sub, tf)),
        out_shape=jax.ShapeDtypeStruct((L, D), F32),
        grid=(L // tm, nf),
        in_specs=[
            pl.BlockSpec((tm, D), lambda i, j: (i, 0)),
            pl.BlockSpec((1, D), lambda i, j: (0, 0)),
            pl.BlockSpec((D, tf), lambda i, j: (0, j)),
            pl.BlockSpec((D, tf), lambda i, j: (0, j + nf)),
            pl.BlockSpec((tf, D), lambda i, j: (j, 0)),
            pl.BlockSpec((1, D), lambda i, j: (0, 0)),
        ],
        out_specs=pl.BlockSpec((tm, D), lambda i, j: (i, 0)),
        scratch_shapes=[pltpu.VMEM((tm, D), BF16)],
        compiler_params=_params(("parallel", "arbitrary")),
        name="ffn",
    )(x, norm_g.reshape(1, D), w_gu, w_gu, w_down, final_g.reshape(1, D))


def _conv3_inner(x, w):
    n = x.shape[0]
    y = pltpu.roll(x, 1, axis=0) * w[0:1] + x * w[1:2] + pltpu.roll(x, n - 1, axis=0) * w[2:3]
    return y[SUBLANES:n - SUBLANES]


def _proj_body(h_ref, hp_ref, hn_ref, g_ref, w_ref, b_ref, wlr_ref, blr_ref, caw_ref, cbw_ref,
               p_ref, lr_ref, a_ref, z_ref, g1_ref, g2_ref, *, tn, wa):
    i = pl.program_id(0)
    tm = h_ref.shape[0]
    g = g_ref[...]
    xn = _rms(h_ref[...], g)
    xn_bf = xn.astype(BF16)
    lr_ref[...] = _dot(xn_bf, wlr_ref[...]) + blr_ref[...]
    conv_w = 6 * wa
    for c in range(p_ref.shape[1] // tn):
        src = slice(conv_w + c * tn, conv_w + (c + 1) * tn)
        p_ref[:, c * tn:(c + 1) * tn] = _dot(xn_bf, w_ref[:, src]) + b_ref[:, src]

    xh = jnp.concatenate([_rms(hp_ref[...], g), xn, _rms(hn_ref[...], g)], axis=0).astype(BF16)
    row = lax.broadcasted_iota(jnp.int32, (tm + 2 * SUBLANES, 1), 0)
    has_prev = (i > 0).astype(F32)
    has_next = (i < pl.num_programs(0) - 1).astype(F32)
    valid = jnp.where(row < SUBLANES, has_prev, jnp.where(row >= tm + SUBLANES, has_next, 1.0))

    def sec(k):
        sl = slice(k * wa, (k + 1) * wa)
        return (_dot(xh, w_ref[:, sl]) + b_ref[:, sl]) * valid

    xa, ba, ca = sec(0), sec(1), sec(2)
    a_ref[...] = (ba[SUBLANES:tm + SUBLANES] * _conv3_inner(ca * xa, caw_ref[...])).astype(BF16)
    for k, o_ref in enumerate((z_ref, g1_ref, g2_ref)):
        _store_groups(o_ref, _conv3_inner(sec(3 + k), cbw_ref[k]))


def _proj(h, norm_g, w, b, w_lr, b_lr, conv_a, conv_b, *, wa, tm=256, tn=512):
    L, D = h.shape
    conv_w = 6 * wa
    n_scan = w.shape[1] - conv_w
    rb = tm // SUBLANES
    nrb = L // SUBLANES
    resident = lambda shape: pl.BlockSpec(shape, lambda i: (0,) * len(shape), pipeline_mode=pl.Buffered(1))
    grouped = pl.BlockSpec((wa // LANES, tm, LANES), lambda i: (0, i, 0))
    return pl.pallas_call(
        functools.partial(_proj_body, tn=tn, wa=wa),
        out_shape=(jax.ShapeDtypeStruct((L, n_scan), F32), jax.ShapeDtypeStruct((L, LANES), F32),
                   jax.ShapeDtypeStruct((L, wa), BF16))
        + (jax.ShapeDtypeStruct((wa // LANES, L, LANES), F32),) * 3,
        grid=(L // tm,),
        in_specs=[
            pl.BlockSpec((tm, D), lambda i: (i, 0)),
            pl.BlockSpec((SUBLANES, D), lambda i: (jnp.maximum(i * rb - 1, 0), 0)),
            pl.BlockSpec((SUBLANES, D), lambda i: (jnp.minimum((i + 1) * rb, nrb - 1), 0)),
            resident((1, D)), resident(w.shape), resident((1, w.shape[1])),
            resident((D, LANES)), resident((1, LANES)), resident((3, wa)), resident((3, 3, wa)),
        ],
        out_specs=(pl.BlockSpec((tm, n_scan), lambda i: (i, 0)),
                   pl.BlockSpec((tm, LANES), lambda i: (i, 0)),
                   pl.BlockSpec((tm, wa), lambda i: (i, 0)), grouped, grouped, grouped),
        compiler_params=_params(("parallel",)),
        name="proj",
    )(h, h, h, norm_g.reshape(1, D), w, b, w_lr, b_lr, conv_a, conv_b)


def _store_groups(o_ref, x):
    for g in range(o_ref.shape[0]):
        o_ref[g] = x[:, g * LANES:(g + 1) * LANES]


def _load_groups(ref):
    return jnp.concatenate([ref[g] for g in range(ref.shape[0])], axis=1)


def _filter_body(w1t_ref, w1cs_ref, b1_ref, w2_ref, b2_ref, w3_ref, b3_ref, fr_ref, bands_ref, wo_ref, dl_ref,
                 k_ref, s_ref, *, half_tiles, tr, seq):
    i = pl.program_id(0)

    def lag_of(n):
        return jnp.where(n < seq, n, jnp.where(n == seq, 0, 2 * seq - n)).astype(F32)

    lag_row = lag_of(i * tr + lax.broadcasted_iota(jnp.int32, (1, tr), 1))
    ang = ((2.0 * math.pi / seq) * lag_row) * bands_ref[...]
    feats = jnp.concatenate([jnp.cos(ang), -jnp.sin(ang)], axis=0)
    fr = fr_ref[...]
    pre = w1t_ref[...] * (lag_row / max(seq - 1, 1)) + _dot_hi(w1cs_ref[...], feats) + b1_ref[...]
    h = jnp.sin(fr * pre)
    h = jnp.sin(fr * (_dot_hi(w2_ref[...], h) + b2_ref[...]))
    h = jnp.sin(fr * (_dot_hi(w3_ref[...], h) + b3_ref[...]))
    t = lag_of(i * tr + lax.broadcasted_iota(jnp.int32, (tr, 1), 0)) / max(seq - 1, 1)
    decay = jnp.exp(-t * jnp.abs(dl_ref[...]))
    hd = _dot3_tn(h, wo_ref[0]) * decay

    @pl.when(i % half_tiles == 0)
    def _():
        s_ref[...] = jnp.zeros_like(s_ref)

    s_ref[0] += jnp.sum(jnp.abs(hd), axis=0, keepdims=True)
    row = lax.broadcasted_iota(jnp.int32, hd.shape, 0)
    zero_row = jnp.where(i == half_tiles, 0, -1)
    _store_groups(k_ref, jnp.where(row == zero_row, 0.0, hd))


def _filters(seq, w1, b1, w2, b2, w3, b3, freq, w_out2, deltas2, *, tr=512):
    n_rows = 2 * seq
    hid = w2.shape[0]
    width = w_out2.shape[2]
    half_tiles = n_rows // (2 * tr)
    bands = jnp.linspace(1e-4, HY_BANDS - 1, HY_BANDS, dtype=F32).reshape(HY_BANDS, 1)
    full = lambda shape: pl.BlockSpec(shape, lambda i: (0,) * len(shape))
    colv = lambda v: v.reshape(hid, 1)
    return pl.pallas_call(
        functools.partial(_filter_body, half_tiles=half_tiles, tr=tr, seq=seq),
        out_shape=(jax.ShapeDtypeStruct((width // LANES, n_rows, LANES), F32),
                   jax.ShapeDtypeStruct((2, 1, width), F32)),
        grid=(n_rows // tr,),
        in_specs=[
            full((hid, 1)), full((hid, 2 * HY_BANDS)), full((hid, 1)), full((hid, hid)), full((hid, 1)),
            full((hid, hid)), full((hid, 1)), full((hid, 1)), full((HY_BANDS, 1)),
            pl.BlockSpec((1, hid, width), lambda i: (i // half_tiles, 0, 0)),
            full((1, width)),
        ],
        out_specs=(pl.BlockSpec((width // LANES, tr, LANES), lambda i: (0, i, 0)),
                   pl.BlockSpec((1, 1, width), lambda i: (i // half_tiles, 0, 0))),
        compiler_params=_params(("arbitrary",)),
        name="hyena_filters",
    )(w1[0:1].T, w1[1:].T, colv(b1), w2.T, colv(b2), w3.T, colv(b3), colv(freq), bands, w_out2, deltas2)


def _dft_consts(n1, n2):
    n = n1 * n2
    k = np.arange(n1)
    m = np.arange(n2)
    ang2 = 2.0 * np.pi * ((m[:, None] * m[None, :]) % n2) / n2
    c2, s2 = np.cos(ang2), np.sin(ang2)
    as_bf = lambda a: jnp.asarray(np.asarray(a, np.float32).astype(jnp.bfloat16))
    time_index = k[None, None, :] * n2 + m[:, None, None]
    ang = 2.0 * np.pi * ((k[None, :, None] * time_index) % n) / n
    stage = np.concatenate([np.cos(ang), -np.sin(ang)], axis=1)
    half = n1 // 2
    return dict(
        fwd1=as_bf(stage),
        inv1=as_bf(stage.transpose(0, 2, 1)[:, :half] / n),
        fwd2=as_bf(np.block([[c2, s2], [-s2, c2]])),
        inv2=as_bf(np.block([[c2, -s2], [s2, c2]])),
    )


def _flatten_block(ref, buf):
    groups, n, t2, lanes = ref.shape
    for g in range(groups):
        buf[g] = ref[g].reshape(n * t2, lanes)


def _unflatten_block(buf, ref):
    groups, n, t2, lanes = ref.shape
    for g in range(groups):
        ref[g] = buf[g].reshape(n, t2, lanes)


def _load_strided(buf, s, t2):
    n = buf.shape[1] // t2
    return jnp.concatenate([buf[g, pl.ds(s, n, stride=t2), :] for g in range(buf.shape[0])], axis=1)


def _store_strided(buf, s, t2, x):
    n = buf.shape[1] // t2
    for g in range(buf.shape[0]):
        buf[g, pl.ds(s, n, stride=t2), :] = x[:, g * LANES:(g + 1) * LANES]


def _flat_scratch(groups, n, t2, dtype=F32):
    return pltpu.VMEM((groups, n * t2, LANES), dtype)


U32 = jnp.uint32


def _pack_complex(re, im):
    hi = lax.bitcast_convert_type(re.astype(BF16).astype(F32), U32)
    lo = lax.bitcast_convert_type(im.astype(BF16).astype(F32), U32)
    return hi | (lo >> 16)


def _unpack_complex(w):
    re = lax.bitcast_convert_type(w & jnp.uint32(0xFFFF0000), F32)
    im = lax.bitcast_convert_type(w << 16, F32)
    return re, im


def _outer_fwd_body(x_ref, f_ref, *rest, scaled):
    if scaled:
        s_ref, a_ref, xb, ab = rest
        inv = 1.0 / (s_ref[...] + EPS)
    else:
        a_ref, xb, ab = rest
    rows, t2 = x_ref.shape[1:3]
    n1 = a_ref.shape[1]
    _flatten_block(x_ref, xb)
    for s in range(t2):
        x = _load_strided(xb, s, t2)
        if scaled:
            half = rows // 2
            x = jnp.concatenate([x[:half] * inv[0], x[half:] * inv[1]], axis=0)
        a = _dot(f_ref[s], x.astype(BF16))
        _store_strided(ab, s, t2, _pack_complex(a[:n1], a[n1:]))
    _unflatten_block(ab, a_ref)


def _outer_fwd(x4, fwd1, sums=None, *, t2=SUBLANES, gt=4):
    groups, rows, n2, _ = x4.shape
    n1 = fwd1.shape[2]
    scaled = sums is not None
    blk = lambda r: pl.BlockSpec((gt, r, t2, LANES), lambda j, c: (c, 0, j, 0))
    args = [x4, fwd1]
    in_specs = [blk(rows), pl.BlockSpec((t2, 2 * n1, rows), lambda j, c: (j, 0, 0))]
    if scaled:
        args.append(sums)
        in_specs.append(pl.BlockSpec((2, 1, gt * LANES), lambda j, c: (0, 0, c)))
    return pl.pallas_call(
        functools.partial(_outer_fwd_body, scaled=scaled),
        out_shape=jax.ShapeDtypeStruct((groups, n1, n2, LANES), U32),
        grid=(n2 // t2, groups // gt),
        in_specs=in_specs,
        out_specs=blk(n1),
        scratch_shapes=[_flat_scratch(gt, rows, t2), _flat_scratch(gt, n1, t2, U32)],
        compiler_params=_params(("parallel", "parallel")),
        name="dft_outer_fwd",
    )(*args)


def _load_slab(ref, s):
    return jnp.concatenate([ref[g, s] for g in range(ref.shape[0])], axis=1)


def _store_slab(ref, s, x):
    for g in range(ref.shape[0]):
        ref[g, s] = x[:, g * LANES:(g + 1) * LANES]


def _mid_conv_body(a_ref, k_ref, f2_ref, g2_ref, d_ref):
    n2 = a_ref.shape[2]
    width = a_ref.shape[0] * LANES
    for s in range(a_ref.shape[1]):
        re, im = _unpack_complex(jnp.concatenate([_load_slab(a_ref, s), _load_slab(k_ref, s)], axis=1))
        x = _dot(f2_ref[...], jnp.concatenate([re, im], axis=0).astype(BF16))
        xr, xi, kr, ki = x[:n2, :width], x[n2:, :width], x[:n2, width:], x[n2:, width:]
        yr = xr * kr - xi * ki
        yi = xr * ki + xi * kr
        c = _dot(g2_ref[...], jnp.concatenate([yr, yi], axis=0).astype(BF16))
        _store_slab(d_ref, s, _pack_complex(c[:n2], c[n2:]))


def _mid(a, kf, order, fwd2, inv2, *, kb=4):
    groups, n1, n2, _ = a.shape
    slab = pl.BlockSpec((groups, kb, n2, LANES), lambda i: (0, i, 0, 0))
    kslab = pl.BlockSpec((groups, kb, n2, LANES), lambda i: (order, i, 0, 0))
    mat = pl.BlockSpec((2 * n2, 2 * n2), lambda i: (0, 0))
    return pl.pallas_call(
        _mid_conv_body,
        out_shape=jax.ShapeDtypeStruct(a.shape, U32),
        grid=(n1 // kb,),
        in_specs=[slab, kslab, mat, mat],
        out_specs=slab,
        compiler_params=_params(("parallel",)),
        name="hyena_mid_conv",
    )(a, kf, fwd2, inv2)


def _outer_inv_body(d_ref, e_ref, gate_ref, z_ref, skip_ref, o_ref, db, gb, zb, ob):
    skip = skip_ref[...]
    t2 = d_ref.shape[2]
    for ref, buf in ((d_ref, db), (gate_ref, gb), (z_ref, zb)):
        _flatten_block(ref, buf)
    for s in range(t2):
        dr, di = _unpack_complex(_load_strided(db, s, t2))
        y = _dot(e_ref[s], jnp.concatenate([dr, di], axis=0).astype(BF16))
        _store_strided(ob, s, t2, _load_strided(gb, s, t2) * (y + skip * _load_strided(zb, s, t2)))
    _unflatten_block(ob, o_ref)


def _outer_inv(d, inv1, gate4, z4, skip, *, t2=SUBLANES):
    groups, n1, n2, _ = d.shape
    half = n1 // 2
    blk = lambda r: pl.BlockSpec((groups, r, t2, LANES), lambda j: (0, 0, j, 0))
    return pl.pallas_call(
        _outer_inv_body,
        out_shape=jax.ShapeDtypeStruct((groups, half, n2, LANES), F32),
        grid=(n2 // t2,),
        in_specs=[blk(n1), pl.BlockSpec((t2, half, 2 * n1), lambda j: (j, 0, 0)), blk(half), blk(half),
                  pl.BlockSpec((1, groups * LANES), lambda j: (0, 0))],
        out_specs=blk(half),
        scratch_shapes=[_flat_scratch(groups, n1, t2, U32)] + [_flat_scratch(groups, half, t2)] * 3,
        compiler_params=_params(("parallel",)),
        name="dft_outer_inv",
    )(d, inv1, gate4, z4, skip)


def _log_sigmoid(x):
    return jnp.minimum(x, 0.0) - jnp.log1p(jnp.exp(-jnp.abs(x)))


_NT = (((1,), (1,)), ((), ()))
_TN = (((0,), (0,)), ((), ()))


def _chunk_matrices(sub):
    i = np.arange(sub)
    same = (i[:, None] // GLA_CHUNK) == (i[None, :] // GLA_CHUNK)
    fwd = same & (i[None, :] <= i[:, None])
    bwd = same & (i[None, :] >= i[:, None])
    return jnp.asarray(np.stack([fwd, bwd, same]), F32)


def _gla_gates_body(q_ref, k_ref, v_ref, lr_ref, w_ref, b_ref, cm_ref,
                    qe_ref, ke_ref, kd_ref, dec_ref, vb_ref, *, scale):
    tg, width = q_ref.shape
    sub = cm_ref.shape[1]
    chunk = GLA_CHUNK
    vb_ref[...] = v_ref[...].astype(BF16)
    same = cm_ref[2].astype(BF16)
    for s in range(tg // sub):
        rs = slice(s * sub, (s + 1) * sub)
        g = _log_sigmoid(_dot(lr_ref[rs, :].astype(BF16), w_ref[...]) + b_ref[...]) / GLA_TAU
        g_hi = g.astype(BF16)
        g_lo = (g - g_hi.astype(F32)).astype(BF16)
        q = q_ref[rs, :] * scale
        k = k_ref[rs, :]
        for d in range(2):
            cs = slice(d * width, (d + 1) * width)
            cum = cm_ref[d].astype(BF16)
            bc = _dot(cum, g_hi[:, cs]) + _dot(cum, g_lo[:, cs])
            bl = _dot(same, g_hi[:, cs]) + _dot(same, g_lo[:, cs])
            qe_ref[d, rs, :] = (q * jnp.exp(bc)).astype(BF16)
            ke_ref[d, rs, :] = (k * jnp.exp(-bc)).astype(BF16)
            kd_ref[d, rs, :] = (k * jnp.exp(bl - bc)).astype(BF16)
            for c in range(sub // chunk):
                row = s * (sub // chunk) + c
                dec_ref[d, row:row + 1, :] = jnp.exp(bl[c * chunk:c * chunk + 1])


def _gla_gates(p, lr, w_both, b_both, *, q_off, k_off, v_off, dk, dv, heads, tg=512, sub=256):
    L = p.shape[0]
    tg = min(tg, L)
    wk, wv = heads * dk, heads * dv
    opnd = jax.ShapeDtypeStruct((2, L, wk), BF16)
    row = lambda width, off: pl.BlockSpec((tg, width), lambda i: (i, off // width))
    both = pl.BlockSpec((2, tg, wk), lambda i: (0, i, 0))
    const = lambda shape: pl.BlockSpec(shape, lambda i: (0,) * len(shape))
    return pl.pallas_call(
        functools.partial(_gla_gates_body, scale=dk ** -0.5),
        out_shape=(opnd, opnd, opnd, jax.ShapeDtypeStruct((2, L // GLA_CHUNK, wk), F32),
                   jax.ShapeDtypeStruct((L, wv), BF16)),
        grid=(L // tg,),
        in_specs=[row(wk, q_off), row(wk, k_off), row(wv, v_off), row(LANES, 0),
                  const((LANES, 2 * wk)), const((1, 2 * wk)), const((3, sub, sub))],
        out_specs=(both, both, both, pl.BlockSpec((2, tg // GLA_CHUNK, wk), lambda i: (0, i, 0)),
                   pl.BlockSpec((tg, wv), lambda i: (i, 0))),
        compiler_params=_params(("parallel",)),
        name="gla_gates",
    )(p, p, p, lr, w_both, b_both, _chunk_matrices(sub))


def _gla_direction(qe_ref, ke_ref, kd_ref, v_ref, dec_ref, mask, o_ref, st_ref, *, reverse, dk, dv):
    tb = qe_ref.shape[0]
    sub = mask.shape[0]
    chunk = GLA_CHUNK
    for hd in range(st_ref.shape[0]):
        ks, vs = slice(hd * dk, (hd + 1) * dk), slice(hd * dv, (hd + 1) * dv)
        intras = []
        for s in range(tb // sub):
            rs = slice(s * sub, (s + 1) * sub)
            a = lax.dot_general(qe_ref[rs, ks], ke_ref[rs, ks], _NT, preferred_element_type=F32)
            intra = _dot(jnp.where(mask > 0.5, a, 0.0).astype(BF16), v_ref[rs, vs])
            intras += [intra[c * chunk:(c + 1) * chunk] for c in range(sub // chunk)]
        st = st_ref[hd]
        chunks = range(tb // chunk)
        for c in (reversed(chunks) if reverse else chunks):
            sl = slice(c * chunk, (c + 1) * chunk)
            o_ref[sl, vs] = intras[c] + lax.dot_general(qe_ref[sl, ks], st.astype(BF16), _NT,
                                                        preferred_element_type=F32)
            upd = lax.dot_general(v_ref[sl, vs], kd_ref[sl, ks], _TN, preferred_element_type=F32)
            st = st * dec_ref[c:c + 1, ks] + upd
        st_ref[hd] = st


def _gla_body(qef, kef, kdf, vf, decf, qeb, keb, kdb, vb, decb, cm_ref, of_ref, ob_ref, sf_ref, sb_ref,
              *, dk, dv):
    @pl.when(pl.program_id(1) == 0)
    def _():
        sf_ref[...] = jnp.zeros_like(sf_ref)
        sb_ref[...] = jnp.zeros_like(sb_ref)

    _gla_direction(qef, kef, kdf, vf, decf, cm_ref[0], of_ref, sf_ref, reverse=False, dk=dk, dv=dv)
    _gla_direction(qeb, keb, kdb, vb, decb, cm_ref[1], ob_ref, sb_ref, reverse=True, dk=dk, dv=dv)


def _gla(qe, ke, kd, dec, vb, *, dk, dv, tb=1024, sub=256, hp=2):
    L = vb.shape[0]
    tb = min(tb, L)
    nb = L // tb
    heads = vb.shape[1] // dv
    nc = tb // GLA_CHUNK
    blk = lambda i, rev: (nb - 1 - i) if rev else i
    opnd = lambda rev: pl.BlockSpec((None, tb, hp * dk), lambda h, i: (int(rev), blk(i, rev), h))
    vspec = lambda rev: pl.BlockSpec((tb, hp * dv), lambda h, i: (blk(i, rev), h))
    dspec = lambda rev: pl.BlockSpec((None, nc, hp * dk), lambda h, i: (int(rev), blk(i, rev), h))
    side = lambda rev: [opnd(rev), opnd(rev), opnd(rev), vspec(rev), dspec(rev)]
    return pl.pallas_call(
        functools.partial(_gla_body, dk=dk, dv=dv),
        out_shape=(jax.ShapeDtypeStruct((L, heads * dv), F32),) * 2,
        grid=(heads // hp, nb),
        in_specs=side(False) + side(True) + [pl.BlockSpec((3, sub, sub), lambda h, i: (0, 0, 0))],
        out_specs=(vspec(False), vspec(True)),
        scratch_shapes=[pltpu.VMEM((hp, dv, dk), F32), pltpu.VMEM((hp, dv, dk), F32)],
        compiler_params=_params(("parallel", "arbitrary")),
        name="gla",
    )(qe, ke, kd, vb, dec, qe, ke, kd, vb, dec, _chunk_matrices(sub))


def _gla_post_body(of_ref, ob_ref, r_ref, gn_ref, c_ref, *, dv):
    o = of_ref[...] + ob_ref[...]
    r = r_ref[...]
    gate = r * jax.nn.sigmoid(r)
    for hd in range(o.shape[1] // dv):
        sl = slice(hd * dv, (hd + 1) * dv)
        c_ref[:, sl] = (_rms(o[:, sl], gn_ref[...]) * gate[:, sl]).astype(BF16)


def _gla_post(o_f, o_b, p, gla_norm, *, r_off, tm=512):
    L, wc = o_f.shape
    dv = gla_norm.shape[0]
    row = lambda blk=0: pl.BlockSpec((tm, wc), lambda i: (i, blk))
    return pl.pallas_call(
        functools.partial(_gla_post_body, dv=dv),
        out_shape=jax.ShapeDtypeStruct((L, wc), BF16),
        grid=(L // tm,),
        in_specs=[row(), row(), row(r_off // wc), pl.BlockSpec((1, dv), lambda i: (0, 0))],
        out_specs=row(),
        compiler_params=_params(("parallel",)),
        name="gla_post",
    )(o_f, o_b, p, gla_norm.reshape(1, dv))


def _merge_body(h_ref, g_ref, a_ref, z_ref, c_ref,
                wga_ref, wgb_ref, wgc_ref, bga_ref, bgb_ref, bgc_ref,
                wa_ref, wb_ref, wc_ref, m_ref, u_ref, zb_ref):
    @pl.when(pl.program_id(1) == 0)
    def _():
        u_ref[...] = _rms(h_ref[...], g_ref[...]).astype(BF16)
        zb_ref[...] = _load_groups(z_ref).astype(BF16)

    u = u_ref[...]
    m = jax.nn.sigmoid(_dot(u, wga_ref[...]) + bga_ref[...]) * _dot(a_ref[...], wa_ref[...])
    m += jax.nn.sigmoid(_dot(u, wgb_ref[...]) + bgb_ref[...]) * _dot(zb_ref[...], wb_ref[...])
    m += jax.nn.sigmoid(_dot(u, wgc_ref[...]) + bgc_ref[...]) * _dot(c_ref[...], wc_ref[...])
    m_ref[...] = m.astype(BF16)


def _merge(h, norm_g, a_pre, z, c_pre, w_gate, b_gate, w_a, w_b, w_c, *, tm=1024, tn=512):
    L, D = h.shape
    wa, wb, wc = w_a.shape[0], w_b.shape[0], w_c.shape[0]
    tm = min(tm, L)
    nj = D // tn
    row = lambda width: pl.BlockSpec((tm, width), lambda i, j: (i, 0))
    gate_w = lambda k: pl.BlockSpec((D, tn), lambda i, j: (0, j + k * nj))
    gate_b = lambda k: pl.BlockSpec((1, tn), lambda i, j: (0, j + k * nj))
    br_w = lambda width: pl.BlockSpec((width, tn), lambda i, j: (0, j))
    return pl.pallas_call(
        _merge_body,
        out_shape=jax.ShapeDtypeStruct((L, D), BF16),
        grid=(L // tm, nj),
        in_specs=[
            pl.BlockSpec((tm, D), lambda i, j: (i, 0), pipeline_mode=pl.Buffered(1)),
            pl.BlockSpec((1, D), lambda i, j: (0, 0)),
            row(wa), pl.BlockSpec((wb // LANES, tm, LANES), lambda i, j: (0, i, 0)), row(wc),
            gate_w(0), gate_w(1), gate_w(2), gate_b(0), gate_b(1), gate_b(2),
            br_w(wa), br_w(wb), br_w(wc),
        ],
        out_specs=pl.BlockSpec((tm, tn), lambda i, j: (i, j)),
        scratch_shapes=[pltpu.VMEM((tm, D), BF16), pltpu.VMEM((tm, wb), BF16)],
        compiler_params=_params(("parallel", "arbitrary")),
        name="merge",
    )(h, norm_g.reshape(1, D), a_pre, z, c_pre,
      w_gate, w_gate, w_gate, b_gate, b_gate, b_gate, w_a, w_b, w_c)


def _out_proj_body(h_ref, m_ref, w_ref, o_ref, *, tn):
    m = m_ref[...]
    for c in range(o_ref.shape[1] // tn):
        sl = slice(c * tn, (c + 1) * tn)
        o_ref[:, sl] = h_ref[:, sl] + _dot(m, w_ref[:, sl])


def _out_proj(h, m, w_o, *, tm=512, tn=512):
    L, D = h.shape
    row = pl.BlockSpec((tm, D), lambda i: (i, 0))
    return pl.pallas_call(
        functools.partial(_out_proj_body, tn=tn),
        out_shape=jax.ShapeDtypeStruct((L, D), F32),
        grid=(L // tm,),
        in_specs=[row, row, pl.BlockSpec((D, D), lambda i: (0, 0), pipeline_mode=pl.Buffered(1))],
        out_specs=row,
        compiler_params=_params(("parallel",)),
        name="out_proj",
    )(h, m, w_o)


def _hyena(z0, g1, g2, skip, filt, consts):
    groups, L, _ = z0.shape
    n1 = DFT_N1
    n2 = 2 * L // n1
    k_time, sums = filt
    kf = _outer_fwd(k_time.reshape(k_time.shape[0], n1, n2, LANES), consts["fwd1"], sums)
    z = z0.reshape(groups, n1 // 2, n2, LANES)
    for order, gate in enumerate((g1, g2)):
        a = _outer_fwd(z, consts["fwd1"])
        d = _mid(a, kf, order, consts["fwd2"], consts["inv2"])
        z = _outer_inv(d, consts["inv1"], gate.reshape(z.shape), z, skip[order].reshape(1, groups * LANES))
    return z.reshape(groups, L, LANES)


def kernel(x, ffn1_norm, ffn1_w_gu, ffn1_w_down, mix_norm, w_in, b_in, conv_a, conv_b, hf_w1, hf_b1, hf_w2, hf_b2, hf_w3, hf_b3, hf_w_out, hf_freq, hy_skip, gk_w2, gk_b, gla_norm, w_br_a, w_br_b, w_br_c, w_o, ffn2_norm, ffn2_w_gu, ffn2_w_down, final_norm):
    bsz, L, D = x.shape
    depth = ffn1_norm.shape[0]
    wa = conv_a.shape[2]
    wb = conv_b.shape[3]
    wck = gk_w2.shape[3]
    dv = gla_norm.shape[1]
    wcv = GLA_HEADS * dv
    dk = wck // GLA_HEADS
    hid = hf_w2.shape[1]
    assert wa == wb, "the projection kernel tiles both conv branches with one section width"
    n_main = 3 * wa + 3 * wb + 2 * wck + 2 * wcv
    q_off, k_off, v_off, r_off = 0, wck, 2 * wck, 2 * wck + wcv
    lr_off = n_main
    gate_off = n_main + 2 * GLA_RANK
    n1 = DFT_N1
    n2 = 2 * L // n1

    consts = _dft_consts(n1, n2)
    deltas =jnp.linspace(math.log(HY_TARGET) / HY_SLOW_DECAY, math.log(HY_TARGET) / HY_FAST_DECAY, wb, dtype=F32)
    deltas2 = jnp.tile(deltas.reshape(1, wb), (1, HY_ORDER))

    outs = []
    for bi in range(bsz):
        h = x[bi]
        for l in range(depth):
            h = _ffn(h, ffn1_norm[l], _to_bf16(ffn1_w_gu, l), _to_bf16(ffn1_w_down, l),
                     final_norm, final=False)

            w_lr = jnp.pad(w_in[l][:, lr_off:gate_off], ((0, 0), (0, LANES - 2 * GLA_RANK))).astype(BF16)
            b_lr = jnp.pad(b_in[l][lr_off:gate_off], (0, LANES - 2 * GLA_RANK)).reshape(1, LANES)
            p, lr, a_pre, z0, g1, g2 = _proj(
                h, mix_norm[l], w_in[l][:, :n_main].astype(BF16), b_in[l][:n_main].reshape(1, n_main),
                w_lr, b_lr, conv_a[l], conv_b[l], wa=wa)

            w_out2 = hf_w_out[l].reshape(hid, HY_ORDER, 2, wb).transpose(2, 0, 1, 3).reshape(2, hid, HY_ORDER * wb)
            filt = _filters(L, hf_w1[l], hf_b1[l], hf_w2[l], hf_b2[l], hf_w3[l], hf_b3[l], hf_freq[l],
                            w_out2, deltas2)
            zb = _hyena(z0, g1, g2, hy_skip[l], filt, consts)

            w_both = jnp.zeros((LANES, 2 * wck), F32)
            w_both = w_both.at[:GLA_RANK, :wck].set(gk_w2[l, 0]).at[GLA_RANK:2 * GLA_RANK, wck:].set(gk_w2[l, 1])
            qe, ke, kd, dec, vb = _gla_gates(p, lr, w_both.astype(BF16), gk_b[l].reshape(1, 2 * wck),
                                             q_off=q_off, k_off=k_off, v_off=v_off, dk=dk, dv=dv,
                                             heads=GLA_HEADS)
            o_f, o_b = _gla(qe, ke, kd, dec, vb, dk=dk, dv=dv)

            c_pre = _gla_post(o_f, o_b, p, gla_norm[l], r_off=r_off)

            merged = _merge(h, mix_norm[l], a_pre, zb, c_pre,
                            w_in[l][:, gate_off:].astype(BF16), b_in[l][gate_off:].reshape(1, 3 * D),
                            _to_bf16(w_br_a, l), _to_bf16(w_br_b, l), _to_bf16(w_br_c, l))
            h = _out_proj(h, merged, _to_bf16(w_o, l))

            h = _ffn(h, ffn2_norm[l], _to_bf16(ffn2_w_gu, l), _to_bf16(ffn2_w_down, l),
                     final_norm, final=(l == depth - 1))
        outs.append(h)
    return jnp.stack(outs, axis=0)
```

```python
import functools
import math

import numpy as np
import jax
import jax.numpy as jnp
from jax import lax
from jax.experimental import pallas as pl
from jax.experimental.pallas import tpu as pltpu

F32 = jnp.float32
BF16 = jnp.bfloat16
HIGHEST = lax.Precision.HIGHEST

EPS = 1e-5
GLA_HEADS = 4
GLA_RANK = 16
GLA_TAU = 16.0
GLA_CHUNK = 64
HY_ORDER = 2
HY_BANDS = 16
HY_FAST_DECAY = 0.3
HY_SLOW_DECAY = 1.5
HY_TARGET = 1e-2

LANES = 128
SUBLANES = 8
DFT_N1 = 256
ROW_CHUNK = 128
VMEM_LIMIT = 56 * 1024 * 1024


def _params(sem):
    return pltpu.CompilerParams(dimension_semantics=sem, vmem_limit_bytes=VMEM_LIMIT)


def _rms(x, g):
    return x * lax.rsqrt(jnp.mean(x * x, axis=-1, keepdims=True) + EPS) * g


def _dot(a, b):
    return jnp.dot(a, b, preferred_element_type=F32)


def _dot_hi(a, b):
    return jnp.dot(a, b, preferred_element_type=F32, precision=HIGHEST)


def _dot3_tn(a, b):
    a_hi, b_hi = a.astype(BF16), b.astype(BF16)
    a_lo = (a - a_hi.astype(F32)).astype(BF16)
    b_lo = (b - b_hi.astype(F32)).astype(BF16)
    tn = lambda x, y: lax.dot_general(x, y, (((0,), (0,)), ((), ())), preferred_element_type=F32)
    return tn(a_hi, b_hi) + tn(a_lo, b_hi) + tn(a_hi, b_lo)


def _cast_body(x_ref, o_ref):
    o_ref[...] = x_ref[...].astype(o_ref.dtype)


def _to_bf16(w, layer, cols=None, *, block_bytes=4 << 20):
    _, rows, width = w.shape
    cols = width if cols is None else cols
    tr = rows
    while tr * cols * 4 > block_bytes and tr % 32 == 0:
        tr //= 2
    return pl.pallas_call(
        _cast_body,
        out_shape=jax.ShapeDtypeStruct((rows, cols), BF16),
        grid=(rows // tr,),
        in_specs=[pl.BlockSpec((None, tr, cols), lambda i: (layer, i, 0))],
        out_specs=pl.BlockSpec((tr, cols), lambda i: (i, 0)),
        compiler_params=_params(("parallel",)),
        name="to_bf16",
    )(w)


def _ffn_body(x_ref, g_ref, wg_ref, wu_ref, wd_ref, fn_ref, o_ref, xn_ref, *, final, sub):
    j = pl.program_id(1)

    row_chunks = [slice(r, r + ROW_CHUNK) for r in range(0, x_ref.shape[0], ROW_CHUNK)]

    @pl.when(j == 0)
    def _():
        for rs in row_chunks:
            xn_ref[rs, :] = _rms(x_ref[rs, :], g_ref[...]).astype(BF16)
        o_ref[...] = jnp.zeros_like(o_ref)

    xn = xn_ref[...]
    hs = []
    for c in range(wg_ref.shape[1] // sub):
        sl = slice(c * sub, (c + 1) * sub)
        g = _dot(xn, wg_ref[:, sl])
        u = _dot(xn, wu_ref[:, sl])
        hs.append((g * jax.nn.sigmoid(g) * u).astype(BF16))
    o_ref[...] += _dot(jnp.concatenate(hs, axis=1), wd_ref[...])

    @pl.when(j == pl.num_programs(1) - 1)
    def _():
        for rs in row_chunks:
            y = x_ref[rs, :] + 0.5 * o_ref[rs, :]
            if final:
                y = _rms(y, fn_ref[...])
            o_ref[rs, :] = y


def _ffn(x, norm_g, w_gu, w_down, final_g, *, final, tm=1024, tf=512, sub=256):
    L, D = x.shape
    F = w_down.shape[0]
    tm = min(tm, L)
    if final:
        tf = sub
    nf = F // tf
    return pl.pallas_call(
        functools.partial(_ffn_body, final=final, sub=min(sub, tf)),
        out_shape=jax.ShapeDtypeStruct((L, D), F32),
        grid=(L // tm, nf),
        in_specs=[
            pl.BlockSpec((tm, D), lambda i, j: (i, 0)),
            pl.BlockSpec((1, D), lambda i, j: (0, 0)),
            pl.BlockSpec((D, tf), lambda i, j: (0, j)),
            pl.BlockSpec((D, tf), lambda i, j: (0, j + nf)),
            pl.BlockSpec((tf, D), lambda i, j: (j, 0)),
            pl.BlockSpec((1, D), lambda i, j: (0, 0)),
        ],
        out_specs=pl.BlockSpec((tm, D), lambda i, j: (i, 0)),
        scratch_shapes=[pltpu.VMEM((tm, D), BF16)],
        compiler_params=_params(("parallel", "arbitrary")),
        name="ffn",
    )(x, norm_g.reshape(1, D), w_gu, w_gu, w_down, final_g.reshape(1, D))


def _conv3_inner(x, w):
    n = x.shape[0]
    y = pltpu.roll(x, 1, axis=0) * w[0:1] + x * w[1:2] + pltpu.roll(x, n - 1, axis=0) * w[2:3]
    return y[SUBLANES:n - SUBLANES]


def _proj_body(h_ref, hp_ref, hn_ref, g_ref, w_ref, b_ref, wlr_ref, blr_ref, caw_ref, cbw_ref,
               p_ref, lr_ref, a_ref, z_ref, g1_ref, g2_ref, *, tn, wa):
    i = pl.program_id(0)
    tm = h_ref.shape[0]
    g = g_ref[...]
    xn = _rms(h_ref[...], g)
    xn_bf = xn.astype(BF16)
    lr_ref[...] = _dot(xn_bf, wlr_ref[...]) + blr_ref[...]
    conv_w = 6 * wa
    for c in range(p_ref.shape[1] // tn):
        src = slice(conv_w + c * tn, conv_w + (c + 1) * tn)
        p_ref[:, c * tn:(c + 1) * tn] = _dot(xn_bf, w_ref[:, src]) + b_ref[:, src]

    xh = jnp.concatenate([_rms(hp_ref[...], g), xn, _rms(hn_ref[...], g)], axis=0).astype(BF16)
    row = lax.broadcasted_iota(jnp.int32, (tm + 2 * SUBLANES, 1), 0)
    has_prev = (i > 0).astype(F32)
    has_next = (i < pl.num_programs(0) - 1).astype(F32)
    valid = jnp.where(row < SUBLANES, has_prev, jnp.where(row >= tm + SUBLANES, has_next, 1.0))

    def sec(k):
        sl = slice(k * wa, (k + 1) * wa)
        return (_dot(xh, w_ref[:, sl]) + b_ref[:, sl]) * valid

    xa, ba, ca = sec(0), sec(1), sec(2)
    a_ref[...] = (ba[SUBLANES:tm + SUBLANES] * _conv3_inner(ca * xa, caw_ref[...])).astype(BF16)
    for k, o_ref in enumerate((z_ref, g1_ref, g2_ref)):
        _store_groups(o_ref, _conv3_inner(sec(3 + k), cbw_ref[k]))


def _proj(h, norm_g, w, b, w_lr, b_lr, conv_a, conv_b, *, wa, tm=256, tn=512):
    L, D = h.shape
    conv_w = 6 * wa
    n_scan = w.shape[1] - conv_w
    rb = tm // SUBLANES
    nrb = L // SUBLANES
    resident = lambda shape: pl.BlockSpec(shape, lambda i: (0,) * len(shape), pipeline_mode=pl.Buffered(1))
    grouped = pl.BlockSpec((wa // LANES, tm, LANES), lambda i: (0, i, 0))
    return pl.pallas_call(
        functools.partial(_proj_body, tn=tn, wa=wa),
        out_shape=(jax.ShapeDtypeStruct((L, n_scan), F32), jax.ShapeDtypeStruct((L, LANES), F32),
                   jax.ShapeDtypeStruct((L, wa), BF16))
        + (jax.ShapeDtypeStruct((wa // LANES, L, LANES), F32),) * 3,
        grid=(L // tm,),
        in_specs=[
            pl.BlockSpec((tm, D), lambda i: (i, 0)),
            pl.BlockSpec((SUBLANES, D), lambda i: (jnp.maximum(i * rb - 1, 0), 0)),
            pl.BlockSpec((SUBLANES, D), lambda i: (jnp.minimum((i + 1) * rb, nrb - 1), 0)),
            resident((1, D)), resident(w.shape), resident((1, w.shape[1])),
            resident((D, LANES)), resident((1, LANES)), resident((3, wa)), resident((3, 3, wa)),
        ],
        out_specs=(pl.BlockSpec((tm, n_scan), lambda i: (i, 0)),
                   pl.BlockSpec((tm, LANES), lambda i: (i, 0)),
                   pl.BlockSpec((tm, wa), lambda i: (i, 0)), grouped, grouped, grouped),
        compiler_params=_params(("parallel",)),
        name="proj",
    )(h, h, h, norm_g.reshape(1, D), w, b, w_lr, b_lr, conv_a, conv_b)


def _store_groups(o_ref, x):
    for g in range(o_ref.shape[0]):
        o_ref[g] = x[:, g * LANES:(g + 1) * LANES]


def _load_groups(ref):
    return jnp.concatenate([ref[g] for g in range(ref.shape[0])], axis=1)


def _filter_body(w1t_ref, w1cs_ref, b1_ref, w2_ref, b2_ref, w3_ref, b3_ref, fr_ref, bands_ref, wo_ref, dl_ref,
                 k_ref, s_ref, *, half_tiles, tr, seq):
    i = pl.program_id(0)

    def lag_of(n):
        return jnp.where(n < seq, n, jnp.where(n == seq, 0, 2 * seq - n)).astype(F32)

    lag_row = lag_of(i * tr + lax.broadcasted_iota(jnp.int32, (1, tr), 1))
    ang = ((2.0 * math.pi / seq) * lag_row) * bands_ref[...]
    feats = jnp.concatenate([jnp.cos(ang), -jnp.sin(ang)], axis=0)
    fr = fr_ref[...]
    pre = w1t_ref[...] * (lag_row / max(seq - 1, 1)) + _dot_hi(w1cs_ref[...], feats) + b1_ref[...]
    h = jnp.sin(fr * pre)
    h = jnp.sin(fr * (_dot_hi(w2_ref[...], h) + b2_ref[...]))
    h = jnp.sin(fr * (_dot_hi(w3_ref[...], h) + b3_ref[...]))
    t = lag_of(i * tr + lax.broadcasted_iota(jnp.int32, (tr, 1), 0)) / max(seq - 1, 1)
    decay = jnp.exp(-t * jnp.abs(dl_ref[...]))
    hd = _dot3_tn(h, wo_ref[0]) * decay

    @pl.when(i % half_tiles == 0)
    def _():
        s_ref[...] = jnp.zeros_like(s_ref)

    s_ref[0] += jnp.sum(jnp.abs(hd), axis=0, keepdims=True)
    row = lax.broadcasted_iota(jnp.int32, hd.shape, 0)
    zero_row = jnp.where(i == half_tiles, 0, -1)
    _store_groups(k_ref, jnp.where(row == zero_row, 0.0, hd))


def _filters(seq, w1, b1, w2, b2, w3, b3, freq, w_out2, deltas2, *, tr=1024):
    n_rows = 2 * seq
    hid = w2.shape[0]
    width = w_out2.shape[2]
    half_tiles = n_rows // (2 * tr)
    bands = jnp.linspace(1e-4, HY_BANDS - 1, HY_BANDS, dtype=F32).reshape(HY_BANDS, 1)
    full = lambda shape: pl.BlockSpec(shape, lambda i: (0,) * len(shape))
    colv = lambda v: v.reshape(hid, 1)
    return pl.pallas_call(
        functools.partial(_filter_body, half_tiles=half_tiles, tr=tr, seq=seq),
        out_shape=(jax.ShapeDtypeStruct((width // LANES, n_rows, LANES), F32),
                   jax.ShapeDtypeStruct((2, 1, width), F32)),
        grid=(n_rows // tr,),
        in_specs=[
            full((hid, 1)), full((hid, 2 * HY_BANDS)), full((hid, 1)), full((hid, hid)), full((hid, 1)),
            full((hid, hid)), full((hid, 1)), full((hid, 1)), full((HY_BANDS, 1)),
            pl.BlockSpec((1, hid, width), lambda i: (i // half_tiles, 0, 0)),
            full((1, width)),
        ],
        out_specs=(pl.BlockSpec((width // LANES, tr, LANES), lambda i: (0, i, 0)),
                   pl.BlockSpec((1, 1, width), lambda i: (i // half_tiles, 0, 0))),
        compiler_params=_params(("arbitrary",)),
        name="hyena_filters",
    )(w1[0:1].T, w1[1:].T, colv(b1), w2.T, colv(b2), w3.T, colv(b3), colv(freq), bands, w_out2, deltas2)


def _dft_consts(n1, n2):
    n = n1 * n2
    k = np.arange(n1)
    m = np.arange(n2)
    ang2 = 2.0 * np.pi * ((m[:, None] * m[None, :]) % n2) / n2
    c2, s2 = np.cos(ang2), np.sin(ang2)
    as_bf = lambda a: jnp.asarray(np.asarray(a, np.float32).astype(jnp.bfloat16))
    time_index = k[None, None, :] * n2 + m[:, None, None]
    ang = 2.0 * np.pi * ((k[None, :, None] * time_index) % n) / n
    stage = np.concatenate([np.cos(ang), -np.sin(ang)], axis=1)
    half = n1 // 2
    return dict(
        fwd1=as_bf(stage),
        inv1=as_bf(stage.transpose(0, 2, 1)[:, :half] / n),
        fwd2=as_bf(np.block([[c2, s2], [-s2, c2]])),
        inv2=as_bf(np.block([[c2, -s2], [s2, c2]])),
    )


def _flatten_block(ref, buf):
    groups, n, t2, lanes = ref.shape
    for g in range(groups):
        buf[g] = ref[g].reshape(n * t2, lanes)


def _unflatten_block(buf, ref):
    groups, n, t2, lanes = ref.shape
    for g in range(groups):
        ref[g] = buf[g].reshape(n, t2, lanes)


def _load_strided(buf, s, t2):
    n = buf.shape[1] // t2
    return jnp.concatenate([buf[g, pl.ds(s, n, stride=t2), :] for g in range(buf.shape[0])], axis=1)


def _store_strided(buf, s, t2, x):
    n = buf.shape[1] // t2
    for g in range(buf.shape[0]):
        buf[g, pl.ds(s, n, stride=t2), :] = x[:, g * LANES:(g + 1) * LANES]


def _flat_scratch(groups, n, t2, dtype=F32):
    return pltpu.VMEM((groups, n * t2, LANES), dtype)


U32 = jnp.uint32


def _pack_complex(re, im):
    hi = lax.bitcast_convert_type(re.astype(BF16).astype(F32), U32)
    lo = lax.bitcast_convert_type(im.astype(BF16).astype(F32), U32)
    return hi | (lo >> 16)


def _unpack_complex(w):
    re = lax.bitcast_convert_type(w & jnp.uint32(0xFFFF0000), F32)
    im = lax.bitcast_convert_type(w << 16, F32)
    return re, im


def _outer_fwd_body(x_ref, f_ref, *rest, scaled):
    if scaled:
        s_ref, a_ref, xb, ab = rest
        inv = 1.0 / (s_ref[...] + EPS)
    else:
        a_ref, xb, ab = rest
    rows, t2 = x_ref.shape[1:3]
    n1 = a_ref.shape[1]
    _flatten_block(x_ref, xb)
    for s in range(t2):
        x = _load_strided(xb, s, t2)
        if scaled:
            half = rows // 2
            x = jnp.concatenate([x[:half] * inv[0], x[half:] * inv[1]], axis=0)
        a = _dot(f_ref[s], x.astype(BF16))
        _store_strided(ab, s, t2, _pack_complex(a[:n1], a[n1:]))
    _unflatten_block(ab, a_ref)


def _outer_fwd(x4, fwd1, sums=None, *, t2=SUBLANES, gt=4):
    groups, rows, n2, _ = x4.shape
    n1 = fwd1.shape[2]
    scaled = sums is not None
    blk = lambda r: pl.BlockSpec((gt, r, t2, LANES), lambda j, c: (c, 0, j, 0))
    args = [x4, fwd1]
    in_specs = [blk(rows), pl.BlockSpec((t2, 2 * n1, rows), lambda j, c: (j, 0, 0))]
    if scaled:
        args.append(sums)
        in_specs.append(pl.BlockSpec((2, 1, gt * LANES), lambda j, c: (0, 0, c)))
    return pl.pallas_call(
        functools.partial(_outer_fwd_body, scaled=scaled),
        out_shape=jax.ShapeDtypeStruct((groups, n1, n2, LANES), U32),
        grid=(n2 // t2, groups // gt),
        in_specs=in_specs,
        out_specs=blk(n1),
        scratch_shapes=[_flat_scratch(gt, rows, t2), _flat_scratch(gt, n1, t2, U32)],
        compiler_params=_params(("parallel", "parallel")),
        name="dft_outer_fwd",
    )(*args)


def _load_slab(ref, s):
    return jnp.concatenate([ref[g, s] for g in range(ref.shape[0])], axis=1)


def _store_slab(ref, s, x):
    for g in range(ref.shape[0]):
        ref[g, s] = x[:, g * LANES:(g + 1) * LANES]


def _mid_conv_body(a_ref, k_ref, f2_ref, g2_ref, d_ref):
    n2 = a_ref.shape[2]
    width = a_ref.shape[0] * LANES
    for s in range(a_ref.shape[1]):
        re, im = _unpack_complex(jnp.concatenate([_load_slab(a_ref, s), _load_slab(k_ref, s)], axis=1))
        x = _dot(f2_ref[...], jnp.concatenate([re, im], axis=0).astype(BF16))
        xr, xi, kr, ki = x[:n2, :width], x[n2:, :width], x[:n2, width:], x[n2:, width:]
        yr = xr * kr - xi * ki
        yi = xr * ki + xi * kr
        c = _dot(g2_ref[...], jnp.concatenate([yr, yi], axis=0).astype(BF16))
        _store_slab(d_ref, s, _pack_complex(c[:n2], c[n2:]))


def _mid(a, kf, order, fwd2, inv2, *, kb=8):
    groups, n1, n2, _ = a.shape
    slab = pl.BlockSpec((groups, kb, n2, LANES), lambda i: (0, i, 0, 0))
    kslab = pl.BlockSpec((groups, kb, n2, LANES), lambda i: (order, i, 0, 0))
    mat = pl.BlockSpec((2 * n2, 2 * n2), lambda i: (0, 0))
    return pl.pallas_call(
        _mid_conv_body,
        out_shape=jax.ShapeDtypeStruct(a.shape, U32),
        grid=(n1 // kb,),
        in_specs=[slab, kslab, mat, mat],
        out_specs=slab,
        compiler_params=_params(("parallel",)),
        name="hyena_mid_conv",
    )(a, kf, fwd2, inv2)


def _outer_inv_body(d_ref, e_ref, gate_ref, z_ref, skip_ref, o_ref, db, gb, zb, ob):
    skip = skip_ref[...]
    t2 = d_ref.shape[2]
    for ref, buf in ((d_ref, db), (gate_ref, gb), (z_ref, zb)):
        _flatten_block(ref, buf)
    for s in range(t2):
        dr, di = _unpack_complex(_load_strided(db, s, t2))
        y = _dot(e_ref[s], jnp.concatenate([dr, di], axis=0).astype(BF16))
        _store_strided(ob, s, t2, _load_strided(gb, s, t2) * (y + skip * _load_strided(zb, s, t2)))
    _unflatten_block(ob, o_ref)


def _outer_inv(d, inv1, gate4, z4, skip, *, t2=SUBLANES):
    groups, n1, n2, _ = d.shape
    half = n1 // 2
    blk = lambda r: pl.BlockSpec((groups, r, t2, LANES), lambda j: (0, 0, j, 0))
    return pl.pallas_call(
        _outer_inv_body,
        out_shape=jax.ShapeDtypeStruct((groups, half, n2, LANES), F32),
        grid=(n2 // t2,),
        in_specs=[blk(n1), pl.BlockSpec((t2, half, 2 * n1), lambda j: (j, 0, 0)), blk(half), blk(half),
                  pl.BlockSpec((1, groups * LANES), lambda j: (0, 0))],
        out_specs=blk(half),
        scratch_shapes=[_flat_scratch(groups, n1, t2, U32)] + [_flat_scratch(groups, half, t2)] * 3,
        compiler_params=_params(("parallel",)),
        name="dft_outer_inv",
    )(d, inv1, gate4, z4, skip)


def _log_sigmoid(x):
    return jnp.minimum(x, 0.0) - jnp.log1p(jnp.exp(-jnp.abs(x)))


_NT = (((1,), (1,)), ((), ()))
_TN = (((0,), (0,)), ((), ()))


def _chunk_matrices(sub):
    i = np.arange(sub)
    same = (i[:, None] // GLA_CHUNK) == (i[None, :] // GLA_CHUNK)
    fwd = same & (i[None, :] <= i[:, None])
    bwd = same & (i[None, :] >= i[:, None])
    return jnp.asarray(np.stack([fwd, bwd, same]), F32)


def _gla_gates_body(q_ref, k_ref, v_ref, lr_ref, w_ref, b_ref, cm_ref,
                    qe_ref, ke_ref, kd_ref, dec_ref, vb_ref, *, scale):
    tg, width = q_ref.shape
    sub = cm_ref.shape[1]
    chunk = GLA_CHUNK
    vb_ref[...] = v_ref[...].astype(BF16)
    same = cm_ref[2].astype(BF16)
    for s in range(tg // sub):
        rs = slice(s * sub, (s + 1) * sub)
        g = _log_sigmoid(_dot(lr_ref[rs, :].astype(BF16), w_ref[...]) + b_ref[...]) / GLA_TAU
        g_hi = g.astype(BF16)
        g_lo = (g - g_hi.astype(F32)).astype(BF16)
        q = q_ref[rs, :] * scale
        k = k_ref[rs, :]
        for d in range(2):
            cs = slice(d * width, (d + 1) * width)
            cum = cm_ref[d].astype(BF16)
            bc = _dot(cum, g_hi[:, cs]) + _dot(cum, g_lo[:, cs])
            bl = _dot(same, g_hi[:, cs]) + _dot(same, g_lo[:, cs])
            qe_ref[d, rs, :] = (q * jnp.exp(bc)).astype(BF16)
            ke_ref[d, rs, :] = (k * jnp.exp(-bc)).astype(BF16)
            kd_ref[d, rs, :] = (k * jnp.exp(bl - bc)).astype(BF16)
            for c in range(sub // chunk):
                row = s * (sub // chunk) + c
                dec_ref[d, row:row + 1, :] = jnp.exp(bl[c * chunk:c * chunk + 1])


def _gla_gates(p, lr, w_both, b_both, *, q_off, k_off, v_off, dk, dv, heads, tg=1024, sub=256):
    L = p.shape[0]
    tg = min(tg, L)
    wk, wv = heads * dk, heads * dv
    opnd = jax.ShapeDtypeStruct((2, L, wk), BF16)
    row = lambda width, off: pl.BlockSpec((tg, width), lambda i: (i, off // width))
    both = pl.BlockSpec((2, tg, wk), lambda i: (0, i, 0))
    const = lambda shape: pl.BlockSpec(shape, lambda i: (0,) * len(shape))
    return pl.pallas_call(
        functools.partial(_gla_gates_body, scale=dk ** -0.5),
        out_shape=(opnd, opnd, opnd, jax.ShapeDtypeStruct((2, L // GLA_CHUNK, wk), F32),
                   jax.ShapeDtypeStruct((L, wv), BF16)),
        grid=(L // tg,),
        in_specs=[row(wk, q_off), row(wk, k_off), row(wv, v_off), row(LANES, 0),
                  const((LANES, 2 * wk)), const((1, 2 * wk)), const((3, sub, sub))],
        out_specs=(both, both, both, pl.BlockSpec((2, tg // GLA_CHUNK, wk), lambda i: (0, i, 0)),
                   pl.BlockSpec((tg, wv), lambda i: (i, 0))),
        compiler_params=_params(("parallel",)),
        name="gla_gates",
    )(p, p, p, lr, w_both, b_both, _chunk_matrices(sub))


def _gla_direction(qe_ref, ke_ref, kd_ref, v_ref, dec_ref, mask, o_ref, st_ref, *, reverse, dk, dv):
    tb = qe_ref.shape[0]
    sub = mask.shape[0]
    chunk = GLA_CHUNK
    for hd in range(st_ref.shape[0]):
        ks, vs = slice(hd * dk, (hd + 1) * dk), slice(hd * dv, (hd + 1) * dv)
        intras = []
        for s in range(tb // sub):
            rs = slice(s * sub, (s + 1) * sub)
            a = lax.dot_general(qe_ref[rs, ks], ke_ref[rs, ks], _NT, preferred_element_type=F32)
            intra = _dot(jnp.where(mask > 0.5, a, 0.0).astype(BF16), v_ref[rs, vs])
            intras += [intra[c * chunk:(c + 1) * chunk] for c in range(sub // chunk)]
        st = st_ref[hd]
        chunks = range(tb // chunk)
        for c in (reversed(chunks) if reverse else chunks):
            sl = slice(c * chunk, (c + 1) * chunk)
            o_ref[sl, vs] = intras[c] + lax.dot_general(qe_ref[sl, ks], st.astype(BF16), _NT,
                                                        preferred_element_type=F32)
            upd = lax.dot_general(v_ref[sl, vs], kd_ref[sl, ks], _TN, preferred_element_type=F32)
            st = st * dec_ref[c:c + 1, ks] + upd
        st_ref[hd] = st


def _gla_body(qef, kef, kdf, vf, decf, qeb, keb, kdb, vb, decb, cm_ref, of_ref, ob_ref, sf_ref, sb_ref,
              *, dk, dv):
    @pl.when(pl.program_id(1) == 0)
    def _():
        sf_ref[...] = jnp.zeros_like(sf_ref)
        sb_ref[...] = jnp.zeros_like(sb_ref)

    _gla_direction(qef, kef, kdf, vf, decf, cm_ref[0], of_ref, sf_ref, reverse=False, dk=dk, dv=dv)
    _gla_direction(qeb, keb, kdb, vb, decb, cm_ref[1], ob_ref, sb_ref, reverse=True, dk=dk, dv=dv)


def _gla(qe, ke, kd, dec, vb, *, dk, dv, tb=1024, sub=256, hp=2):
    L = vb.shape[0]
    tb = min(tb, L)
    nb = L // tb
    heads = vb.shape[1] // dv
    nc = tb // GLA_CHUNK
    blk = lambda i, rev: (nb - 1 - i) if rev else i
    opnd = lambda rev: pl.BlockSpec((None, tb, hp * dk), lambda h, i: (int(rev), blk(i, rev), h))
    vspec = lambda rev: pl.BlockSpec((tb, hp * dv), lambda h, i: (blk(i, rev), h))
    dspec = lambda rev: pl.BlockSpec((None, nc, hp * dk), lambda h, i: (int(rev), blk(i, rev), h))
    side = lambda rev: [opnd(rev), opnd(rev), opnd(rev), vspec(rev), dspec(rev)]
    return pl.pallas_call(
        functools.partial(_gla_body, dk=dk, dv=dv),
        out_shape=(jax.ShapeDtypeStruct((L, heads * dv), F32),) * 2,
        grid=(heads // hp, nb),
        in_specs=side(False) + side(True) + [pl.BlockSpec((3, sub, sub), lambda h, i: (0, 0, 0))],
        out_specs=(vspec(False), vspec(True)),
        scratch_shapes=[pltpu.VMEM((hp, dv, dk), F32), pltpu.VMEM((hp, dv, dk), F32)],
        compiler_params=_params(("parallel", "arbitrary")),
        name="gla",
    )(qe, ke, kd, vb, dec, qe, ke, kd, vb, dec, _chunk_matrices(sub))


def _gla_post_body(of_ref, ob_ref, r_ref, gn_ref, c_ref, *, dv):
    o = of_ref[...] + ob_ref[...]
    r = r_ref[...]
    gate = r * jax.nn.sigmoid(r)
    for hd in range(o.shape[1] // dv):
        sl = slice(hd * dv, (hd + 1) * dv)
        c_ref[:, sl] = (_rms(o[:, sl], gn_ref[...]) * gate[:, sl]).astype(BF16)


def _gla_post(o_f, o_b, p, gla_norm, *, r_off, tm=512):
    L, wc = o_f.shape
    dv = gla_norm.shape[0]
    row = lambda blk=0: pl.BlockSpec((tm, wc), lambda i: (i, blk))
    return pl.pallas_call(
        functools.partial(_gla_post_body, dv=dv),
        out_shape=jax.ShapeDtypeStruct((L, wc), BF16),
        grid=(L // tm,),
        in_specs=[row(), row(), row(r_off // wc), pl.BlockSpec((1, dv), lambda i: (0, 0))],
        out_specs=row(),
        compiler_params=_params(("parallel",)),
        name="gla_post",
    )(o_f, o_b, p, gla_norm.reshape(1, dv))


def _merge_body(h_ref, g_ref, a_ref, z_ref, c_ref,
                wga_ref, wgb_ref, wgc_ref, bga_ref, bgb_ref, bgc_ref,
                wa_ref, wb_ref, wc_ref, m_ref, u_ref, zb_ref):
    @pl.when(pl.program_id(1) == 0)
    def _():
        u_ref[...] = _rms(h_ref[...], g_ref[...]).astype(BF16)
        zb_ref[...] = _load_groups(z_ref).astype(BF16)

    u = u_ref[...]
    m = jax.nn.sigmoid(_dot(u, wga_ref[...]) + bga_ref[...]) * _dot(a_ref[...], wa_ref[...])
    m += jax.nn.sigmoid(_dot(u, wgb_ref[...]) + bgb_ref[...]) * _dot(zb_ref[...], wb_ref[...])
    m += jax.nn.sigmoid(_dot(u, wgc_ref[...]) + bgc_ref[...]) * _dot(c_ref[...], wc_ref[...])
    m_ref[...] = m.astype(BF16)


def _merge(h, norm_g, a_pre, z, c_pre, w_gate, b_gate, w_a, w_b, w_c, *, tm=1024, tn=512):
    L, D = h.shape
    wa, wb, wc = w_a.shape[0], w_b.shape[0], w_c.shape[0]
    tm = min(tm, L)
    nj = D // tn
    row = lambda width: pl.BlockSpec((tm, width), lambda i, j: (i, 0))
    gate_w = lambda k: pl.BlockSpec((D, tn), lambda i, j: (0, j + k * nj))
    gate_b = lambda k: pl.BlockSpec((1, tn), lambda i, j: (0, j + k * nj))
    br_w = lambda width: pl.BlockSpec((width, tn), lambda i, j: (0, j))
    return pl.pallas_call(
        _merge_body,
        out_shape=jax.ShapeDtypeStruct((L, D), BF16),
        grid=(L // tm, nj),
        in_specs=[
            pl.BlockSpec((tm, D), lambda i, j: (i, 0), pipeline_mode=pl.Buffered(1)),
            pl.BlockSpec((1, D), lambda i, j: (0, 0)),
            row(wa), pl.BlockSpec((wb // LANES, tm, LANES), lambda i, j: (0, i, 0)), row(wc),
            gate_w(0), gate_w(1), gate_w(2), gate_b(0), gate_b(1), gate_b(2),
            br_w(wa), br_w(wb), br_w(wc),
        ],
        out_specs=pl.BlockSpec((tm, tn), lambda i, j: (i, j)),
        scratch_shapes=[pltpu.VMEM((tm, D), BF16), pltpu.VMEM((tm, wb), BF16)],
        compiler_params=_params(("parallel", "arbitrary")),
        name="merge",
    )(h, norm_g.reshape(1, D), a_pre, z, c_pre,
      w_gate, w_gate, w_gate, b_gate, b_gate, b_gate, w_a, w_b, w_c)


def _out_proj_body(h_ref, m_ref, w_ref, o_ref, *, tn):
    m = m_ref[...]
    for c in range(o_ref.shape[1] // tn):
        sl = slice(c * tn, (c + 1) * tn)
        o_ref[:, sl] = h_ref[:, sl] + _dot(m, w_ref[:, sl])


def _out_proj(h, m, w_o, *, tm=512, tn=512):
    L, D = h.shape
    row = pl.BlockSpec((tm, D), lambda i: (i, 0))
    return pl.pallas_call(
        functools.partial(_out_proj_body, tn=tn),
        out_shape=jax.ShapeDtypeStruct((L, D), F32),
        grid=(L // tm,),
        in_specs=[row, row, pl.BlockSpec((D, D), lambda i: (0, 0), pipeline_mode=pl.Buffered(1))],
        out_specs=row,
        compiler_params=_params(("parallel",)),
        name="out_proj",
    )(h, m, w_o)


def _hyena(z0, g1, g2, skip, filt, consts):
    groups, L, _ = z0.shape
    n1 = DFT_N1
    n2 = 2 * L // n1
    k_time, sums = filt
    kf = _outer_fwd(k_time.reshape(k_time.shape[0], n1, n2, LANES), consts["fwd1"], sums)
    z = z0.reshape(groups, n1 // 2, n2, LANES)
    for order, gate in enumerate((g1, g2)):
        a = _outer_fwd(z, consts["fwd1"])
        d = _mid(a, kf, order, consts["fwd2"], consts["inv2"])
        z = _outer_inv(d, consts["inv1"], gate.reshape(z.shape), z, skip[order].reshape(1, groups * LANES))
    return z.reshape(groups, L, LANES)


def kernel(x, ffn1_norm, ffn1_w_gu, ffn1_w_down, mix_norm, w_in, b_in, conv_a, conv_b, hf_w1, hf_b1, hf_w2, hf_b2, hf_w3, hf_b3, hf_w_out, hf_freq, hy_skip, gk_w2, gk_b, gla_norm, w_br_a, w_br_b, w_br_c, w_o, ffn2_norm, ffn2_w_gu, ffn2_w_down, final_norm):
    bsz, L, D = x.shape
    depth = ffn1_norm.shape[0]
    wa = conv_a.shape[2]
    wb = conv_b.shape[3]
    wck = gk_w2.shape[3]
    dv = gla_norm.shape[1]
    wcv = GLA_HEADS * dv
    dk = wck // GLA_HEADS
    hid = hf_w2.shape[1]
    assert wa == wb, "the projection kernel tiles both conv branches with one section width"
    n_main = 3 * wa + 3 * wb + 2 * wck + 2 * wcv
    q_off, k_off, v_off, r_off = 0, wck, 2 * wck, 2 * wck + wcv
    lr_off = n_main
    gate_off = n_main + 2 * GLA_RANK
    n1 = DFT_N1
    n2 = 2 * L // n1

    consts = _dft_consts(n1, n2)
    deltas =jnp.linspace(math.log(HY_TARGET) / HY_SLOW_DECAY, math.log(HY_TARGET) / HY_FAST_DECAY, wb, dtype=F32)
    deltas2 = jnp.tile(deltas.reshape(1, wb), (1, HY_ORDER))

    outs = []
    for bi in range(bsz):
        h = x[bi]
        for l in range(depth):
            h = _ffn(h, ffn1_norm[l], _to_bf16(ffn1_w_gu, l), _to_bf16(ffn1_w_down, l),
                     final_norm, final=False)

            w_lr = jnp.pad(w_in[l][:, lr_off:gate_off], ((0, 0), (0, LANES - 2 * GLA_RANK))).astype(BF16)
            b_lr = jnp.pad(b_in[l][lr_off:gate_off], (0, LANES - 2 * GLA_RANK)).reshape(1, LANES)
            p, lr, a_pre, z0, g1, g2 = _proj(
                h, mix_norm[l], w_in[l][:, :n_main].astype(BF16), b_in[l][:n_main].reshape(1, n_main),
                w_lr, b_lr, conv_a[l], conv_b[l], wa=wa)

            w_out2 = hf_w_out[l].reshape(hid, HY_ORDER, 2, wb).transpose(2, 0, 1, 3).reshape(2, hid, HY_ORDER * wb)
            filt = _filters(L, hf_w1[l], hf_b1[l], hf_w2[l], hf_b2[l], hf_w3[l], hf_b3[l], hf_freq[l],
                            w_out2, deltas2)
            zb = _hyena(z0, g1, g2, hy_skip[l], filt, consts)

            w_both = jnp.zeros((LANES, 2 * wck), F32)
            w_both = w_both.at[:GLA_RANK, :wck].set(gk_w2[l, 0]).at[GLA_RANK:2 * GLA_RANK, wck:].set(gk_w2[l, 1])
            qe, ke, kd, dec, vb = _gla_gates(p, lr, w_both.astype(BF16), gk_b[l].reshape(1, 2 * wck),
                                             q_off=q_off, k_off=k_off, v_off=v_off, dk=dk, dv=dv,
                                             heads=GLA_HEADS)
            o_f, o_b = _gla(qe, ke, kd, dec, vb, dk=dk, dv=dv)

            c_pre = _gla_post(o_f, o_b, p, gla_norm[l], r_off=r_off)

            merged = _merge(h, mix_norm[l], a_pre, zb, c_pre,
                            w_in[l][:, gate_off:].astype(BF16), b_in[l][gate_off:].reshape(1, 3 * D),
                            _to_bf16(w_br_a, l), _to_bf16(w_br_b, l), _to_bf16(w_br_c, l))
            h = _out_proj(h, merged, _to_bf16(w_o, l))

            h = _ffn(h, ffn2_norm[l], _to_bf16(ffn2_w_gu, l), _to_bf16(ffn2_w_down, l),
                     final_norm, final=(l == depth - 1))
        outs.append(h)
    return jnp.stack(outs, axis=0)
```

```python
import functools
import math

import numpy as np
import jax
import jax.numpy as jnp
from jax import lax
from jax.experimental import pallas as pl
from jax.experimental.pallas import tpu as pltpu

F32 = jnp.float32
BF16 = jnp.bfloat16
HIGHEST = lax.Precision.HIGHEST

EPS = 1e-5
GLA_HEADS = 4
GLA_RANK = 16
GLA_TAU = 16.0
GLA_CHUNK = 64
HY_ORDER = 2
HY_BANDS = 16
HY_FAST_DECAY = 0.3
HY_SLOW_DECAY = 1.5
HY_TARGET = 1e-2

LANES = 128
SUBLANES = 8
DFT_N1 = 256
ROW_CHUNK = 128
VMEM_LIMIT = 56 * 1024 * 1024


def _params(sem):
    return pltpu.CompilerParams(dimension_semantics=sem, vmem_limit_bytes=VMEM_LIMIT)


def _rms(x, g):
    return x * lax.rsqrt(jnp.mean(x * x, axis=-1, keepdims=True) + EPS) * g


def _dot(a, b):
    return jnp.dot(a, b, preferred_element_type=F32)


def _dot_hi(a, b):
    return jnp.dot(a, b, preferred_element_type=F32, precision=HIGHEST)


def _dot3_tn(a, b):
    a_hi, b_hi = a.astype(BF16), b.astype(BF16)
    a_lo = (a - a_hi.astype(F32)).astype(BF16)
    b_lo = (b - b_hi.astype(F32)).astype(BF16)
    tn = lambda x, y: lax.dot_general(x, y, (((0,), (0,)), ((), ())), preferred_element_type=F32)
    return tn(a_hi, b_hi) + tn(a_lo, b_hi) + tn(a_hi, b_lo)


def _cast_body(x_ref, o_ref):
    o_ref[...] = x_ref[...].astype(o_ref.dtype)


def _to_bf16(w, layer, cols=None, *, block_bytes=4 << 20):
    _, rows, width = w.shape
    cols = width if cols is None else cols
    tr = rows
    while tr * cols * 4 > block_bytes and tr % 32 == 0:
        tr //= 2
    return pl.pallas_call(
        _cast_body,
        out_shape=jax.ShapeDtypeStruct((rows, cols), BF16),
        grid=(rows // tr,),
        in_specs=[pl.BlockSpec((None, tr, cols), lambda i: (layer, i, 0))],
        out_specs=pl.BlockSpec((tr, cols), lambda i: (i, 0)),
        compiler_params=_params(("parallel",)),
        name="to_bf16",
    )(w)


def _ffn_body(x_ref, g_ref, wg_ref, wu_ref, wd_ref, fn_ref, o_ref, xn_ref, *, final, sub):
    j = pl.program_id(1)

    row_chunks = [slice(r, r + ROW_CHUNK) for r in range(0, x_ref.shape[0], ROW_CHUNK)]

    @pl.when(j == 0)
    def _():
        for rs in row_chunks:
            xn_ref[rs, :] = _rms(x_ref[rs, :], g_ref[...]).astype(BF16)
        o_ref[...] = jnp.zeros_like(o_ref)

    xn = xn_ref[...]
    hs = []
    for c in range(wg_ref.shape[1] // sub):
        sl = slice(c * sub, (c + 1) * sub)
        g = _dot(xn, wg_ref[:, sl])
        u = _dot(xn, wu_ref[:, sl])
        hs.append((g * jax.nn.sigmoid(g) * u).astype(BF16))
    o_ref[...] += _dot(jnp.concatenate(hs, axis=1), wd_ref[...])

    @pl.when(j == pl.num_programs(1) - 1)
    def _():
        for rs in row_chunks:
            y = x_ref[rs, :] + 0.5 * o_ref[rs, :]
            if final:
                y = _rms(y, fn_ref[...])
            o_ref[rs, :] = y


def _ffn(x, norm_g, w_gu, w_down, final_g, *, final, tm=1024, tf=512, sub=256):
    L, D = x.shape
    F = w_down.shape[0]
    tm = min(tm, L)
    if final:
        tf = sub
    nf = F // tf
    return pl.pallas_call(
        functools.partial(_ffn_body, final=final, sub=min(sub, tf)),
        out_shape=jax.ShapeDtypeStruct((L, D), F32),
        grid=(L // tm, nf),
        in_specs=[
            pl.BlockSpec((tm, D), lambda i, j: (i, 0)),
            pl.BlockSpec((1, D), lambda i, j: (0, 0)),
            pl.BlockSpec((D, tf), lambda i, j: (0, j)),
            pl.BlockSpec((D, tf), lambda i, j: (0, j + nf)),
            pl.BlockSpec((tf, D), lambda i, j: (j, 0)),
            pl.BlockSpec((1, D), lambda i, j: (0, 0)),
        ],
        out_specs=pl.BlockSpec((tm, D), lambda i, j: (i, 0)),
        scratch_shapes=[pltpu.VMEM((tm, D), BF16)],
        compiler_params=_params(("parallel", "arbitrary")),
        name="ffn",
    )(x, norm_g.reshape(1, D), w_gu, w_gu, w_down, final_g.reshape(1, D))


def _conv3_inner(x, w):
    n = x.shape[0]
    y = pltpu.roll(x, 1, axis=0) * w[0:1] + x * w[1:2] + pltpu.roll(x, n - 1, axis=0) * w[2:3]
    return y[SUBLANES:n - SUBLANES]


def _proj_body(h_ref, hp_ref, hn_ref, g_ref, w_ref, b_ref, wlr_ref, blr_ref, caw_ref, cbw_ref,
               p_ref, lr_ref, a_ref, z_ref, g1_ref, g2_ref, *, tn, wa):
    i = pl.program_id(0)
    tm = h_ref.shape[0]
    g = g_ref[...]
    xn = _rms(h_ref[...], g)
    xn_bf = xn.astype(BF16)
    lr_ref[...] = _dot(xn_bf, wlr_ref[...]) + blr_ref[...]
    conv_w = 6 * wa
    for c in range(p_ref.shape[1] // tn):
        src = slice(conv_w + c * tn, conv_w + (c + 1) * tn)
        p_ref[:, c * tn:(c + 1) * tn] = _dot(xn_bf, w_ref[:, src]) + b_ref[:, src]

    xh = jnp.concatenate([_rms(hp_ref[...], g), xn, _rms(hn_ref[...], g)], axis=0).astype(BF16)
    row = lax.broadcasted_iota(jnp.int32, (tm + 2 * SUBLANES, 1), 0)
    has_prev = (i > 0).astype(F32)
    has_next = (i < pl.num_programs(0) - 1).astype(F32)
    valid = jnp.where(row < SUBLANES, has_prev, jnp.where(row >= tm + SUBLANES, has_next, 1.0))

    def sec(k):
        sl = slice(k * wa, (k + 1) * wa)
        return (_dot(xh, w_ref[:, sl]) + b_ref[:, sl]) * valid

    xa, ba, ca = sec(0), sec(1), sec(2)
    a_ref[...] = (ba[SUBLANES:tm + SUBLANES] * _conv3_inner(ca * xa, caw_ref[...])).astype(BF16)
    for k, o_ref in enumerate((z_ref, g1_ref, g2_ref)):
        _store_groups(o_ref, _conv3_inner(sec(3 + k), cbw_ref[k]))


def _proj(h, norm_g, w, b, w_lr, b_lr, conv_a, conv_b, *, wa, tm=256, tn=512):
    L, D = h.shape
    conv_w = 6 * wa
    n_scan = w.shape[1] - conv_w
    rb = tm // SUBLANES
    nrb = L // SUBLANES
    resident = lambda shape: pl.BlockSpec(shape, lambda i: (0,) * len(shape), pipeline_mode=pl.Buffered(1))
    grouped = pl.BlockSpec((wa // LANES, tm, LANES), lambda i: (0, i, 0))
    return pl.pallas_call(
        functools.partial(_proj_body, tn=tn, wa=wa),
        out_shape=(jax.ShapeDtypeStruct((L, n_scan), F32), jax.ShapeDtypeStruct((L, LANES), F32),
                   jax.ShapeDtypeStruct((L, wa), BF16))
        + (jax.ShapeDtypeStruct((wa // LANES, L, LANES), F32),) * 3,
        grid=(L // tm,),
        in_specs=[
            pl.BlockSpec((tm, D), lambda i: (i, 0)),
            pl.BlockSpec((SUBLANES, D), lambda i: (jnp.maximum(i * rb - 1, 0), 0)),
            pl.BlockSpec((SUBLANES, D), lambda i: (jnp.minimum((i + 1) * rb, nrb - 1), 0)),
            resident((1, D)), resident(w.shape), resident((1, w.shape[1])),
            resident((D, LANES)), resident((1, LANES)), resident((3, wa)), resident((3, 3, wa)),
        ],
        out_specs=(pl.BlockSpec((tm, n_scan), lambda i: (i, 0)),
                   pl.BlockSpec((tm, LANES), lambda i: (i, 0)),
                   pl.BlockSpec((tm, wa), lambda i: (i, 0)), grouped, grouped, grouped),
        compiler_params=_params(("parallel",)),
        name="proj",
    )(h, h, h, norm_g.reshape(1, D), w, b, w_lr, b_lr, conv_a, conv_b)


def _store_groups(o_ref, x):
    for g in range(o_ref.shape[0]):
        o_ref[g] = x[:, g * LANES:(g + 1) * LANES]


def _load_groups(ref):
    return jnp.concatenate([ref[g] for g in range(ref.shape[0])], axis=1)


def _filter_body(w1t_ref, w1cs_ref, b1_ref, w2_ref, b2_ref, w3_ref, b3_ref, fr_ref, bands_ref, wo_ref, dl_ref,
                 k_ref, s_ref, *, half_tiles, tr, seq):
    i = pl.program_id(0)

    def lag_of(n):
        return jnp.where(n < seq, n, jnp.where(n == seq, 0, 2 * seq - n)).astype(F32)

    lag_row = lag_of(i * tr + lax.broadcasted_iota(jnp.int32, (1, tr), 1))
    ang = ((2.0 * math.pi / seq) * lag_row) * bands_ref[...]
    feats = jnp.concatenate([jnp.cos(ang), -jnp.sin(ang)], axis=0)
    fr = fr_ref[...]
    pre = w1t_ref[...] * (lag_row / max(seq - 1, 1)) + _dot_hi(w1cs_ref[...], feats) + b1_ref[...]
    h = jnp.sin(fr * pre)
    h = jnp.sin(fr * (_dot_hi(w2_ref[...], h) + b2_ref[...]))
    h = jnp.sin(fr * (_dot_hi(w3_ref[...], h) + b3_ref[...]))
    t = lag_of(i * tr + lax.broadcasted_iota(jnp.int32, (tr, 1), 0)) / max(seq - 1, 1)
    decay = jnp.exp(-t * jnp.abs(dl_ref[...]))
    hd = _dot3_tn(h, wo_ref[0]) * decay

    @pl.when(i % half_tiles == 0)
    def _():
        s_ref[...] = jnp.zeros_like(s_ref)

    s_ref[0] += jnp.sum(jnp.abs(hd), axis=0, keepdims=True)
    row = lax.broadcasted_iota(jnp.int32, hd.shape, 0)
    zero_row = jnp.where(i == half_tiles, 0, -1)
    _store_groups(k_ref, jnp.where(row == zero_row, 0.0, hd))


def _filters(seq, w1, b1, w2, b2, w3, b3, freq, w_out2, deltas2, *, tr=1024):
    n_rows = 2 * seq
    hid = w2.shape[0]
    width = w_out2.shape[2]
    half_tiles = n_rows // (2 * tr)
    bands = jnp.linspace(1e-4, HY_BANDS - 1, HY_BANDS, dtype=F32).reshape(HY_BANDS, 1)
    full = lambda shape: pl.BlockSpec(shape, lambda i: (0,) * len(shape))
    colv = lambda v: v.reshape(hid, 1)
    return pl.pallas_call(
        functools.partial(_filter_body, half_tiles=half_tiles, tr=tr, seq=seq),
        out_shape=(jax.ShapeDtypeStruct((width // LANES, n_rows, LANES), F32),
                   jax.ShapeDtypeStruct((2, 1, width), F32)),
        grid=(n_rows // tr,),
        in_specs=[
            full((hid, 1)), full((hid, 2 * HY_BANDS)), full((hid, 1)), full((hid, hid)), full((hid, 1)),
            full((hid, hid)), full((hid, 1)), full((hid, 1)), full((HY_BANDS, 1)),
            pl.BlockSpec((1, hid, width), lambda i: (i // half_tiles, 0, 0)),
            full((1, width)),
        ],
        out_specs=(pl.BlockSpec((width // LANES, tr, LANES), lambda i: (0, i, 0)),
                   pl.BlockSpec((1, 1, width), lambda i: (i // half_tiles, 0, 0))),
        compiler_params=_params(("arbitrary",)),
        name="hyena_filters",
    )(w1[0:1].T, w1[1:].T, colv(b1), w2.T, colv(b2), w3.T, colv(b3), colv(freq), bands, w_out2, deltas2)


def _dft_consts(n1, n2):
    n = n1 * n2
    k = np.arange(n1)
    m = np.arange(n2)
    ang2 = 2.0 * np.pi * ((m[:, None] * m[None, :]) % n2) / n2
    c2, s2 = np.cos(ang2), np.sin(ang2)
    as_bf = lambda a: jnp.asarray(np.asarray(a, np.float32).astype(jnp.bfloat16))
    time_index = k[None, None, :] * n2 + m[:, None, None]
    ang = 2.0 * np.pi * ((k[None, :, None] * time_index) % n) / n
    stage = np.concatenate([np.cos(ang), -np.sin(ang)], axis=1)
    half = n1 // 2
    return dict(
        fwd1=as_bf(stage),
        inv1=as_bf(stage.transpose(0, 2, 1)[:, :half] / n),
        fwd2=as_bf(np.block([[c2, s2], [-s2, c2]])),
        inv2=as_bf(np.block([[c2, -s2], [s2, c2]])),
    )


def _flatten_block(ref, buf):
    groups, n, t2, lanes = ref.shape
    for g in range(groups):
        for part in range(t2 // SUBLANES):
            rows = slice(part * SUBLANES, (part + 1) * SUBLANES)
            buf[g, part] = ref[g, :, rows, :].reshape(n * SUBLANES, lanes)


def _unflatten_block(buf, ref):
    groups, n, t2, lanes = ref.shape
    for g in range(groups):
        for part in range(t2 // SUBLANES):
            rows = slice(part * SUBLANES, (part + 1) * SUBLANES)
            ref[g, :, rows, :] = buf[g, part].reshape(n, SUBLANES, lanes)


def _load_strided(buf, s):
    part, sub = divmod(s, SUBLANES)
    n = buf.shape[2] // SUBLANES
    return jnp.concatenate([buf[g, part, pl.ds(sub, n, stride=SUBLANES), :] for g in range(buf.shape[0])],
                           axis=1)


def _store_strided(buf, s, x):
    part, sub = divmod(s, SUBLANES)
    n = buf.shape[2] // SUBLANES
    for g in range(buf.shape[0]):
        buf[g, part, pl.ds(sub, n, stride=SUBLANES), :] = x[:, g * LANES:(g + 1) * LANES]


def _flat_scratch(groups, n, t2, dtype=F32):
    return pltpu.VMEM((groups, t2 // SUBLANES, n * SUBLANES, LANES), dtype)


U32 = jnp.uint32


def _pack_complex(re, im):
    hi = lax.bitcast_convert_type(re.astype(BF16).astype(F32), U32)
    lo = lax.bitcast_convert_type(im.astype(BF16).astype(F32), U32)
    return hi | (lo >> 16)


def _unpack_complex(w):
    re = lax.bitcast_convert_type(w & jnp.uint32(0xFFFF0000), F32)
    im = lax.bitcast_convert_type(w << 16, F32)
    return re, im


def _outer_fwd_body(x_ref, f_ref, *rest, scaled):
    if scaled:
        s_ref, a_ref, xb, ab = rest
        inv = 1.0 / (s_ref[...] + EPS)
    else:
        a_ref, xb, ab = rest
    rows, t2 = x_ref.shape[1:3]
    n1 = a_ref.shape[1]
    _flatten_block(x_ref, xb)
    for s in range(t2):
        x = _load_strided(xb, s)
        if scaled:
            half = rows // 2
            x = jnp.concatenate([x[:half] * inv[0], x[half:] * inv[1]], axis=0)
        a = _dot(f_ref[s], x.astype(BF16))
        _store_strided(ab, s, _pack_complex(a[:n1], a[n1:]))
    _unflatten_block(ab, a_ref)


def _outer_fwd(x4, fwd1, sums=None, *, t2=SUBLANES, gt=4):
    groups, rows, n2, _ = x4.shape
    n1 = fwd1.shape[2]
    scaled = sums is not None
    blk = lambda r: pl.BlockSpec((gt, r, t2, LANES), lambda j, c: (c, 0, j, 0))
    args = [x4, fwd1]
    in_specs = [blk(rows), pl.BlockSpec((t2, 2 * n1, rows), lambda j, c: (j, 0, 0))]
    if scaled:
        args.append(sums)
        in_specs.append(pl.BlockSpec((2, 1, gt * LANES), lambda j, c: (0, 0, c)))
    return pl.pallas_call(
        functools.partial(_outer_fwd_body, scaled=scaled),
        out_shape=jax.ShapeDtypeStruct((groups, n1, n2, LANES), U32),
        grid=(n2 // t2, groups // gt),
        in_specs=in_specs,
        out_specs=blk(n1),
        scratch_shapes=[_flat_scratch(gt, rows, t2), _flat_scratch(gt, n1, t2, U32)],
        compiler_params=_params(("parallel", "parallel")),
        name="dft_outer_fwd",
    )(*args)


def _load_slab(ref, s):
    return jnp.concatenate([ref[g, s] for g in range(ref.shape[0])], axis=1)


def _store_slab(ref, s, x):
    for g in range(ref.shape[0]):
        ref[g, s] = x[:, g * LANES:(g + 1) * LANES]


def _mid_conv_body(a_ref, k_ref, f2_ref, g2_ref, d_ref):
    n2 = a_ref.shape[2]
    width = a_ref.shape[0] * LANES
    for s in range(a_ref.shape[1]):
        re, im = _unpack_complex(jnp.concatenate([_load_slab(a_ref, s), _load_slab(k_ref, s)], axis=1))
        x = _dot(f2_ref[...], jnp.concatenate([re, im], axis=0).astype(BF16))
        xr, xi, kr, ki = x[:n2, :width], x[n2:, :width], x[:n2, width:], x[n2:, width:]
        yr = xr * kr - xi * ki
        yi = xr * ki + xi * kr
        c = _dot(g2_ref[...], jnp.concatenate([yr, yi], axis=0).astype(BF16))
        _store_slab(d_ref, s, _pack_complex(c[:n2], c[n2:]))


def _mid(a, kf, order, fwd2, inv2, *, kb=8):
    groups, n1, n2, _ = a.shape
    slab = pl.BlockSpec((groups, kb, n2, LANES), lambda i: (0, i, 0, 0))
    kslab = pl.BlockSpec((groups, kb, n2, LANES), lambda i: (order, i, 0, 0))
    mat = pl.BlockSpec((2 * n2, 2 * n2), lambda i: (0, 0))
    return pl.pallas_call(
        _mid_conv_body,
        out_shape=jax.ShapeDtypeStruct(a.shape, U32),
        grid=(n1 // kb,),
        in_specs=[slab, kslab, mat, mat],
        out_specs=slab,
        compiler_params=_params(("parallel",)),
        name="hyena_mid_conv",
    )(a, kf, fwd2, inv2)


def _outer_inv_body(d_ref, e_ref, gate_ref, z_ref, skip_ref, o_ref, db, gb, zb, ob):
    skip = skip_ref[...]
    t2 = d_ref.shape[2]
    for ref, buf in ((d_ref, db), (gate_ref, gb), (z_ref, zb)):
        _flatten_block(ref, buf)
    for s in range(t2):
        dr, di = _unpack_complex(_load_strided(db, s))
        y = _dot(e_ref[s], jnp.concatenate([dr, di], axis=0).astype(BF16))
        _store_strided(ob, s, _load_strided(gb, s) * (y + skip * _load_strided(zb, s)))
    _unflatten_block(ob, o_ref)


def _outer_inv(d, inv1, gate4, z4, skip, *, t2=SUBLANES):
    groups, n1, n2, _ = d.shape
    half = n1 // 2
    blk = lambda r: pl.BlockSpec((groups, r, t2, LANES), lambda j: (0, 0, j, 0))
    return pl.pallas_call(
        _outer_inv_body,
        out_shape=jax.ShapeDtypeStruct((groups, half, n2, LANES), F32),
        grid=(n2 // t2,),
        in_specs=[blk(n1), pl.BlockSpec((t2, half, 2 * n1), lambda j: (j, 0, 0)), blk(half), blk(half),
                  pl.BlockSpec((1, groups * LANES), lambda j: (0, 0))],
        out_specs=blk(half),
        scratch_shapes=[_flat_scratch(groups, n1, t2, U32)] + [_flat_scratch(groups, half, t2)] * 3,
        compiler_params=_params(("parallel",)),
        name="dft_outer_inv",
    )(d, inv1, gate4, z4, skip)


def _log_sigmoid(x):
    return jnp.minimum(x, 0.0) - jnp.log1p(jnp.exp(-jnp.abs(x)))


_NT = (((1,), (1,)), ((), ()))
_TN = (((0,), (0,)), ((), ()))


def _chunk_matrices(sub):
    i = np.arange(sub)
    same = (i[:, None] // GLA_CHUNK) == (i[None, :] // GLA_CHUNK)
    fwd = same & (i[None, :] <= i[:, None])
    bwd = same & (i[None, :] >= i[:, None])
    return jnp.asarray(np.stack([fwd, bwd, same]), F32)


def _gla_gates_body(q_ref, k_ref, v_ref, lr_ref, w_ref, b_ref, cm_ref,
                    qe_ref, ke_ref, kd_ref, dec_ref, vb_ref, *, scale):
    tg, width = q_ref.shape
    sub = cm_ref.shape[1]
    chunk = GLA_CHUNK
    vb_ref[...] = v_ref[...].astype(BF16)
    same = cm_ref[2].astype(BF16)
    for s in range(tg // sub):
        rs = slice(s * sub, (s + 1) * sub)
        g = _log_sigmoid(_dot(lr_ref[rs, :].astype(BF16), w_ref[...]) + b_ref[...]) / GLA_TAU
        g_hi = g.astype(BF16)
        g_lo = (g - g_hi.astype(F32)).astype(BF16)
        q = q_ref[rs, :] * scale
        k = k_ref[rs, :]
        for d in range(2):
            cs = slice(d * width, (d + 1) * width)
            cum = cm_ref[d].astype(BF16)
            bc = _dot(cum, g_hi[:, cs]) + _dot(cum, g_lo[:, cs])
            bl = _dot(same, g_hi[:, cs]) + _dot(same, g_lo[:, cs])
            qe_ref[d, rs, :] = (q * jnp.exp(bc)).astype(BF16)
            ke_ref[d, rs, :] = (k * jnp.exp(-bc)).astype(BF16)
            kd_ref[d, rs, :] = (k * jnp.exp(bl - bc)).astype(BF16)
            for c in range(sub // chunk):
                row = s * (sub // chunk) + c
                dec_ref[d, row:row + 1, :] = jnp.exp(bl[c * chunk:c * chunk + 1])


def _gla_gates(p, lr, w_both, b_both, *, q_off, k_off, v_off, dk, dv, heads, tg=1024, sub=256):
    L = p.shape[0]
    tg = min(tg, L)
    wk, wv = heads * dk, heads * dv
    opnd = jax.ShapeDtypeStruct((2, L, wk), BF16)
    row = lambda width, off: pl.BlockSpec((tg, width), lambda i: (i, off // width))
    both = pl.BlockSpec((2, tg, wk), lambda i: (0, i, 0))
    const = lambda shape: pl.BlockSpec(shape, lambda i: (0,) * len(shape))
    return pl.pallas_call(
        functools.partial(_gla_gates_body, scale=dk ** -0.5),
        out_shape=(opnd, opnd, opnd, jax.ShapeDtypeStruct((2, L // GLA_CHUNK, wk), F32),
                   jax.ShapeDtypeStruct((L, wv), BF16)),
        grid=(L // tg,),
        in_specs=[row(wk, q_off), row(wk, k_off), row(wv, v_off), row(LANES, 0),
                  const((LANES, 2 * wk)), const((1, 2 * wk)), const((3, sub, sub))],
        out_specs=(both, both, both, pl.BlockSpec((2, tg // GLA_CHUNK, wk), lambda i: (0, i, 0)),
                   pl.BlockSpec((tg, wv), lambda i: (i, 0))),
        compiler_params=_params(("parallel",)),
        name="gla_gates",
    )(p, p, p, lr, w_both, b_both, _chunk_matrices(sub))


def _gla_direction(qe_ref, ke_ref, kd_ref, v_ref, dec_ref, mask, o_ref, st_ref, *, reverse, dk, dv):
    tb = qe_ref.shape[0]
    sub = mask.shape[0]
    chunk = GLA_CHUNK
    for hd in range(st_ref.shape[0]):
        ks, vs = slice(hd * dk, (hd + 1) * dk), slice(hd * dv, (hd + 1) * dv)
        intras = []
        for s in range(tb // sub):
            rs = slice(s * sub, (s + 1) * sub)
            a = lax.dot_general(qe_ref[rs, ks], ke_ref[rs, ks], _NT, preferred_element_type=F32)
            intra = _dot(jnp.where(mask > 0.5, a, 0.0).astype(BF16), v_ref[rs, vs])
            intras += [intra[c * chunk:(c + 1) * chunk] for c in range(sub // chunk)]
        st = st_ref[hd]
        chunks = range(tb // chunk)
        for c in (reversed(chunks) if reverse else chunks):
            sl = slice(c * chunk, (c + 1) * chunk)
            o_ref[sl, vs] = intras[c] + lax.dot_general(qe_ref[sl, ks], st.astype(BF16), _NT,
                                                        preferred_element_type=F32)
            upd = lax.dot_general(v_ref[sl, vs], kd_ref[sl, ks], _TN, preferred_element_type=F32)
            st = st * dec_ref[c:c + 1, ks] + upd
        st_ref[hd] = st


def _gla_body(qef, kef, kdf, vf, decf, qeb, keb, kdb, vb, decb, cm_ref, of_ref, ob_ref, sf_ref, sb_ref,
              *, dk, dv):
    @pl.when(pl.program_id(1) == 0)
    def _():
        sf_ref[...] = jnp.zeros_like(sf_ref)
        sb_ref[...] = jnp.zeros_like(sb_ref)

    _gla_direction(qef, kef, kdf, vf, decf, cm_ref[0], of_ref, sf_ref, reverse=False, dk=dk, dv=dv)
    _gla_direction(qeb, keb, kdb, vb, decb, cm_ref[1], ob_ref, sb_ref, reverse=True, dk=dk, dv=dv)


def _gla(qe, ke, kd, dec, vb, *, dk, dv, tb=1024, sub=256, hp=2):
    L = vb.shape[0]
    tb = min(tb, L)
    nb = L // tb
    heads = vb.shape[1] // dv
    nc = tb // GLA_CHUNK
    blk = lambda i, rev: (nb - 1 - i) if rev else i
    opnd = lambda rev: pl.BlockSpec((None, tb, hp * dk), lambda h, i: (int(rev), blk(i, rev), h))
    vspec = lambda rev: pl.BlockSpec((tb, hp * dv), lambda h, i: (blk(i, rev), h))
    dspec = lambda rev: pl.BlockSpec((None, nc, hp * dk), lambda h, i: (int(rev), blk(i, rev), h))
    side = lambda rev: [opnd(rev), opnd(rev), opnd(rev), vspec(rev), dspec(rev)]
    return pl.pallas_call(
        functools.partial(_gla_body, dk=dk, dv=dv),
        out_shape=(jax.ShapeDtypeStruct((L, heads * dv), F32),) * 2,
        grid=(heads // hp, nb),
        in_specs=side(False) + side(True) + [pl.BlockSpec((3, sub, sub), lambda h, i: (0, 0, 0))],
        out_specs=(vspec(False), vspec(True)),
        scratch_shapes=[pltpu.VMEM((hp, dv, dk), F32), pltpu.VMEM((hp, dv, dk), F32)],
        compiler_params=_params(("parallel", "arbitrary")),
        name="gla",
    )(qe, ke, kd, vb, dec, qe, ke, kd, vb, dec, _chunk_matrices(sub))


def _gla_post_body(of_ref, ob_ref, r_ref, gn_ref, c_ref, *, dv):
    o = of_ref[...] + ob_ref[...]
    r = r_ref[...]
    gate = r * jax.nn.sigmoid(r)
    for hd in range(o.shape[1] // dv):
        sl = slice(hd * dv, (hd + 1) * dv)
        c_ref[:, sl] = (_rms(o[:, sl], gn_ref[...]) * gate[:, sl]).astype(BF16)


def _gla_post(o_f, o_b, p, gla_norm, *, r_off, tm=512):
    L, wc = o_f.shape
    dv = gla_norm.shape[0]
    row = lambda blk=0: pl.BlockSpec((tm, wc), lambda i: (i, blk))
    return pl.pallas_call(
        functools.partial(_gla_post_body, dv=dv),
        out_shape=jax.ShapeDtypeStruct((L, wc), BF16),
        grid=(L // tm,),
        in_specs=[row(), row(), row(r_off // wc), pl.BlockSpec((1, dv), lambda i: (0, 0))],
        out_specs=row(),
        compiler_params=_params(("parallel",)),
        name="gla_post",
    )(o_f, o_b, p, gla_norm.reshape(1, dv))


def _merge_body(h_ref, g_ref, a_ref, z_ref, c_ref,
                wga_ref, wgb_ref, wgc_ref, bga_ref, bgb_ref, bgc_ref,
                wa_ref, wb_ref, wc_ref, m_ref, u_ref, zb_ref):
    @pl.when(pl.program_id(1) == 0)
    def _():
        u_ref[...] = _rms(h_ref[...], g_ref[...]).astype(BF16)
        zb_ref[...] = _load_groups(z_ref).astype(BF16)

    u = u_ref[...]
    m = jax.nn.sigmoid(_dot(u, wga_ref[...]) + bga_ref[...]) * _dot(a_ref[...], wa_ref[...])
    m += jax.nn.sigmoid(_dot(u, wgb_ref[...]) + bgb_ref[...]) * _dot(zb_ref[...], wb_ref[...])
    m += jax.nn.sigmoid(_dot(u, wgc_ref[...]) + bgc_ref[...]) * _dot(c_ref[...], wc_ref[...])
    m_ref[...] = m.astype(BF16)


def _merge(h, norm_g, a_pre, z, c_pre, w_gate, b_gate, w_a, w_b, w_c, *, tm=1024, tn=256):
    L, D = h.shape
    wa, wb, wc = w_a.shape[0], w_b.shape[0], w_c.shape[0]
    tm = min(tm, L)
    nj = D // tn
    row = lambda width: pl.BlockSpec((tm, width), lambda i, j: (i, 0))
    gate_w = lambda k: pl.BlockSpec((D, tn), lambda i, j: (0, j + k * nj))
    gate_b = lambda k: pl.BlockSpec((1, tn), lambda i, j: (0, j + k * nj))
    br_w = lambda width: pl.BlockSpec((width, tn), lambda i, j: (0, j))
    return pl.pallas_call(
        _merge_body,
        out_shape=jax.ShapeDtypeStruct((L, D), BF16),
        grid=(L // tm, nj),
        in_specs=[
            row(D), pl.BlockSpec((1, D), lambda i, j: (0, 0)),
            row(wa), pl.BlockSpec((wb // LANES, tm, LANES), lambda i, j: (0, i, 0)), row(wc),
            gate_w(0), gate_w(1), gate_w(2), gate_b(0), gate_b(1), gate_b(2),
            br_w(wa), br_w(wb), br_w(wc),
        ],
        out_specs=pl.BlockSpec((tm, tn), lambda i, j: (i, j)),
        scratch_shapes=[pltpu.VMEM((tm, D), BF16), pltpu.VMEM((tm, wb), BF16)],
        compiler_params=_params(("parallel", "arbitrary")),
        name="merge",
    )(h, norm_g.reshape(1, D), a_pre, z, c_pre,
      w_gate, w_gate, w_gate, b_gate, b_gate, b_gate, w_a, w_b, w_c)


def _out_proj_body(h_ref, m_ref, w_ref, o_ref, *, tn):
    m = m_ref[...]
    for c in range(o_ref.shape[1] // tn):
        sl = slice(c * tn, (c + 1) * tn)
        o_ref[:, sl] = h_ref[:, sl] + _dot(m, w_ref[:, sl])


def _out_proj(h, m, w_o, *, tm=512, tn=512):
    L, D = h.shape
    row = pl.BlockSpec((tm, D), lambda i: (i, 0))
    return pl.pallas_call(
        functools.partial(_out_proj_body, tn=tn),
        out_shape=jax.ShapeDtypeStruct((L, D), F32),
        grid=(L // tm,),
        in_specs=[row, row, pl.BlockSpec((D, D), lambda i: (0, 0), pipeline_mode=pl.Buffered(1))],
        out_specs=row,
        compiler_params=_params(("parallel",)),
        name="out_proj",
    )(h, m, w_o)


def _hyena(z0, g1, g2, skip, filt, consts):
    groups, L, _ = z0.shape
    n1 = DFT_N1
    n2 = 2 * L // n1
    k_time, sums = filt
    kf = _outer_fwd(k_time.reshape(k_time.shape[0], n1, n2, LANES), consts["fwd1"], sums)
    z = z0.reshape(groups, n1 // 2, n2, LANES)
    for order, gate in enumerate((g1, g2)):
        a = _outer_fwd(z, consts["fwd1"], t2=min(2 * SUBLANES, n2))
        d = _mid(a, kf, order, consts["fwd2"], consts["inv2"])
        z = _outer_inv(d, consts["inv1"], gate.reshape(z.shape), z, skip[order].reshape(1, groups * LANES))
    return z.reshape(groups, L, LANES)


def kernel(x, ffn1_norm, ffn1_w_gu, ffn1_w_down, mix_norm, w_in, b_in, conv_a, conv_b, hf_w1, hf_b1, hf_w2, hf_b2, hf_w3, hf_b3, hf_w_out, hf_freq, hy_skip, gk_w2, gk_b, gla_norm, w_br_a, w_br_b, w_br_c, w_o, ffn2_norm, ffn2_w_gu, ffn2_w_down, final_norm):
    bsz, L, D = x.shape
    depth = ffn1_norm.shape[0]
    wa = conv_a.shape[2]
    wb = conv_b.shape[3]
    wck = gk_w2.shape[3]
    dv = gla_norm.shape[1]
    wcv = GLA_HEADS * dv
    dk = wck // GLA_HEADS
    hid = hf_w2.shape[1]
    assert wa == wb, "the projection kernel tiles both conv branches with one section width"
    n_main = 3 * wa + 3 * wb + 2 * wck + 2 * wcv
    q_off, k_off, v_off, r_off = 0, wck, 2 * wck, 2 * wck + wcv
    lr_off = n_main
    gate_off = n_main + 2 * GLA_RANK
    n1 = DFT_N1
    n2 = 2 * L // n1

    consts = _dft_consts(n1, n2)
    deltas =jnp.linspace(math.log(HY_TARGET) / HY_SLOW_DECAY, math.log(HY_TARGET) / HY_FAST_DECAY, wb, dtype=F32)
    deltas2 = jnp.tile(deltas.reshape(1, wb), (1, HY_ORDER))

    outs = []
    for bi in range(bsz):
        h = x[bi]
        for l in range(depth):
            h = _ffn(h, ffn1_norm[l], _to_bf16(ffn1_w_gu, l), _to_bf16(ffn1_w_down, l),
                     final_norm, final=False)

            w_lr = jnp.pad(w_in[l][:, lr_off:gate_off], ((0, 0), (0, LANES - 2 * GLA_RANK))).astype(BF16)
            b_lr = jnp.pad(b_in[l][lr_off:gate_off], (0, LANES - 2 * GLA_RANK)).reshape(1, LANES)
            p, lr, a_pre, z0, g1, g2 = _proj(
                h, mix_norm[l], w_in[l][:, :n_main].astype(BF16), b_in[l][:n_main].reshape(1, n_main),
                w_lr, b_lr, conv_a[l], conv_b[l], wa=wa)

            w_out2 = hf_w_out[l].reshape(hid, HY_ORDER, 2, wb).transpose(2, 0, 1, 3).reshape(2, hid, HY_ORDER * wb)
            filt = _filters(L, hf_w1[l], hf_b1[l], hf_w2[l], hf_b2[l], hf_w3[l], hf_b3[l], hf_freq[l],
                            w_out2, deltas2)
            zb = _hyena(z0, g1, g2, hy_skip[l], filt, consts)

            w_both = jnp.zeros((LANES, 2 * wck), F32)
            w_both = w_both.at[:GLA_RANK, :wck].set(gk_w2[l, 0]).at[GLA_RANK:2 * GLA_RANK, wck:].set(gk_w2[l, 1])
            qe, ke, kd, dec, vb = _gla_gates(p, lr, w_both.astype(BF16), gk_b[l].reshape(1, 2 * wck),
                                             q_off=q_off, k_off=k_off, v_off=v_off, dk=dk, dv=dv,
                                             heads=GLA_HEADS)
            o_f, o_b = _gla(qe, ke, kd, dec, vb, dk=dk, dv=dv)

            c_pre = _gla_post(o_f, o_b, p, gla_norm[l], r_off=r_off)

            merged = _merge(h, mix_norm[l], a_pre, zb, c_pre,
                            w_in[l][:, gate_off:].astype(BF16), b_in[l][gate_off:].reshape(1, 3 * D),
                            _to_bf16(w_br_a, l), _to_bf16(w_br_b, l), _to_bf16(w_br_c, l))
            h = _out_proj(h, merged, _to_bf16(w_o, l))

            h = _ffn(h, ffn2_norm[l], _to_bf16(ffn2_w_gu, l), _to_bf16(ffn2_w_down, l),
                     final_norm, final=(l == depth - 1))
        outs.append(h)
    return jnp.stack(outs, axis=0)
```

```python
import functools
import math

import numpy as np
import jax
import jax.numpy as jnp
from jax import lax
from jax.experimental import pallas as pl
from jax.experimental.pallas import tpu as pltpu

F32 = jnp.float32
BF16 = jnp.bfloat16
HIGHEST = lax.Precision.HIGHEST

EPS = 1e-5
GLA_HEADS = 4
GLA_RANK = 16
GLA_TAU = 16.0
GLA_CHUNK = 64
HY_ORDER = 2
HY_BANDS = 16
HY_FAST_DECAY = 0.3
HY_SLOW_DECAY = 1.5
HY_TARGET = 1e-2

LANES = 128
SUBLANES = 8
DFT_N1 = 256
ROW_CHUNK = 128
VMEM_LIMIT = 56 * 1024 * 1024


def _params(sem):
    return pltpu.CompilerParams(dimension_semantics=sem, vmem_limit_bytes=VMEM_LIMIT)


def _rms(x, g):
    return x * lax.rsqrt(jnp.mean(x * x, axis=-1, keepdims=True) + EPS) * g


def _dot(a, b):
    return jnp.dot(a, b, preferred_element_type=F32)


def _dot_hi(a, b):
    return jnp.dot(a, b, preferred_element_type=F32, precision=HIGHEST)


def _dot3_tn(a, b):
    a_hi, b_hi = a.astype(BF16), b.astype(BF16)
    a_lo = (a - a_hi.astype(F32)).astype(BF16)
    b_lo = (b - b_hi.astype(F32)).astype(BF16)
    tn = lambda x, y: lax.dot_general(x, y, (((0,), (0,)), ((), ())), preferred_element_type=F32)
    return tn(a_hi, b_hi) + tn(a_lo, b_hi) + tn(a_hi, b_lo)


def _cast_body(x_ref, o_ref):
    o_ref[...] = x_ref[...].astype(o_ref.dtype)


def _to_bf16(w, layer, cols=None, *, block_bytes=4 << 20):
    _, rows, width = w.shape
    cols = width if cols is None else cols
    tr = rows
    while tr * cols * 4 > block_bytes and tr % 32 == 0:
        tr //= 2
    return pl.pallas_call(
        _cast_body,
        out_shape=jax.ShapeDtypeStruct((rows, cols), BF16),
        grid=(rows // tr,),
        in_specs=[pl.BlockSpec((None, tr, cols), lambda i: (layer, i, 0))],
        out_specs=pl.BlockSpec((tr, cols), lambda i: (i, 0)),
        compiler_params=_params(("parallel",)),
        name="to_bf16",
    )(w)


def _ffn_body(x_ref, g_ref, wg_ref, wu_ref, wd_ref, fn_ref, o_ref, xn_ref, *, final, sub):
    j = pl.program_id(1)

    row_chunks = [slice(r, r + ROW_CHUNK) for r in range(0, x_ref.shape[0], ROW_CHUNK)]

    @pl.when(j == 0)
    def _():
        for rs in row_chunks:
            xn_ref[rs, :] = _rms(x_ref[rs, :], g_ref[...]).astype(BF16)
        o_ref[...] = jnp.zeros_like(o_ref)

    xn = xn_ref[...]
    hs = []
    for c in range(wg_ref.shape[1] // sub):
        sl = slice(c * sub, (c + 1) * sub)
        g = _dot(xn, wg_ref[:, sl])
        u = _dot(xn, wu_ref[:, sl])
        hs.append((g * jax.nn.sigmoid(g) * u).astype(BF16))
    o_ref[...] += _dot(jnp.concatenate(hs, axis=1), wd_ref[...])

    @pl.when(j == pl.num_programs(1) - 1)
    def _():
        for rs in row_chunks:
            y = x_ref[rs, :] + 0.5 * o_ref[rs, :]
            if final:
                y = _rms(y, fn_ref[...])
            o_ref[rs, :] = y


def _ffn(x, norm_g, w_gu, w_down, final_g, *, final, tm=1024, tf=512, sub=256):
    L, D = x.shape
    F = w_down.shape[0]
    tm = min(tm, L)
    if final:
        tf = sub
    nf = F // tf
    return pl.pallas_call(
        functools.partial(_ffn_body, final=final, sub=min(sub, tf)),
        out_shape=jax.ShapeDtypeStruct((L, D), F32),
        grid=(L // tm, nf),
        in_specs=[
            pl.BlockSpec((tm, D), lambda i, j: (i, 0)),
            pl.BlockSpec((1, D), lambda i, j: (0, 0)),
            pl.BlockSpec((D, tf), lambda i, j: (0, j)),
            pl.BlockSpec((D, tf), lambda i, j: (0, j + nf)),
            pl.BlockSpec((tf, D), lambda i, j: (j, 0)),
            pl.BlockSpec((1, D), lambda i, j: (0, 0)),
        ],
        out_specs=pl.BlockSpec((tm, D), lambda i, j: (i, 0)),
        scratch_shapes=[pltpu.VMEM((tm, D), BF16)],
        compiler_params=_params(("parallel", "arbitrary")),
        name="ffn",
    )(x, norm_g.reshape(1, D), w_gu, w_gu, w_down, final_g.reshape(1, D))


def _conv3_inner(x, w):
    n = x.shape[0]
    y = pltpu.roll(x, 1, axis=0) * w[0:1] + x * w[1:2] + pltpu.roll(x, n - 1, axis=0) * w[2:3]
    return y[SUBLANES:n - SUBLANES]


def _proj_body(h_ref, hp_ref, hn_ref, g_ref, w_ref, b_ref, wlr_ref, blr_ref, caw_ref, cbw_ref,
               p_ref, lr_ref, a_ref, z_ref, g1_ref, g2_ref, *, tn, wa):
    i = pl.program_id(0)
    tm = h_ref.shape[0]
    g = g_ref[...]
    xn = _rms(h_ref[...], g)
    xn_bf = xn.astype(BF16)
    lr_ref[...] = _dot(xn_bf, wlr_ref[...]) + blr_ref[...]
    conv_w = 6 * wa
    for c in range(p_ref.shape[1] // tn):
        src = slice(conv_w + c * tn, conv_w + (c + 1) * tn)
        p_ref[:, c * tn:(c + 1) * tn] = _dot(xn_bf, w_ref[:, src]) + b_ref[:, src]

    xh = jnp.concatenate([_rms(hp_ref[...], g), xn, _rms(hn_ref[...], g)], axis=0).astype(BF16)
    row = lax.broadcasted_iota(jnp.int32, (tm + 2 * SUBLANES, 1), 0)
    has_prev = (i > 0).astype(F32)
    has_next = (i < pl.num_programs(0) - 1).astype(F32)
    valid = jnp.where(row < SUBLANES, has_prev, jnp.where(row >= tm + SUBLANES, has_next, 1.0))

    def sec(k):
        sl = slice(k * wa, (k + 1) * wa)
        return (_dot(xh, w_ref[:, sl]) + b_ref[:, sl]) * valid

    xa, ba, ca = sec(0), sec(1), sec(2)
    a_ref[...] = (ba[SUBLANES:tm + SUBLANES] * _conv3_inner(ca * xa, caw_ref[...])).astype(BF16)
    for k, o_ref in enumerate((z_ref, g1_ref, g2_ref)):
        _store_groups(o_ref, _conv3_inner(sec(3 + k), cbw_ref[k]))


def _proj(h, norm_g, w, b, w_lr, b_lr, conv_a, conv_b, *, wa, tm=256, tn=512):
    L, D = h.shape
    conv_w = 6 * wa
    n_scan = w.shape[1] - conv_w
    rb = tm // SUBLANES
    nrb = L // SUBLANES
    resident = lambda shape: pl.BlockSpec(shape, lambda i: (0,) * len(shape), pipeline_mode=pl.Buffered(1))
    grouped = pl.BlockSpec((wa // LANES, tm, LANES), lambda i: (0, i, 0))
    return pl.pallas_call(
        functools.partial(_proj_body, tn=tn, wa=wa),
        out_shape=(jax.ShapeDtypeStruct((L, n_scan), F32), jax.ShapeDtypeStruct((L, LANES), F32),
                   jax.ShapeDtypeStruct((L, wa), BF16))
        + (jax.ShapeDtypeStruct((wa // LANES, L, LANES), F32),) * 3,
        grid=(L // tm,),
        in_specs=[
            pl.BlockSpec((tm, D), lambda i: (i, 0)),
            pl.BlockSpec((SUBLANES, D), lambda i: (jnp.maximum(i * rb - 1, 0), 0)),
            pl.BlockSpec((SUBLANES, D), lambda i: (jnp.minimum((i + 1) * rb, nrb - 1), 0)),
            resident((1, D)), resident(w.shape), resident((1, w.shape[1])),
            resident((D, LANES)), resident((1, LANES)), resident((3, wa)), resident((3, 3, wa)),
        ],
        out_specs=(pl.BlockSpec((tm, n_scan), lambda i: (i, 0)),
                   pl.BlockSpec((tm, LANES), lambda i: (i, 0)),
                   pl.BlockSpec((tm, wa), lambda i: (i, 0)), grouped, grouped, grouped),
        compiler_params=_params(("parallel",)),
        name="proj",
    )(h, h, h, norm_g.reshape(1, D), w, b, w_lr, b_lr, conv_a, conv_b)


def _store_groups(o_ref, x):
    for g in range(o_ref.shape[0]):
        o_ref[g] = x[:, g * LANES:(g + 1) * LANES]


def _load_groups(ref):
    return jnp.concatenate([ref[g] for g in range(ref.shape[0])], axis=1)


def _filter_body(w1t_ref, w1cs_ref, b1_ref, w2_ref, b2_ref, w3_ref, b3_ref, fr_ref, bands_ref, wo_ref, dl_ref,
                 k_ref, s_ref, *, half_tiles, tr, seq):
    i = pl.program_id(0)

    def lag_of(n):
        return jnp.where(n < seq, n, jnp.where(n == seq, 0, 2 * seq - n)).astype(F32)

    lag_row = lag_of(i * tr + lax.broadcasted_iota(jnp.int32, (1, tr), 1))
    ang = ((2.0 * math.pi / seq) * lag_row) * bands_ref[...]
    feats = jnp.concatenate([jnp.cos(ang), -jnp.sin(ang)], axis=0)
    fr = fr_ref[...]
    pre = w1t_ref[...] * (lag_row / max(seq - 1, 1)) + _dot_hi(w1cs_ref[...], feats) + b1_ref[...]
    h = jnp.sin(fr * pre)
    h = jnp.sin(fr * (_dot_hi(w2_ref[...], h) + b2_ref[...]))
    h = jnp.sin(fr * (_dot_hi(w3_ref[...], h) + b3_ref[...]))
    t = lag_of(i * tr + lax.broadcasted_iota(jnp.int32, (tr, 1), 0)) / max(seq - 1, 1)
    decay = jnp.exp(-t * jnp.abs(dl_ref[...]))
    hd = _dot3_tn(h, wo_ref[0]) * decay

    @pl.when(i % half_tiles == 0)
    def _():
        s_ref[...] = jnp.zeros_like(s_ref)

    s_ref[0] += jnp.sum(jnp.abs(hd), axis=0, keepdims=True)
    row = lax.broadcasted_iota(jnp.int32, hd.shape, 0)
    zero_row = jnp.where(i == half_tiles, 0, -1)
    _store_groups(k_ref, jnp.where(row == zero_row, 0.0, hd))


def _filters(seq, w1, b1, w2, b2, w3, b3, freq, w_out2, deltas2, *, tr=1024):
    n_rows = 2 * seq
    hid = w2.shape[0]
    width = w_out2.shape[2]
    half_tiles = n_rows // (2 * tr)
    bands = jnp.linspace(1e-4, HY_BANDS - 1, HY_BANDS, dtype=F32).reshape(HY_BANDS, 1)
    full = lambda shape: pl.BlockSpec(shape, lambda i: (0,) * len(shape))
    colv = lambda v: v.reshape(hid, 1)
    return pl.pallas_call(
        functools.partial(_filter_body, half_tiles=half_tiles, tr=tr, seq=seq),
        out_shape=(jax.ShapeDtypeStruct((width // LANES, n_rows, LANES), F32),
                   jax.ShapeDtypeStruct((2, 1, width), F32)),
        grid=(n_rows // tr,),
        in_specs=[
            full((hid, 1)), full((hid, 2 * HY_BANDS)), full((hid, 1)), full((hid, hid)), full((hid, 1)),
            full((hid, hid)), full((hid, 1)), full((hid, 1)), full((HY_BANDS, 1)),
            pl.BlockSpec((1, hid, width), lambda i: (i // half_tiles, 0, 0)),
            full((1, width)),
        ],
        out_specs=(pl.BlockSpec((width // LANES, tr, LANES), lambda i: (0, i, 0)),
                   pl.BlockSpec((1, 1, width), lambda i: (i // half_tiles, 0, 0))),
        compiler_params=_params(("arbitrary",)),
        name="hyena_filters",
    )(w1[0:1].T, w1[1:].T, colv(b1), w2.T, colv(b2), w3.T, colv(b3), colv(freq), bands, w_out2, deltas2)


def _dft_consts(n1, n2):
    n = n1 * n2
    k = np.arange(n1)
    m = np.arange(n2)
    ang2 = 2.0 * np.pi * ((m[:, None] * m[None, :]) % n2) / n2
    c2, s2 = np.cos(ang2), np.sin(ang2)
    as_bf = lambda a: jnp.asarray(np.asarray(a, np.float32).astype(jnp.bfloat16))
    time_index = k[None, None, :] * n2 + m[:, None, None]
    ang = 2.0 * np.pi * ((k[None, :, None] * time_index) % n) / n
    stage = np.concatenate([np.cos(ang), -np.sin(ang)], axis=1)
    half = n1 // 2
    return dict(
        fwd1=as_bf(stage),
        inv1=as_bf(stage.transpose(0, 2, 1)[:, :half] / n),
        fwd2=as_bf(np.block([[c2, s2], [-s2, c2]])),
        inv2=as_bf(np.block([[c2, -s2], [s2, c2]])),
    )


def _flatten_block(ref, buf):
    groups, n, t2, lanes = ref.shape
    for g in range(groups):
        for part in range(t2 // SUBLANES):
            rows = slice(part * SUBLANES, (part + 1) * SUBLANES)
            buf[g, part] = ref[g, :, rows, :].reshape(n * SUBLANES, lanes)


def _unflatten_block(buf, ref):
    groups, n, t2, lanes = ref.shape
    for g in range(groups):
        for part in range(t2 // SUBLANES):
            rows = slice(part * SUBLANES, (part + 1) * SUBLANES)
            ref[g, :, rows, :] = buf[g, part].reshape(n, SUBLANES, lanes)


def _load_strided(buf, s):
    part, sub = divmod(s, SUBLANES)
    n = buf.shape[2] // SUBLANES
    return jnp.concatenate([buf[g, part, pl.ds(sub, n, stride=SUBLANES), :] for g in range(buf.shape[0])],
                           axis=1)


def _store_strided(buf, s, x):
    part, sub = divmod(s, SUBLANES)
    n = buf.shape[2] // SUBLANES
    for g in range(buf.shape[0]):
        buf[g, part, pl.ds(sub, n, stride=SUBLANES), :] = x[:, g * LANES:(g + 1) * LANES]


def _flat_scratch(groups, n, t2, dtype=F32):
    return pltpu.VMEM((groups, t2 // SUBLANES, n * SUBLANES, LANES), dtype)


U32 = jnp.uint32


def _pack_complex(re, im):
    hi = lax.bitcast_convert_type(re.astype(BF16).astype(F32), U32)
    lo = lax.bitcast_convert_type(im.astype(BF16).astype(F32), U32)
    return hi | (lo >> 16)


def _unpack_complex(w):
    re = lax.bitcast_convert_type(w & jnp.uint32(0xFFFF0000), F32)
    im = lax.bitcast_convert_type(w << 16, F32)
    return re, im


def _outer_fwd_body(x_ref, f_ref, *rest, scaled):
    if scaled:
        s_ref, a_ref, xb, ab = rest
        inv = 1.0 / (s_ref[...] + EPS)
    else:
        a_ref, xb, ab = rest
    rows, t2 = x_ref.shape[1:3]
    n1 = a_ref.shape[1]
    _flatten_block(x_ref, xb)
    for s in range(t2):
        x = _load_strided(xb, s)
        if scaled:
            half = rows // 2
            x = jnp.concatenate([x[:half] * inv[0], x[half:] * inv[1]], axis=0)
        a = _dot(f_ref[s], x.astype(BF16))
        _store_strided(ab, s, _pack_complex(a[:n1], a[n1:]))
    _unflatten_block(ab, a_ref)


def _outer_fwd(x4, fwd1, sums=None, *, t2=SUBLANES, gt=4):
    groups, rows, n2, _ = x4.shape
    n1 = fwd1.shape[2]
    scaled = sums is not None
    blk = lambda r: pl.BlockSpec((gt, r, t2, LANES), lambda j, c: (c, 0, j, 0))
    args = [x4, fwd1]
    in_specs = [blk(rows), pl.BlockSpec((t2, 2 * n1, rows), lambda j, c: (j, 0, 0))]
    if scaled:
        args.append(sums)
        in_specs.append(pl.BlockSpec((2, 1, gt * LANES), lambda j, c: (0, 0, c)))
    return pl.pallas_call(
        functools.partial(_outer_fwd_body, scaled=scaled),
        out_shape=jax.ShapeDtypeStruct((groups, n1, n2, LANES), U32),
        grid=(n2 // t2, groups // gt),
        in_specs=in_specs,
        out_specs=blk(n1),
        scratch_shapes=[_flat_scratch(gt, rows, t2), _flat_scratch(gt, n1, t2, U32)],
        compiler_params=_params(("parallel", "parallel")),
        name="dft_outer_fwd",
    )(*args)


def _load_slab(ref, s):
    return jnp.concatenate([ref[g, s] for g in range(ref.shape[0])], axis=1)


def _store_slab(ref, s, x):
    for g in range(ref.shape[0]):
        ref[g, s] = x[:, g * LANES:(g + 1) * LANES]


def _mid_conv_body(a_ref, k_ref, f2_ref, g2_ref, d_ref):
    n2 = a_ref.shape[2]
    width = a_ref.shape[0] * LANES
    for s in range(a_ref.shape[1]):
        re, im = _unpack_complex(jnp.concatenate([_load_slab(a_ref, s), _load_slab(k_ref, s)], axis=1))
        x = _dot(f2_ref[...], jnp.concatenate([re, im], axis=0).astype(BF16))
        xr, xi, kr, ki = x[:n2, :width], x[n2:, :width], x[:n2, width:], x[n2:, width:]
        yr = xr * kr - xi * ki
        yi = xr * ki + xi * kr
        c = _dot(g2_ref[...], jnp.concatenate([yr, yi], axis=0).astype(BF16))
        _store_slab(d_ref, s, _pack_complex(c[:n2], c[n2:]))


def _mid(a, kf, order, fwd2, inv2, *, kb=16):
    groups, n1, n2, _ = a.shape
    slab = pl.BlockSpec((groups, kb, n2, LANES), lambda i: (0, i, 0, 0))
    kslab = pl.BlockSpec((groups, kb, n2, LANES), lambda i: (order, i, 0, 0))
    mat = pl.BlockSpec((2 * n2, 2 * n2), lambda i: (0, 0))
    return pl.pallas_call(
        _mid_conv_body,
        out_shape=jax.ShapeDtypeStruct(a.shape, U32),
        grid=(n1 // kb,),
        in_specs=[slab, kslab, mat, mat],
        out_specs=slab,
        compiler_params=_params(("parallel",)),
        name="hyena_mid_conv",
    )(a, kf, fwd2, inv2)


def _outer_inv_body(d_ref, e_ref, gate_ref, z_ref, skip_ref, o_ref, db, gb, zb, ob):
    skip = skip_ref[...]
    t2 = d_ref.shape[2]
    for ref, buf in ((d_ref, db), (gate_ref, gb), (z_ref, zb)):
        _flatten_block(ref, buf)
    for s in range(t2):
        dr, di = _unpack_complex(_load_strided(db, s))
        y = _dot(e_ref[s], jnp.concatenate([dr, di], axis=0).astype(BF16))
        _store_strided(ob, s, _load_strided(gb, s) * (y + skip * _load_strided(zb, s)))
    _unflatten_block(ob, o_ref)


def _outer_inv(d, inv1, gate4, z4, skip, *, t2=SUBLANES):
    groups, n1, n2, _ = d.shape
    half = n1 // 2
    blk = lambda r: pl.BlockSpec((groups, r, t2, LANES), lambda j: (0, 0, j, 0))
    return pl.pallas_call(
        _outer_inv_body,
        out_shape=jax.ShapeDtypeStruct((groups, half, n2, LANES), F32),
        grid=(n2 // t2,),
        in_specs=[blk(n1), pl.BlockSpec((t2, half, 2 * n1), lambda j: (j, 0, 0)), blk(half), blk(half),
                  pl.BlockSpec((1, groups * LANES), lambda j: (0, 0))],
        out_specs=blk(half),
        scratch_shapes=[_flat_scratch(groups, n1, t2, U32)] + [_flat_scratch(groups, half, t2)] * 3,
        compiler_params=_params(("parallel",)),
        name="dft_outer_inv",
    )(d, inv1, gate4, z4, skip)


def _log_sigmoid(x):
    return jnp.minimum(x, 0.0) - jnp.log1p(jnp.exp(-jnp.abs(x)))


_NT = (((1,), (1,)), ((), ()))
_TN = (((0,), (0,)), ((), ()))


def _chunk_matrices(sub):
    i = np.arange(sub)
    same = (i[:, None] // GLA_CHUNK) == (i[None, :] // GLA_CHUNK)
    fwd = same & (i[None, :] <= i[:, None])
    bwd = same & (i[None, :] >= i[:, None])
    return jnp.asarray(np.stack([fwd, bwd, same]), F32)


def _gla_gates_body(q_ref, k_ref, v_ref, lr_ref, w_ref, b_ref, cm_ref,
                    qe_ref, ke_ref, kd_ref, dec_ref, vb_ref, *, scale):
    tg, width = q_ref.shape
    sub = cm_ref.shape[1]
    chunk = GLA_CHUNK
    vb_ref[...] = v_ref[...].astype(BF16)
    same = cm_ref[2].astype(BF16)
    for s in range(tg // sub):
        rs = slice(s * sub, (s + 1) * sub)
        g = _log_sigmoid(_dot(lr_ref[rs, :].astype(BF16), w_ref[...]) + b_ref[...]) / GLA_TAU
        g_hi = g.astype(BF16)
        g_lo = (g - g_hi.astype(F32)).astype(BF16)
        q = q_ref[rs, :] * scale
        k = k_ref[rs, :]
        for d in range(2):
            cs = slice(d * width, (d + 1) * width)
            cum = cm_ref[d].astype(BF16)
            bc = _dot(cum, g_hi[:, cs]) + _dot(cum, g_lo[:, cs])
            bl = _dot(same, g_hi[:, cs]) + _dot(same, g_lo[:, cs])
            qe_ref[d, rs, :] = (q * jnp.exp(bc)).astype(BF16)
            ke_ref[d, rs, :] = (k * jnp.exp(-bc)).astype(BF16)
            kd_ref[d, rs, :] = (k * jnp.exp(bl - bc)).astype(BF16)
            for c in range(sub // chunk):
                row = s * (sub // chunk) + c
                dec_ref[d, row:row + 1, :] = jnp.exp(bl[c * chunk:c * chunk + 1])


def _gla_gates(p, lr, w_both, b_both, *, q_off, k_off, v_off, dk, dv, heads, tg=1024, sub=256):
    L = p.shape[0]
    tg = min(tg, L)
    wk, wv = heads * dk, heads * dv
    opnd = jax.ShapeDtypeStruct((2, L, wk), BF16)
    row = lambda width, off: pl.BlockSpec((tg, width), lambda i: (i, off // width))
    both = pl.BlockSpec((2, tg, wk), lambda i: (0, i, 0))
    const = lambda shape: pl.BlockSpec(shape, lambda i: (0,) * len(shape))
    return pl.pallas_call(
        functools.partial(_gla_gates_body, scale=dk ** -0.5),
        out_shape=(opnd, opnd, opnd, jax.ShapeDtypeStruct((2, L // GLA_CHUNK, wk), F32),
                   jax.ShapeDtypeStruct((L, wv), BF16)),
        grid=(L // tg,),
        in_specs=[row(wk, q_off), row(wk, k_off), row(wv, v_off), row(LANES, 0),
                  const((LANES, 2 * wk)), const((1, 2 * wk)), const((3, sub, sub))],
        out_specs=(both, both, both, pl.BlockSpec((2, tg // GLA_CHUNK, wk), lambda i: (0, i, 0)),
                   pl.BlockSpec((tg, wv), lambda i: (i, 0))),
        compiler_params=_params(("parallel",)),
        name="gla_gates",
    )(p, p, p, lr, w_both, b_both, _chunk_matrices(sub))


def _gla_direction(qe_ref, ke_ref, kd_ref, v_ref, dec_ref, mask, o_ref, st_ref, *, reverse, dk, dv):
    tb = qe_ref.shape[0]
    sub = mask.shape[0]
    chunk = GLA_CHUNK
    for hd in range(st_ref.shape[0]):
        ks, vs = slice(hd * dk, (hd + 1) * dk), slice(hd * dv, (hd + 1) * dv)
        intras = []
        for s in range(tb // sub):
            rs = slice(s * sub, (s + 1) * sub)
            a = lax.dot_general(qe_ref[rs, ks], ke_ref[rs, ks], _NT, preferred_element_type=F32)
            intra = _dot(jnp.where(mask > 0.5, a, 0.0).astype(BF16), v_ref[rs, vs])
            intras += [intra[c * chunk:(c + 1) * chunk] for c in range(sub // chunk)]
        st = st_ref[hd]
        chunks = range(tb // chunk)
        for c in (reversed(chunks) if reverse else chunks):
            sl = slice(c * chunk, (c + 1) * chunk)
            o_ref[sl, vs] = intras[c] + lax.dot_general(qe_ref[sl, ks], st.astype(BF16), _NT,
                                                        preferred_element_type=F32)
            upd = lax.dot_general(v_ref[sl, vs], kd_ref[sl, ks], _TN, preferred_element_type=F32)
            st = st * dec_ref[c:c + 1, ks] + upd
        st_ref[hd] = st


def _gla_body(qef, kef, kdf, vf, decf, qeb, keb, kdb, vb, decb, cm_ref, of_ref, ob_ref, sf_ref, sb_ref,
              *, dk, dv):
    @pl.when(pl.program_id(1) == 0)
    def _():
        sf_ref[...] = jnp.zeros_like(sf_ref)
        sb_ref[...] = jnp.zeros_like(sb_ref)

    _gla_direction(qef, kef, kdf, vf, decf, cm_ref[0], of_ref, sf_ref, reverse=False, dk=dk, dv=dv)
    _gla_direction(qeb, keb, kdb, vb, decb, cm_ref[1], ob_ref, sb_ref, reverse=True, dk=dk, dv=dv)


def _gla(qe, ke, kd, dec, vb, *, dk, dv, tb=2048, sub=256, hp=2):
    L = vb.shape[0]
    tb = min(tb, L)
    nb = L // tb
    heads = vb.shape[1] // dv
    nc = tb // GLA_CHUNK
    blk = lambda i, rev: (nb - 1 - i) if rev else i
    opnd = lambda rev: pl.BlockSpec((None, tb, hp * dk), lambda h, i: (int(rev), blk(i, rev), h))
    vspec = lambda rev: pl.BlockSpec((tb, hp * dv), lambda h, i: (blk(i, rev), h))
    dspec = lambda rev: pl.BlockSpec((None, nc, hp * dk), lambda h, i: (int(rev), blk(i, rev), h))
    side = lambda rev: [opnd(rev), opnd(rev), opnd(rev), vspec(rev), dspec(rev)]
    return pl.pallas_call(
        functools.partial(_gla_body, dk=dk, dv=dv),
        out_shape=(jax.ShapeDtypeStruct((L, heads * dv), F32),) * 2,
        grid=(heads // hp, nb),
        in_specs=side(False) + side(True) + [pl.BlockSpec((3, sub, sub), lambda h, i: (0, 0, 0))],
        out_specs=(vspec(False), vspec(True)),
        scratch_shapes=[pltpu.VMEM((hp, dv, dk), F32), pltpu.VMEM((hp, dv, dk), F32)],
        compiler_params=_params(("parallel", "arbitrary")),
        name="gla",
    )(qe, ke, kd, vb, dec, qe, ke, kd, vb, dec, _chunk_matrices(sub))


def _gla_post_body(of_ref, ob_ref, r_ref, gn_ref, c_ref, *, dv):
    o = of_ref[...] + ob_ref[...]
    r = r_ref[...]
    gate = r * jax.nn.sigmoid(r)
    for hd in range(o.shape[1] // dv):
        sl = slice(hd * dv, (hd + 1) * dv)
        c_ref[:, sl] = (_rms(o[:, sl], gn_ref[...]) * gate[:, sl]).astype(BF16)


def _gla_post(o_f, o_b, p, gla_norm, *, r_off, tm=512):
    L, wc = o_f.shape
    dv = gla_norm.shape[0]
    row = lambda blk=0: pl.BlockSpec((tm, wc), lambda i: (i, blk))
    return pl.pallas_call(
        functools.partial(_gla_post_body, dv=dv),
        out_shape=jax.ShapeDtypeStruct((L, wc), BF16),
        grid=(L // tm,),
        in_specs=[row(), row(), row(r_off // wc), pl.BlockSpec((1, dv), lambda i: (0, 0))],
        out_specs=row(),
        compiler_params=_params(("parallel",)),
        name="gla_post",
    )(o_f, o_b, p, gla_norm.reshape(1, dv))


def _merge_body(h_ref, g_ref, a_ref, z_ref, c_ref,
                wga_ref, wgb_ref, wgc_ref, bga_ref, bgb_ref, bgc_ref,
                wa_ref, wb_ref, wc_ref, m_ref, u_ref, zb_ref):
    @pl.when(pl.program_id(1) == 0)
    def _():
        u_ref[...] = _rms(h_ref[...], g_ref[...]).astype(BF16)
        zb_ref[...] = _load_groups(z_ref).astype(BF16)

    u = u_ref[...]
    m = jax.nn.sigmoid(_dot(u, wga_ref[...]) + bga_ref[...]) * _dot(a_ref[...], wa_ref[...])
    m += jax.nn.sigmoid(_dot(u, wgb_ref[...]) + bgb_ref[...]) * _dot(zb_ref[...], wb_ref[...])
    m += jax.nn.sigmoid(_dot(u, wgc_ref[...]) + bgc_ref[...]) * _dot(c_ref[...], wc_ref[...])
    m_ref[...] = m.astype(BF16)


def _merge(h, norm_g, a_pre, z, c_pre, w_gate, b_gate, w_a, w_b, w_c, *, tm=1024, tn=256):
    L, D = h.shape
    wa, wb, wc = w_a.shape[0], w_b.shape[0], w_c.shape[0]
    tm = min(tm, L)
    nj = D // tn
    row = lambda width: pl.BlockSpec((tm, width), lambda i, j: (i, 0))
    gate_w = lambda k: pl.BlockSpec((D, tn), lambda i, j: (0, j + k * nj))
    gate_b = lambda k: pl.BlockSpec((1, tn), lambda i, j: (0, j + k * nj))
    br_w = lambda width: pl.BlockSpec((width, tn), lambda i, j: (0, j))
    return pl.pallas_call(
        _merge_body,
        out_shape=jax.ShapeDtypeStruct((L, D), BF16),
        grid=(L // tm, nj),
        in_specs=[
            row(D), pl.BlockSpec((1, D), lambda i, j: (0, 0)),
            row(wa), pl.BlockSpec((wb // LANES, tm, LANES), lambda i, j: (0, i, 0)), row(wc),
            gate_w(0), gate_w(1), gate_w(2), gate_b(0), gate_b(1), gate_b(2),
            br_w(wa), br_w(wb), br_w(wc),
        ],
        out_specs=pl.BlockSpec((tm, tn), lambda i, j: (i, j)),
        scratch_shapes=[pltpu.VMEM((tm, D), BF16), pltpu.VMEM((tm, wb), BF16)],
        compiler_params=_params(("parallel", "arbitrary")),
        name="merge",
    )(h, norm_g.reshape(1, D), a_pre, z, c_pre,
      w_gate, w_gate, w_gate, b_gate, b_gate, b_gate, w_a, w_b, w_c)


def _out_proj_body(h_ref, m_ref, w_ref, o_ref, *, tn):
    m = m_ref[...]
    for c in range(o_ref.shape[1] // tn):
        sl = slice(c * tn, (c + 1) * tn)
        o_ref[:, sl] = h_ref[:, sl] + _dot(m, w_ref[:, sl])


def _out_proj(h, m, w_o, *, tm=512, tn=512):
    L, D = h.shape
    row = pl.BlockSpec((tm, D), lambda i: (i, 0))
    return pl.pallas_call(
        functools.partial(_out_proj_body, tn=tn),
        out_shape=jax.ShapeDtypeStruct((L, D), F32),
        grid=(L // tm,),
        in_specs=[row, row, pl.BlockSpec((D, D), lambda i: (0, 0), pipeline_mode=pl.Buffered(1))],
        out_specs=row,
        compiler_params=_params(("parallel",)),
        name="out_proj",
    )(h, m, w_o)


def _hyena(z0, g1, g2, skip, filt, consts):
    groups, L, _ = z0.shape
    n1 = DFT_N1
    n2 = 2 * L // n1
    k_time, sums = filt
    kf = _outer_fwd(k_time.reshape(k_time.shape[0], n1, n2, LANES), consts["fwd1"], sums)
    z = z0.reshape(groups, n1 // 2, n2, LANES)
    for order, gate in enumerate((g1, g2)):
        a = _outer_fwd(z, consts["fwd1"], t2=min(2 * SUBLANES, n2))
        d = _mid(a, kf, order, consts["fwd2"], consts["inv2"])
        z = _outer_inv(d, consts["inv1"], gate.reshape(z.shape), z, skip[order].reshape(1, groups * LANES))
    return z.reshape(groups, L, LANES)


def kernel(x, ffn1_norm, ffn1_w_gu, ffn1_w_down, mix_norm, w_in, b_in, conv_a, conv_b, hf_w1, hf_b1, hf_w2, hf_b2, hf_w3, hf_b3, hf_w_out, hf_freq, hy_skip, gk_w2, gk_b, gla_norm, w_br_a, w_br_b, w_br_c, w_o, ffn2_norm, ffn2_w_gu, ffn2_w_down, final_norm):
    bsz, L, D = x.shape
    depth = ffn1_norm.shape[0]
    wa = conv_a.shape[2]
    wb = conv_b.shape[3]
    wck = gk_w2.shape[3]
    dv = gla_norm.shape[1]
    wcv = GLA_HEADS * dv
    dk = wck // GLA_HEADS
    hid = hf_w2.shape[1]
    assert wa == wb, "the projection kernel tiles both conv branches with one section width"
    n_main = 3 * wa + 3 * wb + 2 * wck + 2 * wcv
    q_off, k_off, v_off, r_off = 0, wck, 2 * wck, 2 * wck + wcv
    lr_off = n_main
    gate_off = n_main + 2 * GLA_RANK
    n1 = DFT_N1
    n2 = 2 * L // n1

    consts = _dft_consts(n1, n2)
    deltas =jnp.linspace(math.log(HY_TARGET) / HY_SLOW_DECAY, math.log(HY_TARGET) / HY_FAST_DECAY, wb, dtype=F32)
    deltas2 = jnp.tile(deltas.reshape(1, wb), (1, HY_ORDER))

    outs = []
    for bi in range(bsz):
        h = x[bi]
        for l in range(depth):
            h = _ffn(h, ffn1_norm[l], _to_bf16(ffn1_w_gu, l), _to_bf16(ffn1_w_down, l),
                     final_norm, final=False)

            w_lr = jnp.pad(w_in[l][:, lr_off:gate_off], ((0, 0), (0, LANES - 2 * GLA_RANK))).astype(BF16)
            b_lr = jnp.pad(b_in[l][lr_off:gate_off], (0, LANES - 2 * GLA_RANK)).reshape(1, LANES)
            p, lr, a_pre, z0, g1, g2 = _proj(
                h, mix_norm[l], w_in[l][:, :n_main].astype(BF16), b_in[l][:n_main].reshape(1, n_main),
                w_lr, b_lr, conv_a[l], conv_b[l], wa=wa)

            w_out2 = hf_w_out[l].reshape(hid, HY_ORDER, 2, wb).transpose(2, 0, 1, 3).reshape(2, hid, HY_ORDER * wb)
            filt = _filters(L, hf_w1[l], hf_b1[l], hf_w2[l], hf_b2[l], hf_w3[l], hf_b3[l], hf_freq[l],
                            w_out2, deltas2)
            zb = _hyena(z0, g1, g2, hy_skip[l], filt, consts)

            w_both = jnp.zeros((LANES, 2 * wck), F32)
            w_both = w_both.at[:GLA_RANK, :wck].set(gk_w2[l, 0]).at[GLA_RANK:2 * GLA_RANK, wck:].set(gk_w2[l, 1])
            qe, ke, kd, dec, vb = _gla_gates(p, lr, w_both.astype(BF16), gk_b[l].reshape(1, 2 * wck),
                                             q_off=q_off, k_off=k_off, v_off=v_off, dk=dk, dv=dv,
                                             heads=GLA_HEADS)
            o_f, o_b = _gla(qe, ke, kd, dec, vb, dk=dk, dv=dv)

            c_pre = _gla_post(o_f, o_b, p, gla_norm[l], r_off=r_off)

            merged = _merge(h, mix_norm[l], a_pre, zb, c_pre,
                            w_in[l][:, gate_off:].astype(BF16), b_in[l][gate_off:].reshape(1, 3 * D),
                            _to_bf16(w_br_a, l), _to_bf16(w_br_b, l), _to_bf16(w_br_c, l))
            h = _out_proj(h, merged, _to_bf16(w_o, l))

            h = _ffn(h, ffn2_norm[l], _to_bf16(ffn2_w_gu, l), _to_bf16(ffn2_w_down, l),
                     final_norm, final=(l == depth - 1))
        outs.append(h)
    return jnp.stack(outs, axis=0)
```

```python
import functools
import math

import numpy as np
import jax
import jax.numpy as jnp
from jax import lax
from jax.experimental import pallas as pl
from jax.experimental.pallas import tpu as pltpu

F32 = jnp.float32
BF16 = jnp.bfloat16
HIGHEST = lax.Precision.HIGHEST

EPS = 1e-5
GLA_HEADS = 4
GLA_RANK = 16
GLA_TAU = 16.0
GLA_CHUNK = 64
HY_ORDER = 2
HY_BANDS = 16
HY_FAST_DECAY = 0.3
HY_SLOW_DECAY = 1.5
HY_TARGET = 1e-2

LANES = 128
SUBLANES = 8
DFT_N1 = 256
ROW_CHUNK = 128
VMEM_LIMIT = 56 * 1024 * 1024


def _params(sem):
    return pltpu.CompilerParams(dimension_semantics=sem, vmem_limit_bytes=VMEM_LIMIT)


def _rms(x, g):
    return x * lax.rsqrt(jnp.mean(x * x, axis=-1, keepdims=True) + EPS) * g


def _dot(a, b):
    return jnp.dot(a, b, preferred_element_type=F32)


def _dot_hi(a, b):
    return jnp.dot(a, b, preferred_element_type=F32, precision=HIGHEST)


def _dot3_tn(a, b):
    a_hi, b_hi = a.astype(BF16), b.astype(BF16)
    a_lo = (a - a_hi.astype(F32)).astype(BF16)
    b_lo = (b - b_hi.astype(F32)).astype(BF16)
    tn = lambda x, y: lax.dot_general(x, y, (((0,), (0,)), ((), ())), preferred_element_type=F32)
    return tn(a_hi, b_hi) + tn(a_lo, b_hi) + tn(a_hi, b_lo)


def _cast_body(x_ref, o_ref, *, scale):
    x = x_ref[...]
    o_ref[...] = (x if scale == 1.0 else x * scale).astype(o_ref.dtype)


def _to_bf16(w, layer, cols=None, *, scale=1.0, block_bytes=4 << 20):
    _, rows, width = w.shape
    cols = width if cols is None else cols
    tr = rows
    while tr * cols * 4 > block_bytes and tr % 32 == 0:
        tr //= 2
    return pl.pallas_call(
        functools.partial(_cast_body, scale=scale),
        out_shape=jax.ShapeDtypeStruct((rows, cols), BF16),
        grid=(rows // tr,),
        in_specs=[pl.BlockSpec((None, tr, cols), lambda i: (layer, i, 0))],
        out_specs=pl.BlockSpec((tr, cols), lambda i: (i, 0)),
        compiler_params=_params(("parallel",)),
        name="to_bf16",
    )(w)


def _ffn_body(x_ref, g_ref, wg_ref, wu_ref, wd_ref, fn_ref, o_ref, xn_ref, *, final, sub):
    j = pl.program_id(1)

    row_chunks = [slice(r, r + ROW_CHUNK) for r in range(0, x_ref.shape[0], ROW_CHUNK)]

    @pl.when(j == 0)
    def _():
        for rs in row_chunks:
            x = x_ref[rs, :]
            xn_ref[rs, :] = _rms(x, g_ref[...]).astype(BF16)
            o_ref[rs, :] = x

    xn = xn_ref[...]
    hs = []
    for c in range(wg_ref.shape[1] // sub):
        sl = slice(c * sub, (c + 1) * sub)
        g = _dot(xn, wg_ref[:, sl])
        u = _dot(xn, wu_ref[:, sl])
        hs.append((g * jax.nn.sigmoid(g) * u).astype(BF16))
    o_ref[...] += _dot(jnp.concatenate(hs, axis=1), wd_ref[...])

    if final:
        @pl.when(j == pl.num_programs(1) - 1)
        def _():
            for rs in row_chunks:
                o_ref[rs, :] = _rms(o_ref[rs, :], fn_ref[...])


def _ffn(x, norm_g, w_gu, half_w_down, final_g, *, final, tm=1024, tf=512, sub=256):
    w_down = half_w_down
    L, D = x.shape
    F = w_down.shape[0]
    tm = min(tm, L)
    nf = F // tf
    return pl.pallas_call(
        functools.partial(_ffn_body, final=final, sub=min(sub, tf)),
        out_shape=jax.ShapeDtypeStruct((L, D), F32),
        grid=(L // tm, nf),
        in_specs=[
            pl.BlockSpec((tm, D), lambda i, j: (i, 0)),
            pl.BlockSpec((1, D), lambda i, j: (0, 0)),
            pl.BlockSpec((D, tf), lambda i, j: (0, j)),
            pl.BlockSpec((D, tf), lambda i, j: (0, j + nf)),
            pl.BlockSpec((tf, D), lambda i, j: (j, 0)),
            pl.BlockSpec((1, D), lambda i, j: (0, 0)),
        ],
        out_specs=pl.BlockSpec((tm, D), lambda i, j: (i, 0)),
        scratch_shapes=[pltpu.VMEM((tm, D), BF16)],
        compiler_params=_params(("parallel", "arbitrary")),
        name="ffn",
    )(x, norm_g.reshape(1, D), w_gu, w_gu, w_down, final_g.reshape(1, D))


def _conv3_inner(x, w):
    n = x.shape[0]
    y = pltpu.roll(x, 1, axis=0) * w[0:1] + x * w[1:2] + pltpu.roll(x, n - 1, axis=0) * w[2:3]
    return y[SUBLANES:n - SUBLANES]


def _proj_body(h_ref, hp_ref, hn_ref, g_ref, w_ref, b_ref, wlr_ref, blr_ref, caw_ref, cbw_ref,
               p_ref, lr_ref, a_ref, z_ref, g1_ref, g2_ref, *, tn, wa):
    i = pl.program_id(0)
    tm = h_ref.shape[0]
    g = g_ref[...]
    xn = _rms(h_ref[...], g)
    xn_bf = xn.astype(BF16)
    lr_ref[...] = _dot(xn_bf, wlr_ref[...]) + blr_ref[...]
    conv_w = 6 * wa
    for c in range(p_ref.shape[1] // tn):
        src = slice(conv_w + c * tn, conv_w + (c + 1) * tn)
        p_ref[:, c * tn:(c + 1) * tn] = _dot(xn_bf, w_ref[:, src]) + b_ref[:, src]

    xh = jnp.concatenate([_rms(hp_ref[...], g), xn, _rms(hn_ref[...], g)], axis=0).astype(BF16)
    row = lax.broadcasted_iota(jnp.int32, (tm + 2 * SUBLANES, 1), 0)
    has_prev = (i > 0).astype(F32)
    has_next = (i < pl.num_programs(0) - 1).astype(F32)
    valid = jnp.where(row < SUBLANES, has_prev, jnp.where(row >= tm + SUBLANES, has_next, 1.0))

    def sec(k):
        sl = slice(k * wa, (k + 1) * wa)
        return (_dot(xh, w_ref[:, sl]) + b_ref[:, sl]) * valid

    xa, ba, ca = sec(0), sec(1), sec(2)
    a_ref[...] = (ba[SUBLANES:tm + SUBLANES] * _conv3_inner(ca * xa, caw_ref[...])).astype(BF16)
    for k, o_ref in enumerate((z_ref, g1_ref, g2_ref)):
        _store_groups(o_ref, _conv3_inner(sec(3 + k), cbw_ref[k]))


def _proj(h, norm_g, w, b, w_lr, b_lr, conv_a, conv_b, *, wa, tm=256, tn=512):
    L, D = h.shape
    conv_w = 6 * wa
    n_scan = w.shape[1] - conv_w
    rb = tm // SUBLANES
    nrb = L // SUBLANES
    resident = lambda shape: pl.BlockSpec(shape, lambda i: (0,) * len(shape), pipeline_mode=pl.Buffered(1))
    grouped = pl.BlockSpec((wa // LANES, tm, LANES), lambda i: (0, i, 0))
    return pl.pallas_call(
        functools.partial(_proj_body, tn=tn, wa=wa),
        out_shape=(jax.ShapeDtypeStruct((L, n_scan), F32), jax.ShapeDtypeStruct((L, LANES), F32),
                   jax.ShapeDtypeStruct((L, wa), BF16))
        + (jax.ShapeDtypeStruct((wa // LANES, L, LANES), F32),) * 3,
        grid=(L // tm,),
        in_specs=[
            pl.BlockSpec((tm, D), lambda i: (i, 0)),
            pl.BlockSpec((SUBLANES, D), lambda i: (jnp.maximum(i * rb - 1, 0), 0)),
            pl.BlockSpec((SUBLANES, D), lambda i: (jnp.minimum((i + 1) * rb, nrb - 1), 0)),
            resident((1, D)), resident(w.shape), resident((1, w.shape[1])),
            resident((D, LANES)), resident((1, LANES)), resident((3, wa)), resident((3, 3, wa)),
        ],
        out_specs=(pl.BlockSpec((tm, n_scan), lambda i: (i, 0)),
                   pl.BlockSpec((tm, LANES), lambda i: (i, 0)),
                   pl.BlockSpec((tm, wa), lambda i: (i, 0)), grouped, grouped, grouped),
        compiler_params=_params(("parallel",)),
        name="proj",
    )(h, h, h, norm_g.reshape(1, D), w, b, w_lr, b_lr, conv_a, conv_b)


def _store_groups(o_ref, x):
    for g in range(o_ref.shape[0]):
        o_ref[g] = x[:, g * LANES:(g + 1) * LANES]


def _load_groups(ref):
    return jnp.concatenate([ref[g] for g in range(ref.shape[0])], axis=1)


def _filter_body(w1t_ref, w1cs_ref, b1_ref, w2_ref, b2_ref, w3_ref, b3_ref, fr_ref, bands_ref, wo_ref, dl_ref,
                 k_ref, s_ref, *, half_tiles, tr, seq):
    i = pl.program_id(0)

    def lag_of(n):
        return jnp.where(n < seq, n, jnp.where(n == seq, 0, 2 * seq - n)).astype(F32)

    lag_row = lag_of(i * tr + lax.broadcasted_iota(jnp.int32, (1, tr), 1))
    ang = ((2.0 * math.pi / seq) * lag_row) * bands_ref[...]
    feats = jnp.concatenate([jnp.cos(ang), -jnp.sin(ang)], axis=0)
    fr = fr_ref[...]
    pre = w1t_ref[...] * (lag_row / max(seq - 1, 1)) + _dot_hi(w1cs_ref[...], feats) + b1_ref[...]
    h = jnp.sin(fr * pre)
    h = jnp.sin(fr * (_dot_hi(w2_ref[...], h) + b2_ref[...]))
    h = jnp.sin(fr * (_dot_hi(w3_ref[...], h) + b3_ref[...]))
    t = lag_of(i * tr + lax.broadcasted_iota(jnp.int32, (tr, 1), 0)) / max(seq - 1, 1)
    decay = jnp.exp(-t * jnp.abs(dl_ref[...]))
    hd = _dot3_tn(h, wo_ref[0]) * decay

    @pl.when(i % half_tiles == 0)
    def _():
        s_ref[...] = jnp.zeros_like(s_ref)

    s_ref[0] += jnp.sum(jnp.abs(hd), axis=0, keepdims=True)
    row = lax.broadcasted_iota(jnp.int32, hd.shape, 0)
    zero_row = jnp.where(i == half_tiles, 0, -1)
    _store_groups(k_ref, jnp.where(row == zero_row, 0.0, hd))


def _filters(seq, w1, b1, w2, b2, w3, b3, freq, w_out2, deltas2, *, tr=1024):
    n_rows = 2 * seq
    hid = w2.shape[0]
    width = w_out2.shape[2]
    half_tiles = n_rows // (2 * tr)
    bands = jnp.linspace(1e-4, HY_BANDS - 1, HY_BANDS, dtype=F32).reshape(HY_BANDS, 1)
    full = lambda shape: pl.BlockSpec(shape, lambda i: (0,) * len(shape))
    colv = lambda v: v.reshape(hid, 1)
    return pl.pallas_call(
        functools.partial(_filter_body, half_tiles=half_tiles, tr=tr, seq=seq),
        out_shape=(jax.ShapeDtypeStruct((width // LANES, n_rows, LANES), F32),
                   jax.ShapeDtypeStruct((2, 1, width), F32)),
        grid=(n_rows // tr,),
        in_specs=[
            full((hid, 1)), full((hid, 2 * HY_BANDS)), full((hid, 1)), full((hid, hid)), full((hid, 1)),
            full((hid, hid)), full((hid, 1)), full((hid, 1)), full((HY_BANDS, 1)),
            pl.BlockSpec((1, hid, width), lambda i: (i // half_tiles, 0, 0)),
            full((1, width)),
        ],
        out_specs=(pl.BlockSpec((width // LANES, tr, LANES), lambda i: (0, i, 0)),
                   pl.BlockSpec((1, 1, width), lambda i: (i // half_tiles, 0, 0))),
        compiler_params=_params(("arbitrary",)),
        name="hyena_filters",
    )(w1[0:1].T, w1[1:].T, colv(b1), w2.T, colv(b2), w3.T, colv(b3), colv(freq), bands, w_out2, deltas2)


def _dft_consts(n1, n2):
    n = n1 * n2
    k = np.arange(n1)
    m = np.arange(n2)
    ang2 = 2.0 * np.pi * ((m[:, None] * m[None, :]) % n2) / n2
    c2, s2 = np.cos(ang2), np.sin(ang2)
    as_bf = lambda a: jnp.asarray(np.asarray(a, np.float32).astype(jnp.bfloat16))
    time_index = k[None, None, :] * n2 + m[:, None, None]
    ang = 2.0 * np.pi * ((k[None, :, None] * time_index) % n) / n
    stage = np.concatenate([np.cos(ang), -np.sin(ang)], axis=1)
    half = n1 // 2
    return dict(
        fwd1=as_bf(stage),
        inv1=as_bf(stage.transpose(0, 2, 1)[:, :half] / n),
        fwd2=as_bf(np.block([[c2, s2], [-s2, c2]])),
        inv2=as_bf(np.block([[c2, -s2], [s2, c2]])),
    )


def _flatten_block(ref, buf):
    groups, n, t2, lanes = ref.shape
    for g in range(groups):
        for part in range(t2 // SUBLANES):
            rows = slice(part * SUBLANES, (part + 1) * SUBLANES)
            buf[g, part] = ref[g, :, rows, :].reshape(n * SUBLANES, lanes)


def _unflatten_block(buf, ref):
    groups, n, t2, lanes = ref.shape
    for g in range(groups):
        for part in range(t2 // SUBLANES):
            rows = slice(part * SUBLANES, (part + 1) * SUBLANES)
            ref[g, :, rows, :] = buf[g, part].reshape(n, SUBLANES, lanes)


def _load_strided(buf, s):
    part, sub = divmod(s, SUBLANES)
    n = buf.shape[2] // SUBLANES
    return jnp.concatenate([buf[g, part, pl.ds(sub, n, stride=SUBLANES), :] for g in range(buf.shape[0])],
                           axis=1)


def _store_strided(buf, s, x):
    part, sub = divmod(s, SUBLANES)
    n = buf.shape[2] // SUBLANES
    for g in range(buf.shape[0]):
        buf[g, part, pl.ds(sub, n, stride=SUBLANES), :] = x[:, g * LANES:(g + 1) * LANES]


def _flat_scratch(groups, n, t2, dtype=F32):
    return pltpu.VMEM((groups, t2 // SUBLANES, n * SUBLANES, LANES), dtype)


U32 = jnp.uint32


def _pack_complex(re, im):
    hi = lax.bitcast_convert_type(re.astype(BF16).astype(F32), U32)
    lo = lax.bitcast_convert_type(im.astype(BF16).astype(F32), U32)
    return hi | (lo >> 16)


def _unpack_complex(w):
    re = lax.bitcast_convert_type(w & jnp.uint32(0xFFFF0000), F32)
    im = lax.bitcast_convert_type(w << 16, F32)
    return re, im


def _outer_fwd_body(x_ref, f_ref, *rest, scaled):
    if scaled:
        s_ref, a_ref, xb, ab = rest
        inv = 1.0 / (s_ref[...] + EPS)
    else:
        a_ref, xb, ab = rest
    rows, t2 = x_ref.shape[1:3]
    n1 = a_ref.shape[1]
    _flatten_block(x_ref, xb)
    for s in range(t2):
        x = _load_strided(xb, s)
        if scaled:
            half = rows // 2
            x = jnp.concatenate([x[:half] * inv[0], x[half:] * inv[1]], axis=0)
        a = _dot(f_ref[s], x.astype(BF16))
        _store_strided(ab, s, _pack_complex(a[:n1], a[n1:]))
    _unflatten_block(ab, a_ref)


def _outer_fwd(x4, fwd1, sums=None, *, t2=SUBLANES, gt=4):
    groups, rows, n2, _ = x4.shape
    n1 = fwd1.shape[2]
    scaled = sums is not None
    blk = lambda r: pl.BlockSpec((gt, r, t2, LANES), lambda j, c: (c, 0, j, 0))
    args = [x4, fwd1]
    in_specs = [blk(rows), pl.BlockSpec((t2, 2 * n1, rows), lambda j, c: (j, 0, 0))]
    if scaled:
        args.append(sums)
        in_specs.append(pl.BlockSpec((2, 1, gt * LANES), lambda j, c: (0, 0, c)))
    return pl.pallas_call(
        functools.partial(_outer_fwd_body, scaled=scaled),
        out_shape=jax.ShapeDtypeStruct((groups, n1, n2, LANES), U32),
        grid=(n2 // t2, groups // gt),
        in_specs=in_specs,
        out_specs=blk(n1),
        scratch_shapes=[_flat_scratch(gt, rows, t2), _flat_scratch(gt, n1, t2, U32)],
        compiler_params=_params(("parallel", "parallel")),
        name="dft_outer_fwd",
    )(*args)


def _load_slab(ref, s):
    return jnp.concatenate([ref[g, s] for g in range(ref.shape[0])], axis=1)


def _store_slab(ref, s, x):
    for g in range(ref.shape[0]):
        ref[g, s] = x[:, g * LANES:(g + 1) * LANES]


def _mid_conv_body(a_ref, k_ref, f2_ref, g2_ref, d_ref):
    n2 = a_ref.shape[2]
    width = a_ref.shape[0] * LANES
    for s in range(a_ref.shape[1]):
        re, im = _unpack_complex(jnp.concatenate([_load_slab(a_ref, s), _load_slab(k_ref, s)], axis=1))
        x = _dot(f2_ref[...], jnp.concatenate([re, im], axis=0).astype(BF16))
        xr, xi, kr, ki = x[:n2, :width], x[n2:, :width], x[:n2, width:], x[n2:, width:]
        yr = xr * kr - xi * ki
        yi = xr * ki + xi * kr
        c = _dot(g2_ref[...], jnp.concatenate([yr, yi], axis=0).astype(BF16))
        _store_slab(d_ref, s, _pack_complex(c[:n2], c[n2:]))


def _mid(a, kf, order, fwd2, inv2, *, kb=8):
    groups, n1, n2, _ = a.shape
    slab = pl.BlockSpec((groups, kb, n2, LANES), lambda i: (0, i, 0, 0))
    kslab = pl.BlockSpec((groups, kb, n2, LANES), lambda i: (order, i, 0, 0))
    mat = pl.BlockSpec((2 * n2, 2 * n2), lambda i: (0, 0))
    return pl.pallas_call(
        _mid_conv_body,
        out_shape=jax.ShapeDtypeStruct(a.shape, U32),
        grid=(n1 // kb,),
        in_specs=[slab, kslab, mat, mat],
        out_specs=slab,
        compiler_params=_params(("parallel",)),
        name="hyena_mid_conv",
    )(a, kf, fwd2, inv2)


def _outer_inv_body(d_ref, e_ref, gate_ref, z_ref, skip_ref, o_ref, db, gb, zb, ob):
    skip = skip_ref[...]
    t2 = d_ref.shape[2]
    for ref, buf in ((d_ref, db), (gate_ref, gb), (z_ref, zb)):
        _flatten_block(ref, buf)
    for s in range(t2):
        dr, di = _unpack_complex(_load_strided(db, s))
        y = _dot(e_ref[s], jnp.concatenate([dr, di], axis=0).astype(BF16))
        _store_strided(ob, s, _load_strided(gb, s) * (y + skip * _load_strided(zb, s)))
    _unflatten_block(ob, o_ref)


def _outer_inv(d, inv1, gate4, z4, skip, *, t2=SUBLANES):
    groups, n1, n2, _ = d.shape
    half = n1 // 2
    blk = lambda r: pl.BlockSpec((groups, r, t2, LANES), lambda j: (0, 0, j, 0))
    return pl.pallas_call(
        _outer_inv_body,
        out_shape=jax.ShapeDtypeStruct((groups, half, n2, LANES), F32),
        grid=(n2 // t2,),
        in_specs=[blk(n1), pl.BlockSpec((t2, half, 2 * n1), lambda j: (j, 0, 0)), blk(half), blk(half),
                  pl.BlockSpec((1, groups * LANES), lambda j: (0, 0))],
        out_specs=blk(half),
        scratch_shapes=[_flat_scratch(groups, n1, t2, U32)] + [_flat_scratch(groups, half, t2)] * 3,
        compiler_params=_params(("parallel",)),
        name="dft_outer_inv",
    )(d, inv1, gate4, z4, skip)


def _log_sigmoid(x):
    return jnp.minimum(x, 0.0) - jnp.log1p(jnp.exp(-jnp.abs(x)))


_NT = (((1,), (1,)), ((), ()))
_TN = (((0,), (0,)), ((), ()))


def _chunk_matrices(sub):
    i = np.arange(sub)
    same = (i[:, None] // GLA_CHUNK) == (i[None, :] // GLA_CHUNK)
    fwd = same & (i[None, :] <= i[:, None])
    bwd = same & (i[None, :] >= i[:, None])
    return jnp.asarray(np.stack([fwd, bwd, same]), F32)


def _gla_gates_body(q_ref, k_ref, v_ref, lr_ref, w_ref, b_ref, cm_ref,
                    qe_ref, ke_ref, kd_ref, dec_ref, vb_ref, *, scale):
    tg, width = q_ref.shape
    sub = cm_ref.shape[1]
    chunk = GLA_CHUNK
    vb_ref[...] = v_ref[...].astype(BF16)
    same = cm_ref[2].astype(BF16)
    for s in range(tg // sub):
        rs = slice(s * sub, (s + 1) * sub)
        g = _log_sigmoid(_dot(lr_ref[rs, :].astype(BF16), w_ref[...]) + b_ref[...]) / GLA_TAU
        g_hi = g.astype(BF16)
        g_lo = (g - g_hi.astype(F32)).astype(BF16)
        q = q_ref[rs, :] * scale
        k = k_ref[rs, :]
        for d in range(2):
            cs = slice(d * width, (d + 1) * width)
            cum = cm_ref[d].astype(BF16)
            bc = _dot(cum, g_hi[:, cs]) + _dot(cum, g_lo[:, cs])
            bl = _dot(same, g_hi[:, cs]) + _dot(same, g_lo[:, cs])
            qe_ref[d, rs, :] = (q * jnp.exp(bc)).astype(BF16)
            ke_ref[d, rs, :] = (k * jnp.exp(-bc)).astype(BF16)
            kd_ref[d, rs, :] = (k * jnp.exp(bl - bc)).astype(BF16)
            for c in range(sub // chunk):
                row = s * (sub // chunk) + c
                dec_ref[d, row:row + 1, :] = jnp.exp(bl[c * chunk:c * chunk + 1])


def _gla_gates(p, lr, w_both, b_both, *, q_off, k_off, v_off, dk, dv, heads, tg=1024, sub=256):
    L = p.shape[0]
    tg = min(tg, L)
    wk, wv = heads * dk, heads * dv
    opnd = jax.ShapeDtypeStruct((2, L, wk), BF16)
    row = lambda width, off: pl.BlockSpec((tg, width), lambda i: (i, off // width))
    both = pl.BlockSpec((2, tg, wk), lambda i: (0, i, 0))
    const = lambda shape: pl.BlockSpec(shape, lambda i: (0,) * len(shape))
    return pl.pallas_call(
        functools.partial(_gla_gates_body, scale=dk ** -0.5),
        out_shape=(opnd, opnd, opnd, jax.ShapeDtypeStruct((2, L // GLA_CHUNK, wk), F32),
                   jax.ShapeDtypeStruct((L, wv), BF16)),
        grid=(L // tg,),
        in_specs=[row(wk, q_off), row(wk, k_off), row(wv, v_off), row(LANES, 0),
                  const((LANES, 2 * wk)), const((1, 2 * wk)), const((3, sub, sub))],
        out_specs=(both, both, both, pl.BlockSpec((2, tg // GLA_CHUNK, wk), lambda i: (0, i, 0)),
                   pl.BlockSpec((tg, wv), lambda i: (i, 0))),
        compiler_params=_params(("parallel",)),
        name="gla_gates",
    )(p, p, p, lr, w_both, b_both, _chunk_matrices(sub))


def _gla_direction(qe_ref, ke_ref, kd_ref, v_ref, dec_ref, mask, o_ref, st_ref, *, reverse, dk, dv):
    tb = qe_ref.shape[0]
    sub = mask.shape[0]
    chunk = GLA_CHUNK
    for hd in range(st_ref.shape[0]):
        ks, vs = slice(hd * dk, (hd + 1) * dk), slice(hd * dv, (hd + 1) * dv)
        intras = []
        for s in range(tb // sub):
            rs = slice(s * sub, (s + 1) * sub)
            a = lax.dot_general(qe_ref[rs, ks], ke_ref[rs, ks], _NT, preferred_element_type=F32)
            intra = _dot(jnp.where(mask > 0.5, a, 0.0).astype(BF16), v_ref[rs, vs])
            intras += [intra[c * chunk:(c + 1) * chunk] for c in range(sub // chunk)]
        st = st_ref[hd]
        chunks = range(tb // chunk)
        for c in (reversed(chunks) if reverse else chunks):
            sl = slice(c * chunk, (c + 1) * chunk)
            o_ref[sl, vs] = intras[c] + lax.dot_general(qe_ref[sl, ks], st.astype(BF16), _NT,
                                                        preferred_element_type=F32)
            upd = lax.dot_general(v_ref[sl, vs], kd_ref[sl, ks], _TN, preferred_element_type=F32)
            st = st * dec_ref[c:c + 1, ks] + upd
        st_ref[hd] = st


def _gla_body(qef, kef, kdf, vf, decf, qeb, keb, kdb, vb, decb, cm_ref, of_ref, ob_ref, sf_ref, sb_ref,
              *, dk, dv):
    @pl.when(pl.program_id(1) == 0)
    def _():
        sf_ref[...] = jnp.zeros_like(sf_ref)
        sb_ref[...] = jnp.zeros_like(sb_ref)

    _gla_direction(qef, kef, kdf, vf, decf, cm_ref[0], of_ref, sf_ref, reverse=False, dk=dk, dv=dv)
    _gla_direction(qeb, keb, kdb, vb, decb, cm_ref[1], ob_ref, sb_ref, reverse=True, dk=dk, dv=dv)


def _gla(qe, ke, kd, dec, vb, *, dk, dv, tb=1024, sub=256, hp=2):
    L = vb.shape[0]
    tb = min(tb, L)
    nb = L // tb
    heads = vb.shape[1] // dv
    nc = tb // GLA_CHUNK
    blk = lambda i, rev: (nb - 1 - i) if rev else i
    opnd = lambda rev: pl.BlockSpec((None, tb, hp * dk), lambda h, i: (int(rev), blk(i, rev), h))
    vspec = lambda rev: pl.BlockSpec((tb, hp * dv), lambda h, i: (blk(i, rev), h))
    dspec = lambda rev: pl.BlockSpec((None, nc, hp * dk), lambda h, i: (int(rev), blk(i, rev), h))
    side = lambda rev: [opnd(rev), opnd(rev), opnd(rev), vspec(rev), dspec(rev)]
    return pl.pallas_call(
        functools.partial(_gla_body, dk=dk, dv=dv),
        out_shape=(jax.ShapeDtypeStruct((L, heads * dv), F32),) * 2,
        grid=(heads // hp, nb),
        in_specs=side(False) + side(True) + [pl.BlockSpec((3, sub, sub), lambda h, i: (0, 0, 0))],
        out_specs=(vspec(False), vspec(True)),
        scratch_shapes=[pltpu.VMEM((hp, dv, dk), F32), pltpu.VMEM((hp, dv, dk), F32)],
        compiler_params=_params(("parallel", "arbitrary")),
        name="gla",
    )(qe, ke, kd, vb, dec, qe, ke, kd, vb, dec, _chunk_matrices(sub))


def _gla_post_body(of_ref, ob_ref, r_ref, gn_ref, c_ref, *, dv):
    o = of_ref[...] + ob_ref[...]
    r = r_ref[...]
    gate = r * jax.nn.sigmoid(r)
    for hd in range(o.shape[1] // dv):
        sl = slice(hd * dv, (hd + 1) * dv)
        c_ref[:, sl] = (_rms(o[:, sl], gn_ref[...]) * gate[:, sl]).astype(BF16)


def _gla_post(o_f, o_b, p, gla_norm, *, r_off, tm=512):
    L, wc = o_f.shape
    dv = gla_norm.shape[0]
    row = lambda blk=0: pl.BlockSpec((tm, wc), lambda i: (i, blk))
    return pl.pallas_call(
        functools.partial(_gla_post_body, dv=dv),
        out_shape=jax.ShapeDtypeStruct((L, wc), BF16),
        grid=(L // tm,),
        in_specs=[row(), row(), row(r_off // wc), pl.BlockSpec((1, dv), lambda i: (0, 0))],
        out_specs=row(),
        compiler_params=_params(("parallel",)),
        name="gla_post",
    )(o_f, o_b, p, gla_norm.reshape(1, dv))


def _merge_body(h_ref, g_ref, a_ref, z_ref, c_ref,
                wga_ref, wgb_ref, wgc_ref, bga_ref, bgb_ref, bgc_ref,
                wa_ref, wb_ref, wc_ref, m_ref, u_ref, zb_ref):
    @pl.when(pl.program_id(1) == 0)
    def _():
        u_ref[...] = _rms(h_ref[...], g_ref[...]).astype(BF16)
        zb_ref[...] = _load_groups(z_ref).astype(BF16)

    u = u_ref[...]
    m = jax.nn.sigmoid(_dot(u, wga_ref[...]) + bga_ref[...]) * _dot(a_ref[...], wa_ref[...])
    m += jax.nn.sigmoid(_dot(u, wgb_ref[...]) + bgb_ref[...]) * _dot(zb_ref[...], wb_ref[...])
    m += jax.nn.sigmoid(_dot(u, wgc_ref[...]) + bgc_ref[...]) * _dot(c_ref[...], wc_ref[...])
    m_ref[...] = m.astype(BF16)


def _merge(h, norm_g, a_pre, z, c_pre, w_gate, b_gate, w_a, w_b, w_c, *, tm=1024, tn=256):
    L, D = h.shape
    wa, wb, wc = w_a.shape[0], w_b.shape[0], w_c.shape[0]
    tm = min(tm, L)
    nj = D // tn
    row = lambda width: pl.BlockSpec((tm, width), lambda i, j: (i, 0))
    gate_w = lambda k: pl.BlockSpec((D, tn), lambda i, j: (0, j + k * nj))
    gate_b = lambda k: pl.BlockSpec((1, tn), lambda i, j: (0, j + k * nj))
    br_w = lambda width: pl.BlockSpec((width, tn), lambda i, j: (0, j))
    return pl.pallas_call(
        _merge_body,
        out_shape=jax.ShapeDtypeStruct((L, D), BF16),
        grid=(L // tm, nj),
        in_specs=[
            row(D), pl.BlockSpec((1, D), lambda i, j: (0, 0)),
            row(wa), pl.BlockSpec((wb // LANES, tm, LANES), lambda i, j: (0, i, 0)), row(wc),
            gate_w(0), gate_w(1), gate_w(2), gate_b(0), gate_b(1), gate_b(2),
            br_w(wa), br_w(wb), br_w(wc),
        ],
        out_specs=pl.BlockSpec((tm, tn), lambda i, j: (i, j)),
        scratch_shapes=[pltpu.VMEM((tm, D), BF16), pltpu.VMEM((tm, wb), BF16)],
        compiler_params=_params(("parallel", "arbitrary")),
        name="merge",
    )(h, norm_g.reshape(1, D), a_pre, z, c_pre,
      w_gate, w_gate, w_gate, b_gate, b_gate, b_gate, w_a, w_b, w_c)


def _out_proj_body(h_ref, m_ref, w_ref, o_ref, *, tn):
    m = m_ref[...]
    for c in range(o_ref.shape[1] // tn):
        sl = slice(c * tn, (c + 1) * tn)
        o_ref[:, sl] = h_ref[:, sl] + _dot(m, w_ref[:, sl])


def _out_proj(h, m, w_o, *, tm=512, tn=512):
    L, D = h.shape
    row = pl.BlockSpec((tm, D), lambda i: (i, 0))
    return pl.pallas_call(
        functools.partial(_out_proj_body, tn=tn),
        out_shape=jax.ShapeDtypeStruct((L, D), F32),
        grid=(L // tm,),
        in_specs=[row, row, pl.BlockSpec((D, D), lambda i: (0, 0), pipeline_mode=pl.Buffered(1))],
        out_specs=row,
        compiler_params=_params(("parallel",)),
        name="out_proj",
    )(h, m, w_o)


def _hyena(z0, g1, g2, skip, filt, consts):
    groups, L, _ = z0.shape
    n1 = DFT_N1
    n2 = 2 * L // n1
    k_time, sums = filt
    kf = _outer_fwd(k_time.reshape(k_time.shape[0], n1, n2, LANES), consts["fwd1"], sums)
    z = z0.reshape(groups, n1 // 2, n2, LANES)
    for order, gate in enumerate((g1, g2)):
        a = _outer_fwd(z, consts["fwd1"], t2=min(2 * SUBLANES, n2))
        d = _mid(a, kf, order, consts["fwd2"], consts["inv2"])
        z = _outer_inv(d, consts["inv1"], gate.reshape(z.shape), z, skip[order].reshape(1, groups * LANES))
    return z.reshape(groups, L, LANES)


def kernel(x, ffn1_norm, ffn1_w_gu, ffn1_w_down, mix_norm, w_in, b_in, conv_a, conv_b, hf_w1, hf_b1, hf_w2, hf_b2, hf_w3, hf_b3, hf_w_out, hf_freq, hy_skip, gk_w2, gk_b, gla_norm, w_br_a, w_br_b, w_br_c, w_o, ffn2_norm, ffn2_w_gu, ffn2_w_down, final_norm):
    bsz, L, D = x.shape
    depth = ffn1_norm.shape[0]
    wa = conv_a.shape[2]
    wb = conv_b.shape[3]
    wck = gk_w2.shape[3]
    dv = gla_norm.shape[1]
    wcv = GLA_HEADS * dv
    dk = wck // GLA_HEADS
    hid = hf_w2.shape[1]
    assert wa == wb, "the projection kernel tiles both conv branches with one section width"
    n_main = 3 * wa + 3 * wb + 2 * wck + 2 * wcv
    q_off, k_off, v_off, r_off = 0, wck, 2 * wck, 2 * wck + wcv
    lr_off = n_main
    gate_off = n_main + 2 * GLA_RANK
    n1 = DFT_N1
    n2 = 2 * L // n1

    consts = _dft_consts(n1, n2)
    deltas =jnp.linspace(math.log(HY_TARGET) / HY_SLOW_DECAY, math.log(HY_TARGET) / HY_FAST_DECAY, wb, dtype=F32)
    deltas2 = jnp.tile(deltas.reshape(1, wb), (1, HY_ORDER))

    outs = []
    for bi in range(bsz):
        h = x[bi]
        for l in range(depth):
            h = _ffn(h, ffn1_norm[l], _to_bf16(ffn1_w_gu, l), _to_bf16(ffn1_w_down, l, scale=0.5),
                     final_norm, final=False)

            w_lr = jnp.pad(w_in[l][:, lr_off:gate_off], ((0, 0), (0, LANES - 2 * GLA_RANK))).astype(BF16)
            b_lr = jnp.pad(b_in[l][lr_off:gate_off], (0, LANES - 2 * GLA_RANK)).reshape(1, LANES)
            p, lr, a_pre, z0, g1, g2 = _proj(
                h, mix_norm[l], w_in[l][:, :n_main].astype(BF16), b_in[l][:n_main].reshape(1, n_main),
                w_lr, b_lr, conv_a[l], conv_b[l], wa=wa)

            w_out2 = hf_w_out[l].reshape(hid, HY_ORDER, 2, wb).transpose(2, 0, 1, 3).reshape(2, hid, HY_ORDER * wb)
            filt = _filters(L, hf_w1[l], hf_b1[l], hf_w2[l], hf_b2[l], hf_w3[l], hf_b3[l], hf_freq[l],
                            w_out2, deltas2)
            zb = _hyena(z0, g1, g2, hy_skip[l], filt, consts)

            w_both = jnp.zeros((LANES, 2 * wck), F32)
            w_both = w_both.at[:GLA_RANK, :wck].set(gk_w2[l, 0]).at[GLA_RANK:2 * GLA_RANK, wck:].set(gk_w2[l, 1])
            qe, ke, kd, dec, vb = _gla_gates(p, lr, w_both.astype(BF16), gk_b[l].reshape(1, 2 * wck),
                                             q_off=q_off, k_off=k_off, v_off=v_off, dk=dk, dv=dv,
                                             heads=GLA_HEADS)
            o_f, o_b = _gla(qe, ke, kd, dec, vb, dk=dk, dv=dv)

            c_pre = _gla_post(o_f, o_b, p, gla_norm[l], r_off=r_off)

            merged = _merge(h, mix_norm[l], a_pre, zb, c_pre,
                            w_in[l][:, gate_off:].astype(BF16), b_in[l][gate_off:].reshape(1, 3 * D),
                            _to_bf16(w_br_a, l), _to_bf16(w_br_b, l), _to_bf16(w_br_c, l))
            h = _out_proj(h, merged, _to_bf16(w_o, l))

            h = _ffn(h, ffn2_norm[l], _to_bf16(ffn2_w_gu, l), _to_bf16(ffn2_w_down, l, scale=0.5),
                     final_norm, final=(l == depth - 1))
        outs.append(h)
    return jnp.stack(outs, axis=0)
```

```python
import functools
import math

import numpy as np
import jax
import jax.numpy as jnp
from jax import lax
from jax.experimental import pallas as pl
from jax.experimental.pallas import tpu as pltpu

F32 = jnp.float32
BF16 = jnp.bfloat16
HIGHEST = lax.Precision.HIGHEST

EPS = 1e-5
GLA_HEADS = 4
GLA_RANK = 16
GLA_TAU = 16.0
GLA_CHUNK = 64
HY_ORDER = 2
HY_BANDS = 16
HY_FAST_DECAY = 0.3
HY_SLOW_DECAY = 1.5
HY_TARGET = 1e-2

LANES = 128
SUBLANES = 8
DFT_N1 = 256
ROW_CHUNK = 128
VMEM_LIMIT = 56 * 1024 * 1024


def _params(sem):
    return pltpu.CompilerParams(dimension_semantics=sem, vmem_limit_bytes=VMEM_LIMIT)


def _rms(x, g):
    return x * lax.rsqrt(jnp.mean(x * x, axis=-1, keepdims=True) + EPS) * g


def _dot(a, b):
    return jnp.dot(a, b, preferred_element_type=F32)


def _dot_hi(a, b):
    return jnp.dot(a, b, preferred_element_type=F32, precision=HIGHEST)


def _dot3_tn(a, b):
    a_hi, b_hi = a.astype(BF16), b.astype(BF16)
    a_lo = (a - a_hi.astype(F32)).astype(BF16)
    b_lo = (b - b_hi.astype(F32)).astype(BF16)
    tn = lambda x, y: lax.dot_general(x, y, (((0,), (0,)), ((), ())), preferred_element_type=F32)
    return tn(a_hi, b_hi) + tn(a_lo, b_hi) + tn(a_hi, b_lo)


def _cast_body(x_ref, o_ref, *, scale):
    x = x_ref[...]
    o_ref[...] = (x if scale == 1.0 else x * scale).astype(o_ref.dtype)


def _to_bf16(w, layer, cols=None, *, scale=1.0, block_bytes=4 << 20):
    _, rows, width = w.shape
    cols = width if cols is None else cols
    tr = rows
    while tr * cols * 4 > block_bytes and tr % 32 == 0:
        tr //= 2
    return pl.pallas_call(
        functools.partial(_cast_body, scale=scale),
        out_shape=jax.ShapeDtypeStruct((rows, cols), BF16),
        grid=(rows // tr,),
        in_specs=[pl.BlockSpec((None, tr, cols), lambda i: (layer, i, 0))],
        out_specs=pl.BlockSpec((tr, cols), lambda i: (i, 0)),
        compiler_params=_params(("parallel",)),
        name="to_bf16",
    )(w)


def _ffn_body(x_ref, g_ref, wg_ref, wu_ref, wd_ref, fn_ref, o_ref, xn_ref, *, final, sub):
    j = pl.program_id(1)

    row_chunks = [slice(r, r + ROW_CHUNK) for r in range(0, x_ref.shape[0], ROW_CHUNK)]

    @pl.when(j == 0)
    def _():
        for rs in row_chunks:
            x = x_ref[rs, :]
            xn_ref[rs, :] = _rms(x, g_ref[...]).astype(BF16)
            o_ref[rs, :] = x

    xn = xn_ref[...]
    hs = []
    for c in range(wg_ref.shape[1] // sub):
        sl = slice(c * sub, (c + 1) * sub)
        g = _dot(xn, wg_ref[:, sl])
        u = _dot(xn, wu_ref[:, sl])
        hs.append((g * jax.nn.sigmoid(g) * u).astype(BF16))
    o_ref[...] += _dot(jnp.concatenate(hs, axis=1), wd_ref[...])

    if final:
        @pl.when(j == pl.num_programs(1) - 1)
        def _():
            for rs in row_chunks:
                o_ref[rs, :] = _rms(o_ref[rs, :], fn_ref[...])


def _ffn(x, norm_g, w_gu, half_w_down, final_g, *, final, tm=1024, tf=512, sub=256):
    w_down = half_w_down
    L, D = x.shape
    F = w_down.shape[0]
    tm = min(tm, L)
    nf = F // tf
    return pl.pallas_call(
        functools.partial(_ffn_body, final=final, sub=min(sub, tf)),
        out_shape=jax.ShapeDtypeStruct((L, D), F32),
        grid=(L // tm, nf),
        in_specs=[
            pl.BlockSpec((tm, D), lambda i, j: (i, 0)),
            pl.BlockSpec((1, D), lambda i, j: (0, 0)),
            pl.BlockSpec((D, tf), lambda i, j: (0, j)),
            pl.BlockSpec((D, tf), lambda i, j: (0, j + nf)),
            pl.BlockSpec((tf, D), lambda i, j: (j, 0)),
            pl.BlockSpec((1, D), lambda i, j: (0, 0)),
        ],
        out_specs=pl.BlockSpec((tm, D), lambda i, j: (i, 0)),
        scratch_shapes=[pltpu.VMEM((tm, D), BF16)],
        compiler_params=_params(("parallel", "arbitrary")),
        name="ffn",
    )(x, norm_g.reshape(1, D), w_gu, w_gu, w_down, final_g.reshape(1, D))


def _conv3_inner(x, w):
    n = x.shape[0]
    y = pltpu.roll(x, 1, axis=0) * w[0:1] + x * w[1:2] + pltpu.roll(x, n - 1, axis=0) * w[2:3]
    return y[SUBLANES:n - SUBLANES]


def _proj_body(h_ref, hp_ref, hn_ref, g_ref, w_ref, b_ref, wlr_ref, blr_ref, caw_ref, cbw_ref,
               p_ref, lr_ref, a_ref, z_ref, g1_ref, g2_ref, u_ref, *, tn, wa):
    i = pl.program_id(0)
    tm = h_ref.shape[0]
    g = g_ref[...]
    xn = _rms(h_ref[...], g)
    xn_bf = xn.astype(BF16)
    u_ref[...] = xn_bf
    lr_ref[...] = _dot(xn_bf, wlr_ref[...]) + blr_ref[...]
    conv_w = 6 * wa
    for c in range(p_ref.shape[1] // tn):
        src = slice(conv_w + c * tn, conv_w + (c + 1) * tn)
        p_ref[:, c * tn:(c + 1) * tn] = _dot(xn_bf, w_ref[:, src]) + b_ref[:, src]

    xh = jnp.concatenate([_rms(hp_ref[...], g), xn, _rms(hn_ref[...], g)], axis=0).astype(BF16)
    row = lax.broadcasted_iota(jnp.int32, (tm + 2 * SUBLANES, 1), 0)
    has_prev = (i > 0).astype(F32)
    has_next = (i < pl.num_programs(0) - 1).astype(F32)
    valid = jnp.where(row < SUBLANES, has_prev, jnp.where(row >= tm + SUBLANES, has_next, 1.0))

    def sec(k):
        sl = slice(k * wa, (k + 1) * wa)
        return (_dot(xh, w_ref[:, sl]) + b_ref[:, sl]) * valid

    xa, ba, ca = sec(0), sec(1), sec(2)
    a_ref[...] = (ba[SUBLANES:tm + SUBLANES] * _conv3_inner(ca * xa, caw_ref[...])).astype(BF16)
    for k, o_ref in enumerate((z_ref, g1_ref, g2_ref)):
        _store_groups(o_ref, _conv3_inner(sec(3 + k), cbw_ref[k]))


def _proj(h, norm_g, w, b, w_lr, b_lr, conv_a, conv_b, *, wa, tm=256, tn=512):
    L, D = h.shape
    conv_w = 6 * wa
    n_scan = w.shape[1] - conv_w
    rb = tm // SUBLANES
    nrb = L // SUBLANES
    resident = lambda shape: pl.BlockSpec(shape, lambda i: (0,) * len(shape), pipeline_mode=pl.Buffered(1))
    grouped = pl.BlockSpec((wa // LANES, tm, LANES), lambda i: (0, i, 0))
    return pl.pallas_call(
        functools.partial(_proj_body, tn=tn, wa=wa),
        out_shape=(jax.ShapeDtypeStruct((L, n_scan), F32), jax.ShapeDtypeStruct((L, LANES), F32),
                   jax.ShapeDtypeStruct((L, wa), BF16))
        + (jax.ShapeDtypeStruct((wa // LANES, L, LANES), F32),) * 3
        + (jax.ShapeDtypeStruct((L, D), BF16),),
        grid=(L // tm,),
        in_specs=[
            pl.BlockSpec((tm, D), lambda i: (i, 0)),
            pl.BlockSpec((SUBLANES, D), lambda i: (jnp.maximum(i * rb - 1, 0), 0)),
            pl.BlockSpec((SUBLANES, D), lambda i: (jnp.minimum((i + 1) * rb, nrb - 1), 0)),
            resident((1, D)), resident(w.shape), resident((1, w.shape[1])),
            resident((D, LANES)), resident((1, LANES)), resident((3, wa)), resident((3, 3, wa)),
        ],
        out_specs=(pl.BlockSpec((tm, n_scan), lambda i: (i, 0)),
                   pl.BlockSpec((tm, LANES), lambda i: (i, 0)),
                   pl.BlockSpec((tm, wa), lambda i: (i, 0)), grouped, grouped, grouped,
                   pl.BlockSpec((tm, D), lambda i: (i, 0))),
        compiler_params=_params(("parallel",)),
        name="proj",
    )(h, h, h, norm_g.reshape(1, D), w, b, w_lr, b_lr, conv_a, conv_b)


def _store_groups(o_ref, x):
    for g in range(o_ref.shape[0]):
        o_ref[g] = x[:, g * LANES:(g + 1) * LANES]


def _load_groups(ref):
    return jnp.concatenate([ref[g] for g in range(ref.shape[0])], axis=1)


def _filter_body(w1t_ref, w1cs_ref, b1_ref, w2_ref, b2_ref, w3_ref, b3_ref, fr_ref, bands_ref, wo_ref, dl_ref,
                 k_ref, s_ref, *, half_tiles, tr, seq):
    i = pl.program_id(0)

    def lag_of(n):
        return jnp.where(n < seq, n, jnp.where(n == seq, 0, 2 * seq - n)).astype(F32)

    lag_row = lag_of(i * tr + lax.broadcasted_iota(jnp.int32, (1, tr), 1))
    ang = ((2.0 * math.pi / seq) * lag_row) * bands_ref[...]
    feats = jnp.concatenate([jnp.cos(ang), -jnp.sin(ang)], axis=0)
    fr = fr_ref[...]
    pre = w1t_ref[...] * (lag_row / max(seq - 1, 1)) + _dot_hi(w1cs_ref[...], feats) + b1_ref[...]
    h = jnp.sin(fr * pre)
    h = jnp.sin(fr * (_dot_hi(w2_ref[...], h) + b2_ref[...]))
    h = jnp.sin(fr * (_dot_hi(w3_ref[...], h) + b3_ref[...]))
    t = lag_of(i * tr + lax.broadcasted_iota(jnp.int32, (tr, 1), 0)) / max(seq - 1, 1)
    decay = jnp.exp(-t * jnp.abs(dl_ref[...]))
    hd = _dot3_tn(h, wo_ref[0]) * decay

    @pl.when(i % half_tiles == 0)
    def _():
        s_ref[...] = jnp.zeros_like(s_ref)

    s_ref[0] += jnp.sum(jnp.abs(hd), axis=0, keepdims=True)
    row = lax.broadcasted_iota(jnp.int32, hd.shape, 0)
    zero_row = jnp.where(i == half_tiles, 0, -1)
    _store_groups(k_ref, jnp.where(row == zero_row, 0.0, hd))


def _filters(seq, w1, b1, w2, b2, w3, b3, freq, w_out2, deltas2, *, tr=1024):
    n_rows = 2 * seq
    hid = w2.shape[0]
    width = w_out2.shape[2]
    half_tiles = n_rows // (2 * tr)
    bands = jnp.linspace(1e-4, HY_BANDS - 1, HY_BANDS, dtype=F32).reshape(HY_BANDS, 1)
    full = lambda shape: pl.BlockSpec(shape, lambda i: (0,) * len(shape))
    colv = lambda v: v.reshape(hid, 1)
    return pl.pallas_call(
        functools.partial(_filter_body, half_tiles=half_tiles, tr=tr, seq=seq),
        out_shape=(jax.ShapeDtypeStruct((width // LANES, n_rows, LANES), F32),
                   jax.ShapeDtypeStruct((2, 1, width), F32)),
        grid=(n_rows // tr,),
        in_specs=[
            full((hid, 1)), full((hid, 2 * HY_BANDS)), full((hid, 1)), full((hid, hid)), full((hid, 1)),
            full((hid, hid)), full((hid, 1)), full((hid, 1)), full((HY_BANDS, 1)),
            pl.BlockSpec((1, hid, width), lambda i: (i // half_tiles, 0, 0)),
            full((1, width)),
        ],
        out_specs=(pl.BlockSpec((width // LANES, tr, LANES), lambda i: (0, i, 0)),
                   pl.BlockSpec((1, 1, width), lambda i: (i // half_tiles, 0, 0))),
        compiler_params=_params(("arbitrary",)),
        name="hyena_filters",
    )(w1[0:1].T, w1[1:].T, colv(b1), w2.T, colv(b2), w3.T, colv(b3), colv(freq), bands, w_out2, deltas2)


def _dft_consts(n1, n2):
    n = n1 * n2
    k = np.arange(n1)
    m = np.arange(n2)
    ang2 = 2.0 * np.pi * ((m[:, None] * m[None, :]) % n2) / n2
    c2, s2 = np.cos(ang2), np.sin(ang2)
    as_bf = lambda a: jnp.asarray(np.asarray(a, np.float32).astype(jnp.bfloat16))
    time_index = k[None, None, :] * n2 + m[:, None, None]
    ang = 2.0 * np.pi * ((k[None, :, None] * time_index) % n) / n
    stage = np.concatenate([np.cos(ang), -np.sin(ang)], axis=1)
    half = n1 // 2
    return dict(
        fwd1=as_bf(stage),
        inv1=as_bf(stage.transpose(0, 2, 1)[:, :half] / n),
        fwd2=as_bf(np.block([[c2, s2], [-s2, c2]])),
        inv2=as_bf(np.block([[c2, -s2], [s2, c2]])),
    )


def _flatten_block(ref, buf):
    groups, n, t2, lanes = ref.shape
    for g in range(groups):
        for part in range(t2 // SUBLANES):
            rows = slice(part * SUBLANES, (part + 1) * SUBLANES)
            buf[g, part] = ref[g, :, rows, :].reshape(n * SUBLANES, lanes)


def _unflatten_block(buf, ref):
    groups, n, t2, lanes = ref.shape
    for g in range(groups):
        for part in range(t2 // SUBLANES):
            rows = slice(part * SUBLANES, (part + 1) * SUBLANES)
            ref[g, :, rows, :] = buf[g, part].reshape(n, SUBLANES, lanes)


def _load_strided(buf, s):
    part, sub = divmod(s, SUBLANES)
    n = buf.shape[2] // SUBLANES
    return jnp.concatenate([buf[g, part, pl.ds(sub, n, stride=SUBLANES), :] for g in range(buf.shape[0])],
                           axis=1)


def _store_strided(buf, s, x):
    part, sub = divmod(s, SUBLANES)
    n = buf.shape[2] // SUBLANES
    for g in range(buf.shape[0]):
        buf[g, part, pl.ds(sub, n, stride=SUBLANES), :] = x[:, g * LANES:(g + 1) * LANES]


def _flat_scratch(groups, n, t2, dtype=F32):
    return pltpu.VMEM((groups, t2 // SUBLANES, n * SUBLANES, LANES), dtype)


U32 = jnp.uint32


def _pack_complex(re, im):
    hi = lax.bitcast_convert_type(re.astype(BF16).astype(F32), U32)
    lo = lax.bitcast_convert_type(im.astype(BF16).astype(F32), U32)
    return hi | (lo >> 16)


def _unpack_complex(w):
    re = lax.bitcast_convert_type(w & jnp.uint32(0xFFFF0000), F32)
    im = lax.bitcast_convert_type(w << 16, F32)
    return re, im


def _outer_fwd_body(x_ref, f_ref, *rest, scaled):
    if scaled:
        s_ref, a_ref, xb, ab = rest
        inv = 1.0 / (s_ref[...] + EPS)
    else:
        a_ref, xb, ab = rest
    rows, t2 = x_ref.shape[1:3]
    n1 = a_ref.shape[1]
    _flatten_block(x_ref, xb)
    for s in range(t2):
        x = _load_strided(xb, s)
        if scaled:
            half = rows // 2
            x = jnp.concatenate([x[:half] * inv[0], x[half:] * inv[1]], axis=0)
        a = _dot(f_ref[s], x.astype(BF16))
        _store_strided(ab, s, _pack_complex(a[:n1], a[n1:]))
    _unflatten_block(ab, a_ref)


def _outer_fwd(x4, fwd1, sums=None, *, t2=SUBLANES, gt=4):
    groups, rows, n2, _ = x4.shape
    n1 = fwd1.shape[2]
    scaled = sums is not None
    blk = lambda r: pl.BlockSpec((gt, r, t2, LANES), lambda j, c: (c, 0, j, 0))
    args = [x4, fwd1]
    in_specs = [blk(rows), pl.BlockSpec((t2, 2 * n1, rows), lambda j, c: (j, 0, 0))]
    if scaled:
        args.append(sums)
        in_specs.append(pl.BlockSpec((2, 1, gt * LANES), lambda j, c: (0, 0, c)))
    return pl.pallas_call(
        functools.partial(_outer_fwd_body, scaled=scaled),
        out_shape=jax.ShapeDtypeStruct((groups, n1, n2, LANES), U32),
        grid=(n2 // t2, groups // gt),
        in_specs=in_specs,
        out_specs=blk(n1),
        scratch_shapes=[_flat_scratch(gt, rows, t2), _flat_scratch(gt, n1, t2, U32)],
        compiler_params=_params(("parallel", "parallel")),
        name="dft_outer_fwd",
    )(*args)


def _load_slab(ref, s):
    return jnp.concatenate([ref[g, s] for g in range(ref.shape[0])], axis=1)


def _store_slab(ref, s, x):
    for g in range(ref.shape[0]):
        ref[g, s] = x[:, g * LANES:(g + 1) * LANES]


def _mid_conv_body(a_ref, k_ref, f2_ref, g2_ref, d_ref):
    n2 = a_ref.shape[2]
    width = a_ref.shape[0] * LANES
    for s in range(a_ref.shape[1]):
        re, im = _unpack_complex(jnp.concatenate([_load_slab(a_ref, s), _load_slab(k_ref, s)], axis=1))
        x = _dot(f2_ref[...], jnp.concatenate([re, im], axis=0).astype(BF16))
        xr, xi, kr, ki = x[:n2, :width], x[n2:, :width], x[:n2, width:], x[n2:, width:]
        yr = xr * kr - xi * ki
        yi = xr * ki + xi * kr
        c = _dot(g2_ref[...], jnp.concatenate([yr, yi], axis=0).astype(BF16))
        _store_slab(d_ref, s, _pack_complex(c[:n2], c[n2:]))


def _mid(a, kf, order, fwd2, inv2, *, kb=8):
    groups, n1, n2, _ = a.shape
    slab = pl.BlockSpec((groups, kb, n2, LANES), lambda i: (0, i, 0, 0))
    kslab = pl.BlockSpec((groups, kb, n2, LANES), lambda i: (order, i, 0, 0))
    mat = pl.BlockSpec((2 * n2, 2 * n2), lambda i: (0, 0))
    return pl.pallas_call(
        _mid_conv_body,
        out_shape=jax.ShapeDtypeStruct(a.shape, U32),
        grid=(n1 // kb,),
        in_specs=[slab, kslab, mat, mat],
        out_specs=slab,
        compiler_params=_params(("parallel",)),
        name="hyena_mid_conv",
    )(a, kf, fwd2, inv2)


def _outer_inv_body(d_ref, e_ref, gate_ref, z_ref, skip_ref, o_ref, db, gb, zb, ob):
    skip = skip_ref[...]
    t2 = d_ref.shape[2]
    for ref, buf in ((d_ref, db), (gate_ref, gb), (z_ref, zb)):
        _flatten_block(ref, buf)
    for s in range(t2):
        dr, di = _unpack_complex(_load_strided(db, s))
        y = _dot(e_ref[s], jnp.concatenate([dr, di], axis=0).astype(BF16))
        _store_strided(ob, s, _load_strided(gb, s) * (y + skip * _load_strided(zb, s)))
    _unflatten_block(ob, o_ref)


def _outer_inv(d, inv1, gate4, z4, skip, *, t2=SUBLANES):
    groups, n1, n2, _ = d.shape
    half = n1 // 2
    blk = lambda r: pl.BlockSpec((groups, r, t2, LANES), lambda j: (0, 0, j, 0))
    return pl.pallas_call(
        _outer_inv_body,
        out_shape=jax.ShapeDtypeStruct((groups, half, n2, LANES), F32),
        grid=(n2 // t2,),
        in_specs=[blk(n1), pl.BlockSpec((t2, half, 2 * n1), lambda j: (j, 0, 0)), blk(half), blk(half),
                  pl.BlockSpec((1, groups * LANES), lambda j: (0, 0))],
        out_specs=blk(half),
        scratch_shapes=[_flat_scratch(groups, n1, t2, U32)] + [_flat_scratch(groups, half, t2)] * 3,
        compiler_params=_params(("parallel",)),
        name="dft_outer_inv",
    )(d, inv1, gate4, z4, skip)


def _log_sigmoid(x):
    return jnp.minimum(x, 0.0) - jnp.log1p(jnp.exp(-jnp.abs(x)))


_NT = (((1,), (1,)), ((), ()))
_TN = (((0,), (0,)), ((), ()))


def _chunk_matrices(sub):
    i = np.arange(sub)
    same = (i[:, None] // GLA_CHUNK) == (i[None, :] // GLA_CHUNK)
    fwd = same & (i[None, :] <= i[:, None])
    bwd = same & (i[None, :] >= i[:, None])
    return jnp.asarray(np.stack([fwd, bwd, same]), F32)


def _gla_gates_body(q_ref, k_ref, v_ref, lr_ref, w_ref, b_ref, cm_ref,
                    qe_ref, ke_ref, kd_ref, dec_ref, vb_ref, *, scale):
    tg, width = q_ref.shape
    sub = cm_ref.shape[1]
    chunk = GLA_CHUNK
    vb_ref[...] = v_ref[...].astype(BF16)
    same = cm_ref[2].astype(BF16)
    for s in range(tg // sub):
        rs = slice(s * sub, (s + 1) * sub)
        g = _log_sigmoid(_dot(lr_ref[rs, :].astype(BF16), w_ref[...]) + b_ref[...]) / GLA_TAU
        g_hi = g.astype(BF16)
        g_lo = (g - g_hi.astype(F32)).astype(BF16)
        q = q_ref[rs, :] * scale
        k = k_ref[rs, :]
        for d in range(2):
            cs = slice(d * width, (d + 1) * width)
            cum = cm_ref[d].astype(BF16)
            bc = _dot(cum, g_hi[:, cs]) + _dot(cum, g_lo[:, cs])
            bl = _dot(same, g_hi[:, cs]) + _dot(same, g_lo[:, cs])
            qe_ref[d, rs, :] = (q * jnp.exp(bc)).astype(BF16)
            ke_ref[d, rs, :] = (k * jnp.exp(-bc)).astype(BF16)
            kd_ref[d, rs, :] = (k * jnp.exp(bl - bc)).astype(BF16)
            for c in range(sub // chunk):
                row = s * (sub // chunk) + c
                dec_ref[d, row:row + 1, :] = jnp.exp(bl[c * chunk:c * chunk + 1])


def _gla_gates(p, lr, w_both, b_both, *, q_off, k_off, v_off, dk, dv, heads, tg=1024, sub=256):
    L = p.shape[0]
    tg = min(tg, L)
    wk, wv = heads * dk, heads * dv
    opnd = jax.ShapeDtypeStruct((2, L, wk), BF16)
    row = lambda width, off: pl.BlockSpec((tg, width), lambda i: (i, off // width))
    both = pl.BlockSpec((2, tg, wk), lambda i: (0, i, 0))
    const = lambda shape: pl.BlockSpec(shape, lambda i: (0,) * len(shape))
    return pl.pallas_call(
        functools.partial(_gla_gates_body, scale=dk ** -0.5),
        out_shape=(opnd, opnd, opnd, jax.ShapeDtypeStruct((2, L // GLA_CHUNK, wk), F32),
                   jax.ShapeDtypeStruct((L, wv), BF16)),
        grid=(L // tg,),
        in_specs=[row(wk, q_off), row(wk, k_off), row(wv, v_off), row(LANES, 0),
                  const((LANES, 2 * wk)), const((1, 2 * wk)), const((3, sub, sub))],
        out_specs=(both, both, both, pl.BlockSpec((2, tg // GLA_CHUNK, wk), lambda i: (0, i, 0)),
                   pl.BlockSpec((tg, wv), lambda i: (i, 0))),
        compiler_params=_params(("parallel",)),
        name="gla_gates",
    )(p, p, p, lr, w_both, b_both, _chunk_matrices(sub))


def _gla_direction(qe_ref, ke_ref, kd_ref, v_ref, dec_ref, mask, o_ref, st_ref, *, reverse, dk, dv):
    tb = qe_ref.shape[0]
    sub = mask.shape[0]
    chunk = GLA_CHUNK
    for hd in range(st_ref.shape[0]):
        ks, vs = slice(hd * dk, (hd + 1) * dk), slice(hd * dv, (hd + 1) * dv)
        intras = []
        for s in range(tb // sub):
            rs = slice(s * sub, (s + 1) * sub)
            a = lax.dot_general(qe_ref[rs, ks], ke_ref[rs, ks], _NT, preferred_element_type=F32)
            intra = _dot(jnp.where(mask > 0.5, a, 0.0).astype(BF16), v_ref[rs, vs])
            intras += [intra[c * chunk:(c + 1) * chunk] for c in range(sub // chunk)]
        st = st_ref[hd]
        chunks = range(tb // chunk)
        for c in (reversed(chunks) if reverse else chunks):
            sl = slice(c * chunk, (c + 1) * chunk)
            o_ref[sl, vs] = intras[c] + lax.dot_general(qe_ref[sl, ks], st.astype(BF16), _NT,
                                                        preferred_element_type=F32)
            upd = lax.dot_general(v_ref[sl, vs], kd_ref[sl, ks], _TN, preferred_element_type=F32)
            st = st * dec_ref[c:c + 1, ks] + upd
        st_ref[hd] = st


def _gla_body(qef, kef, kdf, vf, decf, qeb, keb, kdb, vb, decb, cm_ref, of_ref, ob_ref, sf_ref, sb_ref,
              *, dk, dv):
    @pl.when(pl.program_id(1) == 0)
    def _():
        sf_ref[...] = jnp.zeros_like(sf_ref)
        sb_ref[...] = jnp.zeros_like(sb_ref)

    _gla_direction(qef, kef, kdf, vf, decf, cm_ref[0], of_ref, sf_ref, reverse=False, dk=dk, dv=dv)
    _gla_direction(qeb, keb, kdb, vb, decb, cm_ref[1], ob_ref, sb_ref, reverse=True, dk=dk, dv=dv)


def _gla(qe, ke, kd, dec, vb, *, dk, dv, tb=1024, sub=256, hp=2):
    L = vb.shape[0]
    tb = min(tb, L)
    nb = L // tb
    heads = vb.shape[1] // dv
    nc = tb // GLA_CHUNK
    blk = lambda i, rev: (nb - 1 - i) if rev else i
    opnd = lambda rev: pl.BlockSpec((None, tb, hp * dk), lambda h, i: (int(rev), blk(i, rev), h))
    vspec = lambda rev: pl.BlockSpec((tb, hp * dv), lambda h, i: (blk(i, rev), h))
    dspec = lambda rev: pl.BlockSpec((None, nc, hp * dk), lambda h, i: (int(rev), blk(i, rev), h))
    side = lambda rev: [opnd(rev), opnd(rev), opnd(rev), vspec(rev), dspec(rev)]
    return pl.pallas_call(
        functools.partial(_gla_body, dk=dk, dv=dv),
        out_shape=(jax.ShapeDtypeStruct((L, heads * dv), F32),) * 2,
        grid=(heads // hp, nb),
        in_specs=side(False) + side(True) + [pl.BlockSpec((3, sub, sub), lambda h, i: (0, 0, 0))],
        out_specs=(vspec(False), vspec(True)),
        scratch_shapes=[pltpu.VMEM((hp, dv, dk), F32), pltpu.VMEM((hp, dv, dk), F32)],
        compiler_params=_params(("parallel", "arbitrary")),
        name="gla",
    )(qe, ke, kd, vb, dec, qe, ke, kd, vb, dec, _chunk_matrices(sub))


def _gla_post_body(of_ref, ob_ref, r_ref, gn_ref, c_ref, *, dv):
    o = of_ref[...] + ob_ref[...]
    r = r_ref[...]
    gate = r * jax.nn.sigmoid(r)
    for hd in range(o.shape[1] // dv):
        sl = slice(hd * dv, (hd + 1) * dv)
        c_ref[:, sl] = (_rms(o[:, sl], gn_ref[...]) * gate[:, sl]).astype(BF16)


def _gla_post(o_f, o_b, p, gla_norm, *, r_off, tm=512):
    L, wc = o_f.shape
    dv = gla_norm.shape[0]
    row = lambda blk=0: pl.BlockSpec((tm, wc), lambda i: (i, blk))
    return pl.pallas_call(
        functools.partial(_gla_post_body, dv=dv),
        out_shape=jax.ShapeDtypeStruct((L, wc), BF16),
        grid=(L // tm,),
        in_specs=[row(), row(), row(r_off // wc), pl.BlockSpec((1, dv), lambda i: (0, 0))],
        out_specs=row(),
        compiler_params=_params(("parallel",)),
        name="gla_post",
    )(o_f, o_b, p, gla_norm.reshape(1, dv))


def _merge_body(u_ref, a_ref, z_ref, c_ref,
                wga_ref, wgb_ref, wgc_ref, bga_ref, bgb_ref, bgc_ref,
                wa_ref, wb_ref, wc_ref, m_ref, zb_ref):
    @pl.when(pl.program_id(1) == 0)
    def _():
        zb_ref[...] = _load_groups(z_ref).astype(BF16)

    u = u_ref[...]
    m = jax.nn.sigmoid(_dot(u, wga_ref[...]) + bga_ref[...]) * _dot(a_ref[...], wa_ref[...])
    m += jax.nn.sigmoid(_dot(u, wgb_ref[...]) + bgb_ref[...]) * _dot(zb_ref[...], wb_ref[...])
    m += jax.nn.sigmoid(_dot(u, wgc_ref[...]) + bgc_ref[...]) * _dot(c_ref[...], wc_ref[...])
    m_ref[...] = m.astype(BF16)


def _merge(u, a_pre, z, c_pre, w_gate, b_gate, w_a, w_b, w_c, *, tm=1024, tn=512):
    L, D = u.shape
    wa, wb, wc = w_a.shape[0], w_b.shape[0], w_c.shape[0]
    tm = min(tm, L)
    nj = D // tn
    row = lambda width: pl.BlockSpec((tm, width), lambda i, j: (i, 0))
    gate_w = lambda k: pl.BlockSpec((D, tn), lambda i, j: (0, j + k * nj))
    gate_b = lambda k: pl.BlockSpec((1, tn), lambda i, j: (0, j + k * nj))
    br_w = lambda width: pl.BlockSpec((width, tn), lambda i, j: (0, j))
    return pl.pallas_call(
        _merge_body,
        out_shape=jax.ShapeDtypeStruct((L, D), BF16),
        grid=(L // tm, nj),
        in_specs=[
            row(D), row(wa), pl.BlockSpec((wb // LANES, tm, LANES), lambda i, j: (0, i, 0)), row(wc),
            gate_w(0), gate_w(1), gate_w(2), gate_b(0), gate_b(1), gate_b(2),
            br_w(wa), br_w(wb), br_w(wc),
        ],
        out_specs=pl.BlockSpec((tm, tn), lambda i, j: (i, j)),
        scratch_shapes=[pltpu.VMEM((tm, wb), BF16)],
        compiler_params=_params(("parallel", "arbitrary")),
        name="merge",
    )(u, a_pre, z, c_pre, w_gate, w_gate, w_gate, b_gate, b_gate, b_gate, w_a, w_b, w_c)


def _out_proj_body(h_ref, m_ref, w_ref, o_ref, *, tn):
    m = m_ref[...]
    for c in range(o_ref.shape[1] // tn):
        sl = slice(c * tn, (c + 1) * tn)
        o_ref[:, sl] = h_ref[:, sl] + _dot(m, w_ref[:, sl])


def _out_proj(h, m, w_o, *, tm=512, tn=512):
    L, D = h.shape
    row = pl.BlockSpec((tm, D), lambda i: (i, 0))
    return pl.pallas_call(
        functools.partial(_out_proj_body, tn=tn),
        out_shape=jax.ShapeDtypeStruct((L, D), F32),
        grid=(L // tm,),
        in_specs=[row, row, pl.BlockSpec((D, D), lambda i: (0, 0), pipeline_mode=pl.Buffered(1))],
        out_specs=row,
        compiler_params=_params(("parallel",)),
        name="out_proj",
    )(h, m, w_o)


def _hyena(z0, g1, g2, skip, filt, consts):
    groups, L, _ = z0.shape
    n1 = DFT_N1
    n2 = 2 * L // n1
    k_time, sums = filt
    kf = _outer_fwd(k_time.reshape(k_time.shape[0], n1, n2, LANES), consts["fwd1"], sums)
    z = z0.reshape(groups, n1 // 2, n2, LANES)
    for order, gate in enumerate((g1, g2)):
        a = _outer_fwd(z, consts["fwd1"], t2=min(2 * SUBLANES, n2))
        d = _mid(a, kf, order, consts["fwd2"], consts["inv2"])
        z = _outer_inv(d, consts["inv1"], gate.reshape(z.shape), z, skip[order].reshape(1, groups * LANES))
    return z.reshape(groups, L, LANES)


def kernel(x, ffn1_norm, ffn1_w_gu, ffn1_w_down, mix_norm, w_in, b_in, conv_a, conv_b, hf_w1, hf_b1, hf_w2, hf_b2, hf_w3, hf_b3, hf_w_out, hf_freq, hy_skip, gk_w2, gk_b, gla_norm, w_br_a, w_br_b, w_br_c, w_o, ffn2_norm, ffn2_w_gu, ffn2_w_down, final_norm):
    bsz, L, D = x.shape
    depth = ffn1_norm.shape[0]
    wa = conv_a.shape[2]
    wb = conv_b.shape[3]
    wck = gk_w2.shape[3]
    dv = gla_norm.shape[1]
    wcv = GLA_HEADS * dv
    dk = wck // GLA_HEADS
    hid = hf_w2.shape[1]
    assert wa == wb, "the projection kernel tiles both conv branches with one section width"
    n_main = 3 * wa + 3 * wb + 2 * wck + 2 * wcv
    q_off, k_off, v_off, r_off = 0, wck, 2 * wck, 2 * wck + wcv
    lr_off = n_main
    gate_off = n_main + 2 * GLA_RANK
    n1 = DFT_N1
    n2 = 2 * L // n1

    consts = _dft_consts(n1, n2)
    deltas =jnp.linspace(math.log(HY_TARGET) / HY_SLOW_DECAY, math.log(HY_TARGET) / HY_FAST_DECAY, wb, dtype=F32)
    deltas2 = jnp.tile(deltas.reshape(1, wb), (1, HY_ORDER))

    outs = []
    for bi in range(bsz):
        h = x[bi]
        for l in range(depth):
            h = _ffn(h, ffn1_norm[l], _to_bf16(ffn1_w_gu, l), _to_bf16(ffn1_w_down, l, scale=0.5),
                     final_norm, final=False)

            w_lr = jnp.pad(w_in[l][:, lr_off:gate_off], ((0, 0), (0, LANES - 2 * GLA_RANK))).astype(BF16)
            b_lr = jnp.pad(b_in[l][lr_off:gate_off], (0, LANES - 2 * GLA_RANK)).reshape(1, LANES)
            p, lr, a_pre, z0, g1, g2, u = _proj(
                h, mix_norm[l], w_in[l][:, :n_main].astype(BF16), b_in[l][:n_main].reshape(1, n_main),
                w_lr, b_lr, conv_a[l], conv_b[l], wa=wa)

            w_out2 = hf_w_out[l].reshape(hid, HY_ORDER, 2, wb).transpose(2, 0, 1, 3).reshape(2, hid, HY_ORDER * wb)
            filt = _filters(L, hf_w1[l], hf_b1[l], hf_w2[l], hf_b2[l], hf_w3[l], hf_b3[l], hf_freq[l],
                            w_out2, deltas2)
            zb = _hyena(z0, g1, g2, hy_skip[l], filt, consts)

            w_both = jnp.zeros((LANES, 2 * wck), F32)
            w_both = w_both.at[:GLA_RANK, :wck].set(gk_w2[l, 0]).at[GLA_RANK:2 * GLA_RANK, wck:].set(gk_w2[l, 1])
            qe, ke, kd, dec, vb = _gla_gates(p, lr, w_both.astype(BF16), gk_b[l].reshape(1, 2 * wck),
                                             q_off=q_off, k_off=k_off, v_off=v_off, dk=dk, dv=dv,
                                             heads=GLA_HEADS)
            o_f, o_b = _gla(qe, ke, kd, dec, vb, dk=dk, dv=dv)

            c_pre = _gla_post(o_f, o_b, p, gla_norm[l], r_off=r_off)

            merged = _merge(u, a_pre, zb, c_pre,
                            w_in[l][:, gate_off:].astype(BF16), b_in[l][gate_off:].reshape(1, 3 * D),
                            _to_bf16(w_br_a, l), _to_bf16(w_br_b, l), _to_bf16(w_br_c, l))
            h = _out_proj(h, merged, _to_bf16(w_o, l))

            h = _ffn(h, ffn2_norm[l], _to_bf16(ffn2_w_gu, l), _to_bf16(ffn2_w_down, l, scale=0.5),
                     final_norm, final=(l == depth - 1))
        outs.append(h)
    return jnp.stack(outs, axis=0)
```

```python
import functools
import math

import numpy as np
import jax
import jax.numpy as jnp
from jax import lax
from jax.experimental import pallas as pl
from jax.experimental.pallas import tpu as pltpu

F32 = jnp.float32
BF16 = jnp.bfloat16
HIGHEST = lax.Precision.HIGHEST

EPS = 1e-5
GLA_HEADS = 4
GLA_RANK = 16
GLA_TAU = 16.0
GLA_CHUNK = 64
HY_ORDER = 2
HY_BANDS = 16
HY_FAST_DECAY = 0.3
HY_SLOW_DECAY = 1.5
HY_TARGET = 1e-2

LANES = 128
SUBLANES = 8
DFT_N1 = 256
ROW_CHUNK = 128
VMEM_LIMIT = 56 * 1024 * 1024


def _params(sem):
    return pltpu.CompilerParams(dimension_semantics=sem, vmem_limit_bytes=VMEM_LIMIT)


def _rms(x, g):
    return x * lax.rsqrt(jnp.mean(x * x, axis=-1, keepdims=True) + EPS) * g


def _dot(a, b):
    return jnp.dot(a, b, preferred_element_type=F32)


def _dot_hi(a, b):
    return jnp.dot(a, b, preferred_element_type=F32, precision=HIGHEST)


def _dot3_tn(a, b):
    a_hi, b_hi = a.astype(BF16), b.astype(BF16)
    a_lo = (a - a_hi.astype(F32)).astype(BF16)
    b_lo = (b - b_hi.astype(F32)).astype(BF16)
    tn = lambda x, y: lax.dot_general(x, y, (((0,), (0,)), ((), ())), preferred_element_type=F32)
    return tn(a_hi, b_hi) + tn(a_lo, b_hi) + tn(a_hi, b_lo)


def _cast_body(x_ref, o_ref, *, scale):
    x = x_ref[...]
    o_ref[...] = (x if scale == 1.0 else x * scale).astype(o_ref.dtype)


def _to_bf16(w, layer, cols=None, *, scale=1.0, block_bytes=4 << 20):
    _, rows, width = w.shape
    cols = width if cols is None else cols
    tr = rows
    while tr * cols * 4 > block_bytes and tr % 32 == 0:
        tr //= 2
    return pl.pallas_call(
        functools.partial(_cast_body, scale=scale),
        out_shape=jax.ShapeDtypeStruct((rows, cols), BF16),
        grid=(rows // tr,),
        in_specs=[pl.BlockSpec((None, tr, cols), lambda i: (layer, i, 0))],
        out_specs=pl.BlockSpec((tr, cols), lambda i: (i, 0)),
        compiler_params=_params(("parallel",)),
        name="to_bf16",
    )(w)


def _ffn_body(x_ref, g_ref, wg_ref, wu_ref, wd_ref, fn_ref, o_ref, xn_ref, *, final, sub):
    j = pl.program_id(1)

    row_chunks = [slice(r, r + ROW_CHUNK) for r in range(0, x_ref.shape[0], ROW_CHUNK)]

    @pl.when(j == 0)
    def _():
        for rs in row_chunks:
            x = x_ref[rs, :]
            xn_ref[rs, :] = _rms(x, g_ref[...]).astype(BF16)
            o_ref[rs, :] = x

    xn = xn_ref[...]
    hs = []
    for c in range(wg_ref.shape[1] // sub):
        sl = slice(c * sub, (c + 1) * sub)
        g = _dot(xn, wg_ref[:, sl])
        u = _dot(xn, wu_ref[:, sl])
        hs.append((g * jax.nn.sigmoid(g) * u).astype(BF16))
    o_ref[...] += _dot(jnp.concatenate(hs, axis=1), wd_ref[...])

    if final:
        @pl.when(j == pl.num_programs(1) - 1)
        def _():
            for rs in row_chunks:
                o_ref[rs, :] = _rms(o_ref[rs, :], fn_ref[...])


def _ffn(x, norm_g, w_gu, half_w_down, final_g, *, final, tm=1024, tf=512, sub=256):
    w_down = half_w_down
    L, D = x.shape
    F = w_down.shape[0]
    tm = min(tm, L)
    nf = F // tf
    return pl.pallas_call(
        functools.partial(_ffn_body, final=final, sub=min(sub, tf)),
        out_shape=jax.ShapeDtypeStruct((L, D), F32),
        grid=(L // tm, nf),
        in_specs=[
            pl.BlockSpec((tm, D), lambda i, j: (i, 0)),
            pl.BlockSpec((1, D), lambda i, j: (0, 0)),
            pl.BlockSpec((D, tf), lambda i, j: (0, j)),
            pl.BlockSpec((D, tf), lambda i, j: (0, j + nf)),
            pl.BlockSpec((tf, D), lambda i, j: (j, 0)),
            pl.BlockSpec((1, D), lambda i, j: (0, 0)),
        ],
        out_specs=pl.BlockSpec((tm, D), lambda i, j: (i, 0)),
        scratch_shapes=[pltpu.VMEM((tm, D), BF16)],
        compiler_params=_params(("parallel", "arbitrary")),
        name="ffn",
    )(x, norm_g.reshape(1, D), w_gu, w_gu, w_down, final_g.reshape(1, D))


def _conv3_inner(x, w):
    n = x.shape[0]
    y = pltpu.roll(x, 1, axis=0) * w[0:1] + x * w[1:2] + pltpu.roll(x, n - 1, axis=0) * w[2:3]
    return y[SUBLANES:n - SUBLANES]


def _proj_body(h_ref, hp_ref, hn_ref, g_ref, w_ref, b_ref, wlr_ref, blr_ref, caw_ref, cbw_ref,
               p_ref, lr_ref, a_ref, z_ref, g1_ref, g2_ref, u_ref, vb_ref, *, tn, wa):
    i = pl.program_id(0)
    tm = h_ref.shape[0]
    g = g_ref[...]
    xn = _rms(h_ref[...], g)
    xn_bf = xn.astype(BF16)
    u_ref[...] = xn_bf
    lr_ref[...] = _dot(xn_bf, wlr_ref[...]) + blr_ref[...]
    conv_w = 6 * wa
    v_lo = p_ref.shape[1] - vb_ref.shape[1]
    v_hi = v_lo + vb_ref.shape[1]
    for off in range(0, p_ref.shape[1] + vb_ref.shape[1], tn):
        src = slice(conv_w + off, conv_w + off + tn)
        y = _dot(xn_bf, w_ref[:, src]) + b_ref[:, src]
        if off < v_lo:
            p_ref[:, off:off + tn] = y
        elif off < v_hi:
            vb_ref[:, off - v_lo:off - v_lo + tn] = y.astype(BF16)
        else:
            p_ref[:, off - vb_ref.shape[1]:off - vb_ref.shape[1] + tn] = y

    xh = jnp.concatenate([_rms(hp_ref[...], g), xn, _rms(hn_ref[...], g)], axis=0).astype(BF16)
    row = lax.broadcasted_iota(jnp.int32, (tm + 2 * SUBLANES, 1), 0)
    has_prev = (i > 0).astype(F32)
    has_next = (i < pl.num_programs(0) - 1).astype(F32)
    valid = jnp.where(row < SUBLANES, has_prev, jnp.where(row >= tm + SUBLANES, has_next, 1.0))

    def sec(k):
        sl = slice(k * wa, (k + 1) * wa)
        return (_dot(xh, w_ref[:, sl]) + b_ref[:, sl]) * valid

    xa, ba, ca = sec(0), sec(1), sec(2)
    a_ref[...] = (ba[SUBLANES:tm + SUBLANES] * _conv3_inner(ca * xa, caw_ref[...])).astype(BF16)
    for k, o_ref in enumerate((z_ref, g1_ref, g2_ref)):
        _store_groups(o_ref, _conv3_inner(sec(3 + k), cbw_ref[k]))


def _proj(h, norm_g, w, b, w_lr, b_lr, conv_a, conv_b, *, wa, wv, tm=256, tn=512):
    L, D = h.shape
    conv_w = 6 * wa
    n_scan = w.shape[1] - conv_w - wv
    rb = tm // SUBLANES
    nrb = L // SUBLANES
    resident = lambda shape: pl.BlockSpec(shape, lambda i: (0,) * len(shape), pipeline_mode=pl.Buffered(1))
    grouped = pl.BlockSpec((wa // LANES, tm, LANES), lambda i: (0, i, 0))
    return pl.pallas_call(
        functools.partial(_proj_body, tn=tn, wa=wa),
        out_shape=(jax.ShapeDtypeStruct((L, n_scan), F32), jax.ShapeDtypeStruct((L, LANES), F32),
                   jax.ShapeDtypeStruct((L, wa), BF16))
        + (jax.ShapeDtypeStruct((wa // LANES, L, LANES), F32),) * 3
        + (jax.ShapeDtypeStruct((L, D), BF16), jax.ShapeDtypeStruct((L, wv), BF16)),
        grid=(L // tm,),
        in_specs=[
            pl.BlockSpec((tm, D), lambda i: (i, 0)),
            pl.BlockSpec((SUBLANES, D), lambda i: (jnp.maximum(i * rb - 1, 0), 0)),
            pl.BlockSpec((SUBLANES, D), lambda i: (jnp.minimum((i + 1) * rb, nrb - 1), 0)),
            resident((1, D)), resident(w.shape), resident((1, w.shape[1])),
            resident((D, LANES)), resident((1, LANES)), resident((3, wa)), resident((3, 3, wa)),
        ],
        out_specs=(pl.BlockSpec((tm, n_scan), lambda i: (i, 0)),
                   pl.BlockSpec((tm, LANES), lambda i: (i, 0)),
                   pl.BlockSpec((tm, wa), lambda i: (i, 0)), grouped, grouped, grouped,
                   pl.BlockSpec((tm, D), lambda i: (i, 0)), pl.BlockSpec((tm, wv), lambda i: (i, 0))),
        compiler_params=_params(("parallel",)),
        name="proj",
    )(h, h, h, norm_g.reshape(1, D), w, b, w_lr, b_lr, conv_a, conv_b)


def _store_groups(o_ref, x):
    for g in range(o_ref.shape[0]):
        o_ref[g] = x[:, g * LANES:(g + 1) * LANES]


def _load_groups(ref):
    return jnp.concatenate([ref[g] for g in range(ref.shape[0])], axis=1)


def _filter_body(w1t_ref, w1cs_ref, b1_ref, w2_ref, b2_ref, w3_ref, b3_ref, fr_ref, bands_ref, wo_ref, dl_ref,
                 k_ref, s_ref, *, half_tiles, tr, seq):
    i = pl.program_id(0)

    def lag_of(n):
        return jnp.where(n < seq, n, jnp.where(n == seq, 0, 2 * seq - n)).astype(F32)

    lag_row = lag_of(i * tr + lax.broadcasted_iota(jnp.int32, (1, tr), 1))
    ang = ((2.0 * math.pi / seq) * lag_row) * bands_ref[...]
    feats = jnp.concatenate([jnp.cos(ang), -jnp.sin(ang)], axis=0)
    fr = fr_ref[...]
    pre = w1t_ref[...] * (lag_row / max(seq - 1, 1)) + _dot_hi(w1cs_ref[...], feats) + b1_ref[...]
    h = jnp.sin(fr * pre)
    h = jnp.sin(fr * (_dot_hi(w2_ref[...], h) + b2_ref[...]))
    h = jnp.sin(fr * (_dot_hi(w3_ref[...], h) + b3_ref[...]))
    t = lag_of(i * tr + lax.broadcasted_iota(jnp.int32, (tr, 1), 0)) / max(seq - 1, 1)
    decay = jnp.exp(-t * jnp.abs(dl_ref[...]))
    hd = _dot3_tn(h, wo_ref[0]) * decay

    @pl.when(i % half_tiles == 0)
    def _():
        s_ref[...] = jnp.zeros_like(s_ref)

    s_ref[0] += jnp.sum(jnp.abs(hd), axis=0, keepdims=True)
    row = lax.broadcasted_iota(jnp.int32, hd.shape, 0)
    zero_row = jnp.where(i == half_tiles, 0, -1)
    _store_groups(k_ref, jnp.where(row == zero_row, 0.0, hd))


def _filters(seq, w1, b1, w2, b2, w3, b3, freq, w_out2, deltas2, *, tr=1024):
    n_rows = 2 * seq
    hid = w2.shape[0]
    width = w_out2.shape[2]
    half_tiles = n_rows // (2 * tr)
    bands = jnp.linspace(1e-4, HY_BANDS - 1, HY_BANDS, dtype=F32).reshape(HY_BANDS, 1)
    full = lambda shape: pl.BlockSpec(shape, lambda i: (0,) * len(shape))
    colv = lambda v: v.reshape(hid, 1)
    return pl.pallas_call(
        functools.partial(_filter_body, half_tiles=half_tiles, tr=tr, seq=seq),
        out_shape=(jax.ShapeDtypeStruct((width // LANES, n_rows, LANES), F32),
                   jax.ShapeDtypeStruct((2, 1, width), F32)),
        grid=(n_rows // tr,),
        in_specs=[
            full((hid, 1)), full((hid, 2 * HY_BANDS)), full((hid, 1)), full((hid, hid)), full((hid, 1)),
            full((hid, hid)), full((hid, 1)), full((hid, 1)), full((HY_BANDS, 1)),
            pl.BlockSpec((1, hid, width), lambda i: (i // half_tiles, 0, 0)),
            full((1, width)),
        ],
        out_specs=(pl.BlockSpec((width // LANES, tr, LANES), lambda i: (0, i, 0)),
                   pl.BlockSpec((1, 1, width), lambda i: (i // half_tiles, 0, 0))),
        compiler_params=_params(("arbitrary",)),
        name="hyena_filters",
    )(w1[0:1].T, w1[1:].T, colv(b1), w2.T, colv(b2), w3.T, colv(b3), colv(freq), bands, w_out2, deltas2)


def _dft_consts(n1, n2):
    n = n1 * n2
    k = np.arange(n1)
    m = np.arange(n2)
    ang2 = 2.0 * np.pi * ((m[:, None] * m[None, :]) % n2) / n2
    c2, s2 = np.cos(ang2), np.sin(ang2)
    as_bf = lambda a: jnp.asarray(np.asarray(a, np.float32).astype(jnp.bfloat16))
    time_index = k[None, None, :] * n2 + m[:, None, None]
    ang = 2.0 * np.pi * ((k[None, :, None] * time_index) % n) / n
    stage = np.concatenate([np.cos(ang), -np.sin(ang)], axis=1)
    half = n1 // 2
    return dict(
        fwd1=as_bf(stage),
        inv1=as_bf(stage.transpose(0, 2, 1)[:, :half] / n),
        fwd2=as_bf(np.block([[c2, s2], [-s2, c2]])),
        inv2=as_bf(np.block([[c2, -s2], [s2, c2]])),
    )


def _flatten_block(ref, buf):
    groups, n, t2, lanes = ref.shape
    for g in range(groups):
        for part in range(t2 // SUBLANES):
            rows = slice(part * SUBLANES, (part + 1) * SUBLANES)
            buf[g, part] = ref[g, :, rows, :].reshape(n * SUBLANES, lanes)


def _unflatten_block(buf, ref):
    groups, n, t2, lanes = ref.shape
    for g in range(groups):
        for part in range(t2 // SUBLANES):
            rows = slice(part * SUBLANES, (part + 1) * SUBLANES)
            ref[g, :, rows, :] = buf[g, part].reshape(n, SUBLANES, lanes)


def _load_strided(buf, s):
    part, sub = divmod(s, SUBLANES)
    n = buf.shape[2] // SUBLANES
    return jnp.concatenate([buf[g, part, pl.ds(sub, n, stride=SUBLANES), :] for g in range(buf.shape[0])],
                           axis=1)


def _store_strided(buf, s, x):
    part, sub = divmod(s, SUBLANES)
    n = buf.shape[2] // SUBLANES
    for g in range(buf.shape[0]):
        buf[g, part, pl.ds(sub, n, stride=SUBLANES), :] = x[:, g * LANES:(g + 1) * LANES]


def _flat_scratch(groups, n, t2, dtype=F32):
    return pltpu.VMEM((groups, t2 // SUBLANES, n * SUBLANES, LANES), dtype)


U32 = jnp.uint32


def _pack_complex(re, im):
    hi = lax.bitcast_convert_type(re.astype(BF16).astype(F32), U32)
    lo = lax.bitcast_convert_type(im.astype(BF16).astype(F32), U32)
    return hi | (lo >> 16)


def _unpack_complex(w):
    re = lax.bitcast_convert_type(w & jnp.uint32(0xFFFF0000), F32)
    im = lax.bitcast_convert_type(w << 16, F32)
    return re, im


def _outer_fwd_body(x_ref, f_ref, *rest, scaled):
    if scaled:
        s_ref, a_ref, xb, ab = rest
        inv = 1.0 / (s_ref[...] + EPS)
    else:
        a_ref, xb, ab = rest
    rows, t2 = x_ref.shape[1:3]
    n1 = a_ref.shape[1]
    _flatten_block(x_ref, xb)
    for s in range(t2):
        x = _load_strided(xb, s)
        if scaled:
            half = rows // 2
            x = jnp.concatenate([x[:half] * inv[0], x[half:] * inv[1]], axis=0)
        a = _dot(f_ref[s], x.astype(BF16))
        _store_strided(ab, s, _pack_complex(a[:n1], a[n1:]))
    _unflatten_block(ab, a_ref)


def _outer_fwd(x4, fwd1, sums=None, *, t2=SUBLANES, gt=4):
    groups, rows, n2, _ = x4.shape
    n1 = fwd1.shape[2]
    scaled = sums is not None
    blk = lambda r: pl.BlockSpec((gt, r, t2, LANES), lambda j, c: (c, 0, j, 0))
    args = [x4, fwd1]
    in_specs = [blk(rows), pl.BlockSpec((t2, 2 * n1, rows), lambda j, c: (j, 0, 0))]
    if scaled:
        args.append(sums)
        in_specs.append(pl.BlockSpec((2, 1, gt * LANES), lambda j, c: (0, 0, c)))
    return pl.pallas_call(
        functools.partial(_outer_fwd_body, scaled=scaled),
        out_shape=jax.ShapeDtypeStruct((groups, n1, n2, LANES), U32),
        grid=(n2 // t2, groups // gt),
        in_specs=in_specs,
        out_specs=blk(n1),
        scratch_shapes=[_flat_scratch(gt, rows, t2), _flat_scratch(gt, n1, t2, U32)],
        compiler_params=_params(("parallel", "parallel")),
        name="dft_outer_fwd",
    )(*args)


def _load_slab(ref, s):
    return jnp.concatenate([ref[g, s] for g in range(ref.shape[0])], axis=1)


def _store_slab(ref, s, x):
    for g in range(ref.shape[0]):
        ref[g, s] = x[:, g * LANES:(g + 1) * LANES]


def _mid_conv_body(a_ref, k_ref, f2_ref, g2_ref, d_ref):
    n2 = a_ref.shape[2]
    width = a_ref.shape[0] * LANES
    for s in range(a_ref.shape[1]):
        re, im = _unpack_complex(jnp.concatenate([_load_slab(a_ref, s), _load_slab(k_ref, s)], axis=1))
        x = _dot(f2_ref[...], jnp.concatenate([re, im], axis=0).astype(BF16))
        xr, xi, kr, ki = x[:n2, :width], x[n2:, :width], x[:n2, width:], x[n2:, width:]
        yr = xr * kr - xi * ki
        yi = xr * ki + xi * kr
        c = _dot(g2_ref[...], jnp.concatenate([yr, yi], axis=0).astype(BF16))
        _store_slab(d_ref, s, _pack_complex(c[:n2], c[n2:]))


def _mid(a, kf, order, fwd2, inv2, *, kb=8):
    groups, n1, n2, _ = a.shape
    slab = pl.BlockSpec((groups, kb, n2, LANES), lambda i: (0, i, 0, 0))
    kslab = pl.BlockSpec((groups, kb, n2, LANES), lambda i: (order, i, 0, 0))
    mat = pl.BlockSpec((2 * n2, 2 * n2), lambda i: (0, 0))
    return pl.pallas_call(
        _mid_conv_body,
        out_shape=jax.ShapeDtypeStruct(a.shape, U32),
        grid=(n1 // kb,),
        in_specs=[slab, kslab, mat, mat],
        out_specs=slab,
        compiler_params=_params(("parallel",)),
        name="hyena_mid_conv",
    )(a, kf, fwd2, inv2)


def _outer_inv_body(d_ref, e_ref, gate_ref, z_ref, skip_ref, o_ref, db, gb, zb, ob):
    skip = skip_ref[...]
    t2 = d_ref.shape[2]
    for ref, buf in ((d_ref, db), (gate_ref, gb), (z_ref, zb)):
        _flatten_block(ref, buf)
    for s in range(t2):
        dr, di = _unpack_complex(_load_strided(db, s))
        y = _dot(e_ref[s], jnp.concatenate([dr, di], axis=0).astype(BF16))
        _store_strided(ob, s, _load_strided(gb, s) * (y + skip * _load_strided(zb, s)))
    _unflatten_block(ob, o_ref)


def _outer_inv(d, inv1, gate4, z4, skip, *, t2=SUBLANES):
    groups, n1, n2, _ = d.shape
    half = n1 // 2
    blk = lambda r: pl.BlockSpec((groups, r, t2, LANES), lambda j: (0, 0, j, 0))
    return pl.pallas_call(
        _outer_inv_body,
        out_shape=jax.ShapeDtypeStruct((groups, half, n2, LANES), F32),
        grid=(n2 // t2,),
        in_specs=[blk(n1), pl.BlockSpec((t2, half, 2 * n1), lambda j: (j, 0, 0)), blk(half), blk(half),
                  pl.BlockSpec((1, groups * LANES), lambda j: (0, 0))],
        out_specs=blk(half),
        scratch_shapes=[_flat_scratch(groups, n1, t2, U32)] + [_flat_scratch(groups, half, t2)] * 3,
        compiler_params=_params(("parallel",)),
        name="dft_outer_inv",
    )(d, inv1, gate4, z4, skip)


def _log_sigmoid(x):
    return jnp.minimum(x, 0.0) - jnp.log1p(jnp.exp(-jnp.abs(x)))


_NT = (((1,), (1,)), ((), ()))
_TN = (((0,), (0,)), ((), ()))


def _chunk_matrices(sub):
    i = np.arange(sub)
    same = (i[:, None] // GLA_CHUNK) == (i[None, :] // GLA_CHUNK)
    fwd = same & (i[None, :] <= i[:, None])
    bwd = same & (i[None, :] >= i[:, None])
    return jnp.asarray(np.stack([fwd, bwd, same]), F32)


def _gla_gates_body(q_ref, k_ref, lr_ref, w_ref, b_ref, cm_ref, qe_ref, ke_ref, kd_ref, dec_ref, *, scale):
    tg, width = q_ref.shape
    sub = cm_ref.shape[1]
    chunk = GLA_CHUNK
    same = cm_ref[2].astype(BF16)
    for s in range(tg // sub):
        rs = slice(s * sub, (s + 1) * sub)
        g = _log_sigmoid(_dot(lr_ref[rs, :].astype(BF16), w_ref[...]) + b_ref[...]) / GLA_TAU
        g_hi = g.astype(BF16)
        g_lo = (g - g_hi.astype(F32)).astype(BF16)
        q = q_ref[rs, :] * scale
        k = k_ref[rs, :]
        for d in range(2):
            cs = slice(d * width, (d + 1) * width)
            cum = cm_ref[d].astype(BF16)
            bc = _dot(cum, g_hi[:, cs]) + _dot(cum, g_lo[:, cs])
            bl = _dot(same, g_hi[:, cs]) + _dot(same, g_lo[:, cs])
            qe_ref[d, rs, :] = (q * jnp.exp(bc)).astype(BF16)
            ke_ref[d, rs, :] = (k * jnp.exp(-bc)).astype(BF16)
            kd_ref[d, rs, :] = (k * jnp.exp(bl - bc)).astype(BF16)
            for c in range(sub // chunk):
                row = s * (sub // chunk) + c
                dec_ref[d, row:row + 1, :] = jnp.exp(bl[c * chunk:c * chunk + 1])


def _gla_gates(p, lr, w_both, b_both, *, q_off, k_off, dk, heads, tg=1024, sub=256):
    L = p.shape[0]
    tg = min(tg, L)
    wk = heads * dk
    opnd = jax.ShapeDtypeStruct((2, L, wk), BF16)
    row = lambda width, off: pl.BlockSpec((tg, width), lambda i: (i, off // width))
    both = pl.BlockSpec((2, tg, wk), lambda i: (0, i, 0))
    const = lambda shape: pl.BlockSpec(shape, lambda i: (0,) * len(shape))
    return pl.pallas_call(
        functools.partial(_gla_gates_body, scale=dk ** -0.5),
        out_shape=(opnd, opnd, opnd, jax.ShapeDtypeStruct((2, L // GLA_CHUNK, wk), F32)),
        grid=(L // tg,),
        in_specs=[row(wk, q_off), row(wk, k_off), row(LANES, 0),
                  const((LANES, 2 * wk)), const((1, 2 * wk)), const((3, sub, sub))],
        out_specs=(both, both, both, pl.BlockSpec((2, tg // GLA_CHUNK, wk), lambda i: (0, i, 0))),
        compiler_params=_params(("parallel",)),
        name="gla_gates",
    )(p, p, lr, w_both, b_both, _chunk_matrices(sub))


def _gla_direction(qe_ref, ke_ref, kd_ref, v_ref, dec_ref, mask, o_ref, st_ref, *, reverse, dk, dv):
    tb = qe_ref.shape[0]
    sub = mask.shape[0]
    chunk = GLA_CHUNK
    for hd in range(st_ref.shape[0]):
        ks, vs = slice(hd * dk, (hd + 1) * dk), slice(hd * dv, (hd + 1) * dv)
        intras = []
        for s in range(tb // sub):
            rs = slice(s * sub, (s + 1) * sub)
            a = lax.dot_general(qe_ref[rs, ks], ke_ref[rs, ks], _NT, preferred_element_type=F32)
            intra = _dot(jnp.where(mask > 0.5, a, 0.0).astype(BF16), v_ref[rs, vs])
            intras += [intra[c * chunk:(c + 1) * chunk] for c in range(sub // chunk)]
        st = st_ref[hd]
        chunks = range(tb // chunk)
        for c in (reversed(chunks) if reverse else chunks):
            sl = slice(c * chunk, (c + 1) * chunk)
            o_ref[sl, vs] = intras[c] + lax.dot_general(qe_ref[sl, ks], st.astype(BF16), _NT,
                                                        preferred_element_type=F32)
            upd = lax.dot_general(v_ref[sl, vs], kd_ref[sl, ks], _TN, preferred_element_type=F32)
            st = st * dec_ref[c:c + 1, ks] + upd
        st_ref[hd] = st


def _gla_body(qef, kef, kdf, vf, decf, qeb, keb, kdb, vb, decb, cm_ref, of_ref, ob_ref, sf_ref, sb_ref,
              *, dk, dv):
    @pl.when(pl.program_id(1) == 0)
    def _():
        sf_ref[...] = jnp.zeros_like(sf_ref)
        sb_ref[...] = jnp.zeros_like(sb_ref)

    _gla_direction(qef, kef, kdf, vf, decf, cm_ref[0], of_ref, sf_ref, reverse=False, dk=dk, dv=dv)
    _gla_direction(qeb, keb, kdb, vb, decb, cm_ref[1], ob_ref, sb_ref, reverse=True, dk=dk, dv=dv)


def _gla(qe, ke, kd, dec, vb, *, dk, dv, tb=1024, sub=256, hp=2):
    L = vb.shape[0]
    tb = min(tb, L)
    nb = L // tb
    heads = vb.shape[1] // dv
    nc = tb // GLA_CHUNK
    blk = lambda i, rev: (nb - 1 - i) if rev else i
    opnd = lambda rev: pl.BlockSpec((None, tb, hp * dk), lambda h, i: (int(rev), blk(i, rev), h))
    vspec = lambda rev: pl.BlockSpec((tb, hp * dv), lambda h, i: (blk(i, rev), h))
    dspec = lambda rev: pl.BlockSpec((None, nc, hp * dk), lambda h, i: (int(rev), blk(i, rev), h))
    side = lambda rev: [opnd(rev), opnd(rev), opnd(rev), vspec(rev), dspec(rev)]
    return pl.pallas_call(
        functools.partial(_gla_body, dk=dk, dv=dv),
        out_shape=(jax.ShapeDtypeStruct((L, heads * dv), F32),) * 2,
        grid=(heads // hp, nb),
        in_specs=side(False) + side(True) + [pl.BlockSpec((3, sub, sub), lambda h, i: (0, 0, 0))],
        out_specs=(vspec(False), vspec(True)),
        scratch_shapes=[pltpu.VMEM((hp, dv, dk), F32), pltpu.VMEM((hp, dv, dk), F32)],
        compiler_params=_params(("parallel", "arbitrary")),
        name="gla",
    )(qe, ke, kd, vb, dec, qe, ke, kd, vb, dec, _chunk_matrices(sub))


def _gla_post_body(of_ref, ob_ref, r_ref, gn_ref, c_ref, *, dv):
    o = of_ref[...] + ob_ref[...]
    r = r_ref[...]
    gate = r * jax.nn.sigmoid(r)
    for hd in range(o.shape[1] // dv):
        sl = slice(hd * dv, (hd + 1) * dv)
        c_ref[:, sl] = (_rms(o[:, sl], gn_ref[...]) * gate[:, sl]).astype(BF16)


def _gla_post(o_f, o_b, p, gla_norm, *, r_off, tm=512):
    L, wc = o_f.shape
    dv = gla_norm.shape[0]
    row = lambda blk=0: pl.BlockSpec((tm, wc), lambda i: (i, blk))
    return pl.pallas_call(
        functools.partial(_gla_post_body, dv=dv),
        out_shape=jax.ShapeDtypeStruct((L, wc), BF16),
        grid=(L // tm,),
        in_specs=[row(), row(), row(r_off // wc), pl.BlockSpec((1, dv), lambda i: (0, 0))],
        out_specs=row(),
        compiler_params=_params(("parallel",)),
        name="gla_post",
    )(o_f, o_b, p, gla_norm.reshape(1, dv))


def _merge_body(u_ref, a_ref, z_ref, c_ref,
                wga_ref, wgb_ref, wgc_ref, bga_ref, bgb_ref, bgc_ref,
                wa_ref, wb_ref, wc_ref, m_ref, zb_ref):
    @pl.when(pl.program_id(1) == 0)
    def _():
        zb_ref[...] = _load_groups(z_ref).astype(BF16)

    u = u_ref[...]
    m = jax.nn.sigmoid(_dot(u, wga_ref[...]) + bga_ref[...]) * _dot(a_ref[...], wa_ref[...])
    m += jax.nn.sigmoid(_dot(u, wgb_ref[...]) + bgb_ref[...]) * _dot(zb_ref[...], wb_ref[...])
    m += jax.nn.sigmoid(_dot(u, wgc_ref[...]) + bgc_ref[...]) * _dot(c_ref[...], wc_ref[...])
    m_ref[...] = m.astype(BF16)


def _merge(u, a_pre, z, c_pre, w_gate, b_gate, w_a, w_b, w_c, *, tm=1024, tn=512):
    L, D = u.shape
    wa, wb, wc = w_a.shape[0], w_b.shape[0], w_c.shape[0]
    tm = min(tm, L)
    nj = D // tn
    row = lambda width: pl.BlockSpec((tm, width), lambda i, j: (i, 0))
    gate_w = lambda k: pl.BlockSpec((D, tn), lambda i, j: (0, j + k * nj))
    gate_b = lambda k: pl.BlockSpec((1, tn), lambda i, j: (0, j + k * nj))
    br_w = lambda width: pl.BlockSpec((width, tn), lambda i, j: (0, j))
    return pl.pallas_call(
        _merge_body,
        out_shape=jax.ShapeDtypeStruct((L, D), BF16),
        grid=(L // tm, nj),
        in_specs=[
            row(D), row(wa), pl.BlockSpec((wb // LANES, tm, LANES), lambda i, j: (0, i, 0)), row(wc),
            gate_w(0), gate_w(1), gate_w(2), gate_b(0), gate_b(1), gate_b(2),
            br_w(wa), br_w(wb), br_w(wc),
        ],
        out_specs=pl.BlockSpec((tm, tn), lambda i, j: (i, j)),
        scratch_shapes=[pltpu.VMEM((tm, wb), BF16)],
        compiler_params=_params(("parallel", "arbitrary")),
        name="merge",
    )(u, a_pre, z, c_pre, w_gate, w_gate, w_gate, b_gate, b_gate, b_gate, w_a, w_b, w_c)


def _out_proj_body(h_ref, m_ref, w_ref, o_ref, *, tn):
    m = m_ref[...]
    for c in range(o_ref.shape[1] // tn):
        sl = slice(c * tn, (c + 1) * tn)
        o_ref[:, sl] = h_ref[:, sl] + _dot(m, w_ref[:, sl])


def _out_proj(h, m, w_o, *, tm=512, tn=512):
    L, D = h.shape
    row = pl.BlockSpec((tm, D), lambda i: (i, 0))
    return pl.pallas_call(
        functools.partial(_out_proj_body, tn=tn),
        out_shape=jax.ShapeDtypeStruct((L, D), F32),
        grid=(L // tm,),
        in_specs=[row, row, pl.BlockSpec((D, D), lambda i: (0, 0), pipeline_mode=pl.Buffered(1))],
        out_specs=row,
        compiler_params=_params(("parallel",)),
        name="out_proj",
    )(h, m, w_o)


def _hyena(z0, g1, g2, skip, filt, consts):
    groups, L, _ = z0.shape
    n1 = DFT_N1
    n2 = 2 * L // n1
    k_time, sums = filt
    kf = _outer_fwd(k_time.reshape(k_time.shape[0], n1, n2, LANES), consts["fwd1"], sums)
    z = z0.reshape(groups, n1 // 2, n2, LANES)
    for order, gate in enumerate((g1, g2)):
        a = _outer_fwd(z, consts["fwd1"], t2=min(2 * SUBLANES, n2))
        d = _mid(a, kf, order, consts["fwd2"], consts["inv2"])
        z = _outer_inv(d, consts["inv1"], gate.reshape(z.shape), z, skip[order].reshape(1, groups * LANES))
    return z.reshape(groups, L, LANES)


def kernel(x, ffn1_norm, ffn1_w_gu, ffn1_w_down, mix_norm, w_in, b_in, conv_a, conv_b, hf_w1, hf_b1, hf_w2, hf_b2, hf_w3, hf_b3, hf_w_out, hf_freq, hy_skip, gk_w2, gk_b, gla_norm, w_br_a, w_br_b, w_br_c, w_o, ffn2_norm, ffn2_w_gu, ffn2_w_down, final_norm):
    bsz, L, D = x.shape
    depth = ffn1_norm.shape[0]
    wa = conv_a.shape[2]
    wb = conv_b.shape[3]
    wck = gk_w2.shape[3]
    dv = gla_norm.shape[1]
    wcv = GLA_HEADS * dv
    dk = wck // GLA_HEADS
    hid = hf_w2.shape[1]
    assert wa == wb, "the projection kernel tiles both conv branches with one section width"
    n_main = 3 * wa + 3 * wb + 2 * wck + 2 * wcv
    q_off, k_off, r_off = 0, wck, 2 * wck
    lr_off = n_main
    gate_off = n_main + 2 * GLA_RANK
    n1 = DFT_N1
    n2 = 2 * L // n1

    consts = _dft_consts(n1, n2)
    deltas =jnp.linspace(math.log(HY_TARGET) / HY_SLOW_DECAY, math.log(HY_TARGET) / HY_FAST_DECAY, wb, dtype=F32)
    deltas2 = jnp.tile(deltas.reshape(1, wb), (1, HY_ORDER))

    outs = []
    for bi in range(bsz):
        h = x[bi]
        for l in range(depth):
            h = _ffn(h, ffn1_norm[l], _to_bf16(ffn1_w_gu, l), _to_bf16(ffn1_w_down, l, scale=0.5),
                     final_norm, final=False)

            w_lr = jnp.pad(w_in[l][:, lr_off:gate_off], ((0, 0), (0, LANES - 2 * GLA_RANK))).astype(BF16)
            b_lr = jnp.pad(b_in[l][lr_off:gate_off], (0, LANES - 2 * GLA_RANK)).reshape(1, LANES)
            p, lr, a_pre, z0, g1, g2, u, vb = _proj(
                h, mix_norm[l], w_in[l][:, :n_main].astype(BF16), b_in[l][:n_main].reshape(1, n_main),
                w_lr, b_lr, conv_a[l], conv_b[l], wa=wa, wv=wcv)

            w_out2 = hf_w_out[l].reshape(hid, HY_ORDER, 2, wb).transpose(2, 0, 1, 3).reshape(2, hid, HY_ORDER * wb)
            filt = _filters(L, hf_w1[l], hf_b1[l], hf_w2[l], hf_b2[l], hf_w3[l], hf_b3[l], hf_freq[l],
                            w_out2, deltas2)
            zb = _hyena(z0, g1, g2, hy_skip[l], filt, consts)

            w_both = jnp.zeros((LANES, 2 * wck), F32)
            w_both = w_both.at[:GLA_RANK, :wck].set(gk_w2[l, 0]).at[GLA_RANK:2 * GLA_RANK, wck:].set(gk_w2[l, 1])
            qe, ke, kd, dec = _gla_gates(p, lr, w_both.astype(BF16), gk_b[l].reshape(1, 2 * wck),
                                         q_off=q_off, k_off=k_off, dk=dk, heads=GLA_HEADS)
            o_f, o_b = _gla(qe, ke, kd, dec, vb, dk=dk, dv=dv)

            c_pre = _gla_post(o_f, o_b, p, gla_norm[l], r_off=r_off)

            merged = _merge(u, a_pre, zb, c_pre,
                            w_in[l][:, gate_off:].astype(BF16), b_in[l][gate_off:].reshape(1, 3 * D),
                            _to_bf16(w_br_a, l), _to_bf16(w_br_b, l), _to_bf16(w_br_c, l))
            h = _out_proj(h, merged, _to_bf16(w_o, l))

            h = _ffn(h, ffn2_norm[l], _to_bf16(ffn2_w_gu, l), _to_bf16(ffn2_w_down, l, scale=0.5),
                     final_norm, final=(l == depth - 1))
        outs.append(h)
    return jnp.stack(outs, axis=0)
```

```python
import functools
import math

import numpy as np
import jax
import jax.numpy as jnp
from jax import lax
from jax.experimental import pallas as pl
from jax.experimental.pallas import tpu as pltpu

F32 = jnp.float32
BF16 = jnp.bfloat16
HIGHEST = lax.Precision.HIGHEST

EPS = 1e-5
GLA_HEADS = 4
GLA_RANK = 16
GLA_TAU = 16.0
GLA_CHUNK = 64
HY_ORDER = 2
HY_BANDS = 16
HY_FAST_DECAY = 0.3
HY_SLOW_DECAY = 1.5
HY_TARGET = 1e-2

LANES = 128
SUBLANES = 8
DFT_N1 = 256
ROW_CHUNK = 128
VMEM_LIMIT = 56 * 1024 * 1024


def _params(sem):
    return pltpu.CompilerParams(dimension_semantics=sem, vmem_limit_bytes=VMEM_LIMIT)


def _rms(x, g):
    return x * lax.rsqrt(jnp.mean(x * x, axis=-1, keepdims=True) + EPS) * g


def _dot(a, b):
    return jnp.dot(a, b, preferred_element_type=F32)


def _dot_hi(a, b):
    return jnp.dot(a, b, preferred_element_type=F32, precision=HIGHEST)


def _dot3_tn(a, b):
    a_hi, b_hi = a.astype(BF16), b.astype(BF16)
    a_lo = (a - a_hi.astype(F32)).astype(BF16)
    b_lo = (b - b_hi.astype(F32)).astype(BF16)
    tn = lambda x, y: lax.dot_general(x, y, (((0,), (0,)), ((), ())), preferred_element_type=F32)
    return tn(a_hi, b_hi) + tn(a_lo, b_hi) + tn(a_hi, b_lo)


def _cast_body(x_ref, o_ref, *, scale):
    x = x_ref[...]
    o_ref[...] = (x if scale == 1.0 else x * scale).astype(o_ref.dtype)


def _to_bf16(w, layer, cols=None, *, scale=1.0, block_bytes=4 << 20):
    _, rows, width = w.shape
    cols = width if cols is None else cols
    tr = rows
    while tr * cols * 4 > block_bytes and tr % 32 == 0:
        tr //= 2
    return pl.pallas_call(
        functools.partial(_cast_body, scale=scale),
        out_shape=jax.ShapeDtypeStruct((rows, cols), BF16),
        grid=(rows // tr,),
        in_specs=[pl.BlockSpec((None, tr, cols), lambda i: (layer, i, 0))],
        out_specs=pl.BlockSpec((tr, cols), lambda i: (i, 0)),
        compiler_params=_params(("parallel",)),
        name="to_bf16",
    )(w)


def _ffn_body(x_ref, g_ref, wg_ref, wu_ref, wd_ref, fn_ref, o_ref, xn_ref, *, final, sub):
    j = pl.program_id(1)

    row_chunks = [slice(r, r + ROW_CHUNK) for r in range(0, x_ref.shape[0], ROW_CHUNK)]

    @pl.when(j == 0)
    def _():
        for rs in row_chunks:
            x = x_ref[rs, :]
            xn_ref[rs, :] = _rms(x, g_ref[...]).astype(BF16)
            o_ref[rs, :] = x

    xn = xn_ref[...]
    hs = []
    for c in range(wg_ref.shape[1] // sub):
        sl = slice(c * sub, (c + 1) * sub)
        g = _dot(xn, wg_ref[:, sl])
        u = _dot(xn, wu_ref[:, sl])
        hs.append((g * jax.nn.sigmoid(g) * u).astype(BF16))
    o_ref[...] += _dot(jnp.concatenate(hs, axis=1), wd_ref[...])

    if final:
        @pl.when(j == pl.num_programs(1) - 1)
        def _():
            for rs in row_chunks:
                o_ref[rs, :] = _rms(o_ref[rs, :], fn_ref[...])


def _ffn(x, norm_g, w_gu, half_w_down, final_g, *, final, tm=1024, tf=512, sub=256):
    w_down = half_w_down
    L, D = x.shape
    F = w_down.shape[0]
    tm = min(tm, L)
    nf = F // tf
    return pl.pallas_call(
        functools.partial(_ffn_body, final=final, sub=min(sub, tf)),
        out_shape=jax.ShapeDtypeStruct((L, D), F32),
        grid=(L // tm, nf),
        in_specs=[
            pl.BlockSpec((tm, D), lambda i, j: (i, 0)),
            pl.BlockSpec((1, D), lambda i, j: (0, 0)),
            pl.BlockSpec((D, tf), lambda i, j: (0, j)),
            pl.BlockSpec((D, tf), lambda i, j: (0, j + nf)),
            pl.BlockSpec((tf, D), lambda i, j: (j, 0)),
            pl.BlockSpec((1, D), lambda i, j: (0, 0)),
        ],
        out_specs=pl.BlockSpec((tm, D), lambda i, j: (i, 0)),
        scratch_shapes=[pltpu.VMEM((tm, D), BF16)],
        compiler_params=_params(("parallel", "arbitrary")),
        name="ffn",
    )(x, norm_g.reshape(1, D), w_gu, w_gu, w_down, final_g.reshape(1, D))


def _conv3_inner(x, w):
    n = x.shape[0]
    y = pltpu.roll(x, 1, axis=0) * w[0:1] + x * w[1:2] + pltpu.roll(x, n - 1, axis=0) * w[2:3]
    return y[SUBLANES:n - SUBLANES]


def _proj_body(h_ref, hp_ref, hn_ref, g_ref, w_ref, b_ref, wlr_ref, blr_ref, caw_ref, cbw_ref,
               p_ref, lr_ref, a_ref, z_ref, g1_ref, g2_ref, u_ref, vb_ref, *, tn, wa):
    i = pl.program_id(0)
    tm = h_ref.shape[0]
    g = g_ref[...]
    xn = _rms(h_ref[...], g)
    xn_bf = xn.astype(BF16)
    u_ref[...] = xn_bf
    lr_ref[...] = _dot(xn_bf, wlr_ref[...]) + blr_ref[...]
    conv_w = 6 * wa
    v_lo = p_ref.shape[1] - vb_ref.shape[1]
    v_hi = v_lo + vb_ref.shape[1]
    for off in range(0, p_ref.shape[1] + vb_ref.shape[1], tn):
        src = slice(conv_w + off, conv_w + off + tn)
        y = _dot(xn_bf, w_ref[:, src]) + b_ref[:, src]
        if off < v_lo:
            p_ref[:, off:off + tn] = y
        elif off < v_hi:
            vb_ref[:, off - v_lo:off - v_lo + tn] = y.astype(BF16)
        else:
            p_ref[:, off - vb_ref.shape[1]:off - vb_ref.shape[1] + tn] = y

    xh = jnp.concatenate([_rms(hp_ref[...], g), xn, _rms(hn_ref[...], g)], axis=0).astype(BF16)
    row = lax.broadcasted_iota(jnp.int32, (tm + 2 * SUBLANES, 1), 0)
    has_prev = (i > 0).astype(F32)
    has_next = (i < pl.num_programs(0) - 1).astype(F32)
    valid = jnp.where(row < SUBLANES, has_prev, jnp.where(row >= tm + SUBLANES, has_next, 1.0))

    def sec(k):
        sl = slice(k * wa, (k + 1) * wa)
        return (_dot(xh, w_ref[:, sl]) + b_ref[:, sl]) * valid

    xa, ba, ca = sec(0), sec(1), sec(2)
    a_ref[...] = (ba[SUBLANES:tm + SUBLANES] * _conv3_inner(ca * xa, caw_ref[...])).astype(BF16)
    for k, o_ref in enumerate((z_ref, g1_ref, g2_ref)):
        _store_groups(o_ref, _conv3_inner(sec(3 + k), cbw_ref[k]))


def _proj(h, norm_g, w, b, w_lr, b_lr, conv_a, conv_b, *, wa, wv, tm=256, tn=512):
    L, D = h.shape
    conv_w = 6 * wa
    n_scan = w.shape[1] - conv_w - wv
    rb = tm // SUBLANES
    nrb = L // SUBLANES
    resident = lambda shape: pl.BlockSpec(shape, lambda i: (0,) * len(shape), pipeline_mode=pl.Buffered(1))
    grouped = pl.BlockSpec((wa // LANES, tm, LANES), lambda i: (0, i, 0))
    return pl.pallas_call(
        functools.partial(_proj_body, tn=tn, wa=wa),
        out_shape=(jax.ShapeDtypeStruct((L, n_scan), F32), jax.ShapeDtypeStruct((L, LANES), F32),
                   jax.ShapeDtypeStruct((L, wa), BF16))
        + (jax.ShapeDtypeStruct((wa // LANES, L, LANES), F32),) * 3
        + (jax.ShapeDtypeStruct((L, D), BF16), jax.ShapeDtypeStruct((L, wv), BF16)),
        grid=(L // tm,),
        in_specs=[
            pl.BlockSpec((tm, D), lambda i: (i, 0)),
            pl.BlockSpec((SUBLANES, D), lambda i: (jnp.maximum(i * rb - 1, 0), 0)),
            pl.BlockSpec((SUBLANES, D), lambda i: (jnp.minimum((i + 1) * rb, nrb - 1), 0)),
            resident((1, D)), resident(w.shape), resident((1, w.shape[1])),
            resident((D, LANES)), resident((1, LANES)), resident((3, wa)), resident((3, 3, wa)),
        ],
        out_specs=(pl.BlockSpec((tm, n_scan), lambda i: (i, 0)),
                   pl.BlockSpec((tm, LANES), lambda i: (i, 0)),
                   pl.BlockSpec((tm, wa), lambda i: (i, 0)), grouped, grouped, grouped,
                   pl.BlockSpec((tm, D), lambda i: (i, 0)), pl.BlockSpec((tm, wv), lambda i: (i, 0))),
        compiler_params=_params(("parallel",)),
        name="proj",
    )(h, h, h, norm_g.reshape(1, D), w, b, w_lr, b_lr, conv_a, conv_b)


def _store_groups(o_ref, x):
    for g in range(o_ref.shape[0]):
        o_ref[g] = x[:, g * LANES:(g + 1) * LANES]


def _load_groups(ref):
    return jnp.concatenate([ref[g] for g in range(ref.shape[0])], axis=1)


def _filter_body(w1t_ref, w1cs_ref, b1_ref, w2_ref, b2_ref, w3_ref, b3_ref, fr_ref, bands_ref, wo_ref, dl_ref,
                 k_ref, s_ref, *, half_tiles, tr, seq):
    i = pl.program_id(0)

    def lag_of(n):
        return jnp.where(n < seq, n, jnp.where(n == seq, 0, 2 * seq - n)).astype(F32)

    lag_row = lag_of(i * tr + lax.broadcasted_iota(jnp.int32, (1, tr), 1))
    ang = ((2.0 * math.pi / seq) * lag_row) * bands_ref[...]
    feats = jnp.concatenate([jnp.cos(ang), -jnp.sin(ang)], axis=0)
    fr = fr_ref[...]
    pre = w1t_ref[...] * (lag_row / max(seq - 1, 1)) + _dot_hi(w1cs_ref[...], feats) + b1_ref[...]
    h = jnp.sin(fr * pre)
    h = jnp.sin(fr * (_dot_hi(w2_ref[...], h) + b2_ref[...]))
    h = jnp.sin(fr * (_dot_hi(w3_ref[...], h) + b3_ref[...]))
    t = lag_of(i * tr + lax.broadcasted_iota(jnp.int32, (tr, 1), 0)) / max(seq - 1, 1)
    decay = jnp.exp(-t * jnp.abs(dl_ref[...]))
    hd = _dot3_tn(h, wo_ref[0]) * decay

    @pl.when(i % half_tiles == 0)
    def _():
        s_ref[...] = jnp.zeros_like(s_ref)

    s_ref[0] += jnp.sum(jnp.abs(hd), axis=0, keepdims=True)
    row = lax.broadcasted_iota(jnp.int32, hd.shape, 0)
    zero_row = jnp.where(i == half_tiles, 0, -1)
    _store_groups(k_ref, jnp.where(row == zero_row, 0.0, hd))


def _filters(seq, w1, b1, w2, b2, w3, b3, freq, w_out2, deltas2, *, tr=1024):
    n_rows = 2 * seq
    hid = w2.shape[0]
    width = w_out2.shape[2]
    half_tiles = n_rows // (2 * tr)
    bands = jnp.linspace(1e-4, HY_BANDS - 1, HY_BANDS, dtype=F32).reshape(HY_BANDS, 1)
    full = lambda shape: pl.BlockSpec(shape, lambda i: (0,) * len(shape))
    colv = lambda v: v.reshape(hid, 1)
    return pl.pallas_call(
        functools.partial(_filter_body, half_tiles=half_tiles, tr=tr, seq=seq),
        out_shape=(jax.ShapeDtypeStruct((width // LANES, n_rows, LANES), F32),
                   jax.ShapeDtypeStruct((2, 1, width), F32)),
        grid=(n_rows // tr,),
        in_specs=[
            full((hid, 1)), full((hid, 2 * HY_BANDS)), full((hid, 1)), full((hid, hid)), full((hid, 1)),
            full((hid, hid)), full((hid, 1)), full((hid, 1)), full((HY_BANDS, 1)),
            pl.BlockSpec((1, hid, width), lambda i: (i // half_tiles, 0, 0)),
            full((1, width)),
        ],
        out_specs=(pl.BlockSpec((width // LANES, tr, LANES), lambda i: (0, i, 0)),
                   pl.BlockSpec((1, 1, width), lambda i: (i // half_tiles, 0, 0))),
        compiler_params=_params(("arbitrary",)),
        name="hyena_filters",
    )(w1[0:1].T, w1[1:].T, colv(b1), w2.T, colv(b2), w3.T, colv(b3), colv(freq), bands, w_out2, deltas2)


def _dft_consts(n1, n2):
    n = n1 * n2
    k = np.arange(n1)
    m = np.arange(n2)
    ang2 = 2.0 * np.pi * ((m[:, None] * m[None, :]) % n2) / n2
    c2, s2 = np.cos(ang2), np.sin(ang2)
    as_bf = lambda a: jnp.asarray(np.asarray(a, np.float32).astype(jnp.bfloat16))
    time_index = k[None, None, :] * n2 + m[:, None, None]
    ang = 2.0 * np.pi * ((k[None, :, None] * time_index) % n) / n
    stage = np.concatenate([np.cos(ang), -np.sin(ang)], axis=1)
    half = n1 // 2
    return dict(
        fwd1=as_bf(stage),
        inv1=as_bf(stage.transpose(0, 2, 1)[:, :half] / n),
        fwd2=as_bf(np.block([[c2, s2], [-s2, c2]])),
        inv2=as_bf(np.block([[c2, -s2], [s2, c2]])),
    )


def _flatten_block(ref, buf):
    groups, n, t2, lanes = ref.shape
    for g in range(groups):
        for part in range(t2 // SUBLANES):
            rows = slice(part * SUBLANES, (part + 1) * SUBLANES)
            buf[g, part] = ref[g, :, rows, :].reshape(n * SUBLANES, lanes)


def _unflatten_block(buf, ref):
    groups, n, t2, lanes = ref.shape
    for g in range(groups):
        for part in range(t2 // SUBLANES):
            rows = slice(part * SUBLANES, (part + 1) * SUBLANES)
            ref[g, :, rows, :] = buf[g, part].reshape(n, SUBLANES, lanes)


def _load_strided(buf, s):
    part, sub = divmod(s, SUBLANES)
    n = buf.shape[2] // SUBLANES
    return jnp.concatenate([buf[g, part, pl.ds(sub, n, stride=SUBLANES), :] for g in range(buf.shape[0])],
                           axis=1)


def _store_strided(buf, s, x):
    part, sub = divmod(s, SUBLANES)
    n = buf.shape[2] // SUBLANES
    for g in range(buf.shape[0]):
        buf[g, part, pl.ds(sub, n, stride=SUBLANES), :] = x[:, g * LANES:(g + 1) * LANES]


def _flat_scratch(groups, n, t2, dtype=F32):
    return pltpu.VMEM((groups, t2 // SUBLANES, n * SUBLANES, LANES), dtype)


U32 = jnp.uint32


def _pack_complex(re, im):
    hi = lax.bitcast_convert_type(re.astype(BF16).astype(F32), U32)
    lo = lax.bitcast_convert_type(im.astype(BF16).astype(F32), U32)
    return hi | (lo >> 16)


def _unpack_complex(w):
    re = lax.bitcast_convert_type(w & jnp.uint32(0xFFFF0000), F32)
    im = lax.bitcast_convert_type(w << 16, F32)
    return re, im


def _outer_fwd_body(x_ref, f_ref, *rest, scaled):
    if scaled:
        s_ref, a_ref, xb, ab = rest
        inv = 1.0 / (s_ref[...] + EPS)
    else:
        a_ref, xb, ab = rest
    rows, t2 = x_ref.shape[1:3]
    n1 = a_ref.shape[1]
    _flatten_block(x_ref, xb)
    for s in range(t2):
        x = _load_strided(xb, s)
        if scaled:
            half = rows // 2
            x = jnp.concatenate([x[:half] * inv[0], x[half:] * inv[1]], axis=0)
        a = _dot(f_ref[s], x.astype(BF16))
        _store_strided(ab, s, _pack_complex(a[:n1], a[n1:]))
    _unflatten_block(ab, a_ref)


def _outer_fwd(x4, fwd1, sums=None, *, t2=SUBLANES, gt=4):
    groups, rows, n2, _ = x4.shape
    n1 = fwd1.shape[2]
    scaled = sums is not None
    blk = lambda r: pl.BlockSpec((gt, r, t2, LANES), lambda j, c: (c, 0, j, 0))
    args = [x4, fwd1]
    in_specs = [blk(rows), pl.BlockSpec((t2, 2 * n1, rows), lambda j, c: (j, 0, 0))]
    if scaled:
        args.append(sums)
        in_specs.append(pl.BlockSpec((2, 1, gt * LANES), lambda j, c: (0, 0, c)))
    return pl.pallas_call(
        functools.partial(_outer_fwd_body, scaled=scaled),
        out_shape=jax.ShapeDtypeStruct((groups, n1, n2, LANES), U32),
        grid=(n2 // t2, groups // gt),
        in_specs=in_specs,
        out_specs=blk(n1),
        scratch_shapes=[_flat_scratch(gt, rows, t2), _flat_scratch(gt, n1, t2, U32)],
        compiler_params=_params(("parallel", "parallel")),
        name="dft_outer_fwd",
    )(*args)


def _load_slab(ref, s):
    return jnp.concatenate([ref[g, s] for g in range(ref.shape[0])], axis=1)


def _store_slab(ref, s, x):
    for g in range(ref.shape[0]):
        ref[g, s] = x[:, g * LANES:(g + 1) * LANES]


def _mid_conv_body(a_ref, k_ref, f2_ref, g2_ref, d_ref):
    n2 = a_ref.shape[2]
    width = a_ref.shape[0] * LANES
    for s in range(a_ref.shape[1]):
        re, im = _unpack_complex(jnp.concatenate([_load_slab(a_ref, s), _load_slab(k_ref, s)], axis=1))
        x = _dot(f2_ref[...], jnp.concatenate([re, im], axis=0).astype(BF16))
        xr, xi, kr, ki = x[:n2, :width], x[n2:, :width], x[:n2, width:], x[n2:, width:]
        yr = xr * kr - xi * ki
        yi = xr * ki + xi * kr
        c = _dot(g2_ref[...], jnp.concatenate([yr, yi], axis=0).astype(BF16))
        _store_slab(d_ref, s, _pack_complex(c[:n2], c[n2:]))


def _mid(a, kf, order, fwd2, inv2, *, kb=8):
    groups, n1, n2, _ = a.shape
    slab = pl.BlockSpec((groups, kb, n2, LANES), lambda i: (0, i, 0, 0))
    kslab = pl.BlockSpec((groups, kb, n2, LANES), lambda i: (order, i, 0, 0))
    mat = pl.BlockSpec((2 * n2, 2 * n2), lambda i: (0, 0))
    return pl.pallas_call(
        _mid_conv_body,
        out_shape=jax.ShapeDtypeStruct(a.shape, U32),
        grid=(n1 // kb,),
        in_specs=[slab, kslab, mat, mat],
        out_specs=slab,
        compiler_params=_params(("parallel",)),
        name="hyena_mid_conv",
    )(a, kf, fwd2, inv2)


def _outer_inv_body(d_ref, e_ref, gate_ref, z_ref, skip_ref, o_ref, db, gb, zb, ob):
    skip = skip_ref[...]
    t2 = d_ref.shape[2]
    for ref, buf in ((d_ref, db), (gate_ref, gb), (z_ref, zb)):
        _flatten_block(ref, buf)
    for s in range(t2):
        dr, di = _unpack_complex(_load_strided(db, s))
        y = _dot(e_ref[s], jnp.concatenate([dr, di], axis=0).astype(BF16))
        _store_strided(ob, s, _load_strided(gb, s) * (y + skip * _load_strided(zb, s)))
    _unflatten_block(ob, o_ref)


def _outer_inv(d, inv1, gate4, z4, skip, *, t2=SUBLANES):
    groups, n1, n2, _ = d.shape
    half = n1 // 2
    blk = lambda r: pl.BlockSpec((groups, r, t2, LANES), lambda j: (0, 0, j, 0))
    return pl.pallas_call(
        _outer_inv_body,
        out_shape=jax.ShapeDtypeStruct((groups, half, n2, LANES), F32),
        grid=(n2 // t2,),
        in_specs=[blk(n1), pl.BlockSpec((t2, half, 2 * n1), lambda j: (j, 0, 0)), blk(half), blk(half),
                  pl.BlockSpec((1, groups * LANES), lambda j: (0, 0))],
        out_specs=blk(half),
        scratch_shapes=[_flat_scratch(groups, n1, t2, U32)] + [_flat_scratch(groups, half, t2)] * 3,
        compiler_params=_params(("parallel",)),
        name="dft_outer_inv",
    )(d, inv1, gate4, z4, skip)


def _log_sigmoid(x):
    return jnp.minimum(x, 0.0) - jnp.log1p(jnp.exp(-jnp.abs(x)))


_NT = (((1,), (1,)), ((), ()))
_TN = (((0,), (0,)), ((), ()))


def _chunk_matrices(sub):
    i = np.arange(sub)
    same = (i[:, None] // GLA_CHUNK) == (i[None, :] // GLA_CHUNK)
    fwd = same & (i[None, :] <= i[:, None])
    bwd = same & (i[None, :] >= i[:, None])
    return jnp.asarray(np.stack([fwd, bwd, same]), F32)


def _gla_gates_body(q_ref, k_ref, lr_ref, w_ref, b_ref, cm_ref, qe_ref, ke_ref, kd_ref, dec_ref, *, scale):
    tg, width = q_ref.shape
    sub = cm_ref.shape[1]
    chunk = GLA_CHUNK
    same = cm_ref[2].astype(BF16)
    for s in range(tg // sub):
        rs = slice(s * sub, (s + 1) * sub)
        g = _log_sigmoid(_dot(lr_ref[rs, :].astype(BF16), w_ref[...]) + b_ref[...]) / GLA_TAU
        g_hi = g.astype(BF16)
        g_lo = (g - g_hi.astype(F32)).astype(BF16)
        q = q_ref[rs, :] * scale
        k = k_ref[rs, :]
        for d in range(2):
            cs = slice(d * width, (d + 1) * width)
            cum = cm_ref[d].astype(BF16)
            bc = _dot(cum, g_hi[:, cs]) + _dot(cum, g_lo[:, cs])
            bl = _dot(same, g_hi[:, cs]) + _dot(same, g_lo[:, cs])
            qe_ref[d, rs, :] = (q * jnp.exp(bc)).astype(BF16)
            ke_ref[d, rs, :] = (k * jnp.exp(-bc)).astype(BF16)
            kd_ref[d, rs, :] = (k * jnp.exp(bl - bc)).astype(BF16)
            for c in range(sub // chunk):
                row = s * (sub // chunk) + c
                dec_ref[d, row:row + 1, :] = jnp.exp(bl[c * chunk:c * chunk + 1])


def _gla_gates(p, lr, w_both, b_both, *, q_off, k_off, dk, heads, tg=1024, sub=256):
    L = p.shape[0]
    tg = min(tg, L)
    wk = heads * dk
    opnd = jax.ShapeDtypeStruct((2, L, wk), BF16)
    row = lambda width, off: pl.BlockSpec((tg, width), lambda i: (i, off // width))
    both = pl.BlockSpec((2, tg, wk), lambda i: (0, i, 0))
    const = lambda shape: pl.BlockSpec(shape, lambda i: (0,) * len(shape))
    return pl.pallas_call(
        functools.partial(_gla_gates_body, scale=dk ** -0.5),
        out_shape=(opnd, opnd, opnd, jax.ShapeDtypeStruct((2, L // GLA_CHUNK, wk), F32)),
        grid=(L // tg,),
        in_specs=[row(wk, q_off), row(wk, k_off), row(LANES, 0),
                  const((LANES, 2 * wk)), const((1, 2 * wk)), const((3, sub, sub))],
        out_specs=(both, both, both, pl.BlockSpec((2, tg // GLA_CHUNK, wk), lambda i: (0, i, 0))),
        compiler_params=_params(("parallel",)),
        name="gla_gates",
    )(p, p, lr, w_both, b_both, _chunk_matrices(sub))


def _gla_direction(qe_ref, ke_ref, kd_ref, v_ref, dec_ref, mask, o_ref, st_ref, *, reverse, dk, dv):
    tb = qe_ref.shape[0]
    sub = mask.shape[0]
    chunk = GLA_CHUNK
    for hd in range(st_ref.shape[0]):
        ks, vs = slice(hd * dk, (hd + 1) * dk), slice(hd * dv, (hd + 1) * dv)
        intras = []
        for s in range(tb // sub):
            rs = slice(s * sub, (s + 1) * sub)
            a = lax.dot_general(qe_ref[rs, ks], ke_ref[rs, ks], _NT, preferred_element_type=F32)
            intra = _dot(jnp.where(mask > 0.5, a, 0.0).astype(BF16), v_ref[rs, vs])
            intras += [intra[c * chunk:(c + 1) * chunk] for c in range(sub // chunk)]
        st = st_ref[hd]
        chunks = range(tb // chunk)
        for c in (reversed(chunks) if reverse else chunks):
            sl = slice(c * chunk, (c + 1) * chunk)
            o_ref[sl, vs] = intras[c] + lax.dot_general(qe_ref[sl, ks], st.astype(BF16), _NT,
                                                        preferred_element_type=F32)
            upd = lax.dot_general(v_ref[sl, vs], kd_ref[sl, ks], _TN, preferred_element_type=F32)
            st = st * dec_ref[c:c + 1, ks] + upd
        st_ref[hd] = st


def _gla_fwd_body(qe, ke, kd, v, dec, cm_ref, o_ref, st_ref, *, dk, dv):
    @pl.when(pl.program_id(0) == 0)
    def _():
        st_ref[...] = jnp.zeros_like(st_ref)

    _gla_direction(qe, ke, kd, v, dec, cm_ref[0], o_ref, st_ref, reverse=False, dk=dk, dv=dv)


def _gla_bwd_body(qe, ke, kd, v, dec, cm_ref, of_ref, r_ref, gn_ref, c_ref, st_ref, ob_ref, *, dk, dv):
    @pl.when(pl.program_id(0) == 0)
    def _():
        st_ref[...] = jnp.zeros_like(st_ref)

    _gla_direction(qe, ke, kd, v, dec, cm_ref[1], ob_ref, st_ref, reverse=True, dk=dk, dv=dv)
    for rs in [slice(r0, r0 + ROW_CHUNK) for r0 in range(0, ob_ref.shape[0], ROW_CHUNK)]:
        r = r_ref[rs, :]
        gate = r * jax.nn.sigmoid(r)
        for hd in range(ob_ref.shape[1] // dv):
            sl = slice(hd * dv, (hd + 1) * dv)
            o = of_ref[rs, sl] + ob_ref[rs, sl]
            c_ref[rs, sl] = (_rms(o, gn_ref[...]) * gate[:, sl]).astype(BF16)


def _gla(qe, ke, kd, dec, vb, p, gla_norm, *, r_off, dk, dv, tb=512, sub=256):
    L, wv = vb.shape
    tb = min(tb, L)
    nb = L // tb
    heads = wv // dv
    wk = heads * dk
    nc = tb // GLA_CHUNK
    cm = _chunk_matrices(sub)
    cm_spec = pl.BlockSpec((3, sub, sub), lambda i: (0, 0, 0))

    def side(rev):
        blk = (lambda i: nb - 1 - i) if rev else (lambda i: i)
        opnd = pl.BlockSpec((None, tb, wk), lambda i: (int(rev), blk(i), 0))
        return blk, [opnd, opnd, opnd, pl.BlockSpec((tb, wv), lambda i: (blk(i), 0)),
                     pl.BlockSpec((None, nc, wk), lambda i: (int(rev), blk(i), 0))]

    state = pltpu.VMEM((heads, dv, dk), F32)
    _, fwd_specs = side(False)
    o_f = pl.pallas_call(
        functools.partial(_gla_fwd_body, dk=dk, dv=dv),
        out_shape=jax.ShapeDtypeStruct((L, wv), F32),
        grid=(nb,),
        in_specs=fwd_specs + [cm_spec],
        out_specs=pl.BlockSpec((tb, wv), lambda i: (i, 0)),
        scratch_shapes=[state],
        compiler_params=_params(("arbitrary",)),
        name="gla_fwd",
    )(qe, ke, kd, vb, dec, cm)
    blk, bwd_specs = side(True)
    row = lambda col: pl.BlockSpec((tb, wv), lambda i: (blk(i), col))
    return pl.pallas_call(
        functools.partial(_gla_bwd_body, dk=dk, dv=dv),
        out_shape=jax.ShapeDtypeStruct((L, wv), BF16),
        grid=(nb,),
        in_specs=bwd_specs + [cm_spec, row(0), row(r_off // wv), pl.BlockSpec((1, dv), lambda i: (0, 0))],
        out_specs=row(0),
        scratch_shapes=[state, pltpu.VMEM((tb, wv), F32)],
        compiler_params=_params(("arbitrary",)),
        name="gla_bwd",
    )(qe, ke, kd, vb, dec, cm, o_f, p, gla_norm.reshape(1, dv))


def _merge_body(u_ref, a_ref, z_ref, c_ref,
                wga_ref, wgb_ref, wgc_ref, bga_ref, bgb_ref, bgc_ref,
                wa_ref, wb_ref, wc_ref, m_ref, zb_ref):
    @pl.when(pl.program_id(1) == 0)
    def _():
        zb_ref[...] = _load_groups(z_ref).astype(BF16)

    u = u_ref[...]
    m = jax.nn.sigmoid(_dot(u, wga_ref[...]) + bga_ref[...]) * _dot(a_ref[...], wa_ref[...])
    m += jax.nn.sigmoid(_dot(u, wgb_ref[...]) + bgb_ref[...]) * _dot(zb_ref[...], wb_ref[...])
    m += jax.nn.sigmoid(_dot(u, wgc_ref[...]) + bgc_ref[...]) * _dot(c_ref[...], wc_ref[...])
    m_ref[...] = m.astype(BF16)


def _merge(u, a_pre, z, c_pre, w_gate, b_gate, w_a, w_b, w_c, *, tm=1024, tn=512):
    L, D = u.shape
    wa, wb, wc = w_a.shape[0], w_b.shape[0], w_c.shape[0]
    tm = min(tm, L)
    nj = D // tn
    row = lambda width: pl.BlockSpec((tm, width), lambda i, j: (i, 0))
    gate_w = lambda k: pl.BlockSpec((D, tn), lambda i, j: (0, j + k * nj))
    gate_b = lambda k: pl.BlockSpec((1, tn), lambda i, j: (0, j + k * nj))
    br_w = lambda width: pl.BlockSpec((width, tn), lambda i, j: (0, j))
    return pl.pallas_call(
        _merge_body,
        out_shape=jax.ShapeDtypeStruct((L, D), BF16),
        grid=(L // tm, nj),
        in_specs=[
            row(D), row(wa), pl.BlockSpec((wb // LANES, tm, LANES), lambda i, j: (0, i, 0)), row(wc),
            gate_w(0), gate_w(1), gate_w(2), gate_b(0), gate_b(1), gate_b(2),
            br_w(wa), br_w(wb), br_w(wc),
        ],
        out_specs=pl.BlockSpec((tm, tn), lambda i, j: (i, j)),
        scratch_shapes=[pltpu.VMEM((tm, wb), BF16)],
        compiler_params=_params(("parallel", "arbitrary")),
        name="merge",
    )(u, a_pre, z, c_pre, w_gate, w_gate, w_gate, b_gate, b_gate, b_gate, w_a, w_b, w_c)


def _out_proj_body(h_ref, m_ref, w_ref, o_ref, *, tn):
    m = m_ref[...]
    for c in range(o_ref.shape[1] // tn):
        sl = slice(c * tn, (c + 1) * tn)
        o_ref[:, sl] = h_ref[:, sl] + _dot(m, w_ref[:, sl])


def _out_proj(h, m, w_o, *, tm=512, tn=512):
    L, D = h.shape
    row = pl.BlockSpec((tm, D), lambda i: (i, 0))
    return pl.pallas_call(
        functools.partial(_out_proj_body, tn=tn),
        out_shape=jax.ShapeDtypeStruct((L, D), F32),
        grid=(L // tm,),
        in_specs=[row, row, pl.BlockSpec((D, D), lambda i: (0, 0), pipeline_mode=pl.Buffered(1))],
        out_specs=row,
        compiler_params=_params(("parallel",)),
        name="out_proj",
    )(h, m, w_o)


def _hyena(z0, g1, g2, skip, filt, consts):
    groups, L, _ = z0.shape
    n1 = DFT_N1
    n2 = 2 * L // n1
    k_time, sums = filt
    kf = _outer_fwd(k_time.reshape(k_time.shape[0], n1, n2, LANES), consts["fwd1"], sums)
    z = z0.reshape(groups, n1 // 2, n2, LANES)
    for order, gate in enumerate((g1, g2)):
        a = _outer_fwd(z, consts["fwd1"], t2=min(2 * SUBLANES, n2))
        d = _mid(a, kf, order, consts["fwd2"], consts["inv2"])
        z = _outer_inv(d, consts["inv1"], gate.reshape(z.shape), z, skip[order].reshape(1, groups * LANES))
    return z.reshape(groups, L, LANES)


def kernel(x, ffn1_norm, ffn1_w_gu, ffn1_w_down, mix_norm, w_in, b_in, conv_a, conv_b, hf_w1, hf_b1, hf_w2, hf_b2, hf_w3, hf_b3, hf_w_out, hf_freq, hy_skip, gk_w2, gk_b, gla_norm, w_br_a, w_br_b, w_br_c, w_o, ffn2_norm, ffn2_w_gu, ffn2_w_down, final_norm):
    bsz, L, D = x.shape
    depth = ffn1_norm.shape[0]
    wa = conv_a.shape[2]
    wb = conv_b.shape[3]
    wck = gk_w2.shape[3]
    dv = gla_norm.shape[1]
    wcv = GLA_HEADS * dv
    dk = wck // GLA_HEADS
    hid = hf_w2.shape[1]
    assert wa == wb, "the projection kernel tiles both conv branches with one section width"
    n_main = 3 * wa + 3 * wb + 2 * wck + 2 * wcv
    q_off, k_off, r_off = 0, wck, 2 * wck
    lr_off = n_main
    gate_off = n_main + 2 * GLA_RANK
    n1 = DFT_N1
    n2 = 2 * L // n1

    consts = _dft_consts(n1, n2)
    deltas =jnp.linspace(math.log(HY_TARGET) / HY_SLOW_DECAY, math.log(HY_TARGET) / HY_FAST_DECAY, wb, dtype=F32)
    deltas2 = jnp.tile(deltas.reshape(1, wb), (1, HY_ORDER))

    outs = []
    for bi in range(bsz):
        h = x[bi]
        for l in range(depth):
            h = _ffn(h, ffn1_norm[l], _to_bf16(ffn1_w_gu, l), _to_bf16(ffn1_w_down, l, scale=0.5),
                     final_norm, final=False)

            w_lr = jnp.pad(w_in[l][:, lr_off:gate_off], ((0, 0), (0, LANES - 2 * GLA_RANK))).astype(BF16)
            b_lr = jnp.pad(b_in[l][lr_off:gate_off], (0, LANES - 2 * GLA_RANK)).reshape(1, LANES)
            p, lr, a_pre, z0, g1, g2, u, vb = _proj(
                h, mix_norm[l], w_in[l][:, :n_main].astype(BF16), b_in[l][:n_main].reshape(1, n_main),
                w_lr, b_lr, conv_a[l], conv_b[l], wa=wa, wv=wcv)

            w_out2 = hf_w_out[l].reshape(hid, HY_ORDER, 2, wb).transpose(2, 0, 1, 3).reshape(2, hid, HY_ORDER * wb)
            filt = _filters(L, hf_w1[l], hf_b1[l], hf_w2[l], hf_b2[l], hf_w3[l], hf_b3[l], hf_freq[l],
                            w_out2, deltas2)
            zb = _hyena(z0, g1, g2, hy_skip[l], filt, consts)

            w_both = jnp.zeros((LANES, 2 * wck), F32)
            w_both = w_both.at[:GLA_RANK, :wck].set(gk_w2[l, 0]).at[GLA_RANK:2 * GLA_RANK, wck:].set(gk_w2[l, 1])
            qe, ke, kd, dec = _gla_gates(p, lr, w_both.astype(BF16), gk_b[l].reshape(1, 2 * wck),
                                         q_off=q_off, k_off=k_off, dk=dk, heads=GLA_HEADS)
            c_pre = _gla(qe, ke, kd, dec, vb, p, gla_norm[l], r_off=r_off, dk=dk, dv=dv)

            merged = _merge(u, a_pre, zb, c_pre,
                            w_in[l][:, gate_off:].astype(BF16), b_in[l][gate_off:].reshape(1, 3 * D),
                            _to_bf16(w_br_a, l), _to_bf16(w_br_b, l), _to_bf16(w_br_c, l))
            h = _out_proj(h, merged, _to_bf16(w_o, l))

            h = _ffn(h, ffn2_norm[l], _to_bf16(ffn2_w_gu, l), _to_bf16(ffn2_w_down, l, scale=0.5),
                     final_norm, final=(l == depth - 1))
        outs.append(h)
    return jnp.stack(outs, axis=0)
```

```python
import functools
import math

import numpy as np
import jax
import jax.numpy as jnp
from jax import lax
from jax.experimental import pallas as pl
from jax.experimental.pallas import tpu as pltpu

F32 = jnp.float32
BF16 = jnp.bfloat16
HIGHEST = lax.Precision.HIGHEST

EPS = 1e-5
GLA_HEADS = 4
GLA_RANK = 16
GLA_TAU = 16.0
GLA_CHUNK = 64
HY_ORDER = 2
HY_BANDS = 16
HY_FAST_DECAY = 0.3
HY_SLOW_DECAY = 1.5
HY_TARGET = 1e-2

LANES = 128
SUBLANES = 8
DFT_N1 = 256
ROW_CHUNK = 128
VMEM_LIMIT = 56 * 1024 * 1024


def _params(sem):
    return pltpu.CompilerParams(dimension_semantics=sem, vmem_limit_bytes=VMEM_LIMIT)


def _rms(x, g):
    return x * lax.rsqrt(jnp.mean(x * x, axis=-1, keepdims=True) + EPS) * g


def _dot(a, b):
    return jnp.dot(a, b, preferred_element_type=F32)


def _dot_hi(a, b):
    return jnp.dot(a, b, preferred_element_type=F32, precision=HIGHEST)


def _dot3_tn(a, b):
    a_hi, b_hi = a.astype(BF16), b.astype(BF16)
    a_lo = (a - a_hi.astype(F32)).astype(BF16)
    b_lo = (b - b_hi.astype(F32)).astype(BF16)
    tn = lambda x, y: lax.dot_general(x, y, (((0,), (0,)), ((), ())), preferred_element_type=F32)
    return tn(a_hi, b_hi) + tn(a_lo, b_hi) + tn(a_hi, b_lo)


def _cast_body(x_ref, o_ref, *, scale):
    x = x_ref[...]
    o_ref[...] = (x if scale == 1.0 else x * scale).astype(o_ref.dtype)


def _to_bf16(w, layer, cols=None, *, scale=1.0, block_bytes=4 << 20):
    _, rows, width = w.shape
    cols = width if cols is None else cols
    tr = rows
    while tr * cols * 4 > block_bytes and tr % 32 == 0:
        tr //= 2
    return pl.pallas_call(
        functools.partial(_cast_body, scale=scale),
        out_shape=jax.ShapeDtypeStruct((rows, cols), BF16),
        grid=(rows // tr,),
        in_specs=[pl.BlockSpec((None, tr, cols), lambda i: (layer, i, 0))],
        out_specs=pl.BlockSpec((tr, cols), lambda i: (i, 0)),
        compiler_params=_params(("parallel",)),
        name="to_bf16",
    )(w)


def _ffn_body(x_ref, g_ref, wg_ref, wu_ref, wd_ref, fn_ref, o_ref, xn_ref, *, final, sub):
    j = pl.program_id(1)

    row_chunks = [slice(r, r + ROW_CHUNK) for r in range(0, x_ref.shape[0], ROW_CHUNK)]

    @pl.when(j == 0)
    def _():
        for rs in row_chunks:
            x = x_ref[rs, :]
            xn_ref[rs, :] = _rms(x, g_ref[...]).astype(BF16)
            o_ref[rs, :] = x

    xn = xn_ref[...]
    hs = []
    for c in range(wg_ref.shape[1] // sub):
        sl = slice(c * sub, (c + 1) * sub)
        g = _dot(xn, wg_ref[:, sl])
        u = _dot(xn, wu_ref[:, sl])
        hs.append((g * jax.nn.sigmoid(g) * u).astype(BF16))
    o_ref[...] += _dot(jnp.concatenate(hs, axis=1), wd_ref[...])

    if final:
        @pl.when(j == pl.num_programs(1) - 1)
        def _():
            for rs in row_chunks:
                o_ref[rs, :] = _rms(o_ref[rs, :], fn_ref[...])


def _ffn(x, norm_g, w_gu, half_w_down, final_g, *, final, tm=1024, tf=512, sub=256):
    w_down = half_w_down
    L, D = x.shape
    F = w_down.shape[0]
    tm = min(tm, L)
    nf = F // tf
    return pl.pallas_call(
        functools.partial(_ffn_body, final=final, sub=min(sub, tf)),
        out_shape=jax.ShapeDtypeStruct((L, D), F32),
        grid=(L // tm, nf),
        in_specs=[
            pl.BlockSpec((tm, D), lambda i, j: (i, 0)),
            pl.BlockSpec((1, D), lambda i, j: (0, 0)),
            pl.BlockSpec((D, tf), lambda i, j: (0, j)),
            pl.BlockSpec((D, tf), lambda i, j: (0, j + nf)),
            pl.BlockSpec((tf, D), lambda i, j: (j, 0)),
            pl.BlockSpec((1, D), lambda i, j: (0, 0)),
        ],
        out_specs=pl.BlockSpec((tm, D), lambda i, j: (i, 0)),
        scratch_shapes=[pltpu.VMEM((tm, D), BF16)],
        compiler_params=_params(("parallel", "arbitrary")),
        name="ffn",
    )(x, norm_g.reshape(1, D), w_gu, w_gu, w_down, final_g.reshape(1, D))


def _conv3_inner(x, w):
    n = x.shape[0]
    y = pltpu.roll(x, 1, axis=0) * w[0:1] + x * w[1:2] + pltpu.roll(x, n - 1, axis=0) * w[2:3]
    return y[SUBLANES:n - SUBLANES]


def _proj_body(h_ref, hp_ref, hn_ref, g_ref, w_ref, b_ref, wlr_ref, blr_ref, caw_ref, cbw_ref,
               p_ref, lr_ref, a_ref, z_ref, g1_ref, g2_ref, u_ref, vb_ref, *, tn, wa):
    i = pl.program_id(0)
    tm = h_ref.shape[0]
    g = g_ref[...]
    xn = _rms(h_ref[...], g)
    xn_bf = xn.astype(BF16)
    u_ref[...] = xn_bf
    lr_ref[...] = _dot(xn_bf, wlr_ref[...]) + blr_ref[...]
    conv_w = 6 * wa
    v_lo = p_ref.shape[1] - vb_ref.shape[1]
    v_hi = v_lo + vb_ref.shape[1]
    for off in range(0, p_ref.shape[1] + vb_ref.shape[1], tn):
        src = slice(conv_w + off, conv_w + off + tn)
        y = _dot(xn_bf, w_ref[:, src]) + b_ref[:, src]
        if off < v_lo:
            p_ref[:, off:off + tn] = y
        elif off < v_hi:
            vb_ref[:, off - v_lo:off - v_lo + tn] = y.astype(BF16)
        else:
            p_ref[:, off - vb_ref.shape[1]:off - vb_ref.shape[1] + tn] = y

    xh = jnp.concatenate([_rms(hp_ref[...], g), xn, _rms(hn_ref[...], g)], axis=0).astype(BF16)
    row = lax.broadcasted_iota(jnp.int32, (tm + 2 * SUBLANES, 1), 0)
    has_prev = (i > 0).astype(F32)
    has_next = (i < pl.num_programs(0) - 1).astype(F32)
    valid = jnp.where(row < SUBLANES, has_prev, jnp.where(row >= tm + SUBLANES, has_next, 1.0))

    def sec(k):
        sl = slice(k * wa, (k + 1) * wa)
        return (_dot(xh, w_ref[:, sl]) + b_ref[:, sl]) * valid

    xa, ba, ca = sec(0), sec(1), sec(2)
    a_ref[...] = (ba[SUBLANES:tm + SUBLANES] * _conv3_inner(ca * xa, caw_ref[...])).astype(BF16)
    for k, o_ref in enumerate((z_ref, g1_ref, g2_ref)):
        _store_groups(o_ref, _conv3_inner(sec(3 + k), cbw_ref[k]))


def _proj(h, norm_g, w, b, w_lr, b_lr, conv_a, conv_b, *, wa, wv, tm=256, tn=512):
    L, D = h.shape
    conv_w = 6 * wa
    n_scan = w.shape[1] - conv_w - wv
    rb = tm // SUBLANES
    nrb = L // SUBLANES
    resident = lambda shape: pl.BlockSpec(shape, lambda i: (0,) * len(shape), pipeline_mode=pl.Buffered(1))
    grouped = pl.BlockSpec((wa // LANES, tm, LANES), lambda i: (0, i, 0))
    return pl.pallas_call(
        functools.partial(_proj_body, tn=tn, wa=wa),
        out_shape=(jax.ShapeDtypeStruct((L, n_scan), F32), jax.ShapeDtypeStruct((L, LANES), F32),
                   jax.ShapeDtypeStruct((L, wa), BF16))
        + (jax.ShapeDtypeStruct((wa // LANES, L, LANES), F32),) * 3
        + (jax.ShapeDtypeStruct((L, D), BF16), jax.ShapeDtypeStruct((L, wv), BF16)),
        grid=(L // tm,),
        in_specs=[
            pl.BlockSpec((tm, D), lambda i: (i, 0)),
            pl.BlockSpec((SUBLANES, D), lambda i: (jnp.maximum(i * rb - 1, 0), 0)),
            pl.BlockSpec((SUBLANES, D), lambda i: (jnp.minimum((i + 1) * rb, nrb - 1), 0)),
            resident((1, D)), resident(w.shape), resident((1, w.shape[1])),
            resident((D, LANES)), resident((1, LANES)), resident((3, wa)), resident((3, 3, wa)),
        ],
        out_specs=(pl.BlockSpec((tm, n_scan), lambda i: (i, 0)),
                   pl.BlockSpec((tm, LANES), lambda i: (i, 0)),
                   pl.BlockSpec((tm, wa), lambda i: (i, 0)), grouped, grouped, grouped,
                   pl.BlockSpec((tm, D), lambda i: (i, 0)), pl.BlockSpec((tm, wv), lambda i: (i, 0))),
        compiler_params=_params(("parallel",)),
        name="proj",
    )(h, h, h, norm_g.reshape(1, D), w, b, w_lr, b_lr, conv_a, conv_b)


def _store_groups(o_ref, x):
    for g in range(o_ref.shape[0]):
        o_ref[g] = x[:, g * LANES:(g + 1) * LANES]


def _load_groups(ref):
    return jnp.concatenate([ref[g] for g in range(ref.shape[0])], axis=1)


def _filter_body(w1t_ref, w1cs_ref, b1_ref, w2_ref, b2_ref, w3_ref, b3_ref, fr_ref, bands_ref, wo_ref, dl_ref,
                 k_ref, s_ref, *, half_tiles, tr, seq):
    i = pl.program_id(0)

    def lag_of(n):
        return jnp.where(n < seq, n, jnp.where(n == seq, 0, 2 * seq - n)).astype(F32)

    lag_row = lag_of(i * tr + lax.broadcasted_iota(jnp.int32, (1, tr), 1))
    ang = ((2.0 * math.pi / seq) * lag_row) * bands_ref[...]
    feats = jnp.concatenate([jnp.cos(ang), -jnp.sin(ang)], axis=0)
    fr = fr_ref[...]
    pre = w1t_ref[...] * (lag_row / max(seq - 1, 1)) + _dot_hi(w1cs_ref[...], feats) + b1_ref[...]
    h = jnp.sin(fr * pre)
    h = jnp.sin(fr * (_dot_hi(w2_ref[...], h) + b2_ref[...]))
    h = jnp.sin(fr * (_dot_hi(w3_ref[...], h) + b3_ref[...]))
    t = lag_of(i * tr + lax.broadcasted_iota(jnp.int32, (tr, 1), 0)) / max(seq - 1, 1)
    decay = jnp.exp(-t * jnp.abs(dl_ref[...]))
    hd = _dot3_tn(h, wo_ref[0]) * decay

    @pl.when(i % half_tiles == 0)
    def _():
        s_ref[...] = jnp.zeros_like(s_ref)

    s_ref[0] += jnp.sum(jnp.abs(hd), axis=0, keepdims=True)
    row = lax.broadcasted_iota(jnp.int32, hd.shape, 0)
    zero_row = jnp.where(i == half_tiles, 0, -1)
    _store_groups(k_ref, jnp.where(row == zero_row, 0.0, hd))


def _filters(seq, w1, b1, w2, b2, w3, b3, freq, w_out2, deltas2, *, tr=1024):
    n_rows = 2 * seq
    hid = w2.shape[0]
    width = w_out2.shape[2]
    half_tiles = n_rows // (2 * tr)
    bands = jnp.linspace(1e-4, HY_BANDS - 1, HY_BANDS, dtype=F32).reshape(HY_BANDS, 1)
    full = lambda shape: pl.BlockSpec(shape, lambda i: (0,) * len(shape))
    colv = lambda v: v.reshape(hid, 1)
    return pl.pallas_call(
        functools.partial(_filter_body, half_tiles=half_tiles, tr=tr, seq=seq),
        out_shape=(jax.ShapeDtypeStruct((width // LANES, n_rows, LANES), F32),
                   jax.ShapeDtypeStruct((2, 1, width), F32)),
        grid=(n_rows // tr,),
        in_specs=[
            full((hid, 1)), full((hid, 2 * HY_BANDS)), full((hid, 1)), full((hid, hid)), full((hid, 1)),
            full((hid, hid)), full((hid, 1)), full((hid, 1)), full((HY_BANDS, 1)),
            pl.BlockSpec((1, hid, width), lambda i: (i // half_tiles, 0, 0)),
            full((1, width)),
        ],
        out_specs=(pl.BlockSpec((width // LANES, tr, LANES), lambda i: (0, i, 0)),
                   pl.BlockSpec((1, 1, width), lambda i: (i // half_tiles, 0, 0))),
        compiler_params=_params(("arbitrary",)),
        name="hyena_filters",
    )(w1[0:1].T, w1[1:].T, colv(b1), w2.T, colv(b2), w3.T, colv(b3), colv(freq), bands, w_out2, deltas2)


def _dft_consts(n1, n2):
    n = n1 * n2
    k = np.arange(n1)
    m = np.arange(n2)
    ang2 = 2.0 * np.pi * ((m[:, None] * m[None, :]) % n2) / n2
    c2, s2 = np.cos(ang2), np.sin(ang2)
    as_bf = lambda a: jnp.asarray(np.asarray(a, np.float32).astype(jnp.bfloat16))
    time_index = k[None, None, :] * n2 + m[:, None, None]
    ang = 2.0 * np.pi * ((k[None, :, None] * time_index) % n) / n
    stage = np.concatenate([np.cos(ang), -np.sin(ang)], axis=1)
    half = n1 // 2
    return dict(
        fwd1=as_bf(stage),
        inv1=as_bf(stage.transpose(0, 2, 1)[:, :half] / n),
        fwd2=as_bf(np.block([[c2, s2], [-s2, c2]])),
        inv2=as_bf(np.block([[c2, -s2], [s2, c2]])),
    )


def _flatten_block(ref, buf):
    groups, n, t2, lanes = ref.shape
    for g in range(groups):
        for part in range(t2 // SUBLANES):
            rows = slice(part * SUBLANES, (part + 1) * SUBLANES)
            buf[g, part] = ref[g, :, rows, :].reshape(n * SUBLANES, lanes)


def _unflatten_block(buf, ref):
    groups, n, t2, lanes = ref.shape
    for g in range(groups):
        for part in range(t2 // SUBLANES):
            rows = slice(part * SUBLANES, (part + 1) * SUBLANES)
            ref[g, :, rows, :] = buf[g, part].reshape(n, SUBLANES, lanes)


def _load_strided(buf, s):
    part, sub = divmod(s, SUBLANES)
    n = buf.shape[2] // SUBLANES
    return jnp.concatenate([buf[g, part, pl.ds(sub, n, stride=SUBLANES), :] for g in range(buf.shape[0])],
                           axis=1)


def _store_strided(buf, s, x):
    part, sub = divmod(s, SUBLANES)
    n = buf.shape[2] // SUBLANES
    for g in range(buf.shape[0]):
        buf[g, part, pl.ds(sub, n, stride=SUBLANES), :] = x[:, g * LANES:(g + 1) * LANES]


def _flat_scratch(groups, n, t2, dtype=F32):
    return pltpu.VMEM((groups, t2 // SUBLANES, n * SUBLANES, LANES), dtype)


U32 = jnp.uint32


def _pack_complex(re, im):
    hi = lax.bitcast_convert_type(re.astype(BF16).astype(F32), U32)
    lo = lax.bitcast_convert_type(im.astype(BF16).astype(F32), U32)
    return hi | (lo >> 16)


def _unpack_complex(w):
    re = lax.bitcast_convert_type(w & jnp.uint32(0xFFFF0000), F32)
    im = lax.bitcast_convert_type(w << 16, F32)
    return re, im


def _outer_fwd_body(x_ref, f_ref, *rest, scaled):
    if scaled:
        s_ref, a_ref, xb, ab = rest
        inv = 1.0 / (s_ref[...] + EPS)
    else:
        a_ref, xb, ab = rest
    rows, t2 = x_ref.shape[1:3]
    n1 = a_ref.shape[1]
    _flatten_block(x_ref, xb)
    for s in range(t2):
        x = _load_strided(xb, s)
        if scaled:
            half = rows // 2
            x = jnp.concatenate([x[:half] * inv[0], x[half:] * inv[1]], axis=0)
        a = _dot(f_ref[s], x.astype(BF16))
        _store_strided(ab, s, _pack_complex(a[:n1], a[n1:]))
    _unflatten_block(ab, a_ref)


def _outer_fwd(x4, fwd1, sums=None, *, t2=SUBLANES, gt=4):
    groups, rows, n2, _ = x4.shape
    n1 = fwd1.shape[2]
    scaled = sums is not None
    blk = lambda r: pl.BlockSpec((gt, r, t2, LANES), lambda j, c: (c, 0, j, 0))
    args = [x4, fwd1]
    in_specs = [blk(rows), pl.BlockSpec((t2, 2 * n1, rows), lambda j, c: (j, 0, 0))]
    if scaled:
        args.append(sums)
        in_specs.append(pl.BlockSpec((2, 1, gt * LANES), lambda j, c: (0, 0, c)))
    return pl.pallas_call(
        functools.partial(_outer_fwd_body, scaled=scaled),
        out_shape=jax.ShapeDtypeStruct((groups, n1, n2, LANES), U32),
        grid=(n2 // t2, groups // gt),
        in_specs=in_specs,
        out_specs=blk(n1),
        scratch_shapes=[_flat_scratch(gt, rows, t2), _flat_scratch(gt, n1, t2, U32)],
        compiler_params=_params(("parallel", "parallel")),
        name="dft_outer_fwd",
    )(*args)


def _load_slab(ref, s):
    return jnp.concatenate([ref[g, s] for g in range(ref.shape[0])], axis=1)


def _store_slab(ref, s, x):
    for g in range(ref.shape[0]):
        ref[g, s] = x[:, g * LANES:(g + 1) * LANES]


def _mid_conv_body(a_ref, k_ref, f2_ref, g2_ref, d_ref):
    n2 = a_ref.shape[2]
    width = a_ref.shape[0] * LANES
    for s in range(a_ref.shape[1]):
        re, im = _unpack_complex(jnp.concatenate([_load_slab(a_ref, s), _load_slab(k_ref, s)], axis=1))
        x = _dot(f2_ref[...], jnp.concatenate([re, im], axis=0).astype(BF16))
        xr, xi, kr, ki = x[:n2, :width], x[n2:, :width], x[:n2, width:], x[n2:, width:]
        yr = xr * kr - xi * ki
        yi = xr * ki + xi * kr
        c = _dot(g2_ref[...], jnp.concatenate([yr, yi], axis=0).astype(BF16))
        _store_slab(d_ref, s, _pack_complex(c[:n2], c[n2:]))


def _mid(a, kf, order, fwd2, inv2, *, kb=8):
    groups, n1, n2, _ = a.shape
    slab = pl.BlockSpec((groups, kb, n2, LANES), lambda i: (0, i, 0, 0))
    kslab = pl.BlockSpec((groups, kb, n2, LANES), lambda i: (order, i, 0, 0))
    mat = pl.BlockSpec((2 * n2, 2 * n2), lambda i: (0, 0))
    return pl.pallas_call(
        _mid_conv_body,
        out_shape=jax.ShapeDtypeStruct(a.shape, U32),
        grid=(n1 // kb,),
        in_specs=[slab, kslab, mat, mat],
        out_specs=slab,
        compiler_params=_params(("parallel",)),
        name="hyena_mid_conv",
    )(a, kf, fwd2, inv2)


def _outer_inv_body(d_ref, e_ref, gate_ref, z_ref, skip_ref, o_ref, db, gb, zb, ob):
    skip = skip_ref[...]
    t2 = d_ref.shape[2]
    for ref, buf in ((d_ref, db), (gate_ref, gb), (z_ref, zb)):
        _flatten_block(ref, buf)
    for s in range(t2):
        dr, di = _unpack_complex(_load_strided(db, s))
        y = _dot(e_ref[s], jnp.concatenate([dr, di], axis=0).astype(BF16))
        _store_strided(ob, s, _load_strided(gb, s) * (y + skip * _load_strided(zb, s)))
    _unflatten_block(ob, o_ref)


def _outer_inv(d, inv1, gate4, z4, skip, *, t2=SUBLANES):
    groups, n1, n2, _ = d.shape
    half = n1 // 2
    blk = lambda r: pl.BlockSpec((groups, r, t2, LANES), lambda j: (0, 0, j, 0))
    return pl.pallas_call(
        _outer_inv_body,
        out_shape=jax.ShapeDtypeStruct((groups, half, n2, LANES), F32),
        grid=(n2 // t2,),
        in_specs=[blk(n1), pl.BlockSpec((t2, half, 2 * n1), lambda j: (j, 0, 0)), blk(half), blk(half),
                  pl.BlockSpec((1, groups * LANES), lambda j: (0, 0))],
        out_specs=blk(half),
        scratch_shapes=[_flat_scratch(groups, n1, t2, U32)] + [_flat_scratch(groups, half, t2)] * 3,
        compiler_params=_params(("parallel",)),
        name="dft_outer_inv",
    )(d, inv1, gate4, z4, skip)


def _log_sigmoid(x):
    return jnp.minimum(x, 0.0) - jnp.log1p(jnp.exp(-jnp.abs(x)))


_NT = (((1,), (1,)), ((), ()))
_TN = (((0,), (0,)), ((), ()))


def _chunk_matrices(sub):
    i = np.arange(sub)
    same = (i[:, None] // GLA_CHUNK) == (i[None, :] // GLA_CHUNK)
    fwd = same & (i[None, :] <= i[:, None])
    bwd = same & (i[None, :] >= i[:, None])
    return jnp.asarray(np.stack([fwd, bwd, same]), F32)


def _gla_gates_body(q_ref, k_ref, lr_ref, w_ref, b_ref, cm_ref, qe_ref, ke_ref, kd_ref, dec_ref, *, scale):
    tg, width = q_ref.shape
    sub = cm_ref.shape[1]
    chunk = GLA_CHUNK
    same = cm_ref[2].astype(BF16)
    for s in range(tg // sub):
        rs = slice(s * sub, (s + 1) * sub)
        g = _log_sigmoid(_dot(lr_ref[rs, :].astype(BF16), w_ref[...]) + b_ref[...]) / GLA_TAU
        g_hi = g.astype(BF16)
        g_lo = (g - g_hi.astype(F32)).astype(BF16)
        q = q_ref[rs, :] * scale
        k = k_ref[rs, :]
        for d in range(2):
            cs = slice(d * width, (d + 1) * width)
            cum = cm_ref[d].astype(BF16)
            bc = _dot(cum, g_hi[:, cs]) + _dot(cum, g_lo[:, cs])
            bl = _dot(same, g_hi[:, cs]) + _dot(same, g_lo[:, cs])
            qe_ref[d, rs, :] = (q * jnp.exp(bc)).astype(BF16)
            ke_ref[d, rs, :] = (k * jnp.exp(-bc)).astype(BF16)
            kd_ref[d, rs, :] = (k * jnp.exp(bl - bc)).astype(BF16)
            for c in range(sub // chunk):
                row = s * (sub // chunk) + c
                dec_ref[d, row:row + 1, :] = jnp.exp(bl[c * chunk:c * chunk + 1])


def _gla_gates(p, lr, w_both, b_both, *, q_off, k_off, dk, heads, tg=1024, sub=256):
    L = p.shape[0]
    tg = min(tg, L)
    wk = heads * dk
    opnd = jax.ShapeDtypeStruct((2, L, wk), BF16)
    row = lambda width, off: pl.BlockSpec((tg, width), lambda i: (i, off // width))
    both = pl.BlockSpec((2, tg, wk), lambda i: (0, i, 0))
    const = lambda shape: pl.BlockSpec(shape, lambda i: (0,) * len(shape))
    return pl.pallas_call(
        functools.partial(_gla_gates_body, scale=dk ** -0.5),
        out_shape=(opnd, opnd, opnd, jax.ShapeDtypeStruct((2, L // GLA_CHUNK, wk), F32)),
        grid=(L // tg,),
        in_specs=[row(wk, q_off), row(wk, k_off), row(LANES, 0),
                  const((LANES, 2 * wk)), const((1, 2 * wk)), const((3, sub, sub))],
        out_specs=(both, both, both, pl.BlockSpec((2, tg // GLA_CHUNK, wk), lambda i: (0, i, 0))),
        compiler_params=_params(("parallel",)),
        name="gla_gates",
    )(p, p, lr, w_both, b_both, _chunk_matrices(sub))


def _gla_direction(qe_ref, ke_ref, kd_ref, v_ref, dec_ref, mask, o_ref, st_ref, *, reverse, dk, dv):
    tb = qe_ref.shape[0]
    sub = mask.shape[0]
    chunk = GLA_CHUNK
    for hd in range(st_ref.shape[0]):
        ks, vs = slice(hd * dk, (hd + 1) * dk), slice(hd * dv, (hd + 1) * dv)
        intras = []
        for s in range(tb // sub):
            rs = slice(s * sub, (s + 1) * sub)
            a = lax.dot_general(qe_ref[rs, ks], ke_ref[rs, ks], _NT, preferred_element_type=F32)
            intra = _dot(jnp.where(mask > 0.5, a, 0.0).astype(BF16), v_ref[rs, vs])
            intras += [intra[c * chunk:(c + 1) * chunk] for c in range(sub // chunk)]
        st = st_ref[hd]
        chunks = range(tb // chunk)
        for c in (reversed(chunks) if reverse else chunks):
            sl = slice(c * chunk, (c + 1) * chunk)
            o_ref[sl, vs] = intras[c] + lax.dot_general(qe_ref[sl, ks], st.astype(BF16), _NT,
                                                        preferred_element_type=F32)
            upd = lax.dot_general(v_ref[sl, vs], kd_ref[sl, ks], _TN, preferred_element_type=F32)
            st = st * dec_ref[c:c + 1, ks] + upd
        st_ref[hd] = st


def _gla_fwd_body(qe, ke, kd, v, dec, cm_ref, o_ref, st_ref, *, dk, dv):
    @pl.when(pl.program_id(0) == 0)
    def _():
        st_ref[...] = jnp.zeros_like(st_ref)

    _gla_direction(qe, ke, kd, v, dec, cm_ref[0], o_ref, st_ref, reverse=False, dk=dk, dv=dv)


def _gla_bwd_body(qe, ke, kd, v, dec, cm_ref, of_ref, r_ref, gn_ref, c_ref, st_ref, ob_ref, *, dk, dv):
    @pl.when(pl.program_id(0) == 0)
    def _():
        st_ref[...] = jnp.zeros_like(st_ref)

    _gla_direction(qe, ke, kd, v, dec, cm_ref[1], ob_ref, st_ref, reverse=True, dk=dk, dv=dv)
    for rs in [slice(r0, r0 + ROW_CHUNK) for r0 in range(0, ob_ref.shape[0], ROW_CHUNK)]:
        r = r_ref[rs, :]
        gate = r * jax.nn.sigmoid(r)
        for hd in range(ob_ref.shape[1] // dv):
            sl = slice(hd * dv, (hd + 1) * dv)
            o = of_ref[rs, sl] + ob_ref[rs, sl]
            c_ref[rs, sl] = (_rms(o, gn_ref[...]) * gate[:, sl]).astype(BF16)


def _gla(qe, ke, kd, dec, vb, p, gla_norm, *, r_off, dk, dv, tb=1024, sub=256):
    L, wv = vb.shape
    tb = min(tb, L)
    nb = L // tb
    heads = wv // dv
    wk = heads * dk
    nc = tb // GLA_CHUNK
    cm = _chunk_matrices(sub)
    cm_spec = pl.BlockSpec((3, sub, sub), lambda i: (0, 0, 0))

    def side(rev):
        blk = (lambda i: nb - 1 - i) if rev else (lambda i: i)
        opnd = pl.BlockSpec((None, tb, wk), lambda i: (int(rev), blk(i), 0))
        return blk, [opnd, opnd, opnd, pl.BlockSpec((tb, wv), lambda i: (blk(i), 0)),
                     pl.BlockSpec((None, nc, wk), lambda i: (int(rev), blk(i), 0))]

    state = pltpu.VMEM((heads, dv, dk), F32)
    _, fwd_specs = side(False)
    o_f = pl.pallas_call(
        functools.partial(_gla_fwd_body, dk=dk, dv=dv),
        out_shape=jax.ShapeDtypeStruct((L, wv), F32),
        grid=(nb,),
        in_specs=fwd_specs + [cm_spec],
        out_specs=pl.BlockSpec((tb, wv), lambda i: (i, 0)),
        scratch_shapes=[state],
        compiler_params=_params(("arbitrary",)),
        name="gla_fwd",
    )(qe, ke, kd, vb, dec, cm)
    blk, bwd_specs = side(True)
    row = lambda col: pl.BlockSpec((tb, wv), lambda i: (blk(i), col))
    return pl.pallas_call(
        functools.partial(_gla_bwd_body, dk=dk, dv=dv),
        out_shape=jax.ShapeDtypeStruct((L, wv), BF16),
        grid=(nb,),
        in_specs=bwd_specs + [cm_spec, row(0), row(r_off // wv), pl.BlockSpec((1, dv), lambda i: (0, 0))],
        out_specs=row(0),
        scratch_shapes=[state, pltpu.VMEM((tb, wv), F32)],
        compiler_params=_params(("arbitrary",)),
        name="gla_bwd",
    )(qe, ke, kd, vb, dec, cm, o_f, p, gla_norm.reshape(1, dv))


def _merge_body(u_ref, a_ref, z_ref, c_ref,
                wga_ref, wgb_ref, wgc_ref, bga_ref, bgb_ref, bgc_ref,
                wa_ref, wb_ref, wc_ref, m_ref, zb_ref):
    @pl.when(pl.program_id(1) == 0)
    def _():
        zb_ref[...] = _load_groups(z_ref).astype(BF16)

    u = u_ref[...]
    m = jax.nn.sigmoid(_dot(u, wga_ref[...]) + bga_ref[...]) * _dot(a_ref[...], wa_ref[...])
    m += jax.nn.sigmoid(_dot(u, wgb_ref[...]) + bgb_ref[...]) * _dot(zb_ref[...], wb_ref[...])
    m += jax.nn.sigmoid(_dot(u, wgc_ref[...]) + bgc_ref[...]) * _dot(c_ref[...], wc_ref[...])
    m_ref[...] = m.astype(BF16)


def _merge(u, a_pre, z, c_pre, w_gate, b_gate, w_a, w_b, w_c, *, tm=1024, tn=512):
    L, D = u.shape
    wa, wb, wc = w_a.shape[0], w_b.shape[0], w_c.shape[0]
    tm = min(tm, L)
    nj = D // tn
    row = lambda width: pl.BlockSpec((tm, width), lambda i, j: (i, 0))
    gate_w = lambda k: pl.BlockSpec((D, tn), lambda i, j: (0, j + k * nj))
    gate_b = lambda k: pl.BlockSpec((1, tn), lambda i, j: (0, j + k * nj))
    br_w = lambda width: pl.BlockSpec((width, tn), lambda i, j: (0, j))
    return pl.pallas_call(
        _merge_body,
        out_shape=jax.ShapeDtypeStruct((L, D), BF16),
        grid=(L // tm, nj),
        in_specs=[
            row(D), row(wa), pl.BlockSpec((wb // LANES, tm, LANES), lambda i, j: (0, i, 0)), row(wc),
            gate_w(0), gate_w(1), gate_w(2), gate_b(0), gate_b(1), gate_b(2),
            br_w(wa), br_w(wb), br_w(wc),
        ],
        out_specs=pl.BlockSpec((tm, tn), lambda i, j: (i, j)),
        scratch_shapes=[pltpu.VMEM((tm, wb), BF16)],
        compiler_params=_params(("parallel", "arbitrary")),
        name="merge",
    )(u, a_pre, z, c_pre, w_gate, w_gate, w_gate, b_gate, b_gate, b_gate, w_a, w_b, w_c)


def _out_proj_body(h_ref, m_ref, w_ref, o_ref, *, tn):
    m = m_ref[...]
    for c in range(o_ref.shape[1] // tn):
        sl = slice(c * tn, (c + 1) * tn)
        o_ref[:, sl] = h_ref[:, sl] + _dot(m, w_ref[:, sl])


def _out_proj(h, m, w_o, *, tm=512, tn=512):
    L, D = h.shape
    row = pl.BlockSpec((tm, D), lambda i: (i, 0))
    return pl.pallas_call(
        functools.partial(_out_proj_body, tn=tn),
        out_shape=jax.ShapeDtypeStruct((L, D), F32),
        grid=(L // tm,),
        in_specs=[row, row, pl.BlockSpec((D, D), lambda i: (0, 0), pipeline_mode=pl.Buffered(1))],
        out_specs=row,
        compiler_params=_params(("parallel",)),
        name="out_proj",
    )(h, m, w_o)


def _hyena(z0, g1, g2, skip, filt, consts):
    groups, L, _ = z0.shape
    n1 = DFT_N1
    n2 = 2 * L // n1
    k_time, sums = filt
    kf = _outer_fwd(k_time.reshape(k_time.shape[0], n1, n2, LANES), consts["fwd1"], sums)
    z = z0.reshape(groups, n1 // 2, n2, LANES)
    for order, gate in enumerate((g1, g2)):
        a = _outer_fwd(z, consts["fwd1"], t2=min(2 * SUBLANES, n2))
        d = _mid(a, kf, order, consts["fwd2"], consts["inv2"])
        z = _outer_inv(d, consts["inv1"], gate.reshape(z.shape), z, skip[order].reshape(1, groups * LANES))
    return z.reshape(groups, L, LANES)


def kernel(x, ffn1_norm, ffn1_w_gu, ffn1_w_down, mix_norm, w_in, b_in, conv_a, conv_b, hf_w1, hf_b1, hf_w2, hf_b2, hf_w3, hf_b3, hf_w_out, hf_freq, hy_skip, gk_w2, gk_b, gla_norm, w_br_a, w_br_b, w_br_c, w_o, ffn2_norm, ffn2_w_gu, ffn2_w_down, final_norm):
    bsz, L, D = x.shape
    depth = ffn1_norm.shape[0]
    wa = conv_a.shape[2]
    wb = conv_b.shape[3]
    wck = gk_w2.shape[3]
    dv = gla_norm.shape[1]
    wcv = GLA_HEADS * dv
    dk = wck // GLA_HEADS
    hid = hf_w2.shape[1]
    assert wa == wb, "the projection kernel tiles both conv branches with one section width"
    n_main = 3 * wa + 3 * wb + 2 * wck + 2 * wcv
    q_off, k_off, r_off = 0, wck, 2 * wck
    lr_off = n_main
    gate_off = n_main + 2 * GLA_RANK
    n1 = DFT_N1
    n2 = 2 * L // n1

    consts = _dft_consts(n1, n2)
    deltas =jnp.linspace(math.log(HY_TARGET) / HY_SLOW_DECAY, math.log(HY_TARGET) / HY_FAST_DECAY, wb, dtype=F32)
    deltas2 = jnp.tile(deltas.reshape(1, wb), (1, HY_ORDER))

    outs = []
    for bi in range(bsz):
        h = x[bi]
        for l in range(depth):
            h = _ffn(h, ffn1_norm[l], _to_bf16(ffn1_w_gu, l), _to_bf16(ffn1_w_down, l, scale=0.5),
                     final_norm, final=False)

            w_lr = jnp.pad(w_in[l][:, lr_off:gate_off], ((0, 0), (0, LANES - 2 * GLA_RANK))).astype(BF16)
            b_lr = jnp.pad(b_in[l][lr_off:gate_off], (0, LANES - 2 * GLA_RANK)).reshape(1, LANES)
            p, lr, a_pre, z0, g1, g2, u, vb = _proj(
                h, mix_norm[l], w_in[l][:, :n_main].astype(BF16), b_in[l][:n_main].reshape(1, n_main),
                w_lr, b_lr, conv_a[l], conv_b[l], wa=wa, wv=wcv)

            w_out2 = hf_w_out[l].reshape(hid, HY_ORDER, 2, wb).transpose(2, 0, 1, 3).reshape(2, hid, HY_ORDER * wb)
            filt = _filters(L, hf_w1[l], hf_b1[l], hf_w2[l], hf_b2[l], hf_w3[l], hf_b3[l], hf_freq[l],
                            w_out2, deltas2)
            zb = _hyena(z0, g1, g2, hy_skip[l], filt, consts)

            w_both = jnp.zeros((LANES, 2 * wck), F32)
            w_both = w_both.at[:GLA_RANK, :wck].set(gk_w2[l, 0]).at[GLA_RANK:2 * GLA_RANK, wck:].set(gk_w2[l, 1])
            qe, ke, kd, dec = _gla_gates(p, lr, w_both.astype(BF16), gk_b[l].reshape(1, 2 * wck),
                                         q_off=q_off, k_off=k_off, dk=dk, heads=GLA_HEADS)
            c_pre = _gla(qe, ke, kd, dec, vb, p, gla_norm[l], r_off=r_off, dk=dk, dv=dv)

            merged = _merge(u, a_pre, zb, c_pre,
                            w_in[l][:, gate_off:].astype(BF16), b_in[l][gate_off:].reshape(1, 3 * D),
                            _to_bf16(w_br_a, l), _to_bf16(w_br_b, l), _to_bf16(w_br_c, l))
            h = _out_proj(h, merged, _to_bf16(w_o, l))

            h = _ffn(h, ffn2_norm[l], _to_bf16(ffn2_w_gu, l), _to_bf16(ffn2_w_down, l, scale=0.5),
                     final_norm, final=(l == depth - 1))
        outs.append(h)
    return jnp.stack(outs, axis=0)
```

```python
import functools
import math

import numpy as np
import jax
import jax.numpy as jnp
from jax import lax
from jax.experimental import pallas as pl
from jax.experimental.pallas import tpu as pltpu

F32 = jnp.float32
BF16 = jnp.bfloat16
HIGHEST = lax.Precision.HIGHEST

EPS = 1e-5
GLA_HEADS = 4
GLA_RANK = 16
GLA_TAU = 16.0
GLA_CHUNK = 64
HY_ORDER = 2
HY_BANDS = 16
HY_FAST_DECAY = 0.3
HY_SLOW_DECAY = 1.5
HY_TARGET = 1e-2

LANES = 128
SUBLANES = 8
DFT_N1 = 256
ROW_CHUNK = 128
VMEM_LIMIT = 56 * 1024 * 1024


def _params(sem):
    return pltpu.CompilerParams(dimension_semantics=sem, vmem_limit_bytes=VMEM_LIMIT)


def _rms(x, g=None):
    y = x * lax.rsqrt(jnp.mean(x * x, axis=-1, keepdims=True) + EPS)
    return y if g is None else y * g


def _dot(a, b):
    return jnp.dot(a, b, preferred_element_type=F32)


def _dot_hi(a, b):
    return jnp.dot(a, b, preferred_element_type=F32, precision=HIGHEST)


def _dot3_tn(a, b):
    a_hi, b_hi = a.astype(BF16), b.astype(BF16)
    a_lo = (a - a_hi.astype(F32)).astype(BF16)
    b_lo = (b - b_hi.astype(F32)).astype(BF16)
    tn = lambda x, y: lax.dot_general(x, y, (((0,), (0,)), ((), ())), preferred_element_type=F32)
    return tn(a_hi, b_hi) + tn(a_lo, b_hi) + tn(a_hi, b_lo)


def _cast_body(x_ref, *rest, scale):
    o_ref = rest[-1]
    x = x_ref[...]
    if len(rest) == 2:
        x = x * rest[0][...]
    o_ref[...] = (x if scale == 1.0 else x * scale).astype(o_ref.dtype)


def _to_bf16(w, layer, cols=None, *, scale=1.0, row_gain=None, block_bytes=4 << 20):
    _, rows, width = w.shape
    cols = width if cols is None else cols
    tr = rows
    while tr * cols * 4 > block_bytes and tr % 32 == 0:
        tr //= 2
    args, in_specs = [w], [pl.BlockSpec((None, tr, cols), lambda i: (layer, i, 0))]
    if row_gain is not None:
        args.append(row_gain.reshape(rows, 1))
        in_specs.append(pl.BlockSpec((tr, 1), lambda i: (i, 0)))
    return pl.pallas_call(
        functools.partial(_cast_body, scale=scale),
        out_shape=jax.ShapeDtypeStruct((rows, cols), BF16),
        grid=(rows // tr,),
        in_specs=in_specs,
        out_specs=pl.BlockSpec((tr, cols), lambda i: (i, 0)),
        compiler_params=_params(("parallel",)),
        name="to_bf16",
    )(*args)


def _ffn_body(x_ref, wg_ref, wu_ref, wd_ref, fn_ref, o_ref, xn_ref, *, final, sub):
    j = pl.program_id(1)

    row_chunks = [slice(r, r + ROW_CHUNK) for r in range(0, x_ref.shape[0], ROW_CHUNK)]

    @pl.when(j == 0)
    def _():
        for rs in row_chunks:
            x = x_ref[rs, :]
            xn_ref[rs, :] = _rms(x).astype(BF16)
            o_ref[rs, :] = x

    xn = xn_ref[...]
    hs = []
    for c in range(wg_ref.shape[1] // sub):
        sl = slice(c * sub, (c + 1) * sub)
        g = _dot(xn, wg_ref[:, sl])
        u = _dot(xn, wu_ref[:, sl])
        hs.append((g * jax.nn.sigmoid(g) * u).astype(BF16))
    o_ref[...] += _dot(jnp.concatenate(hs, axis=1), wd_ref[...])

    if final:
        @pl.when(j == pl.num_programs(1) - 1)
        def _():
            for rs in row_chunks:
                o_ref[rs, :] = _rms(o_ref[rs, :], fn_ref[...])


def _ffn(x, w_gu, half_w_down, final_g, *, final, tm=1024, tf=512, sub=256):
    w_down = half_w_down
    L, D = x.shape
    F = w_down.shape[0]
    tm = min(tm, L)
    nf = F // tf
    return pl.pallas_call(
        functools.partial(_ffn_body, final=final, sub=min(sub, tf)),
        out_shape=jax.ShapeDtypeStruct((L, D), F32),
        grid=(L // tm, nf),
        in_specs=[
            pl.BlockSpec((tm, D), lambda i, j: (i, 0)),
            pl.BlockSpec((D, tf), lambda i, j: (0, j)),
            pl.BlockSpec((D, tf), lambda i, j: (0, j + nf)),
            pl.BlockSpec((tf, D), lambda i, j: (j, 0)),
            pl.BlockSpec((1, D), lambda i, j: (0, 0)),
        ],
        out_specs=pl.BlockSpec((tm, D), lambda i, j: (i, 0)),
        scratch_shapes=[pltpu.VMEM((tm, D), BF16)],
        compiler_params=_params(("parallel", "arbitrary")),
        name="ffn",
    )(x, w_gu, w_gu, w_down, final_g.reshape(1, D))


def _conv3_inner(x, w):
    n = x.shape[0]
    y = pltpu.roll(x, 1, axis=0) * w[0:1] + x * w[1:2] + pltpu.roll(x, n - 1, axis=0) * w[2:3]
    return y[SUBLANES:n - SUBLANES]


def _proj_body(h_ref, hp_ref, hn_ref, w_ref, b_ref, wlr_ref, blr_ref, caw_ref, cbw_ref,
               p_ref, lr_ref, a_ref, z_ref, g1_ref, g2_ref, u_ref, vb_ref, *, tn, wa):
    i = pl.program_id(0)
    tm = h_ref.shape[0]
    xn = _rms(h_ref[...])
    xn_bf = xn.astype(BF16)
    u_ref[...] = xn_bf
    lr_ref[...] = _dot(xn_bf, wlr_ref[...]) + blr_ref[...]
    conv_w = 6 * wa
    v_lo = p_ref.shape[1] - vb_ref.shape[1]
    v_hi = v_lo + vb_ref.shape[1]
    for off in range(0, p_ref.shape[1] + vb_ref.shape[1], tn):
        src = slice(conv_w + off, conv_w + off + tn)
        y = _dot(xn_bf, w_ref[:, src]) + b_ref[:, src]
        if off < v_lo:
            p_ref[:, off:off + tn] = y
        elif off < v_hi:
            vb_ref[:, off - v_lo:off - v_lo + tn] = y.astype(BF16)
        else:
            p_ref[:, off - vb_ref.shape[1]:off - vb_ref.shape[1] + tn] = y

    xh = jnp.concatenate([_rms(hp_ref[...]), xn, _rms(hn_ref[...])], axis=0).astype(BF16)
    row = lax.broadcasted_iota(jnp.int32, (tm + 2 * SUBLANES, 1), 0)
    has_prev = (i > 0).astype(F32)
    has_next = (i < pl.num_programs(0) - 1).astype(F32)
    valid = jnp.where(row < SUBLANES, has_prev, jnp.where(row >= tm + SUBLANES, has_next, 1.0))

    def sec(k):
        sl = slice(k * wa, (k + 1) * wa)
        return (_dot(xh, w_ref[:, sl]) + b_ref[:, sl]) * valid

    xa, ba, ca = sec(0), sec(1), sec(2)
    a_ref[...] = (ba[SUBLANES:tm + SUBLANES] * _conv3_inner(ca * xa, caw_ref[...])).astype(BF16)
    for k, o_ref in enumerate((z_ref, g1_ref, g2_ref)):
        _store_groups(o_ref, _conv3_inner(sec(3 + k), cbw_ref[k]))


def _proj(h, w, b, w_lr, b_lr, conv_a, conv_b, *, wa, wv, tm=256, tn=512):
    L, D = h.shape
    conv_w = 6 * wa
    n_scan = w.shape[1] - conv_w - wv
    rb = tm // SUBLANES
    nrb = L // SUBLANES
    resident = lambda shape: pl.BlockSpec(shape, lambda i: (0,) * len(shape), pipeline_mode=pl.Buffered(1))
    grouped = pl.BlockSpec((wa // LANES, tm, LANES), lambda i: (0, i, 0))
    return pl.pallas_call(
        functools.partial(_proj_body, tn=tn, wa=wa),
        out_shape=(jax.ShapeDtypeStruct((L, n_scan), F32), jax.ShapeDtypeStruct((L, LANES), F32),
                   jax.ShapeDtypeStruct((L, wa), BF16))
        + (jax.ShapeDtypeStruct((wa // LANES, L, LANES), F32),) * 3
        + (jax.ShapeDtypeStruct((L, D), BF16), jax.ShapeDtypeStruct((L, wv), BF16)),
        grid=(L // tm,),
        in_specs=[
            pl.BlockSpec((tm, D), lambda i: (i, 0)),
            pl.BlockSpec((SUBLANES, D), lambda i: (jnp.maximum(i * rb - 1, 0), 0)),
            pl.BlockSpec((SUBLANES, D), lambda i: (jnp.minimum((i + 1) * rb, nrb - 1), 0)),
            resident(w.shape), resident((1, w.shape[1])),
            resident((D, LANES)), resident((1, LANES)), resident((3, wa)), resident((3, 3, wa)),
        ],
        out_specs=(pl.BlockSpec((tm, n_scan), lambda i: (i, 0)),
                   pl.BlockSpec((tm, LANES), lambda i: (i, 0)),
                   pl.BlockSpec((tm, wa), lambda i: (i, 0)), grouped, grouped, grouped,
                   pl.BlockSpec((tm, D), lambda i: (i, 0)), pl.BlockSpec((tm, wv), lambda i: (i, 0))),
        compiler_params=_params(("parallel",)),
        name="proj",
    )(h, h, h, w, b, w_lr, b_lr, conv_a, conv_b)


def _store_groups(o_ref, x):
    for g in range(o_ref.shape[0]):
        o_ref[g] = x[:, g * LANES:(g + 1) * LANES]


def _load_groups(ref):
    return jnp.concatenate([ref[g] for g in range(ref.shape[0])], axis=1)


def _filter_body(w1t_ref, w1cs_ref, b1_ref, w2_ref, b2_ref, w3_ref, b3_ref, fr_ref, bands_ref, wo_ref, dl_ref,
                 k_ref, s_ref, *, half_tiles, tr, seq):
    i = pl.program_id(0)

    def lag_of(n):
        return jnp.where(n < seq, n, jnp.where(n == seq, 0, 2 * seq - n)).astype(F32)

    lag_row = lag_of(i * tr + lax.broadcasted_iota(jnp.int32, (1, tr), 1))
    ang = ((2.0 * math.pi / seq) * lag_row) * bands_ref[...]
    feats = jnp.concatenate([jnp.cos(ang), -jnp.sin(ang)], axis=0)
    fr = fr_ref[...]
    pre = w1t_ref[...] * (lag_row / max(seq - 1, 1)) + _dot_hi(w1cs_ref[...], feats) + b1_ref[...]
    h = jnp.sin(fr * pre)
    h = jnp.sin(fr * (_dot_hi(w2_ref[...], h) + b2_ref[...]))
    h = jnp.sin(fr * (_dot_hi(w3_ref[...], h) + b3_ref[...]))
    t = lag_of(i * tr + lax.broadcasted_iota(jnp.int32, (tr, 1), 0)) / max(seq - 1, 1)
    decay = jnp.exp(-t * jnp.abs(dl_ref[...]))
    hd = _dot3_tn(h, wo_ref[0]) * decay

    @pl.when(i % half_tiles == 0)
    def _():
        s_ref[...] = jnp.zeros_like(s_ref)

    s_ref[0] += jnp.sum(jnp.abs(hd), axis=0, keepdims=True)
    row = lax.broadcasted_iota(jnp.int32, hd.shape, 0)
    zero_row = jnp.where(i == half_tiles, 0, -1)
    _store_groups(k_ref, jnp.where(row == zero_row, 0.0, hd))


def _filters(seq, w1, b1, w2, b2, w3, b3, freq, w_out2, deltas2, *, tr=1024):
    n_rows = 2 * seq
    hid = w2.shape[0]
    width = w_out2.shape[2]
    half_tiles = n_rows // (2 * tr)
    bands = jnp.linspace(1e-4, HY_BANDS - 1, HY_BANDS, dtype=F32).reshape(HY_BANDS, 1)
    full = lambda shape: pl.BlockSpec(shape, lambda i: (0,) * len(shape))
    colv = lambda v: v.reshape(hid, 1)
    return pl.pallas_call(
        functools.partial(_filter_body, half_tiles=half_tiles, tr=tr, seq=seq),
        out_shape=(jax.ShapeDtypeStruct((width // LANES, n_rows, LANES), F32),
                   jax.ShapeDtypeStruct((2, 1, width), F32)),
        grid=(n_rows // tr,),
        in_specs=[
            full((hid, 1)), full((hid, 2 * HY_BANDS)), full((hid, 1)), full((hid, hid)), full((hid, 1)),
            full((hid, hid)), full((hid, 1)), full((hid, 1)), full((HY_BANDS, 1)),
            pl.BlockSpec((1, hid, width), lambda i: (i // half_tiles, 0, 0)),
            full((1, width)),
        ],
        out_specs=(pl.BlockSpec((width // LANES, tr, LANES), lambda i: (0, i, 0)),
                   pl.BlockSpec((1, 1, width), lambda i: (i // half_tiles, 0, 0))),
        compiler_params=_params(("arbitrary",)),
        name="hyena_filters",
    )(w1[0:1].T, w1[1:].T, colv(b1), w2.T, colv(b2), w3.T, colv(b3), colv(freq), bands, w_out2, deltas2)


def _dft_consts(n1, n2):
    n = n1 * n2
    k = np.arange(n1)
    m = np.arange(n2)
    ang2 = 2.0 * np.pi * ((m[:, None] * m[None, :]) % n2) / n2
    c2, s2 = np.cos(ang2), np.sin(ang2)
    as_bf = lambda a: jnp.asarray(np.asarray(a, np.float32).astype(jnp.bfloat16))
    time_index = k[None, None, :] * n2 + m[:, None, None]
    ang = 2.0 * np.pi * ((k[None, :, None] * time_index) % n) / n
    stage = np.concatenate([np.cos(ang), -np.sin(ang)], axis=1)
    half = n1 // 2
    return dict(
        fwd1=as_bf(stage),
        inv1=as_bf(stage.transpose(0, 2, 1)[:, :half] / n),
        fwd2=as_bf(np.block([[c2, s2], [-s2, c2]])),
        inv2=as_bf(np.block([[c2, -s2], [s2, c2]])),
    )


def _flatten_block(ref, buf):
    groups, n, t2, lanes = ref.shape
    for g in range(groups):
        for part in range(t2 // SUBLANES):
            rows = slice(part * SUBLANES, (part + 1) * SUBLANES)
            buf[g, part] = ref[g, :, rows, :].reshape(n * SUBLANES, lanes)


def _unflatten_block(buf, ref):
    groups, n, t2, lanes = ref.shape
    for g in range(groups):
        for part in range(t2 // SUBLANES):
            rows = slice(part * SUBLANES, (part + 1) * SUBLANES)
            ref[g, :, rows, :] = buf[g, part].reshape(n, SUBLANES, lanes)


def _load_strided(buf, s):
    part, sub = divmod(s, SUBLANES)
    n = buf.shape[2] // SUBLANES
    return jnp.concatenate([buf[g, part, pl.ds(sub, n, stride=SUBLANES), :] for g in range(buf.shape[0])],
                           axis=1)


def _store_strided(buf, s, x):
    part, sub = divmod(s, SUBLANES)
    n = buf.shape[2] // SUBLANES
    for g in range(buf.shape[0]):
        buf[g, part, pl.ds(sub, n, stride=SUBLANES), :] = x[:, g * LANES:(g + 1) * LANES]


def _flat_scratch(groups, n, t2, dtype=F32):
    return pltpu.VMEM((groups, t2 // SUBLANES, n * SUBLANES, LANES), dtype)


U32 = jnp.uint32


def _pack_complex(re, im):
    hi = lax.bitcast_convert_type(re.astype(BF16).astype(F32), U32)
    lo = lax.bitcast_convert_type(im.astype(BF16).astype(F32), U32)
    return hi | (lo >> 16)


def _unpack_complex(w):
    re = lax.bitcast_convert_type(w & jnp.uint32(0xFFFF0000), F32)
    im = lax.bitcast_convert_type(w << 16, F32)
    return re, im


def _outer_fwd_body(x_ref, f_ref, *rest, scaled):
    if scaled:
        s_ref, a_ref, xb, ab = rest
        inv = 1.0 / (s_ref[...] + EPS)
    else:
        a_ref, xb, ab = rest
    rows, t2 = x_ref.shape[1:3]
    n1 = a_ref.shape[1]
    _flatten_block(x_ref, xb)
    for s in range(t2):
        x = _load_strided(xb, s)
        if scaled:
            half = rows // 2
            x = jnp.concatenate([x[:half] * inv[0], x[half:] * inv[1]], axis=0)
        a = _dot(f_ref[s], x.astype(BF16))
        _store_strided(ab, s, _pack_complex(a[:n1], a[n1:]))
    _unflatten_block(ab, a_ref)


def _outer_fwd(x4, fwd1, sums=None, *, t2=SUBLANES, gt=4):
    groups, rows, n2, _ = x4.shape
    n1 = fwd1.shape[2]
    scaled = sums is not None
    blk = lambda r: pl.BlockSpec((gt, r, t2, LANES), lambda j, c: (c, 0, j, 0))
    args = [x4, fwd1]
    in_specs = [blk(rows), pl.BlockSpec((t2, 2 * n1, rows), lambda j, c: (j, 0, 0))]
    if scaled:
        args.append(sums)
        in_specs.append(pl.BlockSpec((2, 1, gt * LANES), lambda j, c: (0, 0, c)))
    return pl.pallas_call(
        functools.partial(_outer_fwd_body, scaled=scaled),
        out_shape=jax.ShapeDtypeStruct((groups, n1, n2, LANES), U32),
        grid=(n2 // t2, groups // gt),
        in_specs=in_specs,
        out_specs=blk(n1),
        scratch_shapes=[_flat_scratch(gt, rows, t2), _flat_scratch(gt, n1, t2, U32)],
        compiler_params=_params(("parallel", "parallel")),
        name="dft_outer_fwd",
    )(*args)


def _load_slab(ref, s):
    return jnp.concatenate([ref[g, s] for g in range(ref.shape[0])], axis=1)


def _store_slab(ref, s, x):
    for g in range(ref.shape[0]):
        ref[g, s] = x[:, g * LANES:(g + 1) * LANES]


def _mid_conv_body(a_ref, k_ref, f2_ref, g2_ref, d_ref):
    n2 = a_ref.shape[2]
    width = a_ref.shape[0] * LANES
    for s in range(a_ref.shape[1]):
        re, im = _unpack_complex(jnp.concatenate([_load_slab(a_ref, s), _load_slab(k_ref, s)], axis=1))
        x = _dot(f2_ref[...], jnp.concatenate([re, im], axis=0).astype(BF16))
        xr, xi, kr, ki = x[:n2, :width], x[n2:, :width], x[:n2, width:], x[n2:, width:]
        yr = xr * kr - xi * ki
        yi = xr * ki + xi * kr
        c = _dot(g2_ref[...], jnp.concatenate([yr, yi], axis=0).astype(BF16))
        _store_slab(d_ref, s, _pack_complex(c[:n2], c[n2:]))


def _mid(a, kf, order, fwd2, inv2, *, kb=8):
    groups, n1, n2, _ = a.shape
    slab = pl.BlockSpec((groups, kb, n2, LANES), lambda i: (0, i, 0, 0))
    kslab = pl.BlockSpec((groups, kb, n2, LANES), lambda i: (order, i, 0, 0))
    mat = pl.BlockSpec((2 * n2, 2 * n2), lambda i: (0, 0))
    return pl.pallas_call(
        _mid_conv_body,
        out_shape=jax.ShapeDtypeStruct(a.shape, U32),
        grid=(n1 // kb,),
        in_specs=[slab, kslab, mat, mat],
        out_specs=slab,
        compiler_params=_params(("parallel",)),
        name="hyena_mid_conv",
    )(a, kf, fwd2, inv2)


def _outer_inv_body(d_ref, e_ref, gate_ref, z_ref, skip_ref, o_ref, db, gb, zb, ob):
    skip = skip_ref[...]
    t2 = d_ref.shape[2]
    for ref, buf in ((d_ref, db), (gate_ref, gb), (z_ref, zb)):
        _flatten_block(ref, buf)
    for s in range(t2):
        dr, di = _unpack_complex(_load_strided(db, s))
        y = _dot(e_ref[s], jnp.concatenate([dr, di], axis=0).astype(BF16))
        _store_strided(ob, s, _load_strided(gb, s) * (y + skip * _load_strided(zb, s)))
    _unflatten_block(ob, o_ref)


def _outer_inv(d, inv1, gate4, z4, skip, *, t2=SUBLANES):
    groups, n1, n2, _ = d.shape
    half = n1 // 2
    blk = lambda r: pl.BlockSpec((groups, r, t2, LANES), lambda j: (0, 0, j, 0))
    return pl.pallas_call(
        _outer_inv_body,
        out_shape=jax.ShapeDtypeStruct((groups, half, n2, LANES), F32),
        grid=(n2 // t2,),
        in_specs=[blk(n1), pl.BlockSpec((t2, half, 2 * n1), lambda j: (j, 0, 0)), blk(half), blk(half),
                  pl.BlockSpec((1, groups * LANES), lambda j: (0, 0))],
        out_specs=blk(half),
        scratch_shapes=[_flat_scratch(groups, n1, t2, U32)] + [_flat_scratch(groups, half, t2)] * 3,
        compiler_params=_params(("parallel",)),
        name="dft_outer_inv",
    )(d, inv1, gate4, z4, skip)


def _log_sigmoid(x):
    return jnp.minimum(x, 0.0) - jnp.log1p(jnp.exp(-jnp.abs(x)))


_NT = (((1,), (1,)), ((), ()))
_TN = (((0,), (0,)), ((), ()))


def _chunk_matrices(sub):
    i = np.arange(sub)
    same = (i[:, None] // GLA_CHUNK) == (i[None, :] // GLA_CHUNK)
    fwd = same & (i[None, :] <= i[:, None])
    bwd = same & (i[None, :] >= i[:, None])
    return jnp.asarray(np.stack([fwd, bwd, same]), F32)


def _gla_gates_body(q_ref, k_ref, lr_ref, w_ref, b_ref, cm_ref, qe_ref, ke_ref, kd_ref, dec_ref, *, scale):
    tg, width = q_ref.shape
    sub = cm_ref.shape[1]
    chunk = GLA_CHUNK
    same = cm_ref[2].astype(BF16)
    for s in range(tg // sub):
        rs = slice(s * sub, (s + 1) * sub)
        g = _log_sigmoid(_dot(lr_ref[rs, :].astype(BF16), w_ref[...]) + b_ref[...]) / GLA_TAU
        g_hi = g.astype(BF16)
        g_lo = (g - g_hi.astype(F32)).astype(BF16)
        q = q_ref[rs, :] * scale
        k = k_ref[rs, :]
        for d in range(2):
            cs = slice(d * width, (d + 1) * width)
            cum = cm_ref[d].astype(BF16)
            bc = _dot(cum, g_hi[:, cs]) + _dot(cum, g_lo[:, cs])
            bl = _dot(same, g_hi[:, cs]) + _dot(same, g_lo[:, cs])
            qe_ref[d, rs, :] = (q * jnp.exp(bc)).astype(BF16)
            ke_ref[d, rs, :] = (k * jnp.exp(-bc)).astype(BF16)
            kd_ref[d, rs, :] = (k * jnp.exp(bl - bc)).astype(BF16)
            for c in range(sub // chunk):
                row = s * (sub // chunk) + c
                dec_ref[d, row:row + 1, :] = jnp.exp(bl[c * chunk:c * chunk + 1])


def _gla_gates(p, lr, w_both, b_both, *, q_off, k_off, dk, heads, tg=1024, sub=256):
    L = p.shape[0]
    tg = min(tg, L)
    wk = heads * dk
    opnd = jax.ShapeDtypeStruct((2, L, wk), BF16)
    row = lambda width, off: pl.BlockSpec((tg, width), lambda i: (i, off // width))
    both = pl.BlockSpec((2, tg, wk), lambda i: (0, i, 0))
    const = lambda shape: pl.BlockSpec(shape, lambda i: (0,) * len(shape))
    return pl.pallas_call(
        functools.partial(_gla_gates_body, scale=dk ** -0.5),
        out_shape=(opnd, opnd, opnd, jax.ShapeDtypeStruct((2, L // GLA_CHUNK, wk), F32)),
        grid=(L // tg,),
        in_specs=[row(wk, q_off), row(wk, k_off), row(LANES, 0),
                  const((LANES, 2 * wk)), const((1, 2 * wk)), const((3, sub, sub))],
        out_specs=(both, both, both, pl.BlockSpec((2, tg // GLA_CHUNK, wk), lambda i: (0, i, 0))),
        compiler_params=_params(("parallel",)),
        name="gla_gates",
    )(p, p, lr, w_both, b_both, _chunk_matrices(sub))


def _gla_direction(qe_ref, ke_ref, kd_ref, v_ref, dec_ref, mask, o_ref, st_ref, *, reverse, dk, dv):
    tb = qe_ref.shape[0]
    sub = mask.shape[0]
    chunk = GLA_CHUNK
    for hd in range(st_ref.shape[0]):
        ks, vs = slice(hd * dk, (hd + 1) * dk), slice(hd * dv, (hd + 1) * dv)
        intras = []
        for s in range(tb // sub):
            rs = slice(s * sub, (s + 1) * sub)
            a = lax.dot_general(qe_ref[rs, ks], ke_ref[rs, ks], _NT, preferred_element_type=F32)
            intra = _dot(jnp.where(mask > 0.5, a, 0.0).astype(BF16), v_ref[rs, vs])
            intras += [intra[c * chunk:(c + 1) * chunk] for c in range(sub // chunk)]
        st = st_ref[hd]
        chunks = range(tb // chunk)
        for c in (reversed(chunks) if reverse else chunks):
            sl = slice(c * chunk, (c + 1) * chunk)
            o_ref[sl, vs] = intras[c] + lax.dot_general(qe_ref[sl, ks], st.astype(BF16), _NT,
                                                        preferred_element_type=F32)
            upd = lax.dot_general(v_ref[sl, vs], kd_ref[sl, ks], _TN, preferred_element_type=F32)
            st = st * dec_ref[c:c + 1, ks] + upd
        st_ref[hd] = st


def _gla_fwd_body(qe, ke, kd, v, dec, cm_ref, o_ref, st_ref, *, dk, dv):
    @pl.when(pl.program_id(0) == 0)
    def _():
        st_ref[...] = jnp.zeros_like(st_ref)

    _gla_direction(qe, ke, kd, v, dec, cm_ref[0], o_ref, st_ref, reverse=False, dk=dk, dv=dv)


def _gla_bwd_body(qe, ke, kd, v, dec, cm_ref, of_ref, r_ref, gn_ref, c_ref, st_ref, ob_ref, *, dk, dv):
    @pl.when(pl.program_id(0) == 0)
    def _():
        st_ref[...] = jnp.zeros_like(st_ref)

    _gla_direction(qe, ke, kd, v, dec, cm_ref[1], ob_ref, st_ref, reverse=True, dk=dk, dv=dv)
    for rs in [slice(r0, r0 + ROW_CHUNK) for r0 in range(0, ob_ref.shape[0], ROW_CHUNK)]:
        r = r_ref[rs, :]
        gate = r * jax.nn.sigmoid(r)
        for hd in range(ob_ref.shape[1] // dv):
            sl = slice(hd * dv, (hd + 1) * dv)
            o = of_ref[rs, sl] + ob_ref[rs, sl]
            c_ref[rs, sl] = (_rms(o, gn_ref[...]) * gate[:, sl]).astype(BF16)


def _gla(qe, ke, kd, dec, vb, p, gla_norm, *, r_off, dk, dv, tb=1024, sub=256):
    L, wv = vb.shape
    tb = min(tb, L)
    nb = L // tb
    heads = wv // dv
    wk = heads * dk
    nc = tb // GLA_CHUNK
    cm = _chunk_matrices(sub)
    cm_spec = pl.BlockSpec((3, sub, sub), lambda i: (0, 0, 0))

    def side(rev):
        blk = (lambda i: nb - 1 - i) if rev else (lambda i: i)
        opnd = pl.BlockSpec((None, tb, wk), lambda i: (int(rev), blk(i), 0))
        return blk, [opnd, opnd, opnd, pl.BlockSpec((tb, wv), lambda i: (blk(i), 0)),
                     pl.BlockSpec((None, nc, wk), lambda i: (int(rev), blk(i), 0))]

    state = pltpu.VMEM((heads, dv, dk), F32)
    _, fwd_specs = side(False)
    o_f = pl.pallas_call(
        functools.partial(_gla_fwd_body, dk=dk, dv=dv),
        out_shape=jax.ShapeDtypeStruct((L, wv), F32),
        grid=(nb,),
        in_specs=fwd_specs + [cm_spec],
        out_specs=pl.BlockSpec((tb, wv), lambda i: (i, 0)),
        scratch_shapes=[state],
        compiler_params=_params(("arbitrary",)),
        name="gla_fwd",
    )(qe, ke, kd, vb, dec, cm)
    blk, bwd_specs = side(True)
    row = lambda col: pl.BlockSpec((tb, wv), lambda i: (blk(i), col))
    return pl.pallas_call(
        functools.partial(_gla_bwd_body, dk=dk, dv=dv),
        out_shape=jax.ShapeDtypeStruct((L, wv), BF16),
        grid=(nb,),
        in_specs=bwd_specs + [cm_spec, row(0), row(r_off // wv), pl.BlockSpec((1, dv), lambda i: (0, 0))],
        out_specs=row(0),
        scratch_shapes=[state, pltpu.VMEM((tb, wv), F32)],
        compiler_params=_params(("arbitrary",)),
        name="gla_bwd",
    )(qe, ke, kd, vb, dec, cm, o_f, p, gla_norm.reshape(1, dv))


def _merge_body(u_ref, a_ref, z_ref, c_ref,
                wga_ref, wgb_ref, wgc_ref, bga_ref, bgb_ref, bgc_ref,
                wa_ref, wb_ref, wc_ref, m_ref, zb_ref):
    @pl.when(pl.program_id(1) == 0)
    def _():
        zb_ref[...] = _load_groups(z_ref).astype(BF16)

    u = u_ref[...]
    m = jax.nn.sigmoid(_dot(u, wga_ref[...]) + bga_ref[...]) * _dot(a_ref[...], wa_ref[...])
    m += jax.nn.sigmoid(_dot(u, wgb_ref[...]) + bgb_ref[...]) * _dot(zb_ref[...], wb_ref[...])
    m += jax.nn.sigmoid(_dot(u, wgc_ref[...]) + bgc_ref[...]) * _dot(c_ref[...], wc_ref[...])
    m_ref[...] = m.astype(BF16)


def _merge(u, a_pre, z, c_pre, w_gate, b_gate, w_a, w_b, w_c, *, tm=1024, tn=512):
    L, D = u.shape
    wa, wb, wc = w_a.shape[0], w_b.shape[0], w_c.shape[0]
    tm = min(tm, L)
    nj = D // tn
    row = lambda width: pl.BlockSpec((tm, width), lambda i, j: (i, 0))
    gate_w = lambda k: pl.BlockSpec((D, tn), lambda i, j: (0, j + k * nj))
    gate_b = lambda k: pl.BlockSpec((1, tn), lambda i, j: (0, j + k * nj))
    br_w = lambda width: pl.BlockSpec((width, tn), lambda i, j: (0, j))
    return pl.pallas_call(
        _merge_body,
        out_shape=jax.ShapeDtypeStruct((L, D), BF16),
        grid=(L // tm, nj),
        in_specs=[
            row(D), row(wa), pl.BlockSpec((wb // LANES, tm, LANES), lambda i, j: (0, i, 0)), row(wc),
            gate_w(0), gate_w(1), gate_w(2), gate_b(0), gate_b(1), gate_b(2),
            br_w(wa), br_w(wb), br_w(wc),
        ],
        out_specs=pl.BlockSpec((tm, tn), lambda i, j: (i, j)),
        scratch_shapes=[pltpu.VMEM((tm, wb), BF16)],
        compiler_params=_params(("parallel", "arbitrary")),
        name="merge",
    )(u, a_pre, z, c_pre, w_gate, w_gate, w_gate, b_gate, b_gate, b_gate, w_a, w_b, w_c)


def _out_proj_body(h_ref, m_ref, w_ref, o_ref, *, tn):
    m = m_ref[...]
    for c in range(o_ref.shape[1] // tn):
        sl = slice(c * tn, (c + 1) * tn)
        o_ref[:, sl] = h_ref[:, sl] + _dot(m, w_ref[:, sl])


def _out_proj(h, m, w_o, *, tm=512, tn=512):
    L, D = h.shape
    row = pl.BlockSpec((tm, D), lambda i: (i, 0))
    return pl.pallas_call(
        functools.partial(_out_proj_body, tn=tn),
        out_shape=jax.ShapeDtypeStruct((L, D), F32),
        grid=(L // tm,),
        in_specs=[row, row, pl.BlockSpec((D, D), lambda i: (0, 0), pipeline_mode=pl.Buffered(1))],
        out_specs=row,
        compiler_params=_params(("parallel",)),
        name="out_proj",
    )(h, m, w_o)


def _hyena(z0, g1, g2, skip, filt, consts):
    groups, L, _ = z0.shape
    n1 = DFT_N1
    n2 = 2 * L // n1
    k_time, sums = filt
    kf = _outer_fwd(k_time.reshape(k_time.shape[0], n1, n2, LANES), consts["fwd1"], sums)
    z = z0.reshape(groups, n1 // 2, n2, LANES)
    for order, gate in enumerate((g1, g2)):
        a = _outer_fwd(z, consts["fwd1"], t2=min(2 * SUBLANES, n2))
        d = _mid(a, kf, order, consts["fwd2"], consts["inv2"])
        z = _outer_inv(d, consts["inv1"], gate.reshape(z.shape), z, skip[order].reshape(1, groups * LANES))
    return z.reshape(groups, L, LANES)


def kernel(x, ffn1_norm, ffn1_w_gu, ffn1_w_down, mix_norm, w_in, b_in, conv_a, conv_b, hf_w1, hf_b1, hf_w2, hf_b2, hf_w3, hf_b3, hf_w_out, hf_freq, hy_skip, gk_w2, gk_b, gla_norm, w_br_a, w_br_b, w_br_c, w_o, ffn2_norm, ffn2_w_gu, ffn2_w_down, final_norm):
    bsz, L, D = x.shape
    depth = ffn1_norm.shape[0]
    wa = conv_a.shape[2]
    wb = conv_b.shape[3]
    wck = gk_w2.shape[3]
    dv = gla_norm.shape[1]
    wcv = GLA_HEADS * dv
    dk = wck // GLA_HEADS
    hid = hf_w2.shape[1]
    assert wa == wb, "the projection kernel tiles both conv branches with one section width"
    n_main = 3 * wa + 3 * wb + 2 * wck + 2 * wcv
    q_off, k_off, r_off = 0, wck, 2 * wck
    lr_off = n_main
    gate_off = n_main + 2 * GLA_RANK
    n1 = DFT_N1
    n2 = 2 * L // n1

    consts = _dft_consts(n1, n2)
    deltas =jnp.linspace(math.log(HY_TARGET) / HY_SLOW_DECAY, math.log(HY_TARGET) / HY_FAST_DECAY, wb, dtype=F32)
    deltas2 = jnp.tile(deltas.reshape(1, wb), (1, HY_ORDER))

    outs = []
    for bi in range(bsz):
        h = x[bi]
        for l in range(depth):
            h = _ffn(h, _to_bf16(ffn1_w_gu, l, row_gain=ffn1_norm[l]), _to_bf16(ffn1_w_down, l, scale=0.5),
                     final_norm, final=False)

            w_mix = w_in[l] * mix_norm[l][:, None]
            w_lr = jnp.pad(w_mix[:, lr_off:gate_off], ((0, 0), (0, LANES - 2 * GLA_RANK))).astype(BF16)
            b_lr = jnp.pad(b_in[l][lr_off:gate_off], (0, LANES - 2 * GLA_RANK)).reshape(1, LANES)
            p, lr, a_pre, z0, g1, g2, u, vb = _proj(
                h, w_mix[:, :n_main].astype(BF16), b_in[l][:n_main].reshape(1, n_main),
                w_lr, b_lr, conv_a[l], conv_b[l], wa=wa, wv=wcv)

            w_out2 = hf_w_out[l].reshape(hid, HY_ORDER, 2, wb).transpose(2, 0, 1, 3).reshape(2, hid, HY_ORDER * wb)
            filt = _filters(L, hf_w1[l], hf_b1[l], hf_w2[l], hf_b2[l], hf_w3[l], hf_b3[l], hf_freq[l],
                            w_out2, deltas2)
            zb = _hyena(z0, g1, g2, hy_skip[l], filt, consts)

            w_both = jnp.zeros((LANES, 2 * wck), F32)
            w_both = w_both.at[:GLA_RANK, :wck].set(gk_w2[l, 0]).at[GLA_RANK:2 * GLA_RANK, wck:].set(gk_w2[l, 1])
            qe, ke, kd, dec = _gla_gates(p, lr, w_both.astype(BF16), gk_b[l].reshape(1, 2 * wck),
                                         q_off=q_off, k_off=k_off, dk=dk, heads=GLA_HEADS)
            c_pre = _gla(qe, ke, kd, dec, vb, p, gla_norm[l], r_off=r_off, dk=dk, dv=dv)

            merged = _merge(u, a_pre, zb, c_pre,
                            w_mix[:, gate_off:].astype(BF16), b_in[l][gate_off:].reshape(1, 3 * D),
                            _to_bf16(w_br_a, l), _to_bf16(w_br_b, l), _to_bf16(w_br_c, l))
            h = _out_proj(h, merged, _to_bf16(w_o, l))

            h = _ffn(h, _to_bf16(ffn2_w_gu, l, row_gain=ffn2_norm[l]), _to_bf16(ffn2_w_down, l, scale=0.5),
                     final_norm, final=(l == depth - 1))
        outs.append(h)
    return jnp.stack(outs, axis=0)
```

```python
import functools
import math

import numpy as np
import jax
import jax.numpy as jnp
from jax import lax
from jax.experimental import pallas as pl
from jax.experimental.pallas import tpu as pltpu

F32 = jnp.float32
BF16 = jnp.bfloat16
HIGHEST = lax.Precision.HIGHEST

EPS = 1e-5
GLA_HEADS = 4
GLA_RANK = 16
GLA_TAU = 16.0
GLA_CHUNK = 64
HY_ORDER = 2
HY_BANDS = 16
HY_FAST_DECAY = 0.3
HY_SLOW_DECAY = 1.5
HY_TARGET = 1e-2

LANES = 128
SUBLANES = 8
DFT_N1 = 256
ROW_CHUNK = 128
VMEM_LIMIT = 56 * 1024 * 1024


def _params(sem):
    return pltpu.CompilerParams(dimension_semantics=sem, vmem_limit_bytes=VMEM_LIMIT)


def _rms(x, g):
    return x * lax.rsqrt(jnp.mean(x * x, axis=-1, keepdims=True) + EPS) * g


def _dot(a, b):
    return jnp.dot(a, b, preferred_element_type=F32)


def _dot_hi(a, b):
    return jnp.dot(a, b, preferred_element_type=F32, precision=HIGHEST)


def _dot3_tn(a, b):
    a_hi, b_hi = a.astype(BF16), b.astype(BF16)
    a_lo = (a - a_hi.astype(F32)).astype(BF16)
    b_lo = (b - b_hi.astype(F32)).astype(BF16)
    tn = lambda x, y: lax.dot_general(x, y, (((0,), (0,)), ((), ())), preferred_element_type=F32)
    return tn(a_hi, b_hi) + tn(a_lo, b_hi) + tn(a_hi, b_lo)


def _cast_body(x_ref, o_ref, *, scale):
    x = x_ref[...]
    o_ref[...] = (x if scale == 1.0 else x * scale).astype(o_ref.dtype)


def _to_bf16(w, layer, cols=None, *, scale=1.0, block_bytes=4 << 20):
    _, rows, width = w.shape
    cols = width if cols is None else cols
    tr = rows
    while tr * cols * 4 > block_bytes and tr % 32 == 0:
        tr //= 2
    return pl.pallas_call(
        functools.partial(_cast_body, scale=scale),
        out_shape=jax.ShapeDtypeStruct((rows, cols), BF16),
        grid=(rows // tr,),
        in_specs=[pl.BlockSpec((None, tr, cols), lambda i: (layer, i, 0))],
        out_specs=pl.BlockSpec((tr, cols), lambda i: (i, 0)),
        compiler_params=_params(("parallel",)),
        name="to_bf16",
    )(w)


def _ffn_body(x_ref, g_ref, wg_ref, wu_ref, wd_ref, fn_ref, o_ref, xn_ref, *, final, sub):
    j = pl.program_id(1)

    row_chunks = [slice(r, r + ROW_CHUNK) for r in range(0, x_ref.shape[0], ROW_CHUNK)]

    @pl.when(j == 0)
    def _():
        for rs in row_chunks:
            x = x_ref[rs, :]
            xn_ref[rs, :] = _rms(x, g_ref[...]).astype(BF16)
            o_ref[rs, :] = x

    xn = xn_ref[...]
    hs = []
    for c in range(wg_ref.shape[1] // sub):
        sl = slice(c * sub, (c + 1) * sub)
        g = _dot(xn, wg_ref[:, sl])
        u = _dot(xn, wu_ref[:, sl])
        hs.append((g * jax.nn.sigmoid(g) * u).astype(BF16))
    o_ref[...] += _dot(jnp.concatenate(hs, axis=1), wd_ref[...])

    if final:
        @pl.when(j == pl.num_programs(1) - 1)
        def _():
            for rs in row_chunks:
                o_ref[rs, :] = _rms(o_ref[rs, :], fn_ref[...])


def _ffn(x, norm_g, w_gu, half_w_down, final_g, *, final, tm=1024, tf=512, sub=256):
    w_down = half_w_down
    L, D = x.shape
    F = w_down.shape[0]
    tm = min(tm, L)
    nf = F // tf
    return pl.pallas_call(
        functools.partial(_ffn_body, final=final, sub=min(sub, tf)),
        out_shape=jax.ShapeDtypeStruct((L, D), F32),
        grid=(L // tm, nf),
        in_specs=[
            pl.BlockSpec((tm, D), lambda i, j: (i, 0)),
            pl.BlockSpec((1, D), lambda i, j: (0, 0)),
            pl.BlockSpec((D, tf), lambda i, j: (0, j)),
            pl.BlockSpec((D, tf), lambda i, j: (0, j + nf)),
            pl.BlockSpec((tf, D), lambda i, j: (j, 0)),
            pl.BlockSpec((1, D), lambda i, j: (0, 0)),
        ],
        out_specs=pl.BlockSpec((tm, D), lambda i, j: (i, 0)),
        scratch_shapes=[pltpu.VMEM((tm, D), BF16)],
        compiler_params=_params(("parallel", "arbitrary")),
        name="ffn",
    )(x, norm_g.reshape(1, D), w_gu, w_gu, w_down, final_g.reshape(1, D))


def _conv3_inner(x, w):
    n = x.shape[0]
    y = pltpu.roll(x, 1, axis=0) * w[0:1] + x * w[1:2] + pltpu.roll(x, n - 1, axis=0) * w[2:3]
    return y[SUBLANES:n - SUBLANES]


def _proj_body(h_ref, hp_ref, hn_ref, g_ref, w_ref, b_ref, wlr_ref, blr_ref, caw_ref, cbw_ref,
               p_ref, lr_ref, a_ref, z_ref, g1_ref, g2_ref, u_ref, vb_ref, *, tn, wa):
    i = pl.program_id(0)
    tm = h_ref.shape[0]
    g = g_ref[...]
    xn = _rms(h_ref[...], g)
    xn_bf = xn.astype(BF16)
    u_ref[...] = xn_bf
    lr_ref[...] = _dot(xn_bf, wlr_ref[...]) + blr_ref[...]
    conv_w = 6 * wa
    v_lo = p_ref.shape[1] - vb_ref.shape[1]
    v_hi = v_lo + vb_ref.shape[1]
    for off in range(0, p_ref.shape[1] + vb_ref.shape[1], tn):
        src = slice(conv_w + off, conv_w + off + tn)
        y = _dot(xn_bf, w_ref[:, src]) + b_ref[:, src]
        if off < v_lo:
            p_ref[:, off:off + tn] = y
        elif off < v_hi:
            vb_ref[:, off - v_lo:off - v_lo + tn] = y.astype(BF16)
        else:
            p_ref[:, off - vb_ref.shape[1]:off - vb_ref.shape[1] + tn] = y

    xh = jnp.concatenate([_rms(hp_ref[...], g), xn, _rms(hn_ref[...], g)], axis=0).astype(BF16)
    row = lax.broadcasted_iota(jnp.int32, (tm + 2 * SUBLANES, 1), 0)
    has_prev = (i > 0).astype(F32)
    has_next = (i < pl.num_programs(0) - 1).astype(F32)
    valid = jnp.where(row < SUBLANES, has_prev, jnp.where(row >= tm + SUBLANES, has_next, 1.0))

    def sec(k):
        sl = slice(k * wa, (k + 1) * wa)
        return (_dot(xh, w_ref[:, sl]) + b_ref[:, sl]) * valid

    xa, ba, ca = sec(0), sec(1), sec(2)
    a_ref[...] = (ba[SUBLANES:tm + SUBLANES] * _conv3_inner(ca * xa, caw_ref[...])).astype(BF16)
    for k, o_ref in enumerate((z_ref, g1_ref, g2_ref)):
        _store_groups(o_ref, _conv3_inner(sec(3 + k), cbw_ref[k]))


def _proj(h, norm_g, w, b, w_lr, b_lr, conv_a, conv_b, *, wa, wv, tm=256, tn=512):
    L, D = h.shape
    conv_w = 6 * wa
    n_scan = w.shape[1] - conv_w - wv
    rb = tm // SUBLANES
    nrb = L // SUBLANES
    resident = lambda shape: pl.BlockSpec(shape, lambda i: (0,) * len(shape), pipeline_mode=pl.Buffered(1))
    grouped = pl.BlockSpec((wa // LANES, tm, LANES), lambda i: (0, i, 0))
    return pl.pallas_call(
        functools.partial(_proj_body, tn=tn, wa=wa),
        out_shape=(jax.ShapeDtypeStruct((L, n_scan), F32), jax.ShapeDtypeStruct((L, LANES), F32),
                   jax.ShapeDtypeStruct((L, wa), BF16))
        + (jax.ShapeDtypeStruct((wa // LANES, L, LANES), F32),) * 3
        + (jax.ShapeDtypeStruct((L, D), BF16), jax.ShapeDtypeStruct((L, wv), BF16)),
        grid=(L // tm,),
        in_specs=[
            pl.BlockSpec((tm, D), lambda i: (i, 0)),
            pl.BlockSpec((SUBLANES, D), lambda i: (jnp.maximum(i * rb - 1, 0), 0)),
            pl.BlockSpec((SUBLANES, D), lambda i: (jnp.minimum((i + 1) * rb, nrb - 1), 0)),
            resident((1, D)), resident(w.shape), resident((1, w.shape[1])),
            resident((D, LANES)), resident((1, LANES)), resident((3, wa)), resident((3, 3, wa)),
        ],
        out_specs=(pl.BlockSpec((tm, n_scan), lambda i: (i, 0)),
                   pl.BlockSpec((tm, LANES), lambda i: (i, 0)),
                   pl.BlockSpec((tm, wa), lambda i: (i, 0)), grouped, grouped, grouped,
                   pl.BlockSpec((tm, D), lambda i: (i, 0)), pl.BlockSpec((tm, wv), lambda i: (i, 0))),
        compiler_params=_params(("parallel",)),
        name="proj",
    )(h, h, h, norm_g.reshape(1, D), w, b, w_lr, b_lr, conv_a, conv_b)


def _store_groups(o_ref, x):
    for g in range(o_ref.shape[0]):
        o_ref[g] = x[:, g * LANES:(g + 1) * LANES]


def _load_groups(ref):
    return jnp.concatenate([ref[g] for g in range(ref.shape[0])], axis=1)


def _filter_body(w1t_ref, w1cs_ref, b1_ref, w2_ref, b2_ref, w3_ref, b3_ref, fr_ref, bands_ref, wo_ref, dl_ref,
                 k_ref, s_ref, *, half_tiles, tr, seq):
    i = pl.program_id(0)

    def lag_of(n):
        return jnp.where(n < seq, n, jnp.where(n == seq, 0, 2 * seq - n)).astype(F32)

    lag_row = lag_of(i * tr + lax.broadcasted_iota(jnp.int32, (1, tr), 1))
    ang = ((2.0 * math.pi / seq) * lag_row) * bands_ref[...]
    feats = jnp.concatenate([jnp.cos(ang), -jnp.sin(ang)], axis=0)
    fr = fr_ref[...]
    pre = w1t_ref[...] * (lag_row / max(seq - 1, 1)) + _dot_hi(w1cs_ref[...], feats) + b1_ref[...]
    h = jnp.sin(fr * pre)
    h = jnp.sin(fr * (_dot_hi(w2_ref[...], h) + b2_ref[...]))
    h = jnp.sin(fr * (_dot_hi(w3_ref[...], h) + b3_ref[...]))
    t = lag_of(i * tr + lax.broadcasted_iota(jnp.int32, (tr, 1), 0)) / max(seq - 1, 1)
    decay = jnp.exp(-t * jnp.abs(dl_ref[...]))
    hd = _dot3_tn(h, wo_ref[0]) * decay

    @pl.when(i % half_tiles == 0)
    def _():
        s_ref[...] = jnp.zeros_like(s_ref)

    s_ref[0] += jnp.sum(jnp.abs(hd), axis=0, keepdims=True)
    row = lax.broadcasted_iota(jnp.int32, hd.shape, 0)
    zero_row = jnp.where(i == half_tiles, 0, -1)
    _store_groups(k_ref, jnp.where(row == zero_row, 0.0, hd))


def _filters(seq, w1, b1, w2, b2, w3, b3, freq, w_out2, deltas2, *, tr=1024):
    n_rows = 2 * seq
    hid = w2.shape[0]
    width = w_out2.shape[2]
    half_tiles = n_rows // (2 * tr)
    bands = jnp.linspace(1e-4, HY_BANDS - 1, HY_BANDS, dtype=F32).reshape(HY_BANDS, 1)
    full = lambda shape: pl.BlockSpec(shape, lambda i: (0,) * len(shape))
    colv = lambda v: v.reshape(hid, 1)
    return pl.pallas_call(
        functools.partial(_filter_body, half_tiles=half_tiles, tr=tr, seq=seq),
        out_shape=(jax.ShapeDtypeStruct((width // LANES, n_rows, LANES), F32),
                   jax.ShapeDtypeStruct((2, 1, width), F32)),
        grid=(n_rows // tr,),
        in_specs=[
            full((hid, 1)), full((hid, 2 * HY_BANDS)), full((hid, 1)), full((hid, hid)), full((hid, 1)),
            full((hid, hid)), full((hid, 1)), full((hid, 1)), full((HY_BANDS, 1)),
            pl.BlockSpec((1, hid, width), lambda i: (i // half_tiles, 0, 0)),
            full((1, width)),
        ],
        out_specs=(pl.BlockSpec((width // LANES, tr, LANES), lambda i: (0, i, 0)),
                   pl.BlockSpec((1, 1, width), lambda i: (i // half_tiles, 0, 0))),
        compiler_params=_params(("arbitrary",)),
        name="hyena_filters",
    )(w1[0:1].T, w1[1:].T, colv(b1), w2.T, colv(b2), w3.T, colv(b3), colv(freq), bands, w_out2, deltas2)


def _dft_consts(n1, n2):
    n = n1 * n2
    k = np.arange(n1)
    m = np.arange(n2)
    ang2 = 2.0 * np.pi * ((m[:, None] * m[None, :]) % n2) / n2
    c2, s2 = np.cos(ang2), np.sin(ang2)
    as_bf = lambda a: jnp.asarray(np.asarray(a, np.float32).astype(jnp.bfloat16))
    time_index = k[None, None, :] * n2 + m[:, None, None]
    ang = 2.0 * np.pi * ((k[None, :, None] * time_index) % n) / n
    stage = np.concatenate([np.cos(ang), -np.sin(ang)], axis=1)
    half = n1 // 2
    return dict(
        fwd1=as_bf(stage),
        inv1=as_bf(stage.transpose(0, 2, 1)[:, :half] / n),
        fwd2=as_bf(np.block([[c2, s2], [-s2, c2]])),
        inv2=as_bf(np.block([[c2, -s2], [s2, c2]])),
    )


def _flatten_block(ref, buf):
    groups, n, t2, lanes = ref.shape
    for g in range(groups):
        for part in range(t2 // SUBLANES):
            rows = slice(part * SUBLANES, (part + 1) * SUBLANES)
            buf[g, part] = ref[g, :, rows, :].reshape(n * SUBLANES, lanes)


def _unflatten_block(buf, ref):
    groups, n, t2, lanes = ref.shape
    for g in range(groups):
        for part in range(t2 // SUBLANES):
            rows = slice(part * SUBLANES, (part + 1) * SUBLANES)
            ref[g, :, rows, :] = buf[g, part].reshape(n, SUBLANES, lanes)


def _load_strided(buf, s):
    part, sub = divmod(s, SUBLANES)
    n = buf.shape[2] // SUBLANES
    return jnp.concatenate([buf[g, part, pl.ds(sub, n, stride=SUBLANES), :] for g in range(buf.shape[0])],
                           axis=1)


def _store_strided(buf, s, x):
    part, sub = divmod(s, SUBLANES)
    n = buf.shape[2] // SUBLANES
    for g in range(buf.shape[0]):
        buf[g, part, pl.ds(sub, n, stride=SUBLANES), :] = x[:, g * LANES:(g + 1) * LANES]


def _flat_scratch(groups, n, t2, dtype=F32):
    return pltpu.VMEM((groups, t2 // SUBLANES, n * SUBLANES, LANES), dtype)


U32 = jnp.uint32


def _pack_complex(re, im):
    hi = lax.bitcast_convert_type(re.astype(BF16).astype(F32), U32)
    lo = lax.bitcast_convert_type(im.astype(BF16).astype(F32), U32)
    return hi | (lo >> 16)


def _unpack_complex(w):
    re = lax.bitcast_convert_type(w & jnp.uint32(0xFFFF0000), F32)
    im = lax.bitcast_convert_type(w << 16, F32)
    return re, im


def _outer_fwd_body(x_ref, f_ref, *rest, scaled):
    if scaled:
        s_ref, a_ref, xb, ab = rest
        inv = 1.0 / (s_ref[...] + EPS)
    else:
        a_ref, xb, ab = rest
    rows, t2 = x_ref.shape[1:3]
    n1 = a_ref.shape[1]
    _flatten_block(x_ref, xb)
    for s in range(t2):
        x = _load_strided(xb, s)
        if scaled:
            half = rows // 2
            x = jnp.concatenate([x[:half] * inv[0], x[half:] * inv[1]], axis=0)
        a = _dot(f_ref[s], x.astype(BF16))
        _store_strided(ab, s, _pack_complex(a[:n1], a[n1:]))
    _unflatten_block(ab, a_ref)


def _outer_fwd(x4, fwd1, sums=None, *, t2=SUBLANES, gt=4):
    groups, rows, n2, _ = x4.shape
    n1 = fwd1.shape[2]
    scaled = sums is not None
    blk = lambda r: pl.BlockSpec((gt, r, t2, LANES), lambda j, c: (c, 0, j, 0))
    args = [x4, fwd1]
    in_specs = [blk(rows), pl.BlockSpec((t2, 2 * n1, rows), lambda j, c: (j, 0, 0))]
    if scaled:
        args.append(sums)
        in_specs.append(pl.BlockSpec((2, 1, gt * LANES), lambda j, c: (0, 0, c)))
    return pl.pallas_call(
        functools.partial(_outer_fwd_body, scaled=scaled),
        out_shape=jax.ShapeDtypeStruct((groups, n1, n2, LANES), U32),
        grid=(n2 // t2, groups // gt),
        in_specs=in_specs,
        out_specs=blk(n1),
        scratch_shapes=[_flat_scratch(gt, rows, t2), _flat_scratch(gt, n1, t2, U32)],
        compiler_params=_params(("parallel", "parallel")),
        name="dft_outer_fwd",
    )(*args)


def _load_slab(ref, s):
    return jnp.concatenate([ref[g, s] for g in range(ref.shape[0])], axis=1)


def _store_slab(ref, s, x):
    for g in range(ref.shape[0]):
        ref[g, s] = x[:, g * LANES:(g + 1) * LANES]


def _mid_conv_body(a_ref, k_ref, f2_ref, g2_ref, d_ref):
    n2 = a_ref.shape[2]
    width = a_ref.shape[0] * LANES
    for s in range(a_ref.shape[1]):
        re, im = _unpack_complex(jnp.concatenate([_load_slab(a_ref, s), _load_slab(k_ref, s)], axis=1))
        x = _dot(f2_ref[...], jnp.concatenate([re, im], axis=0).astype(BF16))
        xr, xi, kr, ki = x[:n2, :width], x[n2:, :width], x[:n2, width:], x[n2:, width:]
        yr = xr * kr - xi * ki
        yi = xr * ki + xi * kr
        c = _dot(g2_ref[...], jnp.concatenate([yr, yi], axis=0).astype(BF16))
        _store_slab(d_ref, s, _pack_complex(c[:n2], c[n2:]))


def _mid(a, kf, order, fwd2, inv2, *, kb=8):
    groups, n1, n2, _ = a.shape
    slab = pl.BlockSpec((groups, kb, n2, LANES), lambda i: (0, i, 0, 0))
    kslab = pl.BlockSpec((groups, kb, n2, LANES), lambda i: (order, i, 0, 0))
    mat = pl.BlockSpec((2 * n2, 2 * n2), lambda i: (0, 0))
    return pl.pallas_call(
        _mid_conv_body,
        out_shape=jax.ShapeDtypeStruct(a.shape, U32),
        grid=(n1 // kb,),
        in_specs=[slab, kslab, mat, mat],
        out_specs=slab,
        compiler_params=_params(("parallel",)),
        name="hyena_mid_conv",
    )(a, kf, fwd2, inv2)


def _outer_inv_body(d_ref, e_ref, gate_ref, z_ref, skip_ref, o_ref, db, gb, zb, ob):
    skip = skip_ref[...]
    t2 = d_ref.shape[2]
    for ref, buf in ((d_ref, db), (gate_ref, gb), (z_ref, zb)):
        _flatten_block(ref, buf)
    for s in range(t2):
        dr, di = _unpack_complex(_load_strided(db, s))
        y = _dot(e_ref[s], jnp.concatenate([dr, di], axis=0).astype(BF16))
        _store_strided(ob, s, _load_strided(gb, s) * (y + skip * _load_strided(zb, s)))
    _unflatten_block(ob, o_ref)


def _outer_inv(d, inv1, gate4, z4, skip, *, t2=SUBLANES):
    groups, n1, n2, _ = d.shape
    half = n1 // 2
    blk = lambda r: pl.BlockSpec((groups, r, t2, LANES), lambda j: (0, 0, j, 0))
    return pl.pallas_call(
        _outer_inv_body,
        out_shape=jax.ShapeDtypeStruct((groups, half, n2, LANES), F32),
        grid=(n2 // t2,),
        in_specs=[blk(n1), pl.BlockSpec((t2, half, 2 * n1), lambda j: (j, 0, 0)), blk(half), blk(half),
                  pl.BlockSpec((1, groups * LANES), lambda j: (0, 0))],
        out_specs=blk(half),
        scratch_shapes=[_flat_scratch(groups, n1, t2, U32)] + [_flat_scratch(groups, half, t2)] * 3,
        compiler_params=_params(("parallel",)),
        name="dft_outer_inv",
    )(d, inv1, gate4, z4, skip)


def _log_sigmoid(x):
    return jnp.minimum(x, 0.0) - jnp.log1p(jnp.exp(-jnp.abs(x)))


_NT = (((1,), (1,)), ((), ()))
_TN = (((0,), (0,)), ((), ()))


def _chunk_matrices(sub):
    i = np.arange(sub)
    same = (i[:, None] // GLA_CHUNK) == (i[None, :] // GLA_CHUNK)
    fwd = same & (i[None, :] <= i[:, None])
    bwd = same & (i[None, :] >= i[:, None])
    return jnp.asarray(np.stack([fwd, bwd]), F32)


def _gla_gates_body(q_ref, k_ref, lr_ref, w_ref, b_ref, cm_ref, qe_ref, ke_ref, kd_ref, dec_ref, *, scale):
    tg, width = q_ref.shape
    sub = cm_ref.shape[1]
    chunk = GLA_CHUNK
    for s in range(tg // sub):
        rs = slice(s * sub, (s + 1) * sub)
        g = _log_sigmoid(_dot(lr_ref[rs, :].astype(BF16), w_ref[...]) + b_ref[...]) / GLA_TAU
        g_hi = g.astype(BF16)
        g_lo = (g - g_hi.astype(F32)).astype(BF16)
        q = q_ref[rs, :] * scale
        k = k_ref[rs, :]
        for d in range(2):
            cs = slice(d * width, (d + 1) * width)
            cum = cm_ref[d].astype(BF16)
            bc = _dot(cum, g_hi[:, cs]) + _dot(cum, g_lo[:, cs])
            qe_ref[d, rs, :] = (q * jnp.exp(bc)).astype(BF16)
            ke_ref[d, rs, :] = (k * jnp.exp(-bc)).astype(BF16)
            for c in range(sub // chunk):
                lo, hi = c * chunk, (c + 1) * chunk
                bl = bc[lo:lo + 1] if d else bc[hi - 1:hi]
                kd_ref[d, s * sub + lo:s * sub + hi, :] = (k[lo:hi] * jnp.exp(bl - bc[lo:hi])).astype(BF16)
                dec_ref[d, s * (sub // chunk) + c:s * (sub // chunk) + c + 1, :] = jnp.exp(bl)


def _gla_gates(p, lr, w_both, b_both, *, q_off, k_off, dk, heads, tg=1024, sub=256):
    L = p.shape[0]
    tg = min(tg, L)
    wk = heads * dk
    opnd = jax.ShapeDtypeStruct((2, L, wk), BF16)
    row = lambda width, off: pl.BlockSpec((tg, width), lambda i: (i, off // width))
    both = pl.BlockSpec((2, tg, wk), lambda i: (0, i, 0))
    const = lambda shape: pl.BlockSpec(shape, lambda i: (0,) * len(shape))
    return pl.pallas_call(
        functools.partial(_gla_gates_body, scale=dk ** -0.5),
        out_shape=(opnd, opnd, opnd, jax.ShapeDtypeStruct((2, L // GLA_CHUNK, wk), F32)),
        grid=(L // tg,),
        in_specs=[row(wk, q_off), row(wk, k_off), row(LANES, 0),
                  const((LANES, 2 * wk)), const((1, 2 * wk)), const((2, sub, sub))],
        out_specs=(both, both, both, pl.BlockSpec((2, tg // GLA_CHUNK, wk), lambda i: (0, i, 0))),
        compiler_params=_params(("parallel",)),
        name="gla_gates",
    )(p, p, lr, w_both, b_both, _chunk_matrices(sub))


def _gla_direction(qe_ref, ke_ref, kd_ref, v_ref, dec_ref, mask, o_ref, st_ref, *, reverse, dk, dv):
    tb = qe_ref.shape[0]
    sub = mask.shape[0]
    chunk = GLA_CHUNK
    for hd in range(st_ref.shape[0]):
        ks, vs = slice(hd * dk, (hd + 1) * dk), slice(hd * dv, (hd + 1) * dv)
        intras = []
        for s in range(tb // sub):
            rs = slice(s * sub, (s + 1) * sub)
            a = lax.dot_general(qe_ref[rs, ks], ke_ref[rs, ks], _NT, preferred_element_type=F32)
            intra = _dot(jnp.where(mask > 0.5, a, 0.0).astype(BF16), v_ref[rs, vs])
            intras += [intra[c * chunk:(c + 1) * chunk] for c in range(sub // chunk)]
        st = st_ref[hd]
        chunks = range(tb // chunk)
        for c in (reversed(chunks) if reverse else chunks):
            sl = slice(c * chunk, (c + 1) * chunk)
            o_ref[sl, vs] = intras[c] + lax.dot_general(qe_ref[sl, ks], st.astype(BF16), _NT,
                                                        preferred_element_type=F32)
            upd = lax.dot_general(v_ref[sl, vs], kd_ref[sl, ks], _TN, preferred_element_type=F32)
            st = st * dec_ref[c:c + 1, ks] + upd
        st_ref[hd] = st


def _gla_fwd_body(qe, ke, kd, v, dec, cm_ref, o_ref, st_ref, *, dk, dv):
    @pl.when(pl.program_id(0) == 0)
    def _():
        st_ref[...] = jnp.zeros_like(st_ref)

    _gla_direction(qe, ke, kd, v, dec, cm_ref[0], o_ref, st_ref, reverse=False, dk=dk, dv=dv)


def _gla_bwd_body(qe, ke, kd, v, dec, cm_ref, of_ref, r_ref, gn_ref, c_ref, st_ref, ob_ref, *, dk, dv):
    @pl.when(pl.program_id(0) == 0)
    def _():
        st_ref[...] = jnp.zeros_like(st_ref)

    _gla_direction(qe, ke, kd, v, dec, cm_ref[1], ob_ref, st_ref, reverse=True, dk=dk, dv=dv)
    for rs in [slice(r0, r0 + ROW_CHUNK) for r0 in range(0, ob_ref.shape[0], ROW_CHUNK)]:
        r = r_ref[rs, :]
        gate = r * jax.nn.sigmoid(r)
        for hd in range(ob_ref.shape[1] // dv):
            sl = slice(hd * dv, (hd + 1) * dv)
            o = of_ref[rs, sl] + ob_ref[rs, sl]
            c_ref[rs, sl] = (_rms(o, gn_ref[...]) * gate[:, sl]).astype(BF16)


def _gla(qe, ke, kd, dec, vb, p, gla_norm, *, r_off, dk, dv, tb=1024, sub=256):
    L, wv = vb.shape
    tb = min(tb, L)
    nb = L // tb
    heads = wv // dv
    wk = heads * dk
    nc = tb // GLA_CHUNK
    cm = _chunk_matrices(sub)
    cm_spec = pl.BlockSpec((2, sub, sub), lambda i: (0, 0, 0))

    def side(rev):
        blk = (lambda i: nb - 1 - i) if rev else (lambda i: i)
        opnd = pl.BlockSpec((None, tb, wk), lambda i: (int(rev), blk(i), 0))
        return blk, [opnd, opnd, opnd, pl.BlockSpec((tb, wv), lambda i: (blk(i), 0)),
                     pl.BlockSpec((None, nc, wk), lambda i: (int(rev), blk(i), 0))]

    state = pltpu.VMEM((heads, dv, dk), F32)
    _, fwd_specs = side(False)
    o_f = pl.pallas_call(
        functools.partial(_gla_fwd_body, dk=dk, dv=dv),
        out_shape=jax.ShapeDtypeStruct((L, wv), F32),
        grid=(nb,),
        in_specs=fwd_specs + [cm_spec],
        out_specs=pl.BlockSpec((tb, wv), lambda i: (i, 0)),
        scratch_shapes=[state],
        compiler_params=_params(("arbitrary",)),
        name="gla_fwd",
    )(qe, ke, kd, vb, dec, cm)
    blk, bwd_specs = side(True)
    row = lambda col: pl.BlockSpec((tb, wv), lambda i: (blk(i), col))
    return pl.pallas_call(
        functools.partial(_gla_bwd_body, dk=dk, dv=dv),
        out_shape=jax.ShapeDtypeStruct((L, wv), BF16),
        grid=(nb,),
        in_specs=bwd_specs + [cm_spec, row(0), row(r_off // wv), pl.BlockSpec((1, dv), lambda i: (0, 0))],
        out_specs=row(0),
        scratch_shapes=[state, pltpu.VMEM((tb, wv), F32)],
        compiler_params=_params(("arbitrary",)),
        name="gla_bwd",
    )(qe, ke, kd, vb, dec, cm, o_f, p, gla_norm.reshape(1, dv))


def _merge_body(u_ref, a_ref, z_ref, c_ref,
                wga_ref, wgb_ref, wgc_ref, bga_ref, bgb_ref, bgc_ref,
                wa_ref, wb_ref, wc_ref, m_ref, zb_ref):
    @pl.when(pl.program_id(1) == 0)
    def _():
        zb_ref[...] = _load_groups(z_ref).astype(BF16)

    u = u_ref[...]
    m = jax.nn.sigmoid(_dot(u, wga_ref[...]) + bga_ref[...]) * _dot(a_ref[...], wa_ref[...])
    m += jax.nn.sigmoid(_dot(u, wgb_ref[...]) + bgb_ref[...]) * _dot(zb_ref[...], wb_ref[...])
    m += jax.nn.sigmoid(_dot(u, wgc_ref[...]) + bgc_ref[...]) * _dot(c_ref[...], wc_ref[...])
    m_ref[...] = m.astype(BF16)


def _merge(u, a_pre, z, c_pre, w_gate, b_gate, w_a, w_b, w_c, *, tm=1024, tn=512):
    L, D = u.shape
    wa, wb, wc = w_a.shape[0], w_b.shape[0], w_c.shape[0]
    tm = min(tm, L)
    nj = D // tn
    row = lambda width: pl.BlockSpec((tm, width), lambda i, j: (i, 0))
    gate_w = lambda k: pl.BlockSpec((D, tn), lambda i, j: (0, j + k * nj))
    gate_b = lambda k: pl.BlockSpec((1, tn), lambda i, j: (0, j + k * nj))
    br_w = lambda width: pl.BlockSpec((width, tn), lambda i, j: (0, j))
    return pl.pallas_call(
        _merge_body,
        out_shape=jax.ShapeDtypeStruct((L, D), BF16),
        grid=(L // tm, nj),
        in_specs=[
            row(D), row(wa), pl.BlockSpec((wb // LANES, tm, LANES), lambda i, j: (0, i, 0)), row(wc),
            gate_w(0), gate_w(1), gate_w(2), gate_b(0), gate_b(1), gate_b(2),
            br_w(wa), br_w(wb), br_w(wc),
        ],
        out_specs=pl.BlockSpec((tm, tn), lambda i, j: (i, j)),
        scratch_shapes=[pltpu.VMEM((tm, wb), BF16)],
        compiler_params=_params(("parallel", "arbitrary")),
        name="merge",
    )(u, a_pre, z, c_pre, w_gate, w_gate, w_gate, b_gate, b_gate, b_gate, w_a, w_b, w_c)


def _out_proj_body(h_ref, m_ref, w_ref, o_ref, *, tn):
    m = m_ref[...]
    for c in range(o_ref.shape[1] // tn):
        sl = slice(c * tn, (c + 1) * tn)
        o_ref[:, sl] = h_ref[:, sl] + _dot(m, w_ref[:, sl])


def _out_proj(h, m, w_o, *, tm=512, tn=512):
    L, D = h.shape
    row = pl.BlockSpec((tm, D), lambda i: (i, 0))
    return pl.pallas_call(
        functools.partial(_out_proj_body, tn=tn),
        out_shape=jax.ShapeDtypeStruct((L, D), F32),
        grid=(L // tm,),
        in_specs=[row, row, pl.BlockSpec((D, D), lambda i: (0, 0), pipeline_mode=pl.Buffered(1))],
        out_specs=row,
        compiler_params=_params(("parallel",)),
        name="out_proj",
    )(h, m, w_o)


def _hyena(z0, g1, g2, skip, filt, consts):
    groups, L, _ = z0.shape
    n1 = DFT_N1
    n2 = 2 * L // n1
    k_time, sums = filt
    kf = _outer_fwd(k_time.reshape(k_time.shape[0], n1, n2, LANES), consts["fwd1"], sums)
    z = z0.reshape(groups, n1 // 2, n2, LANES)
    for order, gate in enumerate((g1, g2)):
        a = _outer_fwd(z, consts["fwd1"], t2=min(2 * SUBLANES, n2))
        d = _mid(a, kf, order, consts["fwd2"], consts["inv2"])
        z = _outer_inv(d, consts["inv1"], gate.reshape(z.shape), z, skip[order].reshape(1, groups * LANES))
    return z.reshape(groups, L, LANES)


def kernel(x, ffn1_norm, ffn1_w_gu, ffn1_w_down, mix_norm, w_in, b_in, conv_a, conv_b, hf_w1, hf_b1, hf_w2, hf_b2, hf_w3, hf_b3, hf_w_out, hf_freq, hy_skip, gk_w2, gk_b, gla_norm, w_br_a, w_br_b, w_br_c, w_o, ffn2_norm, ffn2_w_gu, ffn2_w_down, final_norm):
    bsz, L, D = x.shape
    depth = ffn1_norm.shape[0]
    wa = conv_a.shape[2]
    wb = conv_b.shape[3]
    wck = gk_w2.shape[3]
    dv = gla_norm.shape[1]
    wcv = GLA_HEADS * dv
    dk = wck // GLA_HEADS
    hid = hf_w2.shape[1]
    assert wa == wb, "the projection kernel tiles both conv branches with one section width"
    n_main = 3 * wa + 3 * wb + 2 * wck + 2 * wcv
    q_off, k_off, r_off = 0, wck, 2 * wck
    lr_off = n_main
    gate_off = n_main + 2 * GLA_RANK
    n1 = DFT_N1
    n2 = 2 * L // n1

    consts = _dft_consts(n1, n2)
    deltas =jnp.linspace(math.log(HY_TARGET) / HY_SLOW_DECAY, math.log(HY_TARGET) / HY_FAST_DECAY, wb, dtype=F32)
    deltas2 = jnp.tile(deltas.reshape(1, wb), (1, HY_ORDER))

    outs = []
    for bi in range(bsz):
        h = x[bi]
        for l in range(depth):
            h = _ffn(h, ffn1_norm[l], _to_bf16(ffn1_w_gu, l), _to_bf16(ffn1_w_down, l, scale=0.5),
                     final_norm, final=False)

            w_lr = jnp.pad(w_in[l][:, lr_off:gate_off], ((0, 0), (0, LANES - 2 * GLA_RANK))).astype(BF16)
            b_lr = jnp.pad(b_in[l][lr_off:gate_off], (0, LANES - 2 * GLA_RANK)).reshape(1, LANES)
            p, lr, a_pre, z0, g1, g2, u, vb = _proj(
                h, mix_norm[l], w_in[l][:, :n_main].astype(BF16), b_in[l][:n_main].reshape(1, n_main),
                w_lr, b_lr, conv_a[l], conv_b[l], wa=wa, wv=wcv)

            w_out2 = hf_w_out[l].reshape(hid, HY_ORDER, 2, wb).transpose(2, 0, 1, 3).reshape(2, hid, HY_ORDER * wb)
            filt = _filters(L, hf_w1[l], hf_b1[l], hf_w2[l], hf_b2[l], hf_w3[l], hf_b3[l], hf_freq[l],
                            w_out2, deltas2)
            zb = _hyena(z0, g1, g2, hy_skip[l], filt, consts)

            w_both = jnp.zeros((LANES, 2 * wck), F32)
            w_both = w_both.at[:GLA_RANK, :wck].set(gk_w2[l, 0]).at[GLA_RANK:2 * GLA_RANK, wck:].set(gk_w2[l, 1])
            qe, ke, kd, dec = _gla_gates(p, lr, w_both.astype(BF16), gk_b[l].reshape(1, 2 * wck),
                                         q_off=q_off, k_off=k_off, dk=dk, heads=GLA_HEADS)
            c_pre = _gla(qe, ke, kd, dec, vb, p, gla_norm[l], r_off=r_off, dk=dk, dv=dv)

            merged = _merge(u, a_pre, zb, c_pre,
                            w_in[l][:, gate_off:].astype(BF16), b_in[l][gate_off:].reshape(1, 3 * D),
                            _to_bf16(w_br_a, l), _to_bf16(w_br_b, l), _to_bf16(w_br_c, l))
            h = _out_proj(h, merged, _to_bf16(w_o, l))

            h = _ffn(h, ffn2_norm[l], _to_bf16(ffn2_w_gu, l), _to_bf16(ffn2_w_down, l, scale=0.5),
                     final_norm, final=(l == depth - 1))
        outs.append(h)
    return jnp.stack(outs, axis=0)
```
